```python
import math
import jax, jax.numpy as jnp
from jax import lax
import numpy as np

D_MODEL = 1024
BATCH = 4
SEQ = 4096
DEPTH = 1

MLA_HEADS = 8
MLA_Q_LORA = 256
MLA_KV_LORA = 128
MLA_NOPE = 64
MLA_ROPE = 32
MLA_V = 64
MLA_QK = MLA_NOPE + MLA_ROPE
MLA_WIDTH = MLA_HEADS * MLA_V
ROPE_THETA = 10000.0
FOX_HEADS = 8
FOX_HEAD_DIM = 64
FOX_WIDTH = FOX_HEADS * FOX_HEAD_DIM
Q_BLOCK = 128
IN_SPLIT_SIZES = (MLA_Q_LORA, MLA_KV_LORA, MLA_ROPE, FOX_WIDTH, FOX_WIDTH, FOX_WIDTH, FOX_HEADS, D_MODEL, D_MODEL)
IN_COLS = MLA_Q_LORA + MLA_KV_LORA + MLA_ROPE + 3 * FOX_WIDTH + FOX_HEADS + 2 * D_MODEL
N_EXPERTS = 64
N_GROUPS = 8
TOPK_GROUPS = 4
TOP_K = 6
D_EXPERT = 256
D_SHARED = 256
ROUTED_SCALE = 2.5
EXPERT_BLOCK = 128
N_MOD = 6
NORM_EPS = 1e-6
NEG_INF = -1e30

kernel_name = 'adaln_mla_fox_gated_hybrid_moe'


def rms_norm(x, gain):
    xf = x.astype(jnp.float32)
    y = xf * lax.rsqrt(jnp.mean(xf * xf, axis=-1, keepdims=True) + NORM_EPS)
    return (y * gain.astype(jnp.float32)).astype(x.dtype)


def split_cols(t, sizes):
    out, start = [], 0
    for n in sizes:
        out.append(t[..., start:start + n])
        start += n
    return out


def apply_rope(x, positions):
    half = x.shape[-1] // 2
    inv_freq = jnp.power(ROPE_THETA, -jnp.arange(half, dtype=jnp.float32) / half)
    ang = positions.astype(jnp.float32)[:, None, :, None] * inv_freq
    cos = jnp.cos(ang).astype(x.dtype)
    sin = jnp.sin(ang).astype(x.dtype)
    x1, x2 = x[..., :half], x[..., half:]
    return jnp.concatenate([x1 * cos - x2 * sin, x2 * cos + x1 * sin], axis=-1)


def causal_block_attention(q, k, v, scale, decay=None):
    B, H, S, _ = q.shape
    dv = v.shape[-1]
    nblk = S // Q_BLOCK

    def to_blocks(t):
        return jnp.moveaxis(t.reshape((B, H, nblk, Q_BLOCK) + t.shape[3:]), 2, 0)

    key_pos = jnp.arange(S)

    def one_block(xs):
        i, qb = xs[0], xs[1]
        logits = jnp.einsum('bhqd,bhkd->bhqk', qb, k).astype(jnp.float32) * scale
        if decay is not None:
            logits = logits + xs[2][..., :, None] - decay[:, :, None, :]
        q_pos = i * Q_BLOCK + jnp.arange(Q_BLOCK)
        logits = jnp.where(key_pos[None, :] <= q_pos[:, None], logits, NEG_INF)
        p = jax.nn.softmax(logits, axis=-1).astype(v.dtype)
        return jnp.einsum('bhqk,bhkd->bhqd', p, v)

    xs = (jnp.arange(nblk), to_blocks(q))
    if decay is not None:
        xs = xs + (to_blocks(decay),)
    out = lax.map(one_block, xs)
    return jnp.moveaxis(out, 0, 2).reshape(B, H, S, dv)


def hybrid_mixer(h, positions, w_in, b_forget, g_q_lat, w_q_up, g_kv_lat, w_kv_up, w_o_mla, w_o_fox, w_out):
    B, S, _ = h.shape
    proj = h @ w_in
    q_lat, kv_lat, k_rope, fq, fk, fv, f_logit, gate_a, gate_b = split_cols(proj, IN_SPLIT_SIZES)

    q = (rms_norm(q_lat, g_q_lat) @ w_q_up).reshape(B, S, MLA_HEADS, MLA_QK).transpose(0, 2, 1, 3)
    q_nope, q_pe = q[..., :MLA_NOPE], apply_rope(q[..., MLA_NOPE:], positions)
    kv = (rms_norm(kv_lat, g_kv_lat) @ w_kv_up).reshape(B, S, MLA_HEADS, MLA_NOPE + MLA_V).transpose(0, 2, 1, 3)
    k_nope, v_mla = kv[..., :MLA_NOPE], kv[..., MLA_NOPE:]
    k_pe = apply_rope(k_rope[:, None, :, :], positions)
    q_mla = jnp.concatenate([q_nope, q_pe], axis=-1)
    k_mla = jnp.concatenate([k_nope, jnp.broadcast_to(k_pe, (B, MLA_HEADS, S, MLA_ROPE))], axis=-1)
    o_mla = causal_block_attention(q_mla, k_mla, v_mla, 1.0 / math.sqrt(MLA_QK))
    o_mla = o_mla.transpose(0, 2, 1, 3).reshape(B, S, MLA_WIDTH)

    def heads(t):
        return t.reshape(B, S, FOX_HEADS, FOX_HEAD_DIM).transpose(0, 2, 1, 3)

    log_f = jax.nn.log_sigmoid((f_logit + b_forget).astype(jnp.float32))
    cum_log_f = jnp.cumsum(log_f, axis=1).transpose(0, 2, 1)
    o_fox = causal_block_attention(heads(fq), heads(fk), heads(fv), 1.0 / math.sqrt(FOX_HEAD_DIM), cum_log_f)
    o_fox = o_fox.transpose(0, 2, 1, 3).reshape(B, S, FOX_WIDTH)

    merged = jax.nn.sigmoid(gate_a) * (o_mla @ w_o_mla) + jax.nn.sigmoid(gate_b) * (o_fox @ w_o_fox)
    return merged @ w_out


def moe_ffn(h, w_router, b_router, w_exp_gate, w_exp_up, w_exp_down, w_sh_gate, w_sh_up, w_sh_down):
    B, S, D = h.shape
    T = B * S
    xf = h.reshape(T, D)

    scores = jax.nn.sigmoid((xf @ w_router).astype(jnp.float32))
    choice = scores + b_router.astype(jnp.float32)
    group_score = lax.top_k(choice.reshape(T, N_GROUPS, N_EXPERTS // N_GROUPS), 2)[0].sum(-1)
    _, group_idx = lax.top_k(group_score, TOPK_GROUPS)
    group_keep = (group_idx[..., None] == jnp.arange(N_GROUPS)).any(axis=-2)
    expert_keep = jnp.repeat(group_keep, N_EXPERTS // N_GROUPS, axis=-1)
    _, expert_idx = lax.top_k(jnp.where(expert_keep, choice, NEG_INF), TOP_K)
    gate = jnp.take_along_axis(scores, expert_idx, axis=-1)
    gate = (gate / gate.sum(-1, keepdims=True) * ROUTED_SCALE).astype(h.dtype)

    n_assign = T * TOP_K
    flat_e = expert_idx.reshape(-1).astype(jnp.int32)
    flat_tok = jnp.arange(n_assign, dtype=jnp.int32) // TOP_K
    order = jnp.argsort(flat_e)
    se, stok, sgate = flat_e[order], flat_tok[order], gate.reshape(-1)[order]
    counts = jnp.zeros((N_EXPERTS,), jnp.int32).at[flat_e].add(1)
    padded = (counts + EXPERT_BLOCK - 1) // EXPERT_BLOCK * EXPERT_BLOCK
    start = jnp.cumsum(counts) - counts
    pend = jnp.cumsum(padded)
    pstart = pend - padded
    dest = pstart[se] + jnp.arange(n_assign, dtype=jnp.int32) - start[se]
    n_blocks = (n_assign + EXPERT_BLOCK - 1) // EXPERT_BLOCK + N_EXPERTS
    n_rows = n_blocks * EXPERT_BLOCK
    row_tok = jnp.full((n_rows,), T, jnp.int32).at[dest].set(stok)
    row_gate = jnp.zeros((n_rows,), h.dtype).at[dest].set(sgate)
    block_e = jnp.minimum(
        jnp.searchsorted(pend, jnp.arange(n_blocks, dtype=jnp.int32) * EXPERT_BLOCK, side='right'),
        N_EXPERTS - 1)
    x_pad = jnp.concatenate([xf, jnp.zeros((1, D), xf.dtype)], axis=0)

    def expert_block(xs):
        tok, g, e = xs
        xb = x_pad[tok]
        hb = jax.nn.silu(xb @ w_exp_gate[e]) * (xb @ w_exp_up[e])
        return (hb @ w_exp_down[e]) * g[:, None]

    yb = lax.map(expert_block, (row_tok.reshape(n_blocks, EXPERT_BLOCK),
                                row_gate.reshape(n_blocks, EXPERT_BLOCK), block_e))
    routed = jax.ops.segment_sum(yb.reshape(n_rows, D), row_tok, num_segments=T + 1)[:T]
    shared = (jax.nn.silu(xf @ w_sh_gate) * (xf @ w_sh_up)) @ w_sh_down
    return (shared + routed).reshape(B, S, D)


def dense_init(k, shape, fan_in, scale=1.0):
    return scale * fan_in ** -0.5 * jax.random.normal(k, shape, jnp.float32)


def setup_inputs(seed: int = 0) -> dict:
    key = jax.random.key(seed)
    ks = jax.random.split(key, 25)
    L, D, E = DEPTH, D_MODEL, N_EXPERTS

    def gain(k, shape):
        return 1.0 + 0.02 * jax.random.normal(k, shape, jnp.float32)

    offset = jax.random.randint(ks[2], (BATCH, 1), 0, 1024, dtype=jnp.int32)
    return {
        'x': jax.random.normal(ks[0], (BATCH, SEQ, D), jnp.float32),
        'c': jax.random.normal(ks[1], (BATCH, D), jnp.float32),
        'positions': jnp.arange(SEQ, dtype=jnp.int32)[None, :] + offset,
        'w_mod': dense_init(ks[3], (L, D, N_MOD * D), D, 0.5),
        'b_mod': 0.02 * jax.random.normal(ks[4], (L, N_MOD * D), jnp.float32),
        'g_mix_norm': gain(ks[5], (L, D)),
        'w_in': dense_init(ks[6], (L, D, IN_COLS), D),
        'b_forget': 1.0 + 3.0 * jax.random.uniform(ks[7], (L, FOX_HEADS), jnp.float32),
        'g_q_lat': gain(ks[8], (L, MLA_Q_LORA)),
        'w_q_up': dense_init(ks[9], (L, MLA_Q_LORA, MLA_HEADS * MLA_QK), MLA_Q_LORA),
        'g_kv_lat': gain(ks[10], (L, MLA_KV_LORA)),
        'w_kv_up': dense_init(ks[11], (L, MLA_KV_LORA, MLA_HEADS * (MLA_NOPE + MLA_V)), MLA_KV_LORA),
        'w_o_mla': dense_init(ks[12], (L, MLA_WIDTH, D), MLA_WIDTH),
        'w_o_fox': dense_init(ks[13], (L, FOX_WIDTH, D), FOX_WIDTH),
        'w_out': dense_init(ks[14], (L, D, D), D),
        'g_ffn_norm': gain(ks[15], (L, D)),
        'w_router': dense_init(ks[16], (L, D, E), D),
        'b_router': 0.01 * jax.random.normal(ks[17], (L, E), jnp.float32),
        'w_exp_gate': dense_init(ks[18], (L, E, D, D_EXPERT), D),
        'w_exp_up': dense_init(ks[19], (L, E, D, D_EXPERT), D),
        'w_exp_down': dense_init(ks[20], (L, E, D_EXPERT, D), D_EXPERT),
        'w_sh_gate': dense_init(ks[21], (L, D, D_SHARED), D),
        'w_sh_up': dense_init(ks[22], (L, D, D_SHARED), D),
        'w_sh_down': dense_init(ks[23], (L, D_SHARED, D), D_SHARED),
        'g_final': gain(ks[24], (D,)),
    }


def reference(x, c, positions, w_mod, b_mod, g_mix_norm, w_in, b_forget, g_q_lat, w_q_up, g_kv_lat,
              w_kv_up, w_o_mla, w_o_fox, w_out, g_ffn_norm, w_router, b_router, w_exp_gate, w_exp_up,
              w_exp_down, w_sh_gate, w_sh_up, w_sh_down, g_final):
    cond = jax.nn.silu(c)
    for l in range(DEPTH):
        mod = cond @ w_mod[l] + b_mod[l]
        shift_m, scale_m, gate_m, shift_f, scale_f, gate_f = [m[:, None, :] for m in jnp.split(mod, N_MOD, axis=-1)]
        h = rms_norm(x, g_mix_norm[l]) * (1.0 + scale_m) + shift_m
        x = x + gate_m * hybrid_mixer(h, positions, w_in[l], b_forget[l], g_q_lat[l], w_q_up[l], g_kv_lat[l],
                                      w_kv_up[l], w_o_mla[l], w_o_fox[l], w_out[l])
        h = rms_norm(x, g_ffn_norm[l]) * (1.0 + scale_f) + shift_f
        x = x + gate_f * moe_ffn(h, w_router[l], b_router[l], w_exp_gate[l], w_exp_up[l], w_exp_down[l],
                                 w_sh_gate[l], w_sh_up[l], w_sh_down[l])
    return rms_norm(x, g_final)
```

```python
import functools
import math

import numpy as np
import jax
import jax.numpy as jnp
from jax import lax
from jax.experimental import pallas as pl
from jax.experimental.pallas import tpu as pltpu

F32 = jnp.float32
BF16 = jnp.bfloat16
I32 = jnp.int32
U32 = jnp.uint32

D_MODEL = 1024
MLA_HEADS = 8
MLA_Q_LORA = 256
MLA_KV_LORA = 128
MLA_NOPE = 64
MLA_ROPE = 32
MLA_V = 64
MLA_QK = MLA_NOPE + MLA_ROPE
ROPE_THETA = 10000.0
FOX_HEADS = 8
FOX_HEAD_DIM = 64
FOX_WIDTH = FOX_HEADS * FOX_HEAD_DIM
N_HEADS = MLA_HEADS + FOX_HEADS
N_EXPERTS = 64
N_GROUPS = 8
GROUP_SIZE = N_EXPERTS // N_GROUPS
TOPK_GROUPS = 4
TOP_K = 6
D_EXPERT = 256
ROUTED_SCALE = 2.5
N_MOD = 6
NORM_EPS = 1e-6
NEG_INF = -1e30

LANES = 128
HEAD_PAD = 128
ROW_BLOCK = 256
SLOTS = 8
VMEM_LIMIT = 56 * 1024 * 1024

FQ_COL = FOX_HEAD_DIM
FK_COL = FOX_HEAD_DIM + 3


def _cparams(sem, vmem=VMEM_LIMIT):
    return pltpu.CompilerParams(dimension_semantics=sem, vmem_limit_bytes=vmem)


def _const_spec(shape):
    nd = len(shape)
    return pl.BlockSpec(shape, lambda *_: (0,) * nd)


def _rms(x):
    return x * lax.rsqrt(jnp.mean(x * x, axis=-1, keepdims=True) + NORM_EPS)


def _split3(x):
    hi = x.astype(BF16)
    r = x - hi.astype(F32)
    mid = r.astype(BF16)
    lo = (r - mid.astype(F32)).astype(BF16)
    return hi, mid, lo


def _dot(a, b):
    return jnp.dot(a, b, preferred_element_type=F32)


def _modulated_norm(x, gain, mod, shift_row, scale_row):
    shift = mod[shift_row:shift_row + 1, :]
    scale = mod[scale_row:scale_row + 1, :]
    return _rms(x) * gain * (1.0 + scale) + shift


def _mod_body(c_ref, w_ref, b_ref, o_ref):
    c = c_ref[...]
    cond = c * jax.nn.sigmoid(c)
    o_ref[...] = _dot(cond.astype(BF16), w_ref[...].astype(BF16)) + b_ref[...]


def _mod(c8, w_mod, b_mod):
    n = w_mod.shape[1]
    tn = D_MODEL
    return pl.pallas_call(
        _mod_body,
        out_shape=jax.ShapeDtypeStruct((8, n), F32),
        grid=(n // tn,),
        in_specs=[_const_spec((8, D_MODEL)),
                  pl.BlockSpec((D_MODEL, tn), lambda j: (0, j)),
                  pl.BlockSpec((1, tn), lambda j: (0, j))],
        out_specs=pl.BlockSpec((8, tn), lambda j: (0, j)),
        compiler_params=_cparams(("parallel",)),
        name="mod",
    )(c8, w_mod, b_mod)


def _decay_body(tiles_per_seq, x_ref, mod_ref, g_ref, whi_ref, wlo_ref, b_ref, o_ref, carry_ref):
    i = pl.program_id(0)

    @pl.when(i % tiles_per_seq == 0)
    def _():
        carry_ref[...] = jnp.zeros_like(carry_ref)

    h = _modulated_norm(x_ref[...], g_ref[...], mod_ref[0], 0, 1)
    hhi = h.astype(BF16)
    hlo = (h - hhi.astype(F32)).astype(BF16)
    z = _dot(hhi, whi_ref[...]) + _dot(hhi, wlo_ref[...]) + _dot(hlo, whi_ref[...]) + b_ref[...]
    logf = jnp.minimum(z, 0.0) - jnp.log1p(jnp.exp(-jnp.abs(z)))
    tm = logf.shape[0]
    row = lax.broadcasted_iota(I32, (tm, tm), 0)
    col = lax.broadcasted_iota(I32, (tm, tm), 1)
    tri = jnp.where(col <= row, 1.0, 0.0).astype(BF16)
    hi, mid, lo = _split3(logf)
    cum = _dot(tri, hi) + _dot(tri, mid) + _dot(tri, lo) + carry_ref[...]
    o_ref[...] = cum
    carry_ref[...] = cum[tm - 1:tm, :]


def _fox_decay(x2, mod8, g_mix, wfl_hi, wfl_lo, bfl, seq, tm):
    t = x2.shape[0]
    tps = seq // tm
    return pl.pallas_call(
        functools.partial(_decay_body, tps),
        out_shape=jax.ShapeDtypeStruct((t, LANES), F32),
        grid=(t // tm,),
        in_specs=[pl.BlockSpec((tm, D_MODEL), lambda i: (i, 0)),
                  pl.BlockSpec((1, 8, D_MODEL), lambda i: (i // tps, 0, 0)),
                  _const_spec((1, D_MODEL)),
                  _const_spec((D_MODEL, LANES)),
                  _const_spec((D_MODEL, LANES)),
                  _const_spec((1, LANES))],
        out_specs=pl.BlockSpec((tm, LANES), lambda i: (i, 0)),
        scratch_shapes=[pltpu.VMEM((1, LANES), F32)],
        compiler_params=_cparams(("arbitrary",)),
        name="fox_decay",
    )(x2, mod8, g_mix, wfl_hi, wfl_lo, bfl)


def _in_proj_body(x_ref, mod_ref, g_ref, f_ref, pos_ref, freq_ref,
                  wlat_ref, wkr_ref, gq_ref, gkv_ref, wqa_ref, wqb_ref, wka_ref, wva_ref,
                  wfq_ref, wfk_ref, wfv_ref, pall_ref, cq_ref, ck_ref, wg_ref,
                  q_ref, k_ref, v_ref, sg_ref):
    h = _modulated_norm(x_ref[...], g_ref[...], mod_ref[0], 0, 1)
    hb = h.astype(BF16)

    lat = _dot(hb, wlat_ref[...])
    qn = (_rms(lat[:, :MLA_Q_LORA]) * gq_ref[...]).astype(BF16)
    kvn = (_rms(lat[:, MLA_Q_LORA:]) * gkv_ref[...]).astype(BF16)
    ang = pos_ref[...].astype(F32) * freq_ref[...]
    cs = jnp.cos(ang)
    sn = jnp.sin(ang)
    kr = _dot(hb, wkr_ref[...])
    kpe = kr[:, :HEAD_PAD] * cs + kr[:, HEAD_PAD:] * sn
    qa = _dot(qn, wqa_ref[...])
    qb = _dot(qn, wqb_ref[...])
    ka = _dot(kvn, wka_ref[...])
    va = _dot(kvn, wva_ref[...])
    for hd in range(MLA_HEADS):
        sl = slice(hd * HEAD_PAD, (hd + 1) * HEAD_PAD)
        q_ref[0, hd] = (qa[:, sl] * cs + qb[:, sl] * sn).astype(BF16)
        k_ref[0, hd] = (ka[:, sl] + kpe).astype(BF16)
        v_ref[0, hd] = va[:, sl].astype(BF16)

    hi, mid, lo = _split3(f_ref[...])
    lane = lax.broadcasted_iota(I32, hi.shape, 1)
    f3 = jnp.where(lane < FOX_HEADS, hi, jnp.where(lane < 2 * FOX_HEADS, mid, lo))
    fp = _dot(f3, pall_ref[...])
    half = FOX_HEADS // 2 * 2 * HEAD_PAD
    for hp in range(FOX_HEADS // 2):
        sl = slice(hp * 2 * HEAD_PAD, (hp + 1) * 2 * HEAD_PAD)
        q2 = (_dot(hb, wfq_ref[hp]) + fp[:, sl] + cq_ref[...]).astype(BF16)
        k2 = (_dot(hb, wfk_ref[hp]) + fp[:, half + hp * 2 * HEAD_PAD:half + (hp + 1) * 2 * HEAD_PAD]
              + ck_ref[...]).astype(BF16)
        v2 = _dot(hb, wfv_ref[hp]).astype(BF16)
        for j in range(2):
            hd = MLA_HEADS + 2 * hp + j
            q_ref[0, hd] = q2[:, j * HEAD_PAD:(j + 1) * HEAD_PAD]
            k_ref[0, hd] = k2[:, j * HEAD_PAD:(j + 1) * HEAD_PAD]
            v_ref[0, hd] = v2[:, j * HEAD_PAD:(j + 1) * HEAD_PAD]

    sg_ref[...] = jax.nn.sigmoid(_dot(hb, wg_ref[...])).astype(BF16)


def _in_proj(x2, mod8, g_mix, fdec, pos, freq, w, batch, seq, tm):
    t = x2.shape[0]
    tps = seq // tm
    consts = [w["wlat"], w["wkr"], w["gq"], w["gkv"], w["wqa"], w["wqb"], w["wka"], w["wva"],
              w["wfq"], w["wfk"], w["wfv"], w["pall"], w["cq"], w["ck"], w["wg"]]
    head_shape = jax.ShapeDtypeStruct((batch, N_HEADS, seq, HEAD_PAD), BF16)
    head_spec = pl.BlockSpec((1, N_HEADS, tm, HEAD_PAD), lambda i: (i // tps, 0, i % tps, 0))
    return pl.pallas_call(
        _in_proj_body,
        out_shape=(head_shape, head_shape, head_shape,
                   jax.ShapeDtypeStruct((t, 2 * D_MODEL), BF16)),
        grid=(t // tm,),
        in_specs=[pl.BlockSpec((tm, D_MODEL), lambda i: (i, 0)),
                  pl.BlockSpec((1, 8, D_MODEL), lambda i: (i // tps, 0, 0)),
                  _const_spec((1, D_MODEL)),
                  pl.BlockSpec((tm, LANES), lambda i: (i, 0)),
                  pl.BlockSpec((tm, 1), lambda i: (i, 0)),
                  _const_spec((1, LANES))] + [_const_spec(a.shape) for a in consts],
        out_specs=(head_spec, head_spec, head_spec,
                   pl.BlockSpec((tm, 2 * D_MODEL), lambda i: (i, 0))),
        compiler_params=_cparams(("parallel",)),
        name="in_proj",
    )(x2, mod8, g_mix, fdec, pos, freq, *consts)


def _attn_body(bq, bk, q_ref, k_ref, v_ref, o_ref):
    qi = pl.program_id(2)
    row = lax.broadcasted_iota(I32, (bq, bk), 0)
    col = lax.broadcasted_iota(I32, (bq, bk), 1)
    causal = col <= row

    out = None
    for hh in range(2):
        q = q_ref[0, hh]

        def step(j, carry, masked, hh=hh, q=q):
            m, l, acc = carry
            start = pl.multiple_of(j * bk, bk)
            k = k_ref[0, hh, pl.ds(start, bk), :]
            v = v_ref[0, hh, pl.ds(start, bk), :]
            s = lax.dot_general(q, k, (((1,), (1,)), ((), ())), preferred_element_type=F32)
            if masked:
                s = jnp.where(causal, s, NEG_INF)
            m_new = jnp.maximum(m, jnp.max(s, axis=1, keepdims=True))
            alpha = jnp.exp(m - m_new)
            p = jnp.exp(s - m_new)
            l_new = alpha * l + jnp.sum(p, axis=1, keepdims=True)
            acc_new = alpha * acc + _dot(p.astype(BF16), v)
            return m_new, l_new, acc_new

        init = (jnp.full((bq, 1), NEG_INF, F32), jnp.zeros((bq, 1), F32),
                jnp.zeros((bq, HEAD_PAD), F32))
        carry = lax.fori_loop(0, qi, functools.partial(step, masked=False), init)
        _, l, acc = step(qi, carry, True)
        res = acc / l
        out = res if out is None else out + res
    o_ref[0] = out.astype(BF16)


def _attention(q_all, k_all, v_all, bq):
    batch, _, seq, _ = q_all.shape
    bk = bq
    return pl.pallas_call(
        functools.partial(_attn_body, bq, bk),
        out_shape=jax.ShapeDtypeStruct((batch, seq, N_HEADS * MLA_V), BF16),
        grid=(batch, N_HEADS // 2, seq // bq),
        in_specs=[pl.BlockSpec((1, 2, bq, HEAD_PAD), lambda b, hp, qi: (b, hp, qi, 0)),
                  pl.BlockSpec((1, 2, seq, HEAD_PAD), lambda b, hp, qi: (b, hp, 0, 0)),
                  pl.BlockSpec((1, 2, seq, HEAD_PAD), lambda b, hp, qi: (b, hp, 0, 0))],
        out_specs=pl.BlockSpec((1, bq, 2 * MLA_V), lambda b, hp, qi: (b, qi, hp)),
        compiler_params=_cparams(("parallel", "parallel", "arbitrary")),
        name="attention",
    )(q_all, k_all, v_all)


def _pack_halves(x):
    w = x.shape[1] // 2
    a = lax.bitcast_convert_type(x[:, :w].astype(BF16).astype(F32), U32)
    b = lax.bitcast_convert_type(x[:, w:].astype(BF16).astype(F32), U32)
    return a | (b >> 16)


def _unpack_halves(p):
    a = lax.bitcast_convert_type(p & jnp.uint32(0xFFFF0000), F32)
    b = lax.bitcast_convert_type(p << 16, F32)
    return a, b


def _out_proj_body(o_ref, sg_ref, x_ref, mod_ref, gffn_ref, womla_ref, wofox_ref, wout_ref,
                   wrhi_ref, wrlo_ref, x1_ref, hp_ref, lg_ref):
    o = o_ref[...]
    half = o.shape[1] // 2
    mo = _dot(o[:, :half], womla_ref[...])
    fo = _dot(o[:, half:], wofox_ref[...])
    sg = sg_ref[...]
    merged = sg[:, :D_MODEL].astype(F32) * mo + sg[:, D_MODEL:].astype(F32) * fo
    mix = _dot(merged.astype(BF16), wout_ref[...])
    mod = mod_ref[0]
    x1 = x_ref[...] + mod[2:3, :] * mix
    x1_ref[...] = x1
    h2 = _modulated_norm(x1, gffn_ref[...], mod, 3, 4)
    hhi = h2.astype(BF16)
    hlo = (h2 - hhi.astype(F32)).astype(BF16)
    lg_ref[...] = _dot(hhi, wrhi_ref[...]) + _dot(hhi, wrlo_ref[...]) + _dot(hlo, wrhi_ref[...])
    hp_ref[...] = _pack_halves(h2)


def _out_proj(o2, sg, x2, mod8, g_ffn, w, seq, tm):
    t = x2.shape[0]
    tps = seq // tm
    consts = [w["womla"], w["wofox"], w["wout"], w["wrhi"], w["wrlo"]]
    return pl.pallas_call(
        _out_proj_body,
        out_shape=(jax.ShapeDtypeStruct((t, D_MODEL), F32),
                   jax.ShapeDtypeStruct((t, D_MODEL // 2), U32),
                   jax.ShapeDtypeStruct((t, LANES), F32)),
        grid=(t // tm,),
        in_specs=[pl.BlockSpec((tm, D_MODEL), lambda i: (i, 0)),
                  pl.BlockSpec((tm, 2 * D_MODEL), lambda i: (i, 0)),
                  pl.BlockSpec((tm, D_MODEL), lambda i: (i, 0)),
                  pl.BlockSpec((1, 8, D_MODEL), lambda i: (i // tps, 0, 0)),
                  _const_spec((1, D_MODEL))] + [_const_spec(a.shape) for a in consts],
        out_specs=(pl.BlockSpec((tm, D_MODEL), lambda i: (i, 0)),
                   pl.BlockSpec((tm, D_MODEL // 2), lambda i: (i, 0)),
                   pl.BlockSpec((tm, LANES), lambda i: (i, 0))),
        compiler_params=_cparams(("parallel",)),
        name="out_proj",
    )(o2, sg, x2, mod8, g_ffn, *consts)


def _route_body(lt_ref, b_ref, eidx_ref, rank_ref, gate_ref, cnt_ref, carry_ref):
    i = pl.program_id(0)

    @pl.when(i == 0)
    def _():
        carry_ref[...] = jnp.zeros_like(carry_ref)

    s = jax.nn.sigmoid(lt_ref[...])
    c = s + b_ref[...]
    tr = s.shape[1]
    sub = lax.broadcasted_iota(I32, (GROUP_SIZE, tr), 0).astype(F32)

    gs = []
    for g in range(N_GROUPS):
        cg = c[g * GROUP_SIZE:(g + 1) * GROUP_SIZE, :]
        m1 = jnp.max(cg, axis=0, keepdims=True)
        i1 = jnp.min(jnp.where(cg == m1, sub, float(GROUP_SIZE)), axis=0, keepdims=True)
        m2 = jnp.max(jnp.where(sub == i1, NEG_INF, cg), axis=0, keepdims=True)
        gs.append(m1 + m2)

    masked = []
    for g in range(N_GROUPS):
        beats = jnp.zeros_like(gs[g])
        for o in range(N_GROUPS):
            if o == g:
                continue
            better = (gs[o] >= gs[g]) if o < g else (gs[o] > gs[g])
            beats = beats + jnp.where(better, 1.0, 0.0)
        keep = beats < float(TOPK_GROUPS)
        cg = c[g * GROUP_SIZE:(g + 1) * GROUP_SIZE, :]
        masked.append(jnp.where(keep, cg, NEG_INF))
    mc = jnp.concatenate(masked, axis=0)

    eio = lax.broadcasted_iota(I32, (N_EXPERTS, tr), 0).astype(F32)
    picks = []
    selected = jnp.zeros((N_EXPERTS, tr), F32)
    for _ in range(TOP_K):
        m = jnp.max(mc, axis=0, keepdims=True)
        idx = jnp.min(jnp.where(mc == m, eio, float(N_EXPERTS)), axis=0, keepdims=True)
        hit = eio == idx
        picks.append(idx)
        selected = jnp.where(hit, 1.0, selected)
        mc = jnp.where(hit, -3.0e38, mc)

    ssum = jnp.sum(selected * s, axis=0, keepdims=True)
    gate_full = selected * s / ssum * ROUTED_SCALE

    r_io = lax.broadcasted_iota(I32, (tr, tr), 0)
    c_io = lax.broadcasted_iota(I32, (tr, tr), 1)
    upper = jnp.where(r_io < c_io, 1.0, 0.0).astype(BF16)
    before = _dot(selected.astype(BF16), upper) + carry_ref[...]
    carry_new = carry_ref[...] + jnp.sum(selected, axis=1, keepdims=True)
    carry_ref[...] = carry_new
    cnt_ref[...] = jnp.broadcast_to(carry_new, cnt_ref.shape).astype(I32)

    for r in range(SLOTS):
        if r < TOP_K:
            hit = eio == picks[r]
            eidx_ref[r:r + 1, :] = picks[r].astype(I32)
            rank_ref[r:r + 1, :] = jnp.sum(jnp.where(hit, before, 0.0), axis=0, keepdims=True).astype(I32)
            gate_ref[r:r + 1, :] = jnp.sum(jnp.where(hit, gate_full, 0.0), axis=0, keepdims=True)
        else:
            eidx_ref[r:r + 1, :] = jnp.zeros((1, tr), I32)
            rank_ref[r:r + 1, :] = jnp.zeros((1, tr), I32)
            gate_ref[r:r + 1, :] = jnp.zeros((1, tr), F32)


def _route(logits_t, bias_col, tr):
    t = logits_t.shape[1]
    slot_i = jax.ShapeDtypeStruct((SLOTS, t), I32)
    slot_spec = pl.BlockSpec((SLOTS, tr), lambda i: (0, i))
    return pl.pallas_call(
        _route_body,
        out_shape=(slot_i, slot_i, jax.ShapeDtypeStruct((SLOTS, t), F32),
                   jax.ShapeDtypeStruct((N_EXPERTS, LANES), I32)),
        grid=(t // tr,),
        in_specs=[pl.BlockSpec((N_EXPERTS, tr), lambda i: (0, i)),
                  _const_spec((N_EXPERTS, 1))],
        out_specs=(slot_spec, slot_spec, slot_spec, _const_spec((N_EXPERTS, LANES))),
        scratch_shapes=[pltpu.VMEM((N_EXPERTS, 1), F32)],
        compiler_params=_cparams(("arbitrary",)),
        name="route",
    )(logits_t, bias_col)


def _dest_body(pstart_ref, eidx_ref, rank_ref, o_ref):
    e = eidx_ref[...]
    d = rank_ref[...]
    for j in range(N_EXPERTS):
        d = d + jnp.where(e == j, pstart_ref[j], 0)
    o_ref[...] = d


def _dest(pstart, eidx, rank, tr):
    t = eidx.shape[1]
    spec = pl.BlockSpec((SLOTS, tr), lambda i, ps: (0, i))
    return pl.pallas_call(
        _dest_body,
        out_shape=jax.ShapeDtypeStruct((SLOTS, t), I32),
        grid_spec=pltpu.PrefetchScalarGridSpec(
            num_scalar_prefetch=1, grid=(t // tr,), in_specs=[spec, spec], out_specs=spec),
        compiler_params=_cparams(("parallel",)),
        name="dest",
    )(pstart, eidx, rank)


def _dispatch_body(tp, pend_ref, dest_ref, h_ref, xs_ref, zero_ref, sem, zsem):
    i = pl.program_id(0)

    @pl.when(i == 0)
    def _():
        zero_ref[...] = jnp.zeros_like(zero_ref)

        def zero_copy(e):
            start = pl.multiple_of(pend_ref[e + 1] - ROW_BLOCK, ROW_BLOCK)
            return pltpu.make_async_copy(zero_ref, xs_ref.at[pl.ds(start, ROW_BLOCK), :], zsem)

        def issue(e, _):
            @pl.when(pend_ref[e + 1] > pend_ref[e])
            def _():
                zero_copy(e).start()
            return 0

        def drain(e, _):
            @pl.when(pend_ref[e + 1] > pend_ref[e])
            def _():
                zero_copy(e).wait()
            return 0

        lax.fori_loop(0, N_EXPERTS, issue, 0)
        lax.fori_loop(0, N_EXPERTS, drain, 0)

        def tail_copy(b):
            start = pl.multiple_of(b * ROW_BLOCK, ROW_BLOCK)
            return pltpu.make_async_copy(zero_ref, xs_ref.at[pl.ds(start, ROW_BLOCK), :], zsem)

        first_tail = pend_ref[N_EXPERTS] // ROW_BLOCK
        n_blocks = xs_ref.shape[0] // ROW_BLOCK
        lax.fori_loop(first_tail, n_blocks, lambda b, _: (tail_copy(b).start(), 0)[1], 0)
        lax.fori_loop(first_tail, n_blocks, lambda b, _: (tail_copy(b).wait(), 0)[1], 0)

    def issue_rows(t, _):
        for k in range(TOP_K):
            d = dest_ref[0, k, t]
            pltpu.make_async_copy(h_ref.at[pl.ds(t, 1), :], xs_ref.at[pl.ds(d, 1), :], sem).start()
        return 0

    lax.fori_loop(0, tp, issue_rows, 0)
    for _ in range(TOP_K):
        pltpu.make_async_copy(h_ref, xs_ref.at[pl.ds(0, tp), :], sem).wait()


def _dispatch(pend, dest3, hpk, n_rows, tp):
    t, w = hpk.shape
    return pl.pallas_call(
        functools.partial(_dispatch_body, tp),
        out_shape=jax.ShapeDtypeStruct((n_rows, w), U32),
        grid_spec=pltpu.PrefetchScalarGridSpec(
            num_scalar_prefetch=1, grid=(t // tp,),
            in_specs=[pl.BlockSpec((1, SLOTS, tp), lambda i, pe: (i, 0, 0), memory_space=pltpu.SMEM),
                      pl.BlockSpec((tp, w), lambda i, pe: (i, 0))],
            out_specs=pl.BlockSpec(memory_space=pl.ANY),
            scratch_shapes=[pltpu.VMEM((ROW_BLOCK, w), U32),
                            pltpu.SemaphoreType.DMA, pltpu.SemaphoreType.DMA]),
        compiler_params=_cparams(("arbitrary",)),
        name="dispatch",
    )(pend, dest3, hpk)


def _experts_body(be_ref, nu_ref, xs_ref, wg_ref, wu_ref, wd_ref, ys_ref):
    i = pl.program_id(0)

    @pl.when(i < nu_ref[0])
    def _():
        xa, xb = _unpack_halves(xs_ref[...])
        xa = xa.astype(BF16)
        xb = xb.astype(BF16)
        half = D_MODEL // 2
        g = _dot(xa, wg_ref[0, :half, :]) + _dot(xb, wg_ref[0, half:, :])
        u = _dot(xa, wu_ref[0, :half, :]) + _dot(xb, wu_ref[0, half:, :])
        hb = (g * jax.nn.sigmoid(g) * u).astype(BF16)
        ys_ref[...] = _pack_halves(_dot(hb, wd_ref[0]))

    @pl.when(i >= nu_ref[0])
    def _():
        ys_ref[...] = jnp.zeros_like(ys_ref)


def _experts(block_e, n_used, xs, wg, wu, wd):
    n_rows, w = xs.shape
    n_blocks = n_rows // ROW_BLOCK

    def row_map(i, be, nu):
        return (jnp.minimum(i, nu[0] - 1), 0)

    def w_map(i, be, nu):
        return (be[i], 0, 0)

    return pl.pallas_call(
        _experts_body,
        out_shape=jax.ShapeDtypeStruct((n_rows, w), U32),
        grid_spec=pltpu.PrefetchScalarGridSpec(
            num_scalar_prefetch=2, grid=(n_blocks,),
            in_specs=[pl.BlockSpec((ROW_BLOCK, w), row_map),
                      pl.BlockSpec((1, D_MODEL, D_EXPERT), w_map),
                      pl.BlockSpec((1, D_MODEL, D_EXPERT), w_map),
                      pl.BlockSpec((1, D_EXPERT, D_MODEL), w_map)],
            out_specs=pl.BlockSpec((ROW_BLOCK, w), lambda i, be, nu: (i, 0))),
        compiler_params=_cparams(("arbitrary",)),
        name="experts",
    )(block_e, n_used, xs, wg, wu, wd)


def _combine_body(tq, dest_ref, ys_ref, gate_ref, hp_ref, x1_ref, mod_ref, wsg_ref, wsu_ref, wsd_ref,
                  gfin_ref, o_ref, ybuf_ref, sem):
    def issue_rows(t, _):
        for k in range(TOP_K):
            d = dest_ref[0, k, t]
            pltpu.make_async_copy(ys_ref.at[pl.ds(d, 1), :], ybuf_ref.at[k, pl.ds(t, 1), :], sem).start()
        return 0

    lax.fori_loop(0, tq, issue_rows, 0)

    ha, hb = _unpack_halves(hp_ref[...])
    ha = ha.astype(BF16)
    hb = hb.astype(BF16)
    half = D_MODEL // 2
    g = _dot(ha, wsg_ref[:half, :]) + _dot(hb, wsg_ref[half:, :])
    u = _dot(ha, wsu_ref[:half, :]) + _dot(hb, wsu_ref[half:, :])
    shared = _dot((g * jax.nn.sigmoid(g) * u).astype(BF16), wsd_ref[...])

    for k in range(TOP_K):
        pltpu.make_async_copy(ys_ref.at[pl.ds(0, tq), :], ybuf_ref.at[k], sem).wait()

    gate = gate_ref[...]
    ra = jnp.zeros((tq, half), F32)
    rb = jnp.zeros((tq, half), F32)
    for k in range(TOP_K):
        ya, yb = _unpack_halves(ybuf_ref[k])
        gk = gate[:, k:k + 1]
        ra = ra + gk * ya
        rb = rb + gk * yb
    moe = shared + jnp.concatenate([ra, rb], axis=1)
    mod = mod_ref[0]
    x2 = x1_ref[...] + mod[5:6, :] * moe
    o_ref[...] = _rms(x2) * gfin_ref[...]


def _combine(dest3, ys, gate_t, hpk, x1, mod8, w, g_final, seq, tq):
    t, wd = hpk.shape
    tps = seq // tq
    return pl.pallas_call(
        functools.partial(_combine_body, tq),
        out_shape=jax.ShapeDtypeStruct((t, D_MODEL), F32),
        grid=(t // tq,),
        in_specs=[pl.BlockSpec((1, SLOTS, tq), lambda i: (i, 0, 0), memory_space=pltpu.SMEM),
                  pl.BlockSpec(memory_space=pl.ANY),
                  pl.BlockSpec((tq, SLOTS), lambda i: (i, 0)),
                  pl.BlockSpec((tq, wd), lambda i: (i, 0)),
                  pl.BlockSpec((tq, D_MODEL), lambda i: (i, 0)),
                  pl.BlockSpec((1, 8, D_MODEL), lambda i: (i // tps, 0, 0)),
                  _const_spec(w["wsg"].shape), _const_spec(w["wsu"].shape), _const_spec(w["wsd"].shape),
                  _const_spec((1, D_MODEL))],
        out_specs=pl.BlockSpec((tq, D_MODEL), lambda i: (i, 0)),
        scratch_shapes=[pltpu.VMEM((TOP_K, tq, wd), U32), pltpu.SemaphoreType.DMA],
        compiler_params=_cparams(("arbitrary",)),
        name="combine",
    )(dest3, ys, gate_t, hpk, x1, mod8, w["wsg"], w["wsu"], w["wsd"], g_final)


def _prep_weights(w_in, b_forget, g_q_lat, w_q_up, g_kv_lat, w_kv_up, w_o_mla, w_o_fox, w_out,
                  w_router, w_sh_gate, w_sh_up, w_sh_down):
    o_q, o_kv, o_kr = 0, MLA_Q_LORA, MLA_Q_LORA + MLA_KV_LORA
    o_fq = o_kr + MLA_ROPE
    o_fk, o_fv = o_fq + FOX_WIDTH, o_fq + 2 * FOX_WIDTH
    o_fl = o_fq + 3 * FOX_WIDTH
    o_ga = o_fl + FOX_HEADS
    o_gb = o_ga + D_MODEL
    w = {}
    w["wlat"] = w_in[:, o_q:o_kr].astype(BF16)
    half = MLA_ROPE // 2

    def rope_pair(cols):
        x1, x2 = cols[..., :half], cols[..., half:]
        z = jnp.zeros(cols.shape[:-1] + (MLA_NOPE,), cols.dtype)
        zt = jnp.zeros(cols.shape[:-1] + (HEAD_PAD - MLA_QK,), cols.dtype)
        plain = jnp.concatenate([z, x1, x2, zt], axis=-1)
        rot = jnp.concatenate([z, -x2, x1, zt], axis=-1)
        return plain, rot

    kr_plain, kr_rot = rope_pair(w_in[:, o_kr:o_fq])
    w["wkr"] = jnp.concatenate([kr_plain, kr_rot], axis=1).astype(BF16)
    w["gq"] = g_q_lat.reshape(1, -1)
    w["gkv"] = g_kv_lat.reshape(1, -1)

    scale_a = 1.0 / math.sqrt(MLA_QK)
    wq = (w_q_up * scale_a).reshape(MLA_Q_LORA, MLA_HEADS, MLA_QK)
    q_plain, q_rot = rope_pair(wq[..., MLA_NOPE:])
    nope = jnp.concatenate([wq[..., :MLA_NOPE], jnp.zeros((MLA_Q_LORA, MLA_HEADS, HEAD_PAD - MLA_NOPE), F32)], -1)
    w["wqa"] = (nope + q_plain).reshape(MLA_Q_LORA, -1).astype(BF16)
    w["wqb"] = q_rot.reshape(MLA_Q_LORA, -1).astype(BF16)

    wkv = w_kv_up.reshape(MLA_KV_LORA, MLA_HEADS, MLA_NOPE + MLA_V)
    zpad = jnp.zeros((MLA_KV_LORA, MLA_HEADS, HEAD_PAD - MLA_NOPE), F32)
    w["wka"] = jnp.concatenate([wkv[..., :MLA_NOPE], zpad], -1).reshape(MLA_KV_LORA, -1).astype(BF16)

    def pair_pad(v):
        z = jnp.zeros_like(v)
        even = jnp.concatenate([v, z], -1)
        odd = jnp.concatenate([z, v], -1)
        sel = (jnp.arange(v.shape[1]) % 2 == 0)[None, :, None]
        return jnp.where(sel, even, odd)

    w["wva"] = pair_pad(wkv[..., MLA_NOPE:]).reshape(MLA_KV_LORA, -1).astype(BF16)

    def fox_heads(cols, scale):
        v = (cols * scale).reshape(D_MODEL, FOX_HEADS, FOX_HEAD_DIM)
        v = jnp.concatenate([v, jnp.zeros((D_MODEL, FOX_HEADS, HEAD_PAD - FOX_HEAD_DIM), F32)], -1)
        return v.reshape(D_MODEL, FOX_HEADS // 2, 2 * HEAD_PAD).transpose(1, 0, 2).astype(BF16)

    w["wfq"] = fox_heads(w_in[:, o_fq:o_fk], 1.0 / math.sqrt(FOX_HEAD_DIM))
    w["wfk"] = fox_heads(w_in[:, o_fk:o_fv], 1.0)
    wfv = pair_pad(w_in[:, o_fv:o_fl].reshape(D_MODEL, FOX_HEADS, FOX_HEAD_DIM))
    w["wfv"] = wfv.reshape(D_MODEL, FOX_HEADS // 2, 2 * HEAD_PAD).transpose(1, 0, 2).astype(BF16)

    wfl = jnp.concatenate([w_in[:, o_fl:o_ga]] * 3 + [jnp.zeros((D_MODEL, LANES - 3 * FOX_HEADS), F32)], 1)
    w["wfl_hi"] = wfl.astype(BF16)
    w["wfl_lo"] = (wfl - w["wfl_hi"].astype(F32)).astype(BF16)
    w["bfl"] = jnp.concatenate([b_forget] * 3 + [jnp.zeros((LANES - 3 * FOX_HEADS,), F32)]).reshape(1, LANES)

    pall = np.zeros((LANES, 2 * FOX_HEADS * HEAD_PAD), np.float32)
    cq = np.zeros((1, 2 * HEAD_PAD), np.float32)
    ck = np.zeros((1, 2 * HEAD_PAD), np.float32)
    for hd in range(FOX_HEADS):
        for term in range(3):
            pall[term * FOX_HEADS + hd, hd * HEAD_PAD + FQ_COL + term] = 1.0
            pall[term * FOX_HEADS + hd, FOX_HEADS * HEAD_PAD + hd * HEAD_PAD + FK_COL + term] = -1.0
    for j in range(2):
        cq[0, j * HEAD_PAD + FK_COL:j * HEAD_PAD + FK_COL + 3] = 1.0
        ck[0, j * HEAD_PAD + FQ_COL:j * HEAD_PAD + FQ_COL + 3] = 1.0
    w["pall"] = jnp.asarray(pall, BF16)
    w["cq"] = jnp.asarray(cq)
    w["ck"] = jnp.asarray(ck)

    w["wg"] = w_in[:, o_ga:o_gb + D_MODEL].astype(BF16)
    w["womla"] = w_o_mla.astype(BF16)
    w["wofox"] = w_o_fox.astype(BF16)
    w["wout"] = w_out.astype(BF16)
    wr = jnp.concatenate([w_router, jnp.zeros((D_MODEL, LANES - N_EXPERTS), F32)], 1)
    w["wrhi"] = wr.astype(BF16)
    w["wrlo"] = (wr - w["wrhi"].astype(F32)).astype(BF16)
    w["wsg"] = w_sh_gate.astype(BF16)
    w["wsu"] = w_sh_up.astype(BF16)
    w["wsd"] = w_sh_down.astype(BF16)
    return w


def _rope_freq_row():
    half = MLA_ROPE // 2
    inv = np.power(ROPE_THETA, -np.arange(half, dtype=np.float32) / half).astype(np.float32)
    row = np.zeros((1, LANES), np.float32)
    row[0, MLA_NOPE:MLA_NOPE + half] = inv
    row[0, MLA_NOPE + half:MLA_NOPE + 2 * half] = inv
    return jnp.asarray(row)


def kernel(x, c, positions, w_mod, b_mod, g_mix_norm, w_in, b_forget, g_q_lat, w_q_up, g_kv_lat, w_kv_up,
           w_o_mla, w_o_fox, w_out, g_ffn_norm, w_router, b_router, w_exp_gate, w_exp_up, w_exp_down,
           w_sh_gate, w_sh_up, w_sh_down, g_final):
    batch, seq, d = x.shape
    assert d == D_MODEL and w_mod.shape[0] == 1
    t = batch * seq
    tm = min(512, seq)
    bq = min(512, seq)
    tr = min(512, t)
    tp = 128
    assert seq % tm == 0 and seq % bq == 0 and t % tr == 0 and t % tp == 0 and batch <= 8

    w = _prep_weights(w_in[0], b_forget[0], g_q_lat[0], w_q_up[0], g_kv_lat[0], w_kv_up[0], w_o_mla[0],
                      w_o_fox[0], w_out[0], w_router[0], w_sh_gate[0], w_sh_up[0], w_sh_down[0])

    c8 = jnp.zeros((8, D_MODEL), F32).at[:batch].set(c)
    mod = _mod(c8, w_mod[0], b_mod)
    mod8 = jnp.zeros((batch, 8, D_MODEL), F32).at[:, :N_MOD].set(mod[:batch].reshape(batch, N_MOD, D_MODEL))

    x2 = x.reshape(t, D_MODEL)
    fdec = _fox_decay(x2, mod8, g_mix_norm, w["wfl_hi"], w["wfl_lo"], w["bfl"], seq, tm)
    q_all, k_all, v_all, sg = _in_proj(x2, mod8, g_mix_norm, fdec, positions.reshape(t, 1),
                                       _rope_freq_row(), w, batch, seq, tm)
    o = _attention(q_all, k_all, v_all, bq)
    x1, hpk, logits = _out_proj(o.reshape(t, D_MODEL), sg, x2, mod8, g_ffn_norm, w, seq, tm)

    eidx, rank, gate, counts = _route(logits[:, :N_EXPERTS].T, b_router.reshape(N_EXPERTS, 1), tr)

    cnt = counts[:, 0]
    padded = (cnt + ROW_BLOCK - 1) // ROW_BLOCK * ROW_BLOCK
    pend = jnp.cumsum(padded)
    pstart = pend - padded
    n_blocks = t * TOP_K // ROW_BLOCK + N_EXPERTS
    n_rows = n_blocks * ROW_BLOCK
    block_e = jnp.minimum(
        jnp.searchsorted(pend, jnp.arange(n_blocks, dtype=I32) * ROW_BLOCK, side="right"),
        N_EXPERTS - 1).astype(I32)
    n_used = (pend[-1:] // ROW_BLOCK).astype(I32)
    pend_ext = jnp.concatenate([jnp.zeros((1,), I32), pend.astype(I32)])

    dest = _dest(pstart.astype(I32), eidx, rank, tr)
    dest3 = dest.reshape(SLOTS, t // tp, tp).transpose(1, 0, 2)
    xs = _dispatch(pend_ext, dest3, hpk, n_rows, tp)
    ys = _experts(block_e, n_used, xs, w_exp_gate[0].astype(BF16), w_exp_up[0].astype(BF16),
                  w_exp_down[0].astype(BF16))
    out = _combine(dest3, ys, gate.T, hpk, x1, mod8, w, g_final.reshape(1, D_MODEL), seq, tp)
    return out.reshape(batch, seq, D_MODEL)
```

```python
import functools
import math

import numpy as np
import jax
import jax.numpy as jnp
from jax import lax
from jax.experimental import pallas as pl
from jax.experimental.pallas import tpu as pltpu

F32 = jnp.float32
BF16 = jnp.bfloat16
I32 = jnp.int32
U32 = jnp.uint32

D_MODEL = 1024
MLA_HEADS = 8
MLA_Q_LORA = 256
MLA_KV_LORA = 128
MLA_NOPE = 64
MLA_ROPE = 32
MLA_V = 64
MLA_QK = MLA_NOPE + MLA_ROPE
ROPE_THETA = 10000.0
FOX_HEADS = 8
FOX_HEAD_DIM = 64
FOX_WIDTH = FOX_HEADS * FOX_HEAD_DIM
N_HEADS = MLA_HEADS + FOX_HEADS
N_EXPERTS = 64
N_GROUPS = 8
GROUP_SIZE = N_EXPERTS // N_GROUPS
TOPK_GROUPS = 4
TOP_K = 6
D_EXPERT = 256
ROUTED_SCALE = 2.5
N_MOD = 6
NORM_EPS = 1e-6
NEG_INF = -1e30

LANES = 128
HEAD_PAD = 128
ROW_BLOCK = 256
SLOTS = 8
VMEM_LIMIT = 56 * 1024 * 1024

FQ_COL = FOX_HEAD_DIM
FK_COL = FOX_HEAD_DIM + 3


def _cparams(sem, vmem=VMEM_LIMIT):
    return pltpu.CompilerParams(dimension_semantics=sem, vmem_limit_bytes=vmem)


def _const_spec(shape):
    nd = len(shape)
    return pl.BlockSpec(shape, lambda *_: (0,) * nd)


def _rms(x):
    return x * lax.rsqrt(jnp.mean(x * x, axis=-1, keepdims=True) + NORM_EPS)


def _split3(x):
    hi = x.astype(BF16)
    r = x - hi.astype(F32)
    mid = r.astype(BF16)
    lo = (r - mid.astype(F32)).astype(BF16)
    return hi, mid, lo


def _dot(a, b):
    return jnp.dot(a, b, preferred_element_type=F32)


def _modulated_norm(x, gain, mod, shift_row, scale_row):
    shift = mod[shift_row:shift_row + 1, :]
    scale = mod[scale_row:scale_row + 1, :]
    return _rms(x) * gain * (1.0 + scale) + shift


def _mod_body(c_ref, w_ref, b_ref, o_ref):
    c = c_ref[...]
    cond = c * jax.nn.sigmoid(c)
    o_ref[...] = _dot(cond.astype(BF16), w_ref[...].astype(BF16)) + b_ref[...]


def _mod(c8, w_mod, b_mod):
    n = w_mod.shape[1]
    tn = D_MODEL
    return pl.pallas_call(
        _mod_body,
        out_shape=jax.ShapeDtypeStruct((8, n), F32),
        grid=(n // tn,),
        in_specs=[_const_spec((8, D_MODEL)),
                  pl.BlockSpec((D_MODEL, tn), lambda j: (0, j)),
                  pl.BlockSpec((1, tn), lambda j: (0, j))],
        out_specs=pl.BlockSpec((8, tn), lambda j: (0, j)),
        compiler_params=_cparams(("parallel",)),
        name="mod",
    )(c8, w_mod, b_mod)


def _decay_body(tiles_per_seq, x_ref, mod_ref, g_ref, whi_ref, wlo_ref, b_ref, o_ref, carry_ref):
    i = pl.program_id(0)

    @pl.when(i % tiles_per_seq == 0)
    def _():
        carry_ref[...] = jnp.zeros_like(carry_ref)

    h = _modulated_norm(x_ref[...], g_ref[...], mod_ref[0], 0, 1)
    hhi = h.astype(BF16)
    hlo = (h - hhi.astype(F32)).astype(BF16)
    z = _dot(hhi, whi_ref[...]) + _dot(hhi, wlo_ref[...]) + _dot(hlo, whi_ref[...]) + b_ref[...]
    logf = jnp.minimum(z, 0.0) - jnp.log1p(jnp.exp(-jnp.abs(z)))
    tm = logf.shape[0]
    row = lax.broadcasted_iota(I32, (tm, tm), 0)
    col = lax.broadcasted_iota(I32, (tm, tm), 1)
    tri = jnp.where(col <= row, 1.0, 0.0).astype(BF16)
    hi, mid, lo = _split3(logf)
    cum = _dot(tri, hi) + _dot(tri, mid) + _dot(tri, lo) + carry_ref[...]
    o_ref[...] = cum
    carry_ref[...] = cum[tm - 1:tm, :]


def _fox_decay(x2, mod8, g_mix, wfl_hi, wfl_lo, bfl, seq, tm):
    t = x2.shape[0]
    tps = seq // tm
    return pl.pallas_call(
        functools.partial(_decay_body, tps),
        out_shape=jax.ShapeDtypeStruct((t, LANES), F32),
        grid=(t // tm,),
        in_specs=[pl.BlockSpec((tm, D_MODEL), lambda i: (i, 0)),
                  pl.BlockSpec((1, 8, D_MODEL), lambda i: (i // tps, 0, 0)),
                  _const_spec((1, D_MODEL)),
                  _const_spec((D_MODEL, LANES)),
                  _const_spec((D_MODEL, LANES)),
                  _const_spec((1, LANES))],
        out_specs=pl.BlockSpec((tm, LANES), lambda i: (i, 0)),
        scratch_shapes=[pltpu.VMEM((1, LANES), F32)],
        compiler_params=_cparams(("arbitrary",)),
        name="fox_decay",
    )(x2, mod8, g_mix, wfl_hi, wfl_lo, bfl)


def _in_proj_body(x_ref, mod_ref, g_ref, f_ref, pos_ref, freq_ref,
                  wlat_ref, wkr_ref, gq_ref, gkv_ref, wqa_ref, wqb_ref, wka_ref, wva_ref,
                  wfq_ref, wfk_ref, wfv_ref, pall_ref, cq_ref, ck_ref, wg_ref,
                  q_ref, k_ref, v_ref, sg_ref):
    h = _modulated_norm(x_ref[...], g_ref[...], mod_ref[0], 0, 1)
    hb = h.astype(BF16)

    lat = _dot(hb, wlat_ref[...])
    qn = (_rms(lat[:, :MLA_Q_LORA]) * gq_ref[...]).astype(BF16)
    kvn = (_rms(lat[:, MLA_Q_LORA:]) * gkv_ref[...]).astype(BF16)
    ang = pos_ref[...].astype(F32) * freq_ref[...]
    cs = jnp.cos(ang)
    sn = jnp.sin(ang)
    kr = _dot(hb, wkr_ref[...])
    kpe = kr[:, :HEAD_PAD] * cs + kr[:, HEAD_PAD:] * sn
    qa = _dot(qn, wqa_ref[...])
    qb = _dot(qn, wqb_ref[...])
    ka = _dot(kvn, wka_ref[...])
    va = _dot(kvn, wva_ref[...])
    for hd in range(MLA_HEADS):
        sl = slice(hd * HEAD_PAD, (hd + 1) * HEAD_PAD)
        q_ref[0, hd] = (qa[:, sl] * cs + qb[:, sl] * sn).astype(BF16)
        k_ref[0, hd] = (ka[:, sl] + kpe).astype(BF16)
        v_ref[0, hd] = va[:, sl].astype(BF16)

    hi, mid, lo = _split3(f_ref[...])
    lane = lax.broadcasted_iota(I32, hi.shape, 1)
    f3 = jnp.where(lane < FOX_HEADS, hi, jnp.where(lane < 2 * FOX_HEADS, mid, lo))
    fp = _dot(f3, pall_ref[...])
    half = FOX_HEADS // 2 * 2 * HEAD_PAD
    for hp in range(FOX_HEADS // 2):
        sl = slice(hp * 2 * HEAD_PAD, (hp + 1) * 2 * HEAD_PAD)
        q2 = (_dot(hb, wfq_ref[hp]) + fp[:, sl] + cq_ref[...]).astype(BF16)
        k2 = (_dot(hb, wfk_ref[hp]) + fp[:, half + hp * 2 * HEAD_PAD:half + (hp + 1) * 2 * HEAD_PAD]
              + ck_ref[...]).astype(BF16)
        v2 = _dot(hb, wfv_ref[hp]).astype(BF16)
        for j in range(2):
            hd = MLA_HEADS + 2 * hp + j
            q_ref[0, hd] = q2[:, j * HEAD_PAD:(j + 1) * HEAD_PAD]
            k_ref[0, hd] = k2[:, j * HEAD_PAD:(j + 1) * HEAD_PAD]
            v_ref[0, hd] = v2[:, j * HEAD_PAD:(j + 1) * HEAD_PAD]

    sg_ref[...] = jax.nn.sigmoid(_dot(hb, wg_ref[...])).astype(BF16)


def _in_proj(x2, mod8, g_mix, fdec, pos, freq, w, batch, seq, tm):
    t = x2.shape[0]
    tps = seq // tm
    consts = [w["wlat"], w["wkr"], w["gq"], w["gkv"], w["wqa"], w["wqb"], w["wka"], w["wva"],
              w["wfq"], w["wfk"], w["wfv"], w["pall"], w["cq"], w["ck"], w["wg"]]
    head_shape = jax.ShapeDtypeStruct((batch, N_HEADS, seq, HEAD_PAD), BF16)
    head_spec = pl.BlockSpec((1, N_HEADS, tm, HEAD_PAD), lambda i: (i // tps, 0, i % tps, 0))
    return pl.pallas_call(
        _in_proj_body,
        out_shape=(head_shape, head_shape, head_shape,
                   jax.ShapeDtypeStruct((t, 2 * D_MODEL), BF16)),
        grid=(t // tm,),
        in_specs=[pl.BlockSpec((tm, D_MODEL), lambda i: (i, 0)),
                  pl.BlockSpec((1, 8, D_MODEL), lambda i: (i // tps, 0, 0)),
                  _const_spec((1, D_MODEL)),
                  pl.BlockSpec((tm, LANES), lambda i: (i, 0)),
                  pl.BlockSpec((tm, 1), lambda i: (i, 0)),
                  _const_spec((1, LANES))] + [_const_spec(a.shape) for a in consts],
        out_specs=(head_spec, head_spec, head_spec,
                   pl.BlockSpec((tm, 2 * D_MODEL), lambda i: (i, 0))),
        compiler_params=_cparams(("parallel",)),
        name="in_proj",
    )(x2, mod8, g_mix, fdec, pos, freq, *consts)


def _attn_body(bq, sq, q_ref, k_ref, v_ref, o_ref, s_scr, p_scr, acc_scr):
    qi = pl.program_id(2)
    bk = sq
    n_heads = q_ref.shape[1]
    n_sub = bq // sq
    assert n_sub % 2 == 0
    chains = [(hh, u) for hh in range(n_heads) for u in range(n_sub)]
    n_chains = len(chains)
    n_main = qi * n_sub

    def chunk_start(j):
        return pl.multiple_of(jnp.maximum(j, 0) * bk, bk)

    def scores(c, j, par):
        hh, u = chains[c]
        k = k_ref[0, hh, pl.ds(chunk_start(j), bk), :]
        q = q_ref[0, hh, u * sq:(u + 1) * sq, :]
        s = lax.dot_general(k, q, (((1,), (1,)), ((), ())), preferred_element_type=F32)
        s_scr[par, c] = s
        return jnp.max(s, axis=0, keepdims=True)

    def values(c, j, par, alpha):
        v = v_ref[0, chains[c][0], pl.ds(chunk_start(j), bk), :]
        pv = lax.dot_general(v, p_scr[par, c], (((0,), (0,)), ((), ())), preferred_element_type=F32)
        acc_scr[c] = alpha * acc_scr[c] + pv

    def softmax(c, par, m, l, smax, masked):
        s = s_scr[par, c]
        if masked:
            key = lax.broadcasted_iota(I32, (bk, sq), 0)
            qry = lax.broadcasted_iota(I32, (bk, sq), 1)
            s = jnp.where(key <= qry, s, NEG_INF)
            smax = jnp.max(s, axis=0, keepdims=True)
        m_new = jnp.maximum(m, smax)
        alpha = jnp.exp(m - m_new)
        p = jnp.exp(s - m_new)
        p_scr[par, c] = p.astype(BF16)
        return m_new, alpha * l + jnp.sum(p, axis=0, keepdims=True), alpha

    def stage(j, par, state, active, has_next, masked_of):
        nxt = {c: scores(c, j + 1, 1 - par) if has_next(c) else state[c][2] for c in active}
        for c in active:
            values(c, j - 1, 1 - par, state[c][3])
        out = list(state)
        for c in active:
            m, l, smax, _ = state[c]
            m_new, l_new, alpha = softmax(c, par, m, l, smax, masked_of(c))
            out[c] = (m_new, l_new, nxt[c], alpha)
        return out

    p_scr[1] = jnp.zeros(p_scr.shape[1:], BF16)
    acc_scr[...] = jnp.zeros_like(acc_scr)
    state = [(jnp.full((1, sq), NEG_INF, F32), jnp.zeros((1, sq), F32), scores(c, 0, 0), jnp.ones((1, sq), F32))
             for c in range(n_chains)]

    def step(jj, st):
        for par in range(2):
            st = stage(2 * jj + par, par, list(st), range(n_chains), lambda c: True, lambda c: False)
        return tuple(st)

    state = list(lax.fori_loop(0, n_main // 2, step, tuple(state)))

    for t in range(n_sub):
        active = [c for c, (_, u) in enumerate(chains) if u >= t]
        state = stage(n_main + t, t % 2, state, active, lambda c, t=t: chains[c][1] > t,
                      lambda c, t=t: chains[c][1] == t)

    out = None
    for hh in range(n_heads):
        parts = []
        for u in range(n_sub):
            c = hh * n_sub + u
            values(c, n_main + u, u % 2, state[c][3])
            parts.append(acc_scr[c] / state[c][1])
        res = jnp.concatenate(parts, axis=1)
        out = res if out is None else out + res
    o_ref[0] = out.T.astype(BF16)


def _attention(q_all, k_all, v_all, bq):
    batch, _, seq, _ = q_all.shape
    sq = min(256, bq)
    n_chains = 2 * (bq // sq)
    return pl.pallas_call(
        functools.partial(_attn_body, bq, sq),
        out_shape=jax.ShapeDtypeStruct((batch, seq, N_HEADS * MLA_V), BF16),
        grid=(batch, N_HEADS // 2, seq // bq),
        in_specs=[pl.BlockSpec((1, 2, bq, HEAD_PAD), lambda b, hp, qi: (b, hp, qi, 0)),
                  pl.BlockSpec((1, 2, seq, HEAD_PAD), lambda b, hp, qi: (b, hp, 0, 0)),
                  pl.BlockSpec((1, 2, seq, HEAD_PAD), lambda b, hp, qi: (b, hp, 0, 0))],
        out_specs=pl.BlockSpec((1, bq, 2 * MLA_V), lambda b, hp, qi: (b, qi, hp)),
        scratch_shapes=[pltpu.VMEM((2, n_chains, sq, sq), F32),
                        pltpu.VMEM((2, n_chains, sq, sq), BF16),
                        pltpu.VMEM((n_chains, HEAD_PAD, sq), F32)],
        compiler_params=_cparams(("parallel", "parallel", "arbitrary")),
        name="attention",
    )(q_all, k_all, v_all)


def _pack_halves(x):
    w = x.shape[1] // 2
    a = lax.bitcast_convert_type(x[:, :w].astype(BF16).astype(F32), U32)
    b = lax.bitcast_convert_type(x[:, w:].astype(BF16).astype(F32), U32)
    return a | (b >> 16)


def _unpack_halves(p):
    a = lax.bitcast_convert_type(p & jnp.uint32(0xFFFF0000), F32)
    b = lax.bitcast_convert_type(p << 16, F32)
    return a, b


def _out_proj_body(o_ref, sg_ref, x_ref, mod_ref, gffn_ref, womla_ref, wofox_ref, wout_ref,
                   wrhi_ref, wrlo_ref, x1_ref, hp_ref, lg_ref):
    o = o_ref[...]
    half = o.shape[1] // 2
    mo = _dot(o[:, :half], womla_ref[...])
    fo = _dot(o[:, half:], wofox_ref[...])
    sg = sg_ref[...]
    merged = sg[:, :D_MODEL].astype(F32) * mo + sg[:, D_MODEL:].astype(F32) * fo
    mix = _dot(merged.astype(BF16), wout_ref[...])
    mod = mod_ref[0]
    x1 = x_ref[...] + mod[2:3, :] * mix
    x1_ref[...] = x1
    h2 = _modulated_norm(x1, gffn_ref[...], mod, 3, 4)
    hhi = h2.astype(BF16)
    hlo = (h2 - hhi.astype(F32)).astype(BF16)
    lg_ref[...] = _dot(hhi, wrhi_ref[...]) + _dot(hhi, wrlo_ref[...]) + _dot(hlo, wrhi_ref[...])
    hp_ref[...] = _pack_halves(h2)


def _out_proj(o2, sg, x2, mod8, g_ffn, w, seq, tm):
    t = x2.shape[0]
    tps = seq // tm
    consts = [w["womla"], w["wofox"], w["wout"], w["wrhi"], w["wrlo"]]
    return pl.pallas_call(
        _out_proj_body,
        out_shape=(jax.ShapeDtypeStruct((t, D_MODEL), F32),
                   jax.ShapeDtypeStruct((t, D_MODEL // 2), U32),
                   jax.ShapeDtypeStruct((t, LANES), F32)),
        grid=(t // tm,),
        in_specs=[pl.BlockSpec((tm, D_MODEL), lambda i: (i, 0)),
                  pl.BlockSpec((tm, 2 * D_MODEL), lambda i: (i, 0)),
                  pl.BlockSpec((tm, D_MODEL), lambda i: (i, 0)),
                  pl.BlockSpec((1, 8, D_MODEL), lambda i: (i // tps, 0, 0)),
                  _const_spec((1, D_MODEL))] + [_const_spec(a.shape) for a in consts],
        out_specs=(pl.BlockSpec((tm, D_MODEL), lambda i: (i, 0)),
                   pl.BlockSpec((tm, D_MODEL // 2), lambda i: (i, 0)),
                   pl.BlockSpec((tm, LANES), lambda i: (i, 0))),
        compiler_params=_cparams(("parallel",)),
        name="out_proj",
    )(o2, sg, x2, mod8, g_ffn, *consts)


def _route_body(lt_ref, b_ref, eidx_ref, rank_ref, gate_ref, cnt_ref, carry_ref):
    i = pl.program_id(0)

    @pl.when(i == 0)
    def _():
        carry_ref[...] = jnp.zeros_like(carry_ref)

    s = jax.nn.sigmoid(lt_ref[...])
    c = s + b_ref[...]
    tr = s.shape[1]
    sub = lax.broadcasted_iota(I32, (GROUP_SIZE, tr), 0).astype(F32)

    gs = []
    for g in range(N_GROUPS):
        cg = c[g * GROUP_SIZE:(g + 1) * GROUP_SIZE, :]
        m1 = jnp.max(cg, axis=0, keepdims=True)
        i1 = jnp.min(jnp.where(cg == m1, sub, float(GROUP_SIZE)), axis=0, keepdims=True)
        m2 = jnp.max(jnp.where(sub == i1, NEG_INF, cg), axis=0, keepdims=True)
        gs.append(m1 + m2)

    masked = []
    for g in range(N_GROUPS):
        beats = jnp.zeros_like(gs[g])
        for o in range(N_GROUPS):
            if o == g:
                continue
            better = (gs[o] >= gs[g]) if o < g else (gs[o] > gs[g])
            beats = beats + jnp.where(better, 1.0, 0.0)
        keep = beats < float(TOPK_GROUPS)
        cg = c[g * GROUP_SIZE:(g + 1) * GROUP_SIZE, :]
        masked.append(jnp.where(keep, cg, NEG_INF))
    mc = jnp.concatenate(masked, axis=0)

    eio = lax.broadcasted_iota(I32, (N_EXPERTS, tr), 0).astype(F32)
    picks = []
    selected = jnp.zeros((N_EXPERTS, tr), F32)
    for _ in range(TOP_K):
        m = jnp.max(mc, axis=0, keepdims=True)
        idx = jnp.min(jnp.where(mc == m, eio, float(N_EXPERTS)), axis=0, keepdims=True)
        hit = eio == idx
        picks.append(idx)
        selected = jnp.where(hit, 1.0, selected)
        mc = jnp.where(hit, -3.0e38, mc)

    ssum = jnp.sum(selected * s, axis=0, keepdims=True)
    gate_full = selected * s / ssum * ROUTED_SCALE

    r_io = lax.broadcasted_iota(I32, (tr, tr), 0)
    c_io = lax.broadcasted_iota(I32, (tr, tr), 1)
    upper = jnp.where(r_io < c_io, 1.0, 0.0).astype(BF16)
    before = _dot(selected.astype(BF16), upper) + carry_ref[...]
    carry_new = carry_ref[...] + jnp.sum(selected, axis=1, keepdims=True)
    carry_ref[...] = carry_new
    cnt_ref[...] = jnp.broadcast_to(carry_new, cnt_ref.shape).astype(I32)

    for r in range(SLOTS):
        if r < TOP_K:
            hit = eio == picks[r]
            eidx_ref[r:r + 1, :] = picks[r].astype(I32)
            rank_ref[r:r + 1, :] = jnp.sum(jnp.where(hit, before, 0.0), axis=0, keepdims=True).astype(I32)
            gate_ref[r:r + 1, :] = jnp.sum(jnp.where(hit, gate_full, 0.0), axis=0, keepdims=True)
        else:
            eidx_ref[r:r + 1, :] = jnp.zeros((1, tr), I32)
            rank_ref[r:r + 1, :] = jnp.zeros((1, tr), I32)
            gate_ref[r:r + 1, :] = jnp.zeros((1, tr), F32)


def _route(logits_t, bias_col, tr):
    t = logits_t.shape[1]
    slot_i = jax.ShapeDtypeStruct((SLOTS, t), I32)
    slot_spec = pl.BlockSpec((SLOTS, tr), lambda i: (0, i))
    return pl.pallas_call(
        _route_body,
        out_shape=(slot_i, slot_i, jax.ShapeDtypeStruct((SLOTS, t), F32),
                   jax.ShapeDtypeStruct((N_EXPERTS, LANES), I32)),
        grid=(t // tr,),
        in_specs=[pl.BlockSpec((N_EXPERTS, tr), lambda i: (0, i)),
                  _const_spec((N_EXPERTS, 1))],
        out_specs=(slot_spec, slot_spec, slot_spec, _const_spec((N_EXPERTS, LANES))),
        scratch_shapes=[pltpu.VMEM((N_EXPERTS, 1), F32)],
        compiler_params=_cparams(("arbitrary",)),
        name="route",
    )(logits_t, bias_col)


def _dest_body(pstart_ref, eidx_ref, rank_ref, o_ref):
    e = eidx_ref[...]
    d = rank_ref[...]
    for j in range(N_EXPERTS):
        d = d + jnp.where(e == j, pstart_ref[j], 0)
    o_ref[...] = d


def _dest(pstart, eidx, rank, tr):
    t = eidx.shape[1]
    spec = pl.BlockSpec((SLOTS, tr), lambda i, ps: (0, i))
    return pl.pallas_call(
        _dest_body,
        out_shape=jax.ShapeDtypeStruct((SLOTS, t), I32),
        grid_spec=pltpu.PrefetchScalarGridSpec(
            num_scalar_prefetch=1, grid=(t // tr,), in_specs=[spec, spec], out_specs=spec),
        compiler_params=_cparams(("parallel",)),
        name="dest",
    )(pstart, eidx, rank)


def _dispatch_body(tp, pend_ref, dest_ref, h_ref, xs_ref, zero_ref, sem, zsem):
    i = pl.program_id(0)

    @pl.when(i == 0)
    def _():
        zero_ref[...] = jnp.zeros_like(zero_ref)

        def zero_copy(e):
            start = pl.multiple_of(pend_ref[e + 1] - ROW_BLOCK, ROW_BLOCK)
            return pltpu.make_async_copy(zero_ref, xs_ref.at[pl.ds(start, ROW_BLOCK), :], zsem)

        def issue(e, _):
            @pl.when(pend_ref[e + 1] > pend_ref[e])
            def _():
                zero_copy(e).start()
            return 0

        def drain(e, _):
            @pl.when(pend_ref[e + 1] > pend_ref[e])
            def _():
                zero_copy(e).wait()
            return 0

        lax.fori_loop(0, N_EXPERTS, issue, 0)
        lax.fori_loop(0, N_EXPERTS, drain, 0)

        def tail_copy(b):
            start = pl.multiple_of(b * ROW_BLOCK, ROW_BLOCK)
            return pltpu.make_async_copy(zero_ref, xs_ref.at[pl.ds(start, ROW_BLOCK), :], zsem)

        first_tail = pend_ref[N_EXPERTS] // ROW_BLOCK
        n_blocks = xs_ref.shape[0] // ROW_BLOCK
        lax.fori_loop(first_tail, n_blocks, lambda b, _: (tail_copy(b).start(), 0)[1], 0)
        lax.fori_loop(first_tail, n_blocks, lambda b, _: (tail_copy(b).wait(), 0)[1], 0)

    def issue_rows(t, _):
        for k in range(TOP_K):
            d = dest_ref[0, k, t]
            pltpu.make_async_copy(h_ref.at[pl.ds(t, 1), :], xs_ref.at[pl.ds(d, 1), :], sem).start()
        return 0

    lax.fori_loop(0, tp, issue_rows, 0)
    for _ in range(TOP_K):
        pltpu.make_async_copy(h_ref, xs_ref.at[pl.ds(0, tp), :], sem).wait()


def _dispatch(pend, dest3, hpk, n_rows, tp):
    t, w = hpk.shape
    return pl.pallas_call(
        functools.partial(_dispatch_body, tp),
        out_shape=jax.ShapeDtypeStruct((n_rows, w), U32),
        grid_spec=pltpu.PrefetchScalarGridSpec(
            num_scalar_prefetch=1, grid=(t // tp,),
            in_specs=[pl.BlockSpec((1, SLOTS, tp), lambda i, pe: (i, 0, 0), memory_space=pltpu.SMEM),
                      pl.BlockSpec((tp, w), lambda i, pe: (i, 0))],
            out_specs=pl.BlockSpec(memory_space=pl.ANY),
            scratch_shapes=[pltpu.VMEM((ROW_BLOCK, w), U32),
                            pltpu.SemaphoreType.DMA, pltpu.SemaphoreType.DMA]),
        compiler_params=_cparams(("arbitrary",)),
        name="dispatch",
    )(pend, dest3, hpk)


def _experts_body(be_ref, nu_ref, xs_ref, wg_ref, wu_ref, wd_ref, ys_ref):
    i = pl.program_id(0)

    @pl.when(i < nu_ref[0])
    def _():
        xa, xb = _unpack_halves(xs_ref[...])
        xa = xa.astype(BF16)
        xb = xb.astype(BF16)
        half = D_MODEL // 2
        g = _dot(xa, wg_ref[0, :half, :]) + _dot(xb, wg_ref[0, half:, :])
        u = _dot(xa, wu_ref[0, :half, :]) + _dot(xb, wu_ref[0, half:, :])
        hb = (g * jax.nn.sigmoid(g) * u).astype(BF16)
        ys_ref[...] = _pack_halves(_dot(hb, wd_ref[0]))

    @pl.when(i >= nu_ref[0])
    def _():
        ys_ref[...] = jnp.zeros_like(ys_ref)


def _experts(block_e, n_used, xs, wg, wu, wd):
    n_rows, w = xs.shape
    n_blocks = n_rows // ROW_BLOCK

    def row_map(i, be, nu):
        return (jnp.minimum(i, nu[0] - 1), 0)

    def w_map(i, be, nu):
        return (be[i], 0, 0)

    return pl.pallas_call(
        _experts_body,
        out_shape=jax.ShapeDtypeStruct((n_rows, w), U32),
        grid_spec=pltpu.PrefetchScalarGridSpec(
            num_scalar_prefetch=2, grid=(n_blocks,),
            in_specs=[pl.BlockSpec((ROW_BLOCK, w), row_map),
                      pl.BlockSpec((1, D_MODEL, D_EXPERT), w_map),
                      pl.BlockSpec((1, D_MODEL, D_EXPERT), w_map),
                      pl.BlockSpec((1, D_EXPERT, D_MODEL), w_map)],
            out_specs=pl.BlockSpec((ROW_BLOCK, w), lambda i, be, nu: (i, 0))),
        compiler_params=_cparams(("arbitrary",)),
        name="experts",
    )(block_e, n_used, xs, wg, wu, wd)


def _combine_body(tq, dest_ref, ys_ref, gate_ref, hp_ref, x1_ref, mod_ref, wsg_ref, wsu_ref, wsd_ref,
                  gfin_ref, o_ref, ybuf_ref, sem):
    def issue_rows(t, _):
        for k in range(TOP_K):
            d = dest_ref[0, k, t]
            pltpu.make_async_copy(ys_ref.at[pl.ds(d, 1), :], ybuf_ref.at[k, pl.ds(t, 1), :], sem).start()
        return 0

    lax.fori_loop(0, tq, issue_rows, 0)

    ha, hb = _unpack_halves(hp_ref[...])
    ha = ha.astype(BF16)
    hb = hb.astype(BF16)
    half = D_MODEL // 2
    g = _dot(ha, wsg_ref[:half, :]) + _dot(hb, wsg_ref[half:, :])
    u = _dot(ha, wsu_ref[:half, :]) + _dot(hb, wsu_ref[half:, :])
    shared = _dot((g * jax.nn.sigmoid(g) * u).astype(BF16), wsd_ref[...])

    for k in range(TOP_K):
        pltpu.make_async_copy(ys_ref.at[pl.ds(0, tq), :], ybuf_ref.at[k], sem).wait()

    gate = gate_ref[...]
    ra = jnp.zeros((tq, half), F32)
    rb = jnp.zeros((tq, half), F32)
    for k in range(TOP_K):
        ya, yb = _unpack_halves(ybuf_ref[k])
        gk = gate[:, k:k + 1]
        ra = ra + gk * ya
        rb = rb + gk * yb
    moe = shared + jnp.concatenate([ra, rb], axis=1)
    mod = mod_ref[0]
    x2 = x1_ref[...] + mod[5:6, :] * moe
    o_ref[...] = _rms(x2) * gfin_ref[...]


def _combine(dest3, ys, gate_t, hpk, x1, mod8, w, g_final, seq, tq):
    t, wd = hpk.shape
    tps = seq // tq
    return pl.pallas_call(
        functools.partial(_combine_body, tq),
        out_shape=jax.ShapeDtypeStruct((t, D_MODEL), F32),
        grid=(t // tq,),
        in_specs=[pl.BlockSpec((1, SLOTS, tq), lambda i: (i, 0, 0), memory_space=pltpu.SMEM),
                  pl.BlockSpec(memory_space=pl.ANY),
                  pl.BlockSpec((tq, SLOTS), lambda i: (i, 0)),
                  pl.BlockSpec((tq, wd), lambda i: (i, 0)),
                  pl.BlockSpec((tq, D_MODEL), lambda i: (i, 0)),
                  pl.BlockSpec((1, 8, D_MODEL), lambda i: (i // tps, 0, 0)),
                  _const_spec(w["wsg"].shape), _const_spec(w["wsu"].shape), _const_spec(w["wsd"].shape),
                  _const_spec((1, D_MODEL))],
        out_specs=pl.BlockSpec((tq, D_MODEL), lambda i: (i, 0)),
        scratch_shapes=[pltpu.VMEM((TOP_K, tq, wd), U32), pltpu.SemaphoreType.DMA],
        compiler_params=_cparams(("arbitrary",)),
        name="combine",
    )(dest3, ys, gate_t, hpk, x1, mod8, w["wsg"], w["wsu"], w["wsd"], g_final)


def _prep_weights(w_in, b_forget, g_q_lat, w_q_up, g_kv_lat, w_kv_up, w_o_mla, w_o_fox, w_out,
                  w_router, w_sh_gate, w_sh_up, w_sh_down):
    o_q, o_kv, o_kr = 0, MLA_Q_LORA, MLA_Q_LORA + MLA_KV_LORA
    o_fq = o_kr + MLA_ROPE
    o_fk, o_fv = o_fq + FOX_WIDTH, o_fq + 2 * FOX_WIDTH
    o_fl = o_fq + 3 * FOX_WIDTH
    o_ga = o_fl + FOX_HEADS
    o_gb = o_ga + D_MODEL
    w = {}
    w["wlat"] = w_in[:, o_q:o_kr].astype(BF16)
    half = MLA_ROPE // 2

    def rope_pair(cols):
        x1, x2 = cols[..., :half], cols[..., half:]
        z = jnp.zeros(cols.shape[:-1] + (MLA_NOPE,), cols.dtype)
        zt = jnp.zeros(cols.shape[:-1] + (HEAD_PAD - MLA_QK,), cols.dtype)
        plain = jnp.concatenate([z, x1, x2, zt], axis=-1)
        rot = jnp.concatenate([z, -x2, x1, zt], axis=-1)
        return plain, rot

    kr_plain, kr_rot = rope_pair(w_in[:, o_kr:o_fq])
    w["wkr"] = jnp.concatenate([kr_plain, kr_rot], axis=1).astype(BF16)
    w["gq"] = g_q_lat.reshape(1, -1)
    w["gkv"] = g_kv_lat.reshape(1, -1)

    scale_a = 1.0 / math.sqrt(MLA_QK)
    wq = (w_q_up * scale_a).reshape(MLA_Q_LORA, MLA_HEADS, MLA_QK)
    q_plain, q_rot = rope_pair(wq[..., MLA_NOPE:])
    nope = jnp.concatenate([wq[..., :MLA_NOPE], jnp.zeros((MLA_Q_LORA, MLA_HEADS, HEAD_PAD - MLA_NOPE), F32)], -1)
    w["wqa"] = (nope + q_plain).reshape(MLA_Q_LORA, -1).astype(BF16)
    w["wqb"] = q_rot.reshape(MLA_Q_LORA, -1).astype(BF16)

    wkv = w_kv_up.reshape(MLA_KV_LORA, MLA_HEADS, MLA_NOPE + MLA_V)
    zpad = jnp.zeros((MLA_KV_LORA, MLA_HEADS, HEAD_PAD - MLA_NOPE), F32)
    w["wka"] = jnp.concatenate([wkv[..., :MLA_NOPE], zpad], -1).reshape(MLA_KV_LORA, -1).astype(BF16)

    def pair_pad(v):
        z = jnp.zeros_like(v)
        even = jnp.concatenate([v, z], -1)
        odd = jnp.concatenate([z, v], -1)
        sel = (jnp.arange(v.shape[1]) % 2 == 0)[None, :, None]
        return jnp.where(sel, even, odd)

    w["wva"] = pair_pad(wkv[..., MLA_NOPE:]).reshape(MLA_KV_LORA, -1).astype(BF16)

    def fox_heads(cols, scale):
        v = (cols * scale).reshape(D_MODEL, FOX_HEADS, FOX_HEAD_DIM)
        v = jnp.concatenate([v, jnp.zeros((D_MODEL, FOX_HEADS, HEAD_PAD - FOX_HEAD_DIM), F32)], -1)
        return v.reshape(D_MODEL, FOX_HEADS // 2, 2 * HEAD_PAD).transpose(1, 0, 2).astype(BF16)

    w["wfq"] = fox_heads(w_in[:, o_fq:o_fk], 1.0 / math.sqrt(FOX_HEAD_DIM))
    w["wfk"] = fox_heads(w_in[:, o_fk:o_fv], 1.0)
    wfv = pair_pad(w_in[:, o_fv:o_fl].reshape(D_MODEL, FOX_HEADS, FOX_HEAD_DIM))
    w["wfv"] = wfv.reshape(D_MODEL, FOX_HEADS // 2, 2 * HEAD_PAD).transpose(1, 0, 2).astype(BF16)

    wfl = jnp.concatenate([w_in[:, o_fl:o_ga]] * 3 + [jnp.zeros((D_MODEL, LANES - 3 * FOX_HEADS), F32)], 1)
    w["wfl_hi"] = wfl.astype(BF16)
    w["wfl_lo"] = (wfl - w["wfl_hi"].astype(F32)).astype(BF16)
    w["bfl"] = jnp.concatenate([b_forget] * 3 + [jnp.zeros((LANES - 3 * FOX_HEADS,), F32)]).reshape(1, LANES)

    pall = np.zeros((LANES, 2 * FOX_HEADS * HEAD_PAD), np.float32)
    cq = np.zeros((1, 2 * HEAD_PAD), np.float32)
    ck = np.zeros((1, 2 * HEAD_PAD), np.float32)
    for hd in range(FOX_HEADS):
        for term in range(3):
            pall[term * FOX_HEADS + hd, hd * HEAD_PAD + FQ_COL + term] = 1.0
            pall[term * FOX_HEADS + hd, FOX_HEADS * HEAD_PAD + hd * HEAD_PAD + FK_COL + term] = -1.0
    for j in range(2):
        cq[0, j * HEAD_PAD + FK_COL:j * HEAD_PAD + FK_COL + 3] = 1.0
        ck[0, j * HEAD_PAD + FQ_COL:j * HEAD_PAD + FQ_COL + 3] = 1.0
    w["pall"] = jnp.asarray(pall, BF16)
    w["cq"] = jnp.asarray(cq)
    w["ck"] = jnp.asarray(ck)

    w["wg"] = w_in[:, o_ga:o_gb + D_MODEL].astype(BF16)
    w["womla"] = w_o_mla.astype(BF16)
    w["wofox"] = w_o_fox.astype(BF16)
    w["wout"] = w_out.astype(BF16)
    wr = jnp.concatenate([w_router, jnp.zeros((D_MODEL, LANES - N_EXPERTS), F32)], 1)
    w["wrhi"] = wr.astype(BF16)
    w["wrlo"] = (wr - w["wrhi"].astype(F32)).astype(BF16)
    w["wsg"] = w_sh_gate.astype(BF16)
    w["wsu"] = w_sh_up.astype(BF16)
    w["wsd"] = w_sh_down.astype(BF16)
    return w


def _rope_freq_row():
    half = MLA_ROPE // 2
    inv = np.power(ROPE_THETA, -np.arange(half, dtype=np.float32) / half).astype(np.float32)
    row = np.zeros((1, LANES), np.float32)
    row[0, MLA_NOPE:MLA_NOPE + half] = inv
    row[0, MLA_NOPE + half:MLA_NOPE + 2 * half] = inv
    return jnp.asarray(row)


def kernel(x, c, positions, w_mod, b_mod, g_mix_norm, w_in, b_forget, g_q_lat, w_q_up, g_kv_lat, w_kv_up,
           w_o_mla, w_o_fox, w_out, g_ffn_norm, w_router, b_router, w_exp_gate, w_exp_up, w_exp_down,
           w_sh_gate, w_sh_up, w_sh_down, g_final):
    batch, seq, d = x.shape
    assert d == D_MODEL and w_mod.shape[0] == 1
    t = batch * seq
    tm = min(512, seq)
    bq = min(512, seq)
    tr = min(512, t)
    tp = 128
    assert seq % tm == 0 and seq % bq == 0 and t % tr == 0 and t % tp == 0 and batch <= 8

    w = _prep_weights(w_in[0], b_forget[0], g_q_lat[0], w_q_up[0], g_kv_lat[0], w_kv_up[0], w_o_mla[0],
                      w_o_fox[0], w_out[0], w_router[0], w_sh_gate[0], w_sh_up[0], w_sh_down[0])

    c8 = jnp.zeros((8, D_MODEL), F32).at[:batch].set(c)
    mod = _mod(c8, w_mod[0], b_mod)
    mod8 = jnp.zeros((batch, 8, D_MODEL), F32).at[:, :N_MOD].set(mod[:batch].reshape(batch, N_MOD, D_MODEL))

    x2 = x.reshape(t, D_MODEL)
    fdec = _fox_decay(x2, mod8, g_mix_norm, w["wfl_hi"], w["wfl_lo"], w["bfl"], seq, tm)
    q_all, k_all, v_all, sg = _in_proj(x2, mod8, g_mix_norm, fdec, positions.reshape(t, 1),
                                       _rope_freq_row(), w, batch, seq, tm)
    o = _attention(q_all, k_all, v_all, bq)
    x1, hpk, logits = _out_proj(o.reshape(t, D_MODEL), sg, x2, mod8, g_ffn_norm, w, seq, tm)

    eidx, rank, gate, counts = _route(logits[:, :N_EXPERTS].T, b_router.reshape(N_EXPERTS, 1), tr)

    cnt = counts[:, 0]
    padded = (cnt + ROW_BLOCK - 1) // ROW_BLOCK * ROW_BLOCK
    pend = jnp.cumsum(padded)
    pstart = pend - padded
    n_blocks = t * TOP_K // ROW_BLOCK + N_EXPERTS
    n_rows = n_blocks * ROW_BLOCK
    block_row = jnp.arange(n_blocks, dtype=I32) * ROW_BLOCK
    block_e = jnp.minimum(jnp.sum(pend[None, :] <= block_row[:, None], axis=1), N_EXPERTS - 1).astype(I32)
    n_used = (pend[-1:] // ROW_BLOCK).astype(I32)
    pend_ext = jnp.concatenate([jnp.zeros((1,), I32), pend.astype(I32)])

    dest = _dest(pstart.astype(I32), eidx, rank, tr)
    dest3 = dest.reshape(SLOTS, t // tp, tp).transpose(1, 0, 2)
    xs = _dispatch(pend_ext, dest3, hpk, n_rows, tp)
    ys = _experts(block_e, n_used, xs, w_exp_gate[0].astype(BF16), w_exp_up[0].astype(BF16),
                  w_exp_down[0].astype(BF16))
    out = _combine(dest3, ys, gate.T, hpk, x1, mod8, w, g_final.reshape(1, D_MODEL), seq, tp)
    return out.reshape(batch, seq, D_MODEL)
```

```python
import functools
import math

import numpy as np
import jax
import jax.numpy as jnp
from jax import lax
from jax.experimental import pallas as pl
from jax.experimental.pallas import tpu as pltpu
from jax.experimental.pallas import tpu_sc as plsc

F32 = jnp.float32
BF16 = jnp.bfloat16
I32 = jnp.int32
U32 = jnp.uint32

D_MODEL = 1024
MLA_HEADS = 8
MLA_Q_LORA = 256
MLA_KV_LORA = 128
MLA_NOPE = 64
MLA_ROPE = 32
MLA_V = 64
MLA_QK = MLA_NOPE + MLA_ROPE
ROPE_THETA = 10000.0
FOX_HEADS = 8
FOX_HEAD_DIM = 64
FOX_WIDTH = FOX_HEADS * FOX_HEAD_DIM
N_HEADS = MLA_HEADS + FOX_HEADS
N_EXPERTS = 64
N_GROUPS = 8
GROUP_SIZE = N_EXPERTS // N_GROUPS
TOPK_GROUPS = 4
TOP_K = 6
D_EXPERT = 256
ROUTED_SCALE = 2.5
N_MOD = 6
NORM_EPS = 1e-6
NEG_INF = -1e30

LANES = 128
HEAD_PAD = 128
ROW_BLOCK = 256
SLOTS = 8
SC_CORES = 2
SC_SUBCORES = 16
SC_WORKERS = SC_CORES * SC_SUBCORES
SC_CHUNK = 32
VMEM_LIMIT = 56 * 1024 * 1024

FQ_COL = FOX_HEAD_DIM
FK_COL = FOX_HEAD_DIM + 3


def _cparams(sem, vmem=VMEM_LIMIT):
    return pltpu.CompilerParams(dimension_semantics=sem, vmem_limit_bytes=vmem)


def _const_spec(shape):
    nd = len(shape)
    return pl.BlockSpec(shape, lambda *_: (0,) * nd)


def _rms(x):
    return x * lax.rsqrt(jnp.mean(x * x, axis=-1, keepdims=True) + NORM_EPS)


def _split3(x):
    hi = x.astype(BF16)
    r = x - hi.astype(F32)
    mid = r.astype(BF16)
    lo = (r - mid.astype(F32)).astype(BF16)
    return hi, mid, lo


def _dot(a, b):
    return jnp.dot(a, b, preferred_element_type=F32)


def _modulated_norm(x, gain, mod, shift_row, scale_row):
    shift = mod[shift_row:shift_row + 1, :]
    scale = mod[scale_row:scale_row + 1, :]
    return _rms(x) * gain * (1.0 + scale) + shift


def _mod_body(c_ref, w_ref, b_ref, o_ref):
    c = c_ref[...]
    cond = c * jax.nn.sigmoid(c)
    o_ref[...] = _dot(cond.astype(BF16), w_ref[...].astype(BF16)) + b_ref[...]


def _mod(c8, w_mod, b_mod):
    n = w_mod.shape[1]
    tn = D_MODEL
    return pl.pallas_call(
        _mod_body,
        out_shape=jax.ShapeDtypeStruct((8, n), F32),
        grid=(n // tn,),
        in_specs=[_const_spec((8, D_MODEL)),
                  pl.BlockSpec((D_MODEL, tn), lambda j: (0, j)),
                  pl.BlockSpec((1, tn), lambda j: (0, j))],
        out_specs=pl.BlockSpec((8, tn), lambda j: (0, j)),
        compiler_params=_cparams(("parallel",)),
        name="mod",
    )(c8, w_mod, b_mod)


def _decay_body(tiles_per_seq, x_ref, mod_ref, g_ref, whi_ref, wlo_ref, b_ref, o_ref, carry_ref):
    i = pl.program_id(0)

    @pl.when(i % tiles_per_seq == 0)
    def _():
        carry_ref[...] = jnp.zeros_like(carry_ref)

    h = _modulated_norm(x_ref[...], g_ref[...], mod_ref[0], 0, 1)
    hhi = h.astype(BF16)
    hlo = (h - hhi.astype(F32)).astype(BF16)
    z = _dot(hhi, whi_ref[...]) + _dot(hhi, wlo_ref[...]) + _dot(hlo, whi_ref[...]) + b_ref[...]
    logf = jnp.minimum(z, 0.0) - jnp.log1p(jnp.exp(-jnp.abs(z)))
    tm = logf.shape[0]
    row = lax.broadcasted_iota(I32, (tm, tm), 0)
    col = lax.broadcasted_iota(I32, (tm, tm), 1)
    tri = jnp.where(col <= row, 1.0, 0.0).astype(BF16)
    hi, mid, lo = _split3(logf)
    cum = _dot(tri, hi) + _dot(tri, mid) + _dot(tri, lo) + carry_ref[...]
    o_ref[...] = cum
    carry_ref[...] = cum[tm - 1:tm, :]


def _fox_decay(x2, mod8, g_mix, wfl_hi, wfl_lo, bfl, seq, tm):
    t = x2.shape[0]
    tps = seq // tm
    return pl.pallas_call(
        functools.partial(_decay_body, tps),
        out_shape=jax.ShapeDtypeStruct((t, LANES), F32),
        grid=(t // tm,),
        in_specs=[pl.BlockSpec((tm, D_MODEL), lambda i: (i, 0)),
                  pl.BlockSpec((1, 8, D_MODEL), lambda i: (i // tps, 0, 0)),
                  _const_spec((1, D_MODEL)),
                  _const_spec((D_MODEL, LANES)),
                  _const_spec((D_MODEL, LANES)),
                  _const_spec((1, LANES))],
        out_specs=pl.BlockSpec((tm, LANES), lambda i: (i, 0)),
        scratch_shapes=[pltpu.VMEM((1, LANES), F32)],
        compiler_params=_cparams(("arbitrary",)),
        name="fox_decay",
    )(x2, mod8, g_mix, wfl_hi, wfl_lo, bfl)


def _in_proj_body(x_ref, mod_ref, g_ref, f_ref, pos_ref, freq_ref,
                  wlat_ref, wkr_ref, gq_ref, gkv_ref, wqa_ref, wqb_ref, wka_ref, wva_ref,
                  wfq_ref, wfk_ref, wfv_ref, pall_ref, cq_ref, ck_ref, wg_ref,
                  q_ref, k_ref, v_ref, sg_ref):
    h = _modulated_norm(x_ref[...], g_ref[...], mod_ref[0], 0, 1)
    hb = h.astype(BF16)

    lat = _dot(hb, wlat_ref[...])
    qn = (_rms(lat[:, :MLA_Q_LORA]) * gq_ref[...]).astype(BF16)
    kvn = (_rms(lat[:, MLA_Q_LORA:]) * gkv_ref[...]).astype(BF16)
    ang = pos_ref[...].astype(F32) * freq_ref[...]
    cs = jnp.cos(ang)
    sn = jnp.sin(ang)
    kr = _dot(hb, wkr_ref[...])
    kpe = kr[:, :HEAD_PAD] * cs + kr[:, HEAD_PAD:] * sn
    qa = _dot(qn, wqa_ref[...])
    qb = _dot(qn, wqb_ref[...])
    ka = _dot(kvn, wka_ref[...])
    va = _dot(kvn, wva_ref[...])
    for hd in range(MLA_HEADS):
        sl = slice(hd * HEAD_PAD, (hd + 1) * HEAD_PAD)
        q_ref[0, hd] = (qa[:, sl] * cs + qb[:, sl] * sn).astype(BF16)
        k_ref[0, hd] = (ka[:, sl] + kpe).astype(BF16)
        v_ref[0, hd] = va[:, sl].astype(BF16)

    hi, mid, lo = _split3(f_ref[...])
    lane = lax.broadcasted_iota(I32, hi.shape, 1)
    f3 = jnp.where(lane < FOX_HEADS, hi, jnp.where(lane < 2 * FOX_HEADS, mid, lo))
    fp = _dot(f3, pall_ref[...])
    half = FOX_HEADS // 2 * 2 * HEAD_PAD
    for hp in range(FOX_HEADS // 2):
        sl = slice(hp * 2 * HEAD_PAD, (hp + 1) * 2 * HEAD_PAD)
        q2 = (_dot(hb, wfq_ref[hp]) + fp[:, sl] + cq_ref[...]).astype(BF16)
        k2 = (_dot(hb, wfk_ref[hp]) + fp[:, half + hp * 2 * HEAD_PAD:half + (hp + 1) * 2 * HEAD_PAD]
              + ck_ref[...]).astype(BF16)
        v2 = _dot(hb, wfv_ref[hp]).astype(BF16)
        for j in range(2):
            hd = MLA_HEADS + 2 * hp + j
            q_ref[0, hd] = q2[:, j * HEAD_PAD:(j + 1) * HEAD_PAD]
            k_ref[0, hd] = k2[:, j * HEAD_PAD:(j + 1) * HEAD_PAD]
            v_ref[0, hd] = v2[:, j * HEAD_PAD:(j + 1) * HEAD_PAD]

    sg_ref[...] = jax.nn.sigmoid(_dot(hb, wg_ref[...])).astype(BF16)


def _in_proj(x2, mod8, g_mix, fdec, pos, freq, w, batch, seq, tm):
    t = x2.shape[0]
    tps = seq // tm
    consts = [w["wlat"], w["wkr"], w["gq"], w["gkv"], w["wqa"], w["wqb"], w["wka"], w["wva"],
              w["wfq"], w["wfk"], w["wfv"], w["pall"], w["cq"], w["ck"], w["wg"]]
    head_shape = jax.ShapeDtypeStruct((batch, N_HEADS, seq, HEAD_PAD), BF16)
    head_spec = pl.BlockSpec((1, N_HEADS, tm, HEAD_PAD), lambda i: (i // tps, 0, i % tps, 0))
    return pl.pallas_call(
        _in_proj_body,
        out_shape=(head_shape, head_shape, head_shape,
                   jax.ShapeDtypeStruct((t, 2 * D_MODEL), BF16)),
        grid=(t // tm,),
        in_specs=[pl.BlockSpec((tm, D_MODEL), lambda i: (i, 0)),
                  pl.BlockSpec((1, 8, D_MODEL), lambda i: (i // tps, 0, 0)),
                  _const_spec((1, D_MODEL)),
                  pl.BlockSpec((tm, LANES), lambda i: (i, 0)),
                  pl.BlockSpec((tm, 1), lambda i: (i, 0)),
                  _const_spec((1, LANES))] + [_const_spec(a.shape) for a in consts],
        out_specs=(head_spec, head_spec, head_spec,
                   pl.BlockSpec((tm, 2 * D_MODEL), lambda i: (i, 0))),
        compiler_params=_cparams(("parallel",)),
        name="in_proj",
    )(x2, mod8, g_mix, fdec, pos, freq, *consts)


def _attn_body(bq, sq, q_ref, k_ref, v_ref, o_ref, s_scr, p_scr, acc_scr):
    qi = pl.program_id(2)
    bk = sq
    n_heads = q_ref.shape[1]
    n_sub = bq // sq
    assert n_sub % 2 == 0
    chains = [(hh, u) for hh in range(n_heads) for u in range(n_sub)]
    n_chains = len(chains)
    n_main = qi * n_sub

    def chunk_start(j):
        return pl.multiple_of(jnp.maximum(j, 0) * bk, bk)

    def scores(c, j, par):
        hh, u = chains[c]
        k = k_ref[0, hh, pl.ds(chunk_start(j), bk), :]
        q = q_ref[0, hh, u * sq:(u + 1) * sq, :]
        s = lax.dot_general(k, q, (((1,), (1,)), ((), ())), preferred_element_type=F32)
        s_scr[par, c] = s
        return jnp.max(s, axis=0, keepdims=True)

    def values(c, j, par, alpha):
        v = v_ref[0, chains[c][0], pl.ds(chunk_start(j), bk), :]
        pv = lax.dot_general(v, p_scr[par, c], (((0,), (0,)), ((), ())), preferred_element_type=F32)
        acc_scr[c] = alpha * acc_scr[c] + pv

    def softmax(c, par, m, l, smax, masked):
        s = s_scr[par, c]
        if masked:
            key = lax.broadcasted_iota(I32, (bk, sq), 0)
            qry = lax.broadcasted_iota(I32, (bk, sq), 1)
            s = jnp.where(key <= qry, s, NEG_INF)
            smax = jnp.max(s, axis=0, keepdims=True)
        m_new = jnp.maximum(m, smax)
        alpha = jnp.exp(m - m_new)
        p = jnp.exp(s - m_new)
        p_scr[par, c] = p.astype(BF16)
        return m_new, alpha * l + jnp.sum(p, axis=0, keepdims=True), alpha

    def stage(j, par, state, active, has_next, masked_of):
        nxt = {c: scores(c, j + 1, 1 - par) if has_next(c) else state[c][2] for c in active}
        for c in active:
            values(c, j - 1, 1 - par, state[c][3])
        out = list(state)
        for c in active:
            m, l, smax, _ = state[c]
            m_new, l_new, alpha = softmax(c, par, m, l, smax, masked_of(c))
            out[c] = (m_new, l_new, nxt[c], alpha)
        return out

    p_scr[1] = jnp.zeros(p_scr.shape[1:], BF16)
    acc_scr[...] = jnp.zeros_like(acc_scr)
    state = [(jnp.full((1, sq), NEG_INF, F32), jnp.zeros((1, sq), F32), scores(c, 0, 0), jnp.ones((1, sq), F32))
             for c in range(n_chains)]

    def step(jj, st):
        for par in range(2):
            st = stage(2 * jj + par, par, list(st), range(n_chains), lambda c: True, lambda c: False)
        return tuple(st)

    state = list(lax.fori_loop(0, n_main // 2, step, tuple(state)))

    for t in range(n_sub):
        active = [c for c, (_, u) in enumerate(chains) if u >= t]
        state = stage(n_main + t, t % 2, state, active, lambda c, t=t: chains[c][1] > t,
                      lambda c, t=t: chains[c][1] == t)

    out = None
    for hh in range(n_heads):
        parts = []
        for u in range(n_sub):
            c = hh * n_sub + u
            values(c, n_main + u, u % 2, state[c][3])
            parts.append(acc_scr[c] / state[c][1])
        res = jnp.concatenate(parts, axis=1)
        out = res if out is None else out + res
    o_ref[0] = out.T.astype(BF16)


def _attention(q_all, k_all, v_all, bq):
    batch, _, seq, _ = q_all.shape
    sq = min(256, bq)
    n_chains = 2 * (bq // sq)
    return pl.pallas_call(
        functools.partial(_attn_body, bq, sq),
        out_shape=jax.ShapeDtypeStruct((batch, seq, N_HEADS * MLA_V), BF16),
        grid=(batch, N_HEADS // 2, seq // bq),
        in_specs=[pl.BlockSpec((1, 2, bq, HEAD_PAD), lambda b, hp, qi: (b, hp, qi, 0)),
                  pl.BlockSpec((1, 2, seq, HEAD_PAD), lambda b, hp, qi: (b, hp, 0, 0)),
                  pl.BlockSpec((1, 2, seq, HEAD_PAD), lambda b, hp, qi: (b, hp, 0, 0))],
        out_specs=pl.BlockSpec((1, bq, 2 * MLA_V), lambda b, hp, qi: (b, qi, hp)),
        scratch_shapes=[pltpu.VMEM((2, n_chains, sq, sq), F32),
                        pltpu.VMEM((2, n_chains, sq, sq), BF16),
                        pltpu.VMEM((n_chains, HEAD_PAD, sq), F32)],
        compiler_params=_cparams(("parallel", "parallel", "arbitrary")),
        name="attention",
    )(q_all, k_all, v_all)


def _pack_halves(x):
    w = x.shape[1] // 2
    a = lax.bitcast_convert_type(x[:, :w].astype(BF16).astype(F32), U32)
    b = lax.bitcast_convert_type(x[:, w:].astype(BF16).astype(F32), U32)
    return a | (b >> 16)


def _unpack_halves(p):
    a = lax.bitcast_convert_type(p & jnp.uint32(0xFFFF0000), F32)
    b = lax.bitcast_convert_type(p << 16, F32)
    return a, b


def _out_proj_body(o_ref, sg_ref, x_ref, mod_ref, gffn_ref, womla_ref, wofox_ref, wout_ref,
                   wrhi_ref, wrlo_ref, x1_ref, hp_ref, lg_ref):
    o = o_ref[...]
    half = o.shape[1] // 2
    mo = _dot(o[:, :half], womla_ref[...])
    fo = _dot(o[:, half:], wofox_ref[...])
    sg = sg_ref[...]
    merged = sg[:, :D_MODEL].astype(F32) * mo + sg[:, D_MODEL:].astype(F32) * fo
    mix = _dot(merged.astype(BF16), wout_ref[...])
    mod = mod_ref[0]
    x1 = x_ref[...] + mod[2:3, :] * mix
    x1_ref[...] = x1
    h2 = _modulated_norm(x1, gffn_ref[...], mod, 3, 4)
    hhi = h2.astype(BF16)
    hlo = (h2 - hhi.astype(F32)).astype(BF16)
    lg_ref[...] = _dot(hhi, wrhi_ref[...]) + _dot(hhi, wrlo_ref[...]) + _dot(hlo, wrhi_ref[...])
    hp_ref[...] = _pack_halves(h2)


def _out_proj(o2, sg, x2, mod8, g_ffn, w, seq, tm):
    t = x2.shape[0]
    tps = seq // tm
    consts = [w["womla"], w["wofox"], w["wout"], w["wrhi"], w["wrlo"]]
    return pl.pallas_call(
        _out_proj_body,
        out_shape=(jax.ShapeDtypeStruct((t, D_MODEL), F32),
                   jax.ShapeDtypeStruct((t, D_MODEL // 2), U32),
                   jax.ShapeDtypeStruct((t, LANES), F32)),
        grid=(t // tm,),
        in_specs=[pl.BlockSpec((tm, D_MODEL), lambda i: (i, 0)),
                  pl.BlockSpec((tm, 2 * D_MODEL), lambda i: (i, 0)),
                  pl.BlockSpec((tm, D_MODEL), lambda i: (i, 0)),
                  pl.BlockSpec((1, 8, D_MODEL), lambda i: (i // tps, 0, 0)),
                  _const_spec((1, D_MODEL))] + [_const_spec(a.shape) for a in consts],
        out_specs=(pl.BlockSpec((tm, D_MODEL), lambda i: (i, 0)),
                   pl.BlockSpec((tm, D_MODEL // 2), lambda i: (i, 0)),
                   pl.BlockSpec((tm, LANES), lambda i: (i, 0))),
        compiler_params=_cparams(("parallel",)),
        name="out_proj",
    )(o2, sg, x2, mod8, g_ffn, *consts)


def _route_body(lt_ref, b_ref, eidx_ref, rank_ref, gate_ref, cnt_ref, carry_ref):
    i = pl.program_id(0)

    @pl.when(i == 0)
    def _():
        carry_ref[...] = jnp.zeros_like(carry_ref)

    s = jax.nn.sigmoid(lt_ref[...])
    c = s + b_ref[...]
    tr = s.shape[1]
    sub = lax.broadcasted_iota(I32, (GROUP_SIZE, tr), 0).astype(F32)

    gs = []
    for g in range(N_GROUPS):
        cg = c[g * GROUP_SIZE:(g + 1) * GROUP_SIZE, :]
        m1 = jnp.max(cg, axis=0, keepdims=True)
        i1 = jnp.min(jnp.where(cg == m1, sub, float(GROUP_SIZE)), axis=0, keepdims=True)
        m2 = jnp.max(jnp.where(sub == i1, NEG_INF, cg), axis=0, keepdims=True)
        gs.append(m1 + m2)

    masked = []
    for g in range(N_GROUPS):
        beats = jnp.zeros_like(gs[g])
        for o in range(N_GROUPS):
            if o == g:
                continue
            better = (gs[o] >= gs[g]) if o < g else (gs[o] > gs[g])
            beats = beats + jnp.where(better, 1.0, 0.0)
        keep = beats < float(TOPK_GROUPS)
        cg = c[g * GROUP_SIZE:(g + 1) * GROUP_SIZE, :]
        masked.append(jnp.where(keep, cg, NEG_INF))
    mc = jnp.concatenate(masked, axis=0)

    eio = lax.broadcasted_iota(I32, (N_EXPERTS, tr), 0).astype(F32)
    picks = []
    selected = jnp.zeros((N_EXPERTS, tr), F32)
    for _ in range(TOP_K):
        m = jnp.max(mc, axis=0, keepdims=True)
        idx = jnp.min(jnp.where(mc == m, eio, float(N_EXPERTS)), axis=0, keepdims=True)
        hit = eio == idx
        picks.append(idx)
        selected = jnp.where(hit, 1.0, selected)
        mc = jnp.where(hit, -3.0e38, mc)

    ssum = jnp.sum(selected * s, axis=0, keepdims=True)
    gate_full = selected * s / ssum * ROUTED_SCALE

    r_io = lax.broadcasted_iota(I32, (tr, tr), 0)
    c_io = lax.broadcasted_iota(I32, (tr, tr), 1)
    upper = jnp.where(r_io < c_io, 1.0, 0.0).astype(BF16)
    before = _dot(selected.astype(BF16), upper) + carry_ref[...]
    carry_new = carry_ref[...] + jnp.sum(selected, axis=1, keepdims=True)
    carry_ref[...] = carry_new
    cnt_ref[...] = jnp.broadcast_to(carry_new, cnt_ref.shape).astype(I32)

    for r in range(SLOTS):
        if r < TOP_K:
            hit = eio == picks[r]
            eidx_ref[r:r + 1, :] = picks[r].astype(I32)
            rank_ref[r:r + 1, :] = jnp.sum(jnp.where(hit, before, 0.0), axis=0, keepdims=True).astype(I32)
            gate_ref[r:r + 1, :] = jnp.sum(jnp.where(hit, gate_full, 0.0), axis=0, keepdims=True)
        else:
            eidx_ref[r:r + 1, :] = jnp.zeros((1, tr), I32)
            rank_ref[r:r + 1, :] = jnp.zeros((1, tr), I32)
            gate_ref[r:r + 1, :] = jnp.zeros((1, tr), F32)


def _route(logits_t, bias_col, tr):
    t = logits_t.shape[1]
    slot_i = jax.ShapeDtypeStruct((SLOTS, t), I32)
    slot_spec = pl.BlockSpec((SLOTS, tr), lambda i: (0, i))
    return pl.pallas_call(
        _route_body,
        out_shape=(slot_i, slot_i, jax.ShapeDtypeStruct((SLOTS, t), F32),
                   jax.ShapeDtypeStruct((N_EXPERTS, LANES), I32)),
        grid=(t // tr,),
        in_specs=[pl.BlockSpec((N_EXPERTS, tr), lambda i: (0, i)),
                  _const_spec((N_EXPERTS, 1))],
        out_specs=(slot_spec, slot_spec, slot_spec, _const_spec((N_EXPERTS, LANES))),
        scratch_shapes=[pltpu.VMEM((N_EXPERTS, 1), F32)],
        compiler_params=_cparams(("arbitrary",)),
        name="route",
    )(logits_t, bias_col)


def _dest_body(pstart_ref, eidx_ref, rank_ref, o_ref):
    e = eidx_ref[...]
    d = rank_ref[...]
    for j in range(N_EXPERTS):
        d = d + jnp.where(e == j, pstart_ref[j], 0)
    o_ref[...] = d


def _dest(pstart, eidx, rank, tr):
    t = eidx.shape[1]
    spec = pl.BlockSpec((SLOTS, tr), lambda i, ps: (0, i))
    return pl.pallas_call(
        _dest_body,
        out_shape=jax.ShapeDtypeStruct((SLOTS, t), I32),
        grid_spec=pltpu.PrefetchScalarGridSpec(
            num_scalar_prefetch=1, grid=(t // tr,), in_specs=[spec, spec], out_specs=spec),
        compiler_params=_cparams(("parallel",)),
        name="dest",
    )(pstart, eidx, rank)


def _sc_mesh():
    return plsc.VectorSubcoreMesh(core_axis_name="c", subcore_axis_name="s")


def _sc_worker_id():
    return lax.axis_index("s") * SC_CORES + lax.axis_index("c")


def _dispatch(hpk, dest3, n_rows):
    _, w = hpk.shape
    n_chunks, _, c = dest3.shape
    per_worker = n_chunks // SC_WORKERS

    def body(h_hbm, d_hbm, xs_hbm, idx_v, rows_v, sem):
        wid = _sc_worker_id()

        @pl.loop(0, per_worker)
        def _(i):
            ch = wid * per_worker + i
            pltpu.sync_copy(d_hbm.at[ch], idx_v)
            pltpu.sync_copy(h_hbm.at[pl.ds(ch * c, c)], rows_v)
            copies = [pltpu.async_copy(rows_v, xs_hbm.at[idx_v.at[k]], sem) for k in range(TOP_K)]
            for cp in copies:
                cp.wait()

    return pl.kernel(
        body, mesh=_sc_mesh(),
        out_type=jax.ShapeDtypeStruct((n_rows, w), U32),
        scratch_types=[pltpu.VMEM((SLOTS, c), I32), pltpu.VMEM((c, w), U32), pltpu.SemaphoreType.DMA],
        name="dispatch",
    )(hpk, dest3)


def _experts_body(be_ref, nv_ref, nu_ref, xs_ref, wg_ref, wu_ref, wd_ref, ys_ref):
    i = pl.program_id(0)

    @pl.when(i < nu_ref[0])
    def _():
        row = lax.broadcasted_iota(I32, xs_ref.shape, 0)
        xa, xb = _unpack_halves(jnp.where(row < nv_ref[i], xs_ref[...], jnp.uint32(0)))
        xa = xa.astype(BF16)
        xb = xb.astype(BF16)
        half = D_MODEL // 2
        g = _dot(xa, wg_ref[0, :half, :]) + _dot(xb, wg_ref[0, half:, :])
        u = _dot(xa, wu_ref[0, :half, :]) + _dot(xb, wu_ref[0, half:, :])
        hb = (g * jax.nn.sigmoid(g) * u).astype(BF16)
        ys_ref[...] = _pack_halves(_dot(hb, wd_ref[0]))

    @pl.when(i >= nu_ref[0])
    def _():
        ys_ref[...] = jnp.zeros_like(ys_ref)


def _experts(block_e, block_valid, n_used, xs, wg, wu, wd):
    n_rows, w = xs.shape
    n_blocks = n_rows // ROW_BLOCK

    def row_map(i, be, nv, nu):
        return (jnp.minimum(i, nu[0] - 1), 0)

    def w_map(i, be, nv, nu):
        return (be[i], 0, 0)

    return pl.pallas_call(
        _experts_body,
        out_shape=jax.ShapeDtypeStruct((n_rows, w), U32),
        grid_spec=pltpu.PrefetchScalarGridSpec(
            num_scalar_prefetch=3, grid=(n_blocks,),
            in_specs=[pl.BlockSpec((ROW_BLOCK, w), row_map),
                      pl.BlockSpec((1, D_MODEL, D_EXPERT), w_map),
                      pl.BlockSpec((1, D_MODEL, D_EXPERT), w_map),
                      pl.BlockSpec((1, D_EXPERT, D_MODEL), w_map)],
            out_specs=pl.BlockSpec((ROW_BLOCK, w), lambda i, be, nv, nu: (i, 0))),
        compiler_params=_cparams(("arbitrary",)),
        name="experts",
    )(block_e, block_valid, n_used, xs, wg, wu, wd)


def _gather_back(ys, dest3):
    _, w = ys.shape
    n_chunks, _, c = dest3.shape
    per_worker = n_chunks // SC_WORKERS

    def body(ys_hbm, d_hbm, yk_hbm, idx_v, rows_v, gsem, osem):
        wid = _sc_worker_id()

        @pl.loop(0, per_worker)
        def _(i):
            ch = wid * per_worker + i
            pltpu.sync_copy(d_hbm.at[ch], idx_v)
            gathers = [pltpu.async_copy(ys_hbm.at[idx_v.at[k]], rows_v.at[k], gsem) for k in range(TOP_K)]
            for g in gathers:
                g.wait()
            outs = [pltpu.async_copy(rows_v.at[k], yk_hbm.at[k, pl.ds(ch * c, c)], osem) for k in range(TOP_K)]
            for o in outs:
                o.wait()

    return pl.kernel(
        body, mesh=_sc_mesh(),
        out_type=jax.ShapeDtypeStruct((TOP_K, n_chunks * c, w), U32),
        scratch_types=[pltpu.VMEM((SLOTS, c), I32), pltpu.VMEM((TOP_K, c, w), U32),
                       pltpu.SemaphoreType.DMA, pltpu.SemaphoreType.DMA],
        name="gather_back",
    )(ys, dest3)


def _combine_body(yk_ref, gate_ref, hp_ref, x1_ref, mod_ref, wsg_ref, wsu_ref, wsd_ref, gfin_ref, o_ref):
    ha, hb = _unpack_halves(hp_ref[...])
    ha = ha.astype(BF16)
    hb = hb.astype(BF16)
    half = D_MODEL // 2
    g = _dot(ha, wsg_ref[:half, :]) + _dot(hb, wsg_ref[half:, :])
    u = _dot(ha, wsu_ref[:half, :]) + _dot(hb, wsu_ref[half:, :])
    shared = _dot((g * jax.nn.sigmoid(g) * u).astype(BF16), wsd_ref[...])

    gate = gate_ref[...]
    ra = jnp.zeros(ha.shape, F32)
    rb = jnp.zeros(ha.shape, F32)
    for k in range(TOP_K):
        ya, yb = _unpack_halves(yk_ref[k])
        gk = gate[:, k:k + 1]
        ra = ra + gk * ya
        rb = rb + gk * yb
    moe = shared + jnp.concatenate([ra, rb], axis=1)
    mod = mod_ref[0]
    x2 = x1_ref[...] + mod[5:6, :] * moe
    o_ref[...] = _rms(x2) * gfin_ref[...]


def _combine(yk, gate_t, hpk, x1, mod8, w, g_final, seq, tq):
    t, wd = hpk.shape
    tps = seq // tq
    return pl.pallas_call(
        _combine_body,
        out_shape=jax.ShapeDtypeStruct((t, D_MODEL), F32),
        grid=(t // tq,),
        in_specs=[pl.BlockSpec((TOP_K, tq, wd), lambda i: (0, i, 0)),
                  pl.BlockSpec((tq, SLOTS), lambda i: (i, 0)),
                  pl.BlockSpec((tq, wd), lambda i: (i, 0)),
                  pl.BlockSpec((tq, D_MODEL), lambda i: (i, 0)),
                  pl.BlockSpec((1, 8, D_MODEL), lambda i: (i // tps, 0, 0)),
                  _const_spec(w["wsg"].shape), _const_spec(w["wsu"].shape), _const_spec(w["wsd"].shape),
                  _const_spec((1, D_MODEL))],
        out_specs=pl.BlockSpec((tq, D_MODEL), lambda i: (i, 0)),
        compiler_params=_cparams(("parallel",)),
        name="combine",
    )(yk, gate_t, hpk, x1, mod8, w["wsg"], w["wsu"], w["wsd"], g_final)


def _prep_weights(w_in, b_forget, g_q_lat, w_q_up, g_kv_lat, w_kv_up, w_o_mla, w_o_fox, w_out,
                  w_router, w_sh_gate, w_sh_up, w_sh_down):
    o_q, o_kv, o_kr = 0, MLA_Q_LORA, MLA_Q_LORA + MLA_KV_LORA
    o_fq = o_kr + MLA_ROPE
    o_fk, o_fv = o_fq + FOX_WIDTH, o_fq + 2 * FOX_WIDTH
    o_fl = o_fq + 3 * FOX_WIDTH
    o_ga = o_fl + FOX_HEADS
    o_gb = o_ga + D_MODEL
    w = {}
    w["wlat"] = w_in[:, o_q:o_kr].astype(BF16)
    half = MLA_ROPE // 2

    def rope_pair(cols):
        x1, x2 = cols[..., :half], cols[..., half:]
        z = jnp.zeros(cols.shape[:-1] + (MLA_NOPE,), cols.dtype)
        zt = jnp.zeros(cols.shape[:-1] + (HEAD_PAD - MLA_QK,), cols.dtype)
        plain = jnp.concatenate([z, x1, x2, zt], axis=-1)
        rot = jnp.concatenate([z, -x2, x1, zt], axis=-1)
        return plain, rot

    kr_plain, kr_rot = rope_pair(w_in[:, o_kr:o_fq])
    w["wkr"] = jnp.concatenate([kr_plain, kr_rot], axis=1).astype(BF16)
    w["gq"] = g_q_lat.reshape(1, -1)
    w["gkv"] = g_kv_lat.reshape(1, -1)

    scale_a = 1.0 / math.sqrt(MLA_QK)
    wq = (w_q_up * scale_a).reshape(MLA_Q_LORA, MLA_HEADS, MLA_QK)
    q_plain, q_rot = rope_pair(wq[..., MLA_NOPE:])
    nope = jnp.concatenate([wq[..., :MLA_NOPE], jnp.zeros((MLA_Q_LORA, MLA_HEADS, HEAD_PAD - MLA_NOPE), F32)], -1)
    w["wqa"] = (nope + q_plain).reshape(MLA_Q_LORA, -1).astype(BF16)
    w["wqb"] = q_rot.reshape(MLA_Q_LORA, -1).astype(BF16)

    wkv = w_kv_up.reshape(MLA_KV_LORA, MLA_HEADS, MLA_NOPE + MLA_V)
    zpad = jnp.zeros((MLA_KV_LORA, MLA_HEADS, HEAD_PAD - MLA_NOPE), F32)
    w["wka"] = jnp.concatenate([wkv[..., :MLA_NOPE], zpad], -1).reshape(MLA_KV_LORA, -1).astype(BF16)

    def pair_pad(v):
        z = jnp.zeros_like(v)
        even = jnp.concatenate([v, z], -1)
        odd = jnp.concatenate([z, v], -1)
        sel = (jnp.arange(v.shape[1]) % 2 == 0)[None, :, None]
        return jnp.where(sel, even, odd)

    w["wva"] = pair_pad(wkv[..., MLA_NOPE:]).reshape(MLA_KV_LORA, -1).astype(BF16)

    def fox_heads(cols, scale):
        v = (cols * scale).reshape(D_MODEL, FOX_HEADS, FOX_HEAD_DIM)
        v = jnp.concatenate([v, jnp.zeros((D_MODEL, FOX_HEADS, HEAD_PAD - FOX_HEAD_DIM), F32)], -1)
        return v.reshape(D_MODEL, FOX_HEADS // 2, 2 * HEAD_PAD).transpose(1, 0, 2).astype(BF16)

    w["wfq"] = fox_heads(w_in[:, o_fq:o_fk], 1.0 / math.sqrt(FOX_HEAD_DIM))
    w["wfk"] = fox_heads(w_in[:, o_fk:o_fv], 1.0)
    wfv = pair_pad(w_in[:, o_fv:o_fl].reshape(D_MODEL, FOX_HEADS, FOX_HEAD_DIM))
    w["wfv"] = wfv.reshape(D_MODEL, FOX_HEADS // 2, 2 * HEAD_PAD).transpose(1, 0, 2).astype(BF16)

    wfl = jnp.concatenate([w_in[:, o_fl:o_ga]] * 3 + [jnp.zeros((D_MODEL, LANES - 3 * FOX_HEADS), F32)], 1)
    w["wfl_hi"] = wfl.astype(BF16)
    w["wfl_lo"] = (wfl - w["wfl_hi"].astype(F32)).astype(BF16)
    w["bfl"] = jnp.concatenate([b_forget] * 3 + [jnp.zeros((LANES - 3 * FOX_HEADS,), F32)]).reshape(1, LANES)

    pall = np.zeros((LANES, 2 * FOX_HEADS * HEAD_PAD), np.float32)
    cq = np.zeros((1, 2 * HEAD_PAD), np.float32)
    ck = np.zeros((1, 2 * HEAD_PAD), np.float32)
    for hd in range(FOX_HEADS):
        for term in range(3):
            pall[term * FOX_HEADS + hd, hd * HEAD_PAD + FQ_COL + term] = 1.0
            pall[term * FOX_HEADS + hd, FOX_HEADS * HEAD_PAD + hd * HEAD_PAD + FK_COL + term] = -1.0
    for j in range(2):
        cq[0, j * HEAD_PAD + FK_COL:j * HEAD_PAD + FK_COL + 3] = 1.0
        ck[0, j * HEAD_PAD + FQ_COL:j * HEAD_PAD + FQ_COL + 3] = 1.0
    w["pall"] = jnp.asarray(pall, BF16)
    w["cq"] = jnp.asarray(cq)
    w["ck"] = jnp.asarray(ck)

    w["wg"] = w_in[:, o_ga:o_gb + D_MODEL].astype(BF16)
    w["womla"] = w_o_mla.astype(BF16)
    w["wofox"] = w_o_fox.astype(BF16)
    w["wout"] = w_out.astype(BF16)
    wr = jnp.concatenate([w_router, jnp.zeros((D_MODEL, LANES - N_EXPERTS), F32)], 1)
    w["wrhi"] = wr.astype(BF16)
    w["wrlo"] = (wr - w["wrhi"].astype(F32)).astype(BF16)
    w["wsg"] = w_sh_gate.astype(BF16)
    w["wsu"] = w_sh_up.astype(BF16)
    w["wsd"] = w_sh_down.astype(BF16)
    return w


def _rope_freq_row():
    half = MLA_ROPE // 2
    inv = np.power(ROPE_THETA, -np.arange(half, dtype=np.float32) / half).astype(np.float32)
    row = np.zeros((1, LANES), np.float32)
    row[0, MLA_NOPE:MLA_NOPE + half] = inv
    row[0, MLA_NOPE + half:MLA_NOPE + 2 * half] = inv
    return jnp.asarray(row)


def kernel(x, c, positions, w_mod, b_mod, g_mix_norm, w_in, b_forget, g_q_lat, w_q_up, g_kv_lat, w_kv_up,
           w_o_mla, w_o_fox, w_out, g_ffn_norm, w_router, b_router, w_exp_gate, w_exp_up, w_exp_down,
           w_sh_gate, w_sh_up, w_sh_down, g_final):
    batch, seq, d = x.shape
    assert d == D_MODEL and w_mod.shape[0] == 1
    t = batch * seq
    tm = min(512, seq)
    bq = min(1024, seq)
    tr = min(512, t)
    tc = min(256, seq)
    assert seq % tm == 0 and seq % bq == 0 and t % tr == 0 and seq % tc == 0 and batch <= 8
    assert t % (SC_CHUNK * SC_WORKERS) == 0

    w = _prep_weights(w_in[0], b_forget[0], g_q_lat[0], w_q_up[0], g_kv_lat[0], w_kv_up[0], w_o_mla[0],
                      w_o_fox[0], w_out[0], w_router[0], w_sh_gate[0], w_sh_up[0], w_sh_down[0])

    c8 = jnp.zeros((8, D_MODEL), F32).at[:batch].set(c)
    mod = _mod(c8, w_mod[0], b_mod)
    mod8 = jnp.zeros((batch, 8, D_MODEL), F32).at[:, :N_MOD].set(mod[:batch].reshape(batch, N_MOD, D_MODEL))

    x2 = x.reshape(t, D_MODEL)
    fdec = _fox_decay(x2, mod8, g_mix_norm, w["wfl_hi"], w["wfl_lo"], w["bfl"], seq, tm)
    q_all, k_all, v_all, sg = _in_proj(x2, mod8, g_mix_norm, fdec, positions.reshape(t, 1),
                                       _rope_freq_row(), w, batch, seq, tm)
    o = _attention(q_all, k_all, v_all, bq)
    x1, hpk, logits = _out_proj(o.reshape(t, D_MODEL), sg, x2, mod8, g_ffn_norm, w, seq, tm)

    eidx, rank, gate, counts = _route(logits[:, :N_EXPERTS].T, b_router.reshape(N_EXPERTS, 1), tr)

    cnt = counts[:, 0]
    padded = (cnt + ROW_BLOCK - 1) // ROW_BLOCK * ROW_BLOCK
    pend = jnp.cumsum(padded)
    pstart = pend - padded
    n_blocks = t * TOP_K // ROW_BLOCK + N_EXPERTS
    n_rows = n_blocks * ROW_BLOCK
    block_row = jnp.arange(n_blocks, dtype=I32) * ROW_BLOCK
    block_e = jnp.minimum(jnp.sum(pend[None, :] <= block_row[:, None], axis=1), N_EXPERTS - 1).astype(I32)
    n_used = (pend[-1:] // ROW_BLOCK).astype(I32)
    block_valid = jnp.clip((pstart + cnt)[block_e] - block_row, 0, ROW_BLOCK).astype(I32)

    dest = _dest(pstart.astype(I32), eidx, rank, tr)
    dest3 = dest.reshape(SLOTS, t // SC_CHUNK, SC_CHUNK).transpose(1, 0, 2)
    xs = _dispatch(hpk, dest3, n_rows)
    ys = _experts(block_e, block_valid, n_used, xs, w_exp_gate[0].astype(BF16), w_exp_up[0].astype(BF16),
                  w_exp_down[0].astype(BF16))
    yk = _gather_back(ys, dest3)
    out = _combine(yk, gate.T, hpk, x1, mod8, w, g_final.reshape(1, D_MODEL), seq, tc)
    return out.reshape(batch, seq, D_MODEL)
```

```python
import functools
import math

import numpy as np
import jax
import jax.numpy as jnp
from jax import lax
from jax.experimental import pallas as pl
from jax.experimental.pallas import tpu as pltpu
from jax.experimental.pallas import tpu_sc as plsc

F32 = jnp.float32
BF16 = jnp.bfloat16
I32 = jnp.int32
U32 = jnp.uint32

D_MODEL = 1024
MLA_HEADS = 8
MLA_Q_LORA = 256
MLA_KV_LORA = 128
MLA_NOPE = 64
MLA_ROPE = 32
MLA_V = 64
MLA_QK = MLA_NOPE + MLA_ROPE
ROPE_THETA = 10000.0
FOX_HEADS = 8
FOX_HEAD_DIM = 64
FOX_WIDTH = FOX_HEADS * FOX_HEAD_DIM
N_HEADS = MLA_HEADS + FOX_HEADS
N_EXPERTS = 64
N_GROUPS = 8
GROUP_SIZE = N_EXPERTS // N_GROUPS
TOPK_GROUPS = 4
TOP_K = 6
D_EXPERT = 256
ROUTED_SCALE = 2.5
N_MOD = 6
NORM_EPS = 1e-6
NEG_INF = -1e30

LANES = 128
HEAD_PAD = 128
ROW_BLOCK = 512
SLOTS = 8
SC_CORES = 2
SC_SUBCORES = 16
SC_WORKERS = SC_CORES * SC_SUBCORES
SC_CHUNK = 32
VMEM_LIMIT = 56 * 1024 * 1024

FQ_COL = FOX_HEAD_DIM
FK_COL = FOX_HEAD_DIM + 3


def _cparams(sem, vmem=VMEM_LIMIT):
    return pltpu.CompilerParams(dimension_semantics=sem, vmem_limit_bytes=vmem)


def _const_spec(shape):
    nd = len(shape)
    return pl.BlockSpec(shape, lambda *_: (0,) * nd)


def _rms(x):
    return x * lax.rsqrt(jnp.mean(x * x, axis=-1, keepdims=True) + NORM_EPS)


def _split3(x):
    hi = x.astype(BF16)
    r = x - hi.astype(F32)
    mid = r.astype(BF16)
    lo = (r - mid.astype(F32)).astype(BF16)
    return hi, mid, lo


def _dot(a, b):
    return jnp.dot(a, b, preferred_element_type=F32)


def _modulated_norm(x, gain, mod, shift_row, scale_row):
    shift = mod[shift_row:shift_row + 1, :]
    scale = mod[scale_row:scale_row + 1, :]
    return _rms(x) * gain * (1.0 + scale) + shift


def _mod_body(c_ref, w_ref, b_ref, o_ref):
    c = c_ref[...]
    cond = c * jax.nn.sigmoid(c)
    o_ref[...] = _dot(cond.astype(BF16), w_ref[...].astype(BF16)) + b_ref[...]


def _mod(c8, w_mod, b_mod):
    n = w_mod.shape[1]
    tn = D_MODEL
    return pl.pallas_call(
        _mod_body,
        out_shape=jax.ShapeDtypeStruct((8, n), F32),
        grid=(n // tn,),
        in_specs=[_const_spec((8, D_MODEL)),
                  pl.BlockSpec((D_MODEL, tn), lambda j: (0, j)),
                  pl.BlockSpec((1, tn), lambda j: (0, j))],
        out_specs=pl.BlockSpec((8, tn), lambda j: (0, j)),
        compiler_params=_cparams(("parallel",)),
        name="mod",
    )(c8, w_mod, b_mod)


def _decay_body(tiles_per_seq, x_ref, mod_ref, g_ref, whi_ref, wlo_ref, b_ref, o_ref, carry_ref):
    i = pl.program_id(0)

    @pl.when(i % tiles_per_seq == 0)
    def _():
        carry_ref[...] = jnp.zeros_like(carry_ref)

    h = _modulated_norm(x_ref[...], g_ref[...], mod_ref[0], 0, 1)
    hhi = h.astype(BF16)
    hlo = (h - hhi.astype(F32)).astype(BF16)
    z = _dot(hhi, whi_ref[...]) + _dot(hhi, wlo_ref[...]) + _dot(hlo, whi_ref[...]) + b_ref[...]
    logf = jnp.minimum(z, 0.0) - jnp.log1p(jnp.exp(-jnp.abs(z)))
    tm = logf.shape[0]
    row = lax.broadcasted_iota(I32, (tm, tm), 0)
    col = lax.broadcasted_iota(I32, (tm, tm), 1)
    tri = jnp.where(col <= row, 1.0, 0.0).astype(BF16)
    hi, mid, lo = _split3(logf)
    cum = _dot(tri, hi) + _dot(tri, mid) + _dot(tri, lo) + carry_ref[...]
    o_ref[...] = cum
    carry_ref[...] = cum[tm - 1:tm, :]


def _fox_decay(x2, mod8, g_mix, wfl_hi, wfl_lo, bfl, seq, tm):
    t = x2.shape[0]
    tps = seq // tm
    return pl.pallas_call(
        functools.partial(_decay_body, tps),
        out_shape=jax.ShapeDtypeStruct((t, LANES), F32),
        grid=(t // tm,),
        in_specs=[pl.BlockSpec((tm, D_MODEL), lambda i: (i, 0)),
                  pl.BlockSpec((1, 8, D_MODEL), lambda i: (i // tps, 0, 0)),
                  _const_spec((1, D_MODEL)),
                  _const_spec((D_MODEL, LANES)),
                  _const_spec((D_MODEL, LANES)),
                  _const_spec((1, LANES))],
        out_specs=pl.BlockSpec((tm, LANES), lambda i: (i, 0)),
        scratch_shapes=[pltpu.VMEM((1, LANES), F32)],
        compiler_params=_cparams(("arbitrary",)),
        name="fox_decay",
    )(x2, mod8, g_mix, wfl_hi, wfl_lo, bfl)


def _in_proj_body(x_ref, mod_ref, g_ref, f_ref, pos_ref, freq_ref,
                  wlat_ref, wkr_ref, gq_ref, gkv_ref, wqa_ref, wqb_ref, wka_ref, wva_ref,
                  wfq_ref, wfk_ref, wfv_ref, pall_ref, cq_ref, ck_ref, wg_ref,
                  q_ref, k_ref, v_ref, sg_ref):
    h = _modulated_norm(x_ref[...], g_ref[...], mod_ref[0], 0, 1)
    hb = h.astype(BF16)

    lat = _dot(hb, wlat_ref[...])
    qn = (_rms(lat[:, :MLA_Q_LORA]) * gq_ref[...]).astype(BF16)
    kvn = (_rms(lat[:, MLA_Q_LORA:]) * gkv_ref[...]).astype(BF16)
    ang = pos_ref[...].astype(F32) * freq_ref[...]
    cs = jnp.cos(ang)
    sn = jnp.sin(ang)
    kr = _dot(hb, wkr_ref[...])
    kpe = kr[:, :HEAD_PAD] * cs + kr[:, HEAD_PAD:] * sn
    qa = _dot(qn, wqa_ref[...])
    qb = _dot(qn, wqb_ref[...])
    ka = _dot(kvn, wka_ref[...])
    va = _dot(kvn, wva_ref[...])
    for hd in range(MLA_HEADS):
        sl = slice(hd * HEAD_PAD, (hd + 1) * HEAD_PAD)
        q_ref[0, hd] = (qa[:, sl] * cs + qb[:, sl] * sn).astype(BF16)
        k_ref[0, hd] = (ka[:, sl] + kpe).astype(BF16)
        v_ref[0, hd] = va[:, sl].astype(BF16)

    hi, mid, lo = _split3(f_ref[...])
    lane = lax.broadcasted_iota(I32, hi.shape, 1)
    f3 = jnp.where(lane < FOX_HEADS, hi, jnp.where(lane < 2 * FOX_HEADS, mid, lo))
    fp = _dot(f3, pall_ref[...])
    half = FOX_HEADS // 2 * 2 * HEAD_PAD
    for hp in range(FOX_HEADS // 2):
        sl = slice(hp * 2 * HEAD_PAD, (hp + 1) * 2 * HEAD_PAD)
        q2 = (_dot(hb, wfq_ref[hp]) + fp[:, sl] + cq_ref[...]).astype(BF16)
        k2 = (_dot(hb, wfk_ref[hp]) + fp[:, half + hp * 2 * HEAD_PAD:half + (hp + 1) * 2 * HEAD_PAD]
              + ck_ref[...]).astype(BF16)
        v2 = _dot(hb, wfv_ref[hp]).astype(BF16)
        for j in range(2):
            hd = MLA_HEADS + 2 * hp + j
            q_ref[0, hd] = q2[:, j * HEAD_PAD:(j + 1) * HEAD_PAD]
            k_ref[0, hd] = k2[:, j * HEAD_PAD:(j + 1) * HEAD_PAD]
            v_ref[0, hd] = v2[:, j * HEAD_PAD:(j + 1) * HEAD_PAD]

    sg_ref[...] = jax.nn.sigmoid(_dot(hb, wg_ref[...])).astype(BF16)


def _in_proj(x2, mod8, g_mix, fdec, pos, freq, w, batch, seq, tm):
    t = x2.shape[0]
    tps = seq // tm
    consts = [w["wlat"], w["wkr"], w["gq"], w["gkv"], w["wqa"], w["wqb"], w["wka"], w["wva"],
              w["wfq"], w["wfk"], w["wfv"], w["pall"], w["cq"], w["ck"], w["wg"]]
    head_shape = jax.ShapeDtypeStruct((batch, N_HEADS, seq, HEAD_PAD), BF16)
    head_spec = pl.BlockSpec((1, N_HEADS, tm, HEAD_PAD), lambda i: (i // tps, 0, i % tps, 0))
    return pl.pallas_call(
        _in_proj_body,
        out_shape=(head_shape, head_shape, head_shape,
                   jax.ShapeDtypeStruct((t, 2 * D_MODEL), BF16)),
        grid=(t // tm,),
        in_specs=[pl.BlockSpec((tm, D_MODEL), lambda i: (i, 0)),
                  pl.BlockSpec((1, 8, D_MODEL), lambda i: (i // tps, 0, 0)),
                  _const_spec((1, D_MODEL)),
                  pl.BlockSpec((tm, LANES), lambda i: (i, 0)),
                  pl.BlockSpec((tm, 1), lambda i: (i, 0)),
                  _const_spec((1, LANES))] + [_const_spec(a.shape) for a in consts],
        out_specs=(head_spec, head_spec, head_spec,
                   pl.BlockSpec((tm, 2 * D_MODEL), lambda i: (i, 0))),
        compiler_params=_cparams(("parallel",)),
        name="in_proj",
    )(x2, mod8, g_mix, fdec, pos, freq, *consts)


def _attn_body(bq, sq, q_ref, k_ref, v_ref, o_ref, s_scr, p_scr, acc_scr):
    qi = pl.program_id(2)
    bk = sq
    n_heads = q_ref.shape[1]
    n_sub = bq // sq
    assert n_sub % 2 == 0
    chains = [(hh, u) for hh in range(n_heads) for u in range(n_sub)]
    n_chains = len(chains)
    n_main = qi * n_sub

    def chunk_start(j):
        return pl.multiple_of(jnp.maximum(j, 0) * bk, bk)

    def scores(c, j, par):
        hh, u = chains[c]
        k = k_ref[0, hh, pl.ds(chunk_start(j), bk), :]
        q = q_ref[0, hh, u * sq:(u + 1) * sq, :]
        s = lax.dot_general(k, q, (((1,), (1,)), ((), ())), preferred_element_type=F32)
        s_scr[par, c] = s
        return jnp.max(s, axis=0, keepdims=True)

    def values(c, j, par, alpha):
        v = v_ref[0, chains[c][0], pl.ds(chunk_start(j), bk), :]
        pv = lax.dot_general(v, p_scr[par, c], (((0,), (0,)), ((), ())), preferred_element_type=F32)
        acc_scr[c] = alpha * acc_scr[c] + pv

    def softmax(c, par, m, l, smax, masked):
        s = s_scr[par, c]
        if masked:
            key = lax.broadcasted_iota(I32, (bk, sq), 0)
            qry = lax.broadcasted_iota(I32, (bk, sq), 1)
            s = jnp.where(key <= qry, s, NEG_INF)
            smax = jnp.max(s, axis=0, keepdims=True)
        m_new = jnp.maximum(m, smax)
        alpha = jnp.exp(m - m_new)
        p = jnp.exp(s - m_new)
        p_scr[par, c] = p.astype(BF16)
        return m_new, alpha * l + jnp.sum(p, axis=0, keepdims=True), alpha

    def stage(j, par, state, active, has_next, masked_of):
        nxt = {c: scores(c, j + 1, 1 - par) if has_next(c) else state[c][2] for c in active}
        for c in active:
            values(c, j - 1, 1 - par, state[c][3])
        out = list(state)
        for c in active:
            m, l, smax, _ = state[c]
            m_new, l_new, alpha = softmax(c, par, m, l, smax, masked_of(c))
            out[c] = (m_new, l_new, nxt[c], alpha)
        return out

    p_scr[1] = jnp.zeros(p_scr.shape[1:], BF16)
    acc_scr[...] = jnp.zeros_like(acc_scr)
    state = [(jnp.full((1, sq), NEG_INF, F32), jnp.zeros((1, sq), F32), scores(c, 0, 0), jnp.ones((1, sq), F32))
             for c in range(n_chains)]

    def step(jj, st):
        for par in range(2):
            st = stage(2 * jj + par, par, list(st), range(n_chains), lambda c: True, lambda c: False)
        return tuple(st)

    state = list(lax.fori_loop(0, n_main // 2, step, tuple(state)))

    for t in range(n_sub):
        active = [c for c, (_, u) in enumerate(chains) if u >= t]
        state = stage(n_main + t, t % 2, state, active, lambda c, t=t: chains[c][1] > t,
                      lambda c, t=t: chains[c][1] == t)

    out = None
    for hh in range(n_heads):
        parts = []
        for u in range(n_sub):
            c = hh * n_sub + u
            values(c, n_main + u, u % 2, state[c][3])
            parts.append(acc_scr[c] / state[c][1])
        res = jnp.concatenate(parts, axis=1)
        out = res if out is None else out + res
    o_ref[0] = out.T.astype(BF16)


def _attention(q_all, k_all, v_all, bq):
    batch, _, seq, _ = q_all.shape
    sq = min(256, bq)
    n_chains = 2 * (bq // sq)
    return pl.pallas_call(
        functools.partial(_attn_body, bq, sq),
        out_shape=jax.ShapeDtypeStruct((batch, seq, N_HEADS * MLA_V), BF16),
        grid=(batch, N_HEADS // 2, seq // bq),
        in_specs=[pl.BlockSpec((1, 2, bq, HEAD_PAD), lambda b, hp, qi: (b, hp, qi, 0)),
                  pl.BlockSpec((1, 2, seq, HEAD_PAD), lambda b, hp, qi: (b, hp, 0, 0)),
                  pl.BlockSpec((1, 2, seq, HEAD_PAD), lambda b, hp, qi: (b, hp, 0, 0))],
        out_specs=pl.BlockSpec((1, bq, 2 * MLA_V), lambda b, hp, qi: (b, qi, hp)),
        scratch_shapes=[pltpu.VMEM((2, n_chains, sq, sq), F32),
                        pltpu.VMEM((2, n_chains, sq, sq), BF16),
                        pltpu.VMEM((n_chains, HEAD_PAD, sq), F32)],
        compiler_params=_cparams(("parallel", "parallel", "arbitrary")),
        name="attention",
    )(q_all, k_all, v_all)


def _pack_halves(x):
    w = x.shape[1] // 2
    a = lax.bitcast_convert_type(x[:, :w].astype(BF16).astype(F32), U32)
    b = lax.bitcast_convert_type(x[:, w:].astype(BF16).astype(F32), U32)
    return a | (b >> 16)


def _unpack_halves(p):
    a = lax.bitcast_convert_type(p & jnp.uint32(0xFFFF0000), F32)
    b = lax.bitcast_convert_type(p << 16, F32)
    return a, b


def _out_proj_body(o_ref, sg_ref, x_ref, mod_ref, gffn_ref, womla_ref, wofox_ref, wout_ref,
                   wrhi_ref, wrlo_ref, x1_ref, hp_ref, lg_ref):
    o = o_ref[...]
    half = o.shape[1] // 2
    mo = _dot(o[:, :half], womla_ref[...])
    fo = _dot(o[:, half:], wofox_ref[...])
    sg = sg_ref[...]
    merged = sg[:, :D_MODEL].astype(F32) * mo + sg[:, D_MODEL:].astype(F32) * fo
    mix = _dot(merged.astype(BF16), wout_ref[...])
    mod = mod_ref[0]
    x1 = x_ref[...] + mod[2:3, :] * mix
    x1_ref[...] = x1
    h2 = _modulated_norm(x1, gffn_ref[...], mod, 3, 4)
    hhi = h2.astype(BF16)
    hlo = (h2 - hhi.astype(F32)).astype(BF16)
    lg_ref[...] = _dot(hhi, wrhi_ref[...]) + _dot(hhi, wrlo_ref[...]) + _dot(hlo, wrhi_ref[...])
    hp_ref[...] = _pack_halves(h2)


def _out_proj(o2, sg, x2, mod8, g_ffn, w, seq, tm):
    t = x2.shape[0]
    tps = seq // tm
    consts = [w["womla"], w["wofox"], w["wout"], w["wrhi"], w["wrlo"]]
    return pl.pallas_call(
        _out_proj_body,
        out_shape=(jax.ShapeDtypeStruct((t, D_MODEL), F32),
                   jax.ShapeDtypeStruct((t, D_MODEL // 2), U32),
                   jax.ShapeDtypeStruct((t, LANES), F32)),
        grid=(t // tm,),
        in_specs=[pl.BlockSpec((tm, D_MODEL), lambda i: (i, 0)),
                  pl.BlockSpec((tm, 2 * D_MODEL), lambda i: (i, 0)),
                  pl.BlockSpec((tm, D_MODEL), lambda i: (i, 0)),
                  pl.BlockSpec((1, 8, D_MODEL), lambda i: (i // tps, 0, 0)),
                  _const_spec((1, D_MODEL))] + [_const_spec(a.shape) for a in consts],
        out_specs=(pl.BlockSpec((tm, D_MODEL), lambda i: (i, 0)),
                   pl.BlockSpec((tm, D_MODEL // 2), lambda i: (i, 0)),
                   pl.BlockSpec((tm, LANES), lambda i: (i, 0))),
        compiler_params=_cparams(("parallel",)),
        name="out_proj",
    )(o2, sg, x2, mod8, g_ffn, *consts)


def _route_body(lt_ref, b_ref, eidx_ref, rank_ref, gate_ref, cnt_ref, carry_ref):
    i = pl.program_id(0)

    @pl.when(i == 0)
    def _():
        carry_ref[...] = jnp.zeros_like(carry_ref)

    s = jax.nn.sigmoid(lt_ref[...])
    c = s + b_ref[...]
    tr = s.shape[1]
    sub = lax.broadcasted_iota(I32, (GROUP_SIZE, tr), 0).astype(F32)

    gs = []
    for g in range(N_GROUPS):
        cg = c[g * GROUP_SIZE:(g + 1) * GROUP_SIZE, :]
        m1 = jnp.max(cg, axis=0, keepdims=True)
        i1 = jnp.min(jnp.where(cg == m1, sub, float(GROUP_SIZE)), axis=0, keepdims=True)
        m2 = jnp.max(jnp.where(sub == i1, NEG_INF, cg), axis=0, keepdims=True)
        gs.append(m1 + m2)

    masked = []
    for g in range(N_GROUPS):
        beats = jnp.zeros_like(gs[g])
        for o in range(N_GROUPS):
            if o == g:
                continue
            better = (gs[o] >= gs[g]) if o < g else (gs[o] > gs[g])
            beats = beats + jnp.where(better, 1.0, 0.0)
        keep = beats < float(TOPK_GROUPS)
        cg = c[g * GROUP_SIZE:(g + 1) * GROUP_SIZE, :]
        masked.append(jnp.where(keep, cg, NEG_INF))
    mc = jnp.concatenate(masked, axis=0)

    eio = lax.broadcasted_iota(I32, (N_EXPERTS, tr), 0).astype(F32)
    picks = []
    selected = jnp.zeros((N_EXPERTS, tr), F32)
    for _ in range(TOP_K):
        m = jnp.max(mc, axis=0, keepdims=True)
        idx = jnp.min(jnp.where(mc == m, eio, float(N_EXPERTS)), axis=0, keepdims=True)
        hit = eio == idx
        picks.append(idx)
        selected = jnp.where(hit, 1.0, selected)
        mc = jnp.where(hit, -3.0e38, mc)

    ssum = jnp.sum(selected * s, axis=0, keepdims=True)
    gate_full = selected * s / ssum * ROUTED_SCALE

    r_io = lax.broadcasted_iota(I32, (tr, tr), 0)
    c_io = lax.broadcasted_iota(I32, (tr, tr), 1)
    upper = jnp.where(r_io < c_io, 1.0, 0.0).astype(BF16)
    before = _dot(selected.astype(BF16), upper) + carry_ref[...]
    carry_new = carry_ref[...] + jnp.sum(selected, axis=1, keepdims=True)
    carry_ref[...] = carry_new
    cnt_ref[...] = jnp.broadcast_to(carry_new, cnt_ref.shape).astype(I32)

    for r in range(SLOTS):
        if r < TOP_K:
            hit = eio == picks[r]
            eidx_ref[r:r + 1, :] = picks[r].astype(I32)
            rank_ref[r:r + 1, :] = jnp.sum(jnp.where(hit, before, 0.0), axis=0, keepdims=True).astype(I32)
            gate_ref[r:r + 1, :] = jnp.sum(jnp.where(hit, gate_full, 0.0), axis=0, keepdims=True)
        else:
            eidx_ref[r:r + 1, :] = jnp.zeros((1, tr), I32)
            rank_ref[r:r + 1, :] = jnp.zeros((1, tr), I32)
            gate_ref[r:r + 1, :] = jnp.zeros((1, tr), F32)


def _route(logits_t, bias_col, tr):
    t = logits_t.shape[1]
    slot_i = jax.ShapeDtypeStruct((SLOTS, t), I32)
    slot_spec = pl.BlockSpec((SLOTS, tr), lambda i: (0, i))
    return pl.pallas_call(
        _route_body,
        out_shape=(slot_i, slot_i, jax.ShapeDtypeStruct((SLOTS, t), F32),
                   jax.ShapeDtypeStruct((N_EXPERTS, LANES), I32)),
        grid=(t // tr,),
        in_specs=[pl.BlockSpec((N_EXPERTS, tr), lambda i: (0, i)),
                  _const_spec((N_EXPERTS, 1))],
        out_specs=(slot_spec, slot_spec, slot_spec, _const_spec((N_EXPERTS, LANES))),
        scratch_shapes=[pltpu.VMEM((N_EXPERTS, 1), F32)],
        compiler_params=_cparams(("arbitrary",)),
        name="route",
    )(logits_t, bias_col)


def _dest_body(pstart_ref, eidx_ref, rank_ref, o_ref):
    e = eidx_ref[...]
    d = rank_ref[...]
    for j in range(N_EXPERTS):
        d = d + jnp.where(e == j, pstart_ref[j], 0)
    o_ref[...] = d


def _dest(pstart, eidx, rank, tr):
    t = eidx.shape[1]
    spec = pl.BlockSpec((SLOTS, tr), lambda i, ps: (0, i))
    return pl.pallas_call(
        _dest_body,
        out_shape=jax.ShapeDtypeStruct((SLOTS, t), I32),
        grid_spec=pltpu.PrefetchScalarGridSpec(
            num_scalar_prefetch=1, grid=(t // tr,), in_specs=[spec, spec], out_specs=spec),
        compiler_params=_cparams(("parallel",)),
        name="dest",
    )(pstart, eidx, rank)


def _sc_mesh():
    return plsc.VectorSubcoreMesh(core_axis_name="c", subcore_axis_name="s")


def _sc_worker_id():
    return lax.axis_index("s") * SC_CORES + lax.axis_index("c")


def _dispatch(hpk, dest3, n_rows):
    _, w = hpk.shape
    n_chunks, _, c = dest3.shape
    per_worker = n_chunks // SC_WORKERS

    def body(h_hbm, d_hbm, xs_hbm, idx_v, rows_v, sem):
        wid = _sc_worker_id()

        @pl.loop(0, per_worker)
        def _(i):
            ch = wid * per_worker + i
            pltpu.sync_copy(d_hbm.at[ch], idx_v)
            pltpu.sync_copy(h_hbm.at[pl.ds(ch * c, c)], rows_v)
            copies = [pltpu.async_copy(rows_v, xs_hbm.at[idx_v.at[k]], sem) for k in range(TOP_K)]
            for cp in copies:
                cp.wait()

    return pl.kernel(
        body, mesh=_sc_mesh(),
        out_type=jax.ShapeDtypeStruct((n_rows, w), U32),
        scratch_types=[pltpu.VMEM((SLOTS, c), I32), pltpu.VMEM((c, w), U32), pltpu.SemaphoreType.DMA],
        name="dispatch",
    )(hpk, dest3)


def _experts_body(n_sub, be_ref, nv_ref, first_ref, nu_ref, xs_ref, wg_ref, wu_ref, wd_ref, ys_ref,
                  wgb_ref, wub_ref, wdb_ref):
    del be_ref
    i = pl.program_id(0)

    @pl.when(first_ref[i] == 1)
    def _():
        wgb_ref[...] = wg_ref[0].astype(BF16)
        wub_ref[...] = wu_ref[0].astype(BF16)
        wdb_ref[...] = wd_ref[0].astype(BF16)

    @pl.when(i < nu_ref[0])
    def _():
        half = D_MODEL // 2
        sub = ROW_BLOCK // n_sub
        row = lax.broadcasted_iota(I32, (sub, xs_ref.shape[1]), 0)
        gu = []
        for s in range(n_sub):
            x = jnp.where(row + s * sub < nv_ref[i], xs_ref[s * sub:(s + 1) * sub, :], jnp.uint32(0))
            xa, xb = _unpack_halves(x)
            xa = xa.astype(BF16)
            xb = xb.astype(BF16)
            g = _dot(xa, wgb_ref[:half, :]) + _dot(xb, wgb_ref[half:, :])
            u = _dot(xa, wub_ref[:half, :]) + _dot(xb, wub_ref[half:, :])
            gu.append((g, u))
        for s, (g, u) in enumerate(gu):
            hb = (g * jax.nn.sigmoid(g) * u).astype(BF16)
            ys_ref[s * sub:(s + 1) * sub, :] = _pack_halves(_dot(hb, wdb_ref[...]))

    @pl.when(i >= nu_ref[0])
    def _():
        ys_ref[...] = jnp.zeros_like(ys_ref)


def _experts(block_e, block_valid, block_first, n_used, xs, wg, wu, wd):
    n_rows, w = xs.shape
    n_blocks = n_rows // ROW_BLOCK

    def row_map(i, be, nv, bf, nu):
        return (jnp.minimum(i, nu[0] - 1), 0)

    def w_map(i, be, nv, bf, nu):
        return (be[i], 0, 0)

    return pl.pallas_call(
        functools.partial(_experts_body, ROW_BLOCK // 256),
        out_shape=jax.ShapeDtypeStruct((n_rows, w), U32),
        grid_spec=pltpu.PrefetchScalarGridSpec(
            num_scalar_prefetch=4, grid=(n_blocks,),
            in_specs=[pl.BlockSpec((ROW_BLOCK, w), row_map),
                      pl.BlockSpec((1, D_MODEL, D_EXPERT), w_map),
                      pl.BlockSpec((1, D_MODEL, D_EXPERT), w_map),
                      pl.BlockSpec((1, D_EXPERT, D_MODEL), w_map)],
            out_specs=pl.BlockSpec((ROW_BLOCK, w), lambda i, be, nv, bf, nu: (i, 0)),
            scratch_shapes=[pltpu.VMEM((D_MODEL, D_EXPERT), BF16), pltpu.VMEM((D_MODEL, D_EXPERT), BF16),
                            pltpu.VMEM((D_EXPERT, D_MODEL), BF16)]),
        compiler_params=_cparams(("arbitrary",)),
        name="experts",
    )(block_e, block_valid, block_first, n_used, xs, wg, wu, wd)


def _gather_back(ys, dest3):
    _, w = ys.shape
    n_chunks, _, c = dest3.shape
    per_worker = n_chunks // SC_WORKERS

    def body(ys_hbm, d_hbm, yk_hbm, idx_v, rows_v, gsem, osem):
        wid = _sc_worker_id()

        @pl.loop(0, per_worker)
        def _(i):
            ch = wid * per_worker + i
            pltpu.sync_copy(d_hbm.at[ch], idx_v)
            gathers = [pltpu.async_copy(ys_hbm.at[idx_v.at[k]], rows_v.at[k], gsem) for k in range(TOP_K)]
            for g in gathers:
                g.wait()
            outs = [pltpu.async_copy(rows_v.at[k], yk_hbm.at[k, pl.ds(ch * c, c)], osem) for k in range(TOP_K)]
            for o in outs:
                o.wait()

    return pl.kernel(
        body, mesh=_sc_mesh(),
        out_type=jax.ShapeDtypeStruct((TOP_K, n_chunks * c, w), U32),
        scratch_types=[pltpu.VMEM((SLOTS, c), I32), pltpu.VMEM((TOP_K, c, w), U32),
                       pltpu.SemaphoreType.DMA, pltpu.SemaphoreType.DMA],
        name="gather_back",
    )(ys, dest3)


def _combine_body(yk_ref, gate_ref, hp_ref, x1_ref, mod_ref, wsg_ref, wsu_ref, wsd_ref, gfin_ref, o_ref):
    ha, hb = _unpack_halves(hp_ref[...])
    ha = ha.astype(BF16)
    hb = hb.astype(BF16)
    half = D_MODEL // 2
    g = _dot(ha, wsg_ref[:half, :]) + _dot(hb, wsg_ref[half:, :])
    u = _dot(ha, wsu_ref[:half, :]) + _dot(hb, wsu_ref[half:, :])
    shared = _dot((g * jax.nn.sigmoid(g) * u).astype(BF16), wsd_ref[...])

    gate = gate_ref[...]
    ra = jnp.zeros(ha.shape, F32)
    rb = jnp.zeros(ha.shape, F32)
    for k in range(TOP_K):
        ya, yb = _unpack_halves(yk_ref[k])
        gk = gate[:, k:k + 1]
        ra = ra + gk * ya
        rb = rb + gk * yb
    moe = shared + jnp.concatenate([ra, rb], axis=1)
    mod = mod_ref[0]
    x2 = x1_ref[...] + mod[5:6, :] * moe
    o_ref[...] = _rms(x2) * gfin_ref[...]


def _combine(yk, gate_t, hpk, x1, mod8, w, g_final, seq, tq):
    t, wd = hpk.shape
    tps = seq // tq
    return pl.pallas_call(
        _combine_body,
        out_shape=jax.ShapeDtypeStruct((t, D_MODEL), F32),
        grid=(t // tq,),
        in_specs=[pl.BlockSpec((TOP_K, tq, wd), lambda i: (0, i, 0)),
                  pl.BlockSpec((tq, SLOTS), lambda i: (i, 0)),
                  pl.BlockSpec((tq, wd), lambda i: (i, 0)),
                  pl.BlockSpec((tq, D_MODEL), lambda i: (i, 0)),
                  pl.BlockSpec((1, 8, D_MODEL), lambda i: (i // tps, 0, 0)),
                  _const_spec(w["wsg"].shape), _const_spec(w["wsu"].shape), _const_spec(w["wsd"].shape),
                  _const_spec((1, D_MODEL))],
        out_specs=pl.BlockSpec((tq, D_MODEL), lambda i: (i, 0)),
        compiler_params=_cparams(("parallel",)),
        name="combine",
    )(yk, gate_t, hpk, x1, mod8, w["wsg"], w["wsu"], w["wsd"], g_final)


def _prep_weights(w_in, b_forget, g_q_lat, w_q_up, g_kv_lat, w_kv_up, w_o_mla, w_o_fox, w_out,
                  w_router, w_sh_gate, w_sh_up, w_sh_down):
    o_q, o_kv, o_kr = 0, MLA_Q_LORA, MLA_Q_LORA + MLA_KV_LORA
    o_fq = o_kr + MLA_ROPE
    o_fk, o_fv = o_fq + FOX_WIDTH, o_fq + 2 * FOX_WIDTH
    o_fl = o_fq + 3 * FOX_WIDTH
    o_ga = o_fl + FOX_HEADS
    o_gb = o_ga + D_MODEL
    w = {}
    w["wlat"] = w_in[:, o_q:o_kr].astype(BF16)
    half = MLA_ROPE // 2

    def rope_pair(cols):
        x1, x2 = cols[..., :half], cols[..., half:]
        z = jnp.zeros(cols.shape[:-1] + (MLA_NOPE,), cols.dtype)
        zt = jnp.zeros(cols.shape[:-1] + (HEAD_PAD - MLA_QK,), cols.dtype)
        plain = jnp.concatenate([z, x1, x2, zt], axis=-1)
        rot = jnp.concatenate([z, -x2, x1, zt], axis=-1)
        return plain, rot

    kr_plain, kr_rot = rope_pair(w_in[:, o_kr:o_fq])
    w["wkr"] = jnp.concatenate([kr_plain, kr_rot], axis=1).astype(BF16)
    w["gq"] = g_q_lat.reshape(1, -1)
    w["gkv"] = g_kv_lat.reshape(1, -1)

    scale_a = 1.0 / math.sqrt(MLA_QK)
    wq = (w_q_up * scale_a).reshape(MLA_Q_LORA, MLA_HEADS, MLA_QK)
    q_plain, q_rot = rope_pair(wq[..., MLA_NOPE:])
    nope = jnp.concatenate([wq[..., :MLA_NOPE], jnp.zeros((MLA_Q_LORA, MLA_HEADS, HEAD_PAD - MLA_NOPE), F32)], -1)
    w["wqa"] = (nope + q_plain).reshape(MLA_Q_LORA, -1).astype(BF16)
    w["wqb"] = q_rot.reshape(MLA_Q_LORA, -1).astype(BF16)

    wkv = w_kv_up.reshape(MLA_KV_LORA, MLA_HEADS, MLA_NOPE + MLA_V)
    zpad = jnp.zeros((MLA_KV_LORA, MLA_HEADS, HEAD_PAD - MLA_NOPE), F32)
    w["wka"] = jnp.concatenate([wkv[..., :MLA_NOPE], zpad], -1).reshape(MLA_KV_LORA, -1).astype(BF16)

    def pair_pad(v):
        z = jnp.zeros_like(v)
        even = jnp.concatenate([v, z], -1)
        odd = jnp.concatenate([z, v], -1)
        sel = (jnp.arange(v.shape[1]) % 2 == 0)[None, :, None]
        return jnp.where(sel, even, odd)

    w["wva"] = pair_pad(wkv[..., MLA_NOPE:]).reshape(MLA_KV_LORA, -1).astype(BF16)

    def fox_heads(cols, scale):
        v = (cols * scale).reshape(D_MODEL, FOX_HEADS, FOX_HEAD_DIM)
        v = jnp.concatenate([v, jnp.zeros((D_MODEL, FOX_HEADS, HEAD_PAD - FOX_HEAD_DIM), F32)], -1)
        return v.reshape(D_MODEL, FOX_HEADS // 2, 2 * HEAD_PAD).transpose(1, 0, 2).astype(BF16)

    w["wfq"] = fox_heads(w_in[:, o_fq:o_fk], 1.0 / math.sqrt(FOX_HEAD_DIM))
    w["wfk"] = fox_heads(w_in[:, o_fk:o_fv], 1.0)
    wfv = pair_pad(w_in[:, o_fv:o_fl].reshape(D_MODEL, FOX_HEADS, FOX_HEAD_DIM))
    w["wfv"] = wfv.reshape(D_MODEL, FOX_HEADS // 2, 2 * HEAD_PAD).transpose(1, 0, 2).astype(BF16)

    wfl = jnp.concatenate([w_in[:, o_fl:o_ga]] * 3 + [jnp.zeros((D_MODEL, LANES - 3 * FOX_HEADS), F32)], 1)
    w["wfl_hi"] = wfl.astype(BF16)
    w["wfl_lo"] = (wfl - w["wfl_hi"].astype(F32)).astype(BF16)
    w["bfl"] = jnp.concatenate([b_forget] * 3 + [jnp.zeros((LANES - 3 * FOX_HEADS,), F32)]).reshape(1, LANES)

    pall = np.zeros((LANES, 2 * FOX_HEADS * HEAD_PAD), np.float32)
    cq = np.zeros((1, 2 * HEAD_PAD), np.float32)
    ck = np.zeros((1, 2 * HEAD_PAD), np.float32)
    for hd in range(FOX_HEADS):
        for term in range(3):
            pall[term * FOX_HEADS + hd, hd * HEAD_PAD + FQ_COL + term] = 1.0
            pall[term * FOX_HEADS + hd, FOX_HEADS * HEAD_PAD + hd * HEAD_PAD + FK_COL + term] = -1.0
    for j in range(2):
        cq[0, j * HEAD_PAD + FK_COL:j * HEAD_PAD + FK_COL + 3] = 1.0
        ck[0, j * HEAD_PAD + FQ_COL:j * HEAD_PAD + FQ_COL + 3] = 1.0
    w["pall"] = jnp.asarray(pall, BF16)
    w["cq"] = jnp.asarray(cq)
    w["ck"] = jnp.asarray(ck)

    w["wg"] = w_in[:, o_ga:o_gb + D_MODEL].astype(BF16)
    w["womla"] = w_o_mla.astype(BF16)
    w["wofox"] = w_o_fox.astype(BF16)
    w["wout"] = w_out.astype(BF16)
    wr = jnp.concatenate([w_router, jnp.zeros((D_MODEL, LANES - N_EXPERTS), F32)], 1)
    w["wrhi"] = wr.astype(BF16)
    w["wrlo"] = (wr - w["wrhi"].astype(F32)).astype(BF16)
    w["wsg"] = w_sh_gate.astype(BF16)
    w["wsu"] = w_sh_up.astype(BF16)
    w["wsd"] = w_sh_down.astype(BF16)
    return w


def _rope_freq_row():
    half = MLA_ROPE // 2
    inv = np.power(ROPE_THETA, -np.arange(half, dtype=np.float32) / half).astype(np.float32)
    row = np.zeros((1, LANES), np.float32)
    row[0, MLA_NOPE:MLA_NOPE + half] = inv
    row[0, MLA_NOPE + half:MLA_NOPE + 2 * half] = inv
    return jnp.asarray(row)


def kernel(x, c, positions, w_mod, b_mod, g_mix_norm, w_in, b_forget, g_q_lat, w_q_up, g_kv_lat, w_kv_up,
           w_o_mla, w_o_fox, w_out, g_ffn_norm, w_router, b_router, w_exp_gate, w_exp_up, w_exp_down,
           w_sh_gate, w_sh_up, w_sh_down, g_final):
    batch, seq, d = x.shape
    assert d == D_MODEL and w_mod.shape[0] == 1
    t = batch * seq
    tm = min(512, seq)
    bq = min(1024, seq)
    tr = min(512, t)
    tc = min(256, seq)
    assert seq % tm == 0 and seq % bq == 0 and t % tr == 0 and seq % tc == 0 and batch <= 8
    assert t % (SC_CHUNK * SC_WORKERS) == 0

    w = _prep_weights(w_in[0], b_forget[0], g_q_lat[0], w_q_up[0], g_kv_lat[0], w_kv_up[0], w_o_mla[0],
                      w_o_fox[0], w_out[0], w_router[0], w_sh_gate[0], w_sh_up[0], w_sh_down[0])

    c8 = jnp.zeros((8, D_MODEL), F32).at[:batch].set(c)
    mod = _mod(c8, w_mod[0], b_mod)
    mod8 = jnp.zeros((batch, 8, D_MODEL), F32).at[:, :N_MOD].set(mod[:batch].reshape(batch, N_MOD, D_MODEL))

    x2 = x.reshape(t, D_MODEL)
    fdec = _fox_decay(x2, mod8, g_mix_norm, w["wfl_hi"], w["wfl_lo"], w["bfl"], seq, tm)
    q_all, k_all, v_all, sg = _in_proj(x2, mod8, g_mix_norm, fdec, positions.reshape(t, 1),
                                       _rope_freq_row(), w, batch, seq, tm)
    o = _attention(q_all, k_all, v_all, bq)
    x1, hpk, logits = _out_proj(o.reshape(t, D_MODEL), sg, x2, mod8, g_ffn_norm, w, seq, tm)

    eidx, rank, gate, counts = _route(logits[:, :N_EXPERTS].T, b_router.reshape(N_EXPERTS, 1), tr)

    cnt = counts[:, 0]
    padded = (cnt + ROW_BLOCK - 1) // ROW_BLOCK * ROW_BLOCK
    pend = jnp.cumsum(padded)
    pstart = pend - padded
    n_blocks = t * TOP_K // ROW_BLOCK + N_EXPERTS
    n_rows = n_blocks * ROW_BLOCK
    block_row = jnp.arange(n_blocks, dtype=I32) * ROW_BLOCK
    block_e = jnp.minimum(jnp.sum(pend[None, :] <= block_row[:, None], axis=1), N_EXPERTS - 1).astype(I32)
    n_used = (pend[-1:] // ROW_BLOCK).astype(I32)
    block_valid = jnp.clip((pstart + cnt)[block_e] - block_row, 0, ROW_BLOCK).astype(I32)
    block_first = jnp.concatenate([jnp.ones((1,), I32), (block_e[1:] != block_e[:-1]).astype(I32)])

    dest = _dest(pstart.astype(I32), eidx, rank, tr)
    dest3 = dest.reshape(SLOTS, t // SC_CHUNK, SC_CHUNK).transpose(1, 0, 2)
    xs = _dispatch(hpk, dest3, n_rows)
    ys = _experts(block_e, block_valid, block_first, n_used, xs, w_exp_gate[0], w_exp_up[0], w_exp_down[0])
    yk = _gather_back(ys, dest3)
    out = _combine(yk, gate.T, hpk, x1, mod8, w, g_final.reshape(1, D_MODEL), seq, tc)
    return out.reshape(batch, seq, D_MODEL)
```

```python
import functools
import math

import numpy as np
import jax
import jax.numpy as jnp
from jax import lax
from jax.experimental import pallas as pl
from jax.experimental.pallas import tpu as pltpu
from jax.experimental.pallas import tpu_sc as plsc

F32 = jnp.float32
BF16 = jnp.bfloat16
I32 = jnp.int32
U32 = jnp.uint32

D_MODEL = 1024
MLA_HEADS = 8
MLA_Q_LORA = 256
MLA_KV_LORA = 128
MLA_NOPE = 64
MLA_ROPE = 32
MLA_V = 64
MLA_QK = MLA_NOPE + MLA_ROPE
ROPE_THETA = 10000.0
FOX_HEADS = 8
FOX_HEAD_DIM = 64
FOX_WIDTH = FOX_HEADS * FOX_HEAD_DIM
N_HEADS = MLA_HEADS + FOX_HEADS
N_EXPERTS = 64
N_GROUPS = 8
GROUP_SIZE = N_EXPERTS // N_GROUPS
TOPK_GROUPS = 4
TOP_K = 6
D_EXPERT = 256
ROUTED_SCALE = 2.5
N_MOD = 6
NORM_EPS = 1e-6
NEG_INF = -1e30
LOG2E = math.log2(math.e)

LANES = 128
HEAD_PAD = 128
ROW_BLOCK = 512
SLOTS = 8
SC_CORES = 2
SC_SUBCORES = 16
SC_WORKERS = SC_CORES * SC_SUBCORES
SC_CHUNK = 32
VMEM_LIMIT = 56 * 1024 * 1024

FQ_COL = FOX_HEAD_DIM
FK_COL = FOX_HEAD_DIM + 3


def _cparams(sem, vmem=VMEM_LIMIT):
    return pltpu.CompilerParams(dimension_semantics=sem, vmem_limit_bytes=vmem)


def _const_spec(shape):
    nd = len(shape)
    return pl.BlockSpec(shape, lambda *_: (0,) * nd)


def _rms(x):
    return x * lax.rsqrt(jnp.mean(x * x, axis=-1, keepdims=True) + NORM_EPS)


def _split3(x):
    hi = x.astype(BF16)
    r = x - hi.astype(F32)
    mid = r.astype(BF16)
    lo = (r - mid.astype(F32)).astype(BF16)
    return hi, mid, lo


def _dot(a, b):
    return jnp.dot(a, b, preferred_element_type=F32)


def _modulated_norm(x, gain, mod, shift_row, scale_row):
    shift = mod[shift_row:shift_row + 1, :]
    scale = mod[scale_row:scale_row + 1, :]
    return _rms(x) * gain * (1.0 + scale) + shift


def _mod_body(c_ref, w_ref, b_ref, o_ref):
    c = c_ref[...]
    cond = c * jax.nn.sigmoid(c)
    o_ref[...] = _dot(cond.astype(BF16), w_ref[...].astype(BF16)) + b_ref[...]


def _mod(c8, w_mod, b_mod):
    n = w_mod.shape[1]
    tn = D_MODEL
    return pl.pallas_call(
        _mod_body,
        out_shape=jax.ShapeDtypeStruct((8, n), F32),
        grid=(n // tn,),
        in_specs=[_const_spec((8, D_MODEL)),
                  pl.BlockSpec((D_MODEL, tn), lambda j: (0, j)),
                  pl.BlockSpec((1, tn), lambda j: (0, j))],
        out_specs=pl.BlockSpec((8, tn), lambda j: (0, j)),
        compiler_params=_cparams(("parallel",)),
        name="mod",
    )(c8, w_mod, b_mod)


def _decay_body(tiles_per_seq, x_ref, mod_ref, g_ref, whi_ref, wlo_ref, b_ref, o_ref, carry_ref):
    i = pl.program_id(0)

    @pl.when(i % tiles_per_seq == 0)
    def _():
        carry_ref[...] = jnp.zeros_like(carry_ref)

    h = _modulated_norm(x_ref[...], g_ref[...], mod_ref[0], 0, 1)
    hhi = h.astype(BF16)
    hlo = (h - hhi.astype(F32)).astype(BF16)
    z = _dot(hhi, whi_ref[...]) + _dot(hhi, wlo_ref[...]) + _dot(hlo, whi_ref[...]) + b_ref[...]
    logf = jnp.minimum(z, 0.0) - jnp.log1p(jnp.exp(-jnp.abs(z)))
    tm = logf.shape[0]
    row = lax.broadcasted_iota(I32, (tm, tm), 0)
    col = lax.broadcasted_iota(I32, (tm, tm), 1)
    tri = jnp.where(col <= row, 1.0, 0.0).astype(BF16)
    hi, mid, lo = _split3(logf)
    cum = _dot(tri, hi) + _dot(tri, mid) + _dot(tri, lo) + carry_ref[...]
    o_ref[...] = cum
    carry_ref[...] = cum[tm - 1:tm, :]


def _fox_decay(x2, mod8, g_mix, wfl_hi, wfl_lo, bfl, seq, tm):
    t = x2.shape[0]
    tps = seq // tm
    return pl.pallas_call(
        functools.partial(_decay_body, tps),
        out_shape=jax.ShapeDtypeStruct((t, LANES), F32),
        grid=(t // tm,),
        in_specs=[pl.BlockSpec((tm, D_MODEL), lambda i: (i, 0)),
                  pl.BlockSpec((1, 8, D_MODEL), lambda i: (i // tps, 0, 0)),
                  _const_spec((1, D_MODEL)),
                  _const_spec((D_MODEL, LANES)),
                  _const_spec((D_MODEL, LANES)),
                  _const_spec((1, LANES))],
        out_specs=pl.BlockSpec((tm, LANES), lambda i: (i, 0)),
        scratch_shapes=[pltpu.VMEM((1, LANES), F32)],
        compiler_params=_cparams(("arbitrary",)),
        name="fox_decay",
    )(x2, mod8, g_mix, wfl_hi, wfl_lo, bfl)


def _in_proj_body(x_ref, mod_ref, g_ref, f_ref, pos_ref, freq_ref,
                  wlat_ref, wkr_ref, gq_ref, gkv_ref, wqa_ref, wqb_ref, wka_ref, wva_ref,
                  wfq_ref, wfk_ref, wfv_ref, pall_ref, cq_ref, ck_ref, wg_ref,
                  q_ref, k_ref, v_ref, sg_ref):
    h = _modulated_norm(x_ref[...], g_ref[...], mod_ref[0], 0, 1)
    hb = h.astype(BF16)

    lat = _dot(hb, wlat_ref[...])
    qn = (_rms(lat[:, :MLA_Q_LORA]) * gq_ref[...]).astype(BF16)
    kvn = (_rms(lat[:, MLA_Q_LORA:]) * gkv_ref[...]).astype(BF16)
    ang = pos_ref[...].astype(F32) * freq_ref[...]
    cs = jnp.cos(ang)
    sn = jnp.sin(ang)
    kr = _dot(hb, wkr_ref[...])
    kpe = kr[:, :HEAD_PAD] * cs + kr[:, HEAD_PAD:] * sn
    qa = _dot(qn, wqa_ref[...])
    qb = _dot(qn, wqb_ref[...])
    ka = _dot(kvn, wka_ref[...])
    va = _dot(kvn, wva_ref[...]).astype(BF16)
    for hd in range(MLA_HEADS):
        sl = slice(hd * HEAD_PAD, (hd + 1) * HEAD_PAD)
        q_ref[0, hd] = (qa[:, sl] * cs + qb[:, sl] * sn).astype(BF16)
        k_ref[0, hd] = (ka[:, sl] + kpe).astype(BF16)
    for hp in range(MLA_HEADS // 2):
        v_ref[0, hp] = va[:, hp * 2 * MLA_V:(hp + 1) * 2 * MLA_V]

    hi, mid, lo = _split3(f_ref[...] * LOG2E)
    lane = lax.broadcasted_iota(I32, hi.shape, 1)
    f3 = jnp.where(lane < FOX_HEADS, hi, jnp.where(lane < 2 * FOX_HEADS, mid, lo))
    fp = _dot(f3, pall_ref[...])
    half = FOX_HEADS // 2 * 2 * HEAD_PAD
    for hp in range(FOX_HEADS // 2):
        sl = slice(hp * 2 * HEAD_PAD, (hp + 1) * 2 * HEAD_PAD)
        q2 = (_dot(hb, wfq_ref[hp]) + fp[:, sl] + cq_ref[...]).astype(BF16)
        k2 = (_dot(hb, wfk_ref[hp]) + fp[:, half + hp * 2 * HEAD_PAD:half + (hp + 1) * 2 * HEAD_PAD]
              + ck_ref[...]).astype(BF16)
        for j in range(2):
            hd = MLA_HEADS + 2 * hp + j
            q_ref[0, hd] = q2[:, j * HEAD_PAD:(j + 1) * HEAD_PAD]
            k_ref[0, hd] = k2[:, j * HEAD_PAD:(j + 1) * HEAD_PAD]
    vf = _dot(hb, wfv_ref[...]).astype(BF16)
    for hp in range(FOX_HEADS // 2):
        v_ref[0, MLA_HEADS // 2 + hp] = vf[:, hp * 2 * FOX_HEAD_DIM:(hp + 1) * 2 * FOX_HEAD_DIM]

    sg_ref[...] = jax.nn.sigmoid(_dot(hb, wg_ref[...])).astype(BF16)


def _in_proj(x2, mod8, g_mix, fdec, pos, freq, w, batch, seq, tm):
    t = x2.shape[0]
    tps = seq // tm
    consts = [w["wlat"], w["wkr"], w["gq"], w["gkv"], w["wqa"], w["wqb"], w["wka"], w["wva"],
              w["wfq"], w["wfk"], w["wfv"], w["pall"], w["cq"], w["ck"], w["wg"]]
    head_shape = jax.ShapeDtypeStruct((batch, N_HEADS, seq, HEAD_PAD), BF16)
    head_spec = pl.BlockSpec((1, N_HEADS, tm, HEAD_PAD), lambda i: (i // tps, 0, i % tps, 0))
    pair_shape = jax.ShapeDtypeStruct((batch, N_HEADS // 2, seq, 2 * MLA_V), BF16)
    pair_spec = pl.BlockSpec((1, N_HEADS // 2, tm, 2 * MLA_V), lambda i: (i // tps, 0, i % tps, 0))
    return pl.pallas_call(
        _in_proj_body,
        out_shape=(head_shape, head_shape, pair_shape,
                   jax.ShapeDtypeStruct((t, 2 * D_MODEL), BF16)),
        grid=(t // tm,),
        in_specs=[pl.BlockSpec((tm, D_MODEL), lambda i: (i, 0)),
                  pl.BlockSpec((1, 8, D_MODEL), lambda i: (i // tps, 0, 0)),
                  _const_spec((1, D_MODEL)),
                  pl.BlockSpec((tm, LANES), lambda i: (i, 0)),
                  pl.BlockSpec((tm, 1), lambda i: (i, 0)),
                  _const_spec((1, LANES))] + [_const_spec(a.shape) for a in consts],
        out_specs=(head_spec, head_spec, pair_spec,
                   pl.BlockSpec((tm, 2 * D_MODEL), lambda i: (i, 0))),
        compiler_params=_cparams(("parallel",)),
        name="in_proj",
    )(x2, mod8, g_mix, fdec, pos, freq, *consts)


def _attn_body(bq, sq, q_ref, k_ref, v_ref, o_ref, s_scr, p_scr, acc_scr):
    qi = pl.program_id(2)
    bk = sq
    n_heads = q_ref.shape[1]
    n_sub = bq // sq
    assert n_sub % 2 == 0
    chains = [(hh, u) for hh in range(n_heads) for u in range(n_sub)]
    n_chains = len(chains)
    n_main = qi * n_sub

    def chunk_start(j):
        return pl.multiple_of(jnp.maximum(j, 0) * bk, bk)

    def scores(c, j, par):
        hh, u = chains[c]
        k = k_ref[0, hh, pl.ds(chunk_start(j), bk), :]
        q = q_ref[0, hh, u * sq:(u + 1) * sq, :]
        s = lax.dot_general(k, q, (((1,), (1,)), ((), ())), preferred_element_type=F32)
        s_scr[par, c] = s
        return jnp.max(s, axis=0, keepdims=True)

    def values(c, j, par, alpha):
        hh = chains[c][0]
        v = v_ref[0, 0, pl.ds(chunk_start(j), bk), hh * MLA_V:(hh + 1) * MLA_V]
        pv = lax.dot_general(v, p_scr[par, c], (((0,), (0,)), ((), ())), preferred_element_type=F32)
        acc_scr[c] = alpha * acc_scr[c] + pv

    def softmax(c, par, m, l, smax, masked):
        s = s_scr[par, c]
        if masked:
            key = lax.broadcasted_iota(I32, (bk, sq), 0)
            qry = lax.broadcasted_iota(I32, (bk, sq), 1)
            s = jnp.where(key <= qry, s, NEG_INF)
            smax = jnp.max(s, axis=0, keepdims=True)
        m_new = jnp.maximum(m, smax)
        alpha = jnp.exp2(m - m_new)
        p = jnp.exp2(s - m_new)
        p_scr[par, c] = p.astype(BF16)
        return m_new, alpha * l + jnp.sum(p, axis=0, keepdims=True), alpha

    def stage(j, par, state, active, has_next, masked_of):
        nxt = {c: scores(c, j + 1, 1 - par) if has_next(c) else state[c][2] for c in active}
        for c in active:
            values(c, j - 1, 1 - par, state[c][3])
        out = list(state)
        for c in active:
            m, l, smax, _ = state[c]
            m_new, l_new, alpha = softmax(c, par, m, l, smax, masked_of(c))
            out[c] = (m_new, l_new, nxt[c], alpha)
        return out

    p_scr[1] = jnp.zeros(p_scr.shape[1:], BF16)
    acc_scr[...] = jnp.zeros_like(acc_scr)
    state = [(jnp.full((1, sq), NEG_INF, F32), jnp.zeros((1, sq), F32), scores(c, 0, 0), jnp.ones((1, sq), F32))
             for c in range(n_chains)]

    def step(jj, st):
        for par in range(2):
            st = stage(2 * jj + par, par, list(st), range(n_chains), lambda c: True, lambda c: False)
        return tuple(st)

    state = list(lax.fori_loop(0, n_main // 2, step, tuple(state)))

    for t in range(n_sub):
        active = [c for c, (_, u) in enumerate(chains) if u >= t]
        state = stage(n_main + t, t % 2, state, active, lambda c, t=t: chains[c][1] > t,
                      lambda c, t=t: chains[c][1] == t)

    heads = []
    for hh in range(n_heads):
        parts = []
        for u in range(n_sub):
            c = hh * n_sub + u
            values(c, n_main + u, u % 2, state[c][3])
            parts.append(acc_scr[c] / state[c][1])
        heads.append(jnp.concatenate(parts, axis=1))
    o_ref[0] = jnp.concatenate(heads, axis=0).T.astype(BF16)


def _attention(q_all, k_all, v_all, bq):
    batch, _, seq, _ = q_all.shape
    sq = min(256, bq)
    n_chains = 2 * (bq // sq)
    return pl.pallas_call(
        functools.partial(_attn_body, bq, sq),
        out_shape=jax.ShapeDtypeStruct((batch, seq, N_HEADS * MLA_V), BF16),
        grid=(batch, N_HEADS // 2, seq // bq),
        in_specs=[pl.BlockSpec((1, 2, bq, HEAD_PAD), lambda b, hp, qi: (b, hp, qi, 0)),
                  pl.BlockSpec((1, 2, seq, HEAD_PAD), lambda b, hp, qi: (b, hp, 0, 0)),
                  pl.BlockSpec((1, 1, seq, 2 * MLA_V), lambda b, hp, qi: (b, hp, 0, 0))],
        out_specs=pl.BlockSpec((1, bq, 2 * MLA_V), lambda b, hp, qi: (b, qi, hp)),
        scratch_shapes=[pltpu.VMEM((2, n_chains, sq, sq), F32),
                        pltpu.VMEM((2, n_chains, sq, sq), BF16),
                        pltpu.VMEM((n_chains, MLA_V, sq), F32)],
        compiler_params=_cparams(("parallel", "parallel", "arbitrary")),
        name="attention",
    )(q_all, k_all, v_all)


def _pack_halves(x):
    w = x.shape[1] // 2
    a = lax.bitcast_convert_type(x[:, :w].astype(BF16).astype(F32), U32)
    b = lax.bitcast_convert_type(x[:, w:].astype(BF16).astype(F32), U32)
    return a | (b >> 16)


def _unpack_halves(p):
    a = lax.bitcast_convert_type(p & jnp.uint32(0xFFFF0000), F32)
    b = lax.bitcast_convert_type(p << 16, F32)
    return a, b


def _out_proj_body(o_ref, sg_ref, x_ref, mod_ref, gffn_ref, womla_ref, wofox_ref, wout_ref,
                   wrhi_ref, wrlo_ref, x1_ref, hp_ref, lg_ref):
    o = o_ref[...]
    half = o.shape[1] // 2
    mo = _dot(o[:, :half], womla_ref[...])
    fo = _dot(o[:, half:], wofox_ref[...])
    sg = sg_ref[...]
    merged = sg[:, :D_MODEL].astype(F32) * mo + sg[:, D_MODEL:].astype(F32) * fo
    mix = _dot(merged.astype(BF16), wout_ref[...])
    mod = mod_ref[0]
    x1 = x_ref[...] + mod[2:3, :] * mix
    x1_ref[...] = x1
    h2 = _modulated_norm(x1, gffn_ref[...], mod, 3, 4)
    hhi = h2.astype(BF16)
    hlo = (h2 - hhi.astype(F32)).astype(BF16)
    lg_ref[...] = _dot(hhi, wrhi_ref[...]) + _dot(hhi, wrlo_ref[...]) + _dot(hlo, wrhi_ref[...])
    hp_ref[...] = _pack_halves(h2)


def _out_proj(o2, sg, x2, mod8, g_ffn, w, seq, tm):
    t = x2.shape[0]
    tps = seq // tm
    consts = [w["womla"], w["wofox"], w["wout"], w["wrhi"], w["wrlo"]]
    return pl.pallas_call(
        _out_proj_body,
        out_shape=(jax.ShapeDtypeStruct((t, D_MODEL), F32),
                   jax.ShapeDtypeStruct((t, D_MODEL // 2), U32),
                   jax.ShapeDtypeStruct((t, LANES), F32)),
        grid=(t // tm,),
        in_specs=[pl.BlockSpec((tm, D_MODEL), lambda i: (i, 0)),
                  pl.BlockSpec((tm, 2 * D_MODEL), lambda i: (i, 0)),
                  pl.BlockSpec((tm, D_MODEL), lambda i: (i, 0)),
                  pl.BlockSpec((1, 8, D_MODEL), lambda i: (i // tps, 0, 0)),
                  _const_spec((1, D_MODEL))] + [_const_spec(a.shape) for a in consts],
        out_specs=(pl.BlockSpec((tm, D_MODEL), lambda i: (i, 0)),
                   pl.BlockSpec((tm, D_MODEL // 2), lambda i: (i, 0)),
                   pl.BlockSpec((tm, LANES), lambda i: (i, 0))),
        compiler_params=_cparams(("parallel",)),
        name="out_proj",
    )(o2, sg, x2, mod8, g_ffn, *consts)


def _route_body(lt_ref, b_ref, eidx_ref, rank_ref, gate_ref, cnt_ref, carry_ref):
    i = pl.program_id(0)

    @pl.when(i == 0)
    def _():
        carry_ref[...] = jnp.zeros_like(carry_ref)

    s = jax.nn.sigmoid(lt_ref[...])
    c = s + b_ref[...]
    tr = s.shape[1]
    sub = lax.broadcasted_iota(I32, (GROUP_SIZE, tr), 0).astype(F32)

    gs = []
    for g in range(N_GROUPS):
        cg = c[g * GROUP_SIZE:(g + 1) * GROUP_SIZE, :]
        m1 = jnp.max(cg, axis=0, keepdims=True)
        i1 = jnp.min(jnp.where(cg == m1, sub, float(GROUP_SIZE)), axis=0, keepdims=True)
        m2 = jnp.max(jnp.where(sub == i1, NEG_INF, cg), axis=0, keepdims=True)
        gs.append(m1 + m2)

    masked = []
    for g in range(N_GROUPS):
        beats = jnp.zeros_like(gs[g])
        for o in range(N_GROUPS):
            if o == g:
                continue
            better = (gs[o] >= gs[g]) if o < g else (gs[o] > gs[g])
            beats = beats + jnp.where(better, 1.0, 0.0)
        keep = beats < float(TOPK_GROUPS)
        cg = c[g * GROUP_SIZE:(g + 1) * GROUP_SIZE, :]
        masked.append(jnp.where(keep, cg, NEG_INF))
    mc = jnp.concatenate(masked, axis=0)

    eio = lax.broadcasted_iota(I32, (N_EXPERTS, tr), 0).astype(F32)
    picks = []
    selected = jnp.zeros((N_EXPERTS, tr), F32)
    for _ in range(TOP_K):
        m = jnp.max(mc, axis=0, keepdims=True)
        idx = jnp.min(jnp.where(mc == m, eio, float(N_EXPERTS)), axis=0, keepdims=True)
        hit = eio == idx
        picks.append(idx)
        selected = jnp.where(hit, 1.0, selected)
        mc = jnp.where(hit, -3.0e38, mc)

    ssum = jnp.sum(selected * s, axis=0, keepdims=True)
    gate_full = selected * s / ssum * ROUTED_SCALE

    r_io = lax.broadcasted_iota(I32, (tr, tr), 0)
    c_io = lax.broadcasted_iota(I32, (tr, tr), 1)
    upper = jnp.where(r_io < c_io, 1.0, 0.0).astype(BF16)
    before = _dot(selected.astype(BF16), upper) + carry_ref[...]
    carry_new = carry_ref[...] + jnp.sum(selected, axis=1, keepdims=True)
    carry_ref[...] = carry_new
    cnt_ref[...] = jnp.broadcast_to(carry_new, cnt_ref.shape).astype(I32)

    for r in range(SLOTS):
        if r < TOP_K:
            hit = eio == picks[r]
            eidx_ref[r:r + 1, :] = picks[r].astype(I32)
            rank_ref[r:r + 1, :] = jnp.sum(jnp.where(hit, before, 0.0), axis=0, keepdims=True).astype(I32)
            gate_ref[r:r + 1, :] = jnp.sum(jnp.where(hit, gate_full, 0.0), axis=0, keepdims=True)
        else:
            eidx_ref[r:r + 1, :] = jnp.zeros((1, tr), I32)
            rank_ref[r:r + 1, :] = jnp.zeros((1, tr), I32)
            gate_ref[r:r + 1, :] = jnp.zeros((1, tr), F32)


def _route(logits_t, bias_col, tr):
    t = logits_t.shape[1]
    slot_i = jax.ShapeDtypeStruct((SLOTS, t), I32)
    slot_spec = pl.BlockSpec((SLOTS, tr), lambda i: (0, i))
    return pl.pallas_call(
        _route_body,
        out_shape=(slot_i, slot_i, jax.ShapeDtypeStruct((SLOTS, t), F32),
                   jax.ShapeDtypeStruct((N_EXPERTS, LANES), I32)),
        grid=(t // tr,),
        in_specs=[pl.BlockSpec((N_EXPERTS, tr), lambda i: (0, i)),
                  _const_spec((N_EXPERTS, 1))],
        out_specs=(slot_spec, slot_spec, slot_spec, _const_spec((N_EXPERTS, LANES))),
        scratch_shapes=[pltpu.VMEM((N_EXPERTS, 1), F32)],
        compiler_params=_cparams(("arbitrary",)),
        name="route",
    )(logits_t, bias_col)


def _dest_body(pstart_ref, eidx_ref, rank_ref, o_ref):
    e = eidx_ref[...]
    d = rank_ref[...]
    for j in range(N_EXPERTS):
        d = d + jnp.where(e == j, pstart_ref[j], 0)
    o_ref[...] = d


def _dest(pstart, eidx, rank, tr):
    t = eidx.shape[1]
    spec = pl.BlockSpec((SLOTS, tr), lambda i, ps: (0, i))
    return pl.pallas_call(
        _dest_body,
        out_shape=jax.ShapeDtypeStruct((SLOTS, t), I32),
        grid_spec=pltpu.PrefetchScalarGridSpec(
            num_scalar_prefetch=1, grid=(t // tr,), in_specs=[spec, spec], out_specs=spec),
        compiler_params=_cparams(("parallel",)),
        name="dest",
    )(pstart, eidx, rank)


def _sc_mesh():
    return plsc.VectorSubcoreMesh(core_axis_name="c", subcore_axis_name="s")


def _sc_worker_id():
    return lax.axis_index("s") * SC_CORES + lax.axis_index("c")


def _dispatch(hpk, dest3, n_rows):
    _, w = hpk.shape
    n_chunks, _, c = dest3.shape
    per_worker = n_chunks // SC_WORKERS

    def body(h_hbm, d_hbm, xs_hbm, idx_v, rows_v, sem):
        wid = _sc_worker_id()

        @pl.loop(0, per_worker)
        def _(i):
            ch = wid * per_worker + i
            pltpu.sync_copy(d_hbm.at[ch], idx_v)
            pltpu.sync_copy(h_hbm.at[pl.ds(ch * c, c)], rows_v)
            copies = [pltpu.async_copy(rows_v, xs_hbm.at[idx_v.at[k]], sem) for k in range(TOP_K)]
            for cp in copies:
                cp.wait()

    return pl.kernel(
        body, mesh=_sc_mesh(),
        out_type=jax.ShapeDtypeStruct((n_rows, w), U32),
        scratch_types=[pltpu.VMEM((SLOTS, c), I32), pltpu.VMEM((c, w), U32), pltpu.SemaphoreType.DMA],
        name="dispatch",
    )(hpk, dest3)


def _experts_body(n_sub, be_ref, nv_ref, first_ref, nu_ref, xs_ref, wg_ref, wu_ref, wd_ref, ys_ref,
                  wgb_ref, wub_ref, wdb_ref):
    del be_ref
    i = pl.program_id(0)

    @pl.when(first_ref[i] == 1)
    def _():
        wgb_ref[...] = wg_ref[0].astype(BF16)
        wub_ref[...] = wu_ref[0].astype(BF16)
        wdb_ref[...] = wd_ref[0].astype(BF16)

    @pl.when(i < nu_ref[0])
    def _():
        half = D_MODEL // 2
        sub = ROW_BLOCK // n_sub
        row = lax.broadcasted_iota(I32, (sub, xs_ref.shape[1]), 0)
        gu = []
        for s in range(n_sub):
            x = jnp.where(row + s * sub < nv_ref[i], xs_ref[s * sub:(s + 1) * sub, :], jnp.uint32(0))
            xa, xb = _unpack_halves(x)
            xa = xa.astype(BF16)
            xb = xb.astype(BF16)
            g = _dot(xa, wgb_ref[:half, :]) + _dot(xb, wgb_ref[half:, :])
            u = _dot(xa, wub_ref[:half, :]) + _dot(xb, wub_ref[half:, :])
            gu.append((g, u))
        for s, (g, u) in enumerate(gu):
            hb = (g * jax.nn.sigmoid(g) * u).astype(BF16)
            ys_ref[s * sub:(s + 1) * sub, :] = _pack_halves(_dot(hb, wdb_ref[...]))

    @pl.when(i >= nu_ref[0])
    def _():
        ys_ref[...] = jnp.zeros_like(ys_ref)


def _experts(block_e, block_valid, block_first, n_used, xs, wg, wu, wd):
    n_rows, w = xs.shape
    n_blocks = n_rows // ROW_BLOCK

    def row_map(i, be, nv, bf, nu):
        return (jnp.minimum(i, nu[0] - 1), 0)

    def w_map(i, be, nv, bf, nu):
        return (be[i], 0, 0)

    return pl.pallas_call(
        functools.partial(_experts_body, ROW_BLOCK // 256),
        out_shape=jax.ShapeDtypeStruct((n_rows, w), U32),
        grid_spec=pltpu.PrefetchScalarGridSpec(
            num_scalar_prefetch=4, grid=(n_blocks,),
            in_specs=[pl.BlockSpec((ROW_BLOCK, w), row_map),
                      pl.BlockSpec((1, D_MODEL, D_EXPERT), w_map),
                      pl.BlockSpec((1, D_MODEL, D_EXPERT), w_map),
                      pl.BlockSpec((1, D_EXPERT, D_MODEL), w_map)],
            out_specs=pl.BlockSpec((ROW_BLOCK, w), lambda i, be, nv, bf, nu: (i, 0)),
            scratch_shapes=[pltpu.VMEM((D_MODEL, D_EXPERT), BF16), pltpu.VMEM((D_MODEL, D_EXPERT), BF16),
                            pltpu.VMEM((D_EXPERT, D_MODEL), BF16)]),
        compiler_params=_cparams(("arbitrary",)),
        name="experts",
    )(block_e, block_valid, block_first, n_used, xs, wg, wu, wd)


def _gather_back(ys, dest3):
    _, w = ys.shape
    n_chunks, _, c = dest3.shape
    per_worker = n_chunks // SC_WORKERS

    def body(ys_hbm, d_hbm, yk_hbm, idx_v, rows_v, gsem, osem):
        wid = _sc_worker_id()

        @pl.loop(0, per_worker)
        def _(i):
            ch = wid * per_worker + i
            pltpu.sync_copy(d_hbm.at[ch], idx_v)
            gathers = [pltpu.async_copy(ys_hbm.at[idx_v.at[k]], rows_v.at[k], gsem) for k in range(TOP_K)]
            for g in gathers:
                g.wait()
            outs = [pltpu.async_copy(rows_v.at[k], yk_hbm.at[k, pl.ds(ch * c, c)], osem) for k in range(TOP_K)]
            for o in outs:
                o.wait()

    return pl.kernel(
        body, mesh=_sc_mesh(),
        out_type=jax.ShapeDtypeStruct((TOP_K, n_chunks * c, w), U32),
        scratch_types=[pltpu.VMEM((SLOTS, c), I32), pltpu.VMEM((TOP_K, c, w), U32),
                       pltpu.SemaphoreType.DMA, pltpu.SemaphoreType.DMA],
        name="gather_back",
    )(ys, dest3)


def _combine_body(yk_ref, gate_ref, hp_ref, x1_ref, mod_ref, wsg_ref, wsu_ref, wsd_ref, gfin_ref, o_ref):
    ha, hb = _unpack_halves(hp_ref[...])
    ha = ha.astype(BF16)
    hb = hb.astype(BF16)
    half = D_MODEL // 2
    g = _dot(ha, wsg_ref[:half, :]) + _dot(hb, wsg_ref[half:, :])
    u = _dot(ha, wsu_ref[:half, :]) + _dot(hb, wsu_ref[half:, :])
    shared = _dot((g * jax.nn.sigmoid(g) * u).astype(BF16), wsd_ref[...])

    gate = gate_ref[...]
    ra = jnp.zeros(ha.shape, F32)
    rb = jnp.zeros(ha.shape, F32)
    for k in range(TOP_K):
        ya, yb = _unpack_halves(yk_ref[k])
        gk = gate[:, k:k + 1]
        ra = ra + gk * ya
        rb = rb + gk * yb
    moe = shared + jnp.concatenate([ra, rb], axis=1)
    mod = mod_ref[0]
    x2 = x1_ref[...] + mod[5:6, :] * moe
    o_ref[...] = _rms(x2) * gfin_ref[...]


def _combine(yk, gate_t, hpk, x1, mod8, w, g_final, seq, tq):
    t, wd = hpk.shape
    tps = seq // tq
    return pl.pallas_call(
        _combine_body,
        out_shape=jax.ShapeDtypeStruct((t, D_MODEL), F32),
        grid=(t // tq,),
        in_specs=[pl.BlockSpec((TOP_K, tq, wd), lambda i: (0, i, 0)),
                  pl.BlockSpec((tq, SLOTS), lambda i: (i, 0)),
                  pl.BlockSpec((tq, wd), lambda i: (i, 0)),
                  pl.BlockSpec((tq, D_MODEL), lambda i: (i, 0)),
                  pl.BlockSpec((1, 8, D_MODEL), lambda i: (i // tps, 0, 0)),
                  _const_spec(w["wsg"].shape), _const_spec(w["wsu"].shape), _const_spec(w["wsd"].shape),
                  _const_spec((1, D_MODEL))],
        out_specs=pl.BlockSpec((tq, D_MODEL), lambda i: (i, 0)),
        compiler_params=_cparams(("parallel",)),
        name="combine",
    )(yk, gate_t, hpk, x1, mod8, w["wsg"], w["wsu"], w["wsd"], g_final)


def _prep_weights(w_in, b_forget, g_q_lat, w_q_up, g_kv_lat, w_kv_up, w_o_mla, w_o_fox, w_out,
                  w_router, w_sh_gate, w_sh_up, w_sh_down):
    o_q, o_kv, o_kr = 0, MLA_Q_LORA, MLA_Q_LORA + MLA_KV_LORA
    o_fq = o_kr + MLA_ROPE
    o_fk, o_fv = o_fq + FOX_WIDTH, o_fq + 2 * FOX_WIDTH
    o_fl = o_fq + 3 * FOX_WIDTH
    o_ga = o_fl + FOX_HEADS
    o_gb = o_ga + D_MODEL
    w = {}
    w["wlat"] = w_in[:, o_q:o_kr].astype(BF16)
    half = MLA_ROPE // 2

    def rope_pair(cols):
        x1, x2 = cols[..., :half], cols[..., half:]
        z = jnp.zeros(cols.shape[:-1] + (MLA_NOPE,), cols.dtype)
        zt = jnp.zeros(cols.shape[:-1] + (HEAD_PAD - MLA_QK,), cols.dtype)
        plain = jnp.concatenate([z, x1, x2, zt], axis=-1)
        rot = jnp.concatenate([z, -x2, x1, zt], axis=-1)
        return plain, rot

    kr_plain, kr_rot = rope_pair(w_in[:, o_kr:o_fq])
    w["wkr"] = jnp.concatenate([kr_plain, kr_rot], axis=1).astype(BF16)
    w["gq"] = g_q_lat.reshape(1, -1)
    w["gkv"] = g_kv_lat.reshape(1, -1)

    scale_a = LOG2E / math.sqrt(MLA_QK)
    wq = (w_q_up * scale_a).reshape(MLA_Q_LORA, MLA_HEADS, MLA_QK)
    q_plain, q_rot = rope_pair(wq[..., MLA_NOPE:])
    nope = jnp.concatenate([wq[..., :MLA_NOPE], jnp.zeros((MLA_Q_LORA, MLA_HEADS, HEAD_PAD - MLA_NOPE), F32)], -1)
    w["wqa"] = (nope + q_plain).reshape(MLA_Q_LORA, -1).astype(BF16)
    w["wqb"] = q_rot.reshape(MLA_Q_LORA, -1).astype(BF16)

    wkv = w_kv_up.reshape(MLA_KV_LORA, MLA_HEADS, MLA_NOPE + MLA_V)
    zpad = jnp.zeros((MLA_KV_LORA, MLA_HEADS, HEAD_PAD - MLA_NOPE), F32)
    w["wka"] = jnp.concatenate([wkv[..., :MLA_NOPE], zpad], -1).reshape(MLA_KV_LORA, -1).astype(BF16)

    w["wva"] = wkv[..., MLA_NOPE:].reshape(MLA_KV_LORA, -1).astype(BF16)

    def fox_heads(cols, scale):
        v = (cols * scale).reshape(D_MODEL, FOX_HEADS, FOX_HEAD_DIM)
        v = jnp.concatenate([v, jnp.zeros((D_MODEL, FOX_HEADS, HEAD_PAD - FOX_HEAD_DIM), F32)], -1)
        return v.reshape(D_MODEL, FOX_HEADS // 2, 2 * HEAD_PAD).transpose(1, 0, 2).astype(BF16)

    w["wfq"] = fox_heads(w_in[:, o_fq:o_fk], LOG2E / math.sqrt(FOX_HEAD_DIM))
    w["wfk"] = fox_heads(w_in[:, o_fk:o_fv], 1.0)
    w["wfv"] = w_in[:, o_fv:o_fl].astype(BF16)

    wfl = jnp.concatenate([w_in[:, o_fl:o_ga]] * 3 + [jnp.zeros((D_MODEL, LANES - 3 * FOX_HEADS), F32)], 1)
    w["wfl_hi"] = wfl.astype(BF16)
    w["wfl_lo"] = (wfl - w["wfl_hi"].astype(F32)).astype(BF16)
    w["bfl"] = jnp.concatenate([b_forget] * 3 + [jnp.zeros((LANES - 3 * FOX_HEADS,), F32)]).reshape(1, LANES)

    pall = np.zeros((LANES, 2 * FOX_HEADS * HEAD_PAD), np.float32)
    cq = np.zeros((1, 2 * HEAD_PAD), np.float32)
    ck = np.zeros((1, 2 * HEAD_PAD), np.float32)
    for hd in range(FOX_HEADS):
        for term in range(3):
            pall[term * FOX_HEADS + hd, hd * HEAD_PAD + FQ_COL + term] = 1.0
            pall[term * FOX_HEADS + hd, FOX_HEADS * HEAD_PAD + hd * HEAD_PAD + FK_COL + term] = -1.0
    for j in range(2):
        cq[0, j * HEAD_PAD + FK_COL:j * HEAD_PAD + FK_COL + 3] = 1.0
        ck[0, j * HEAD_PAD + FQ_COL:j * HEAD_PAD + FQ_COL + 3] = 1.0
    w["pall"] = jnp.asarray(pall, BF16)
    w["cq"] = jnp.asarray(cq)
    w["ck"] = jnp.asarray(ck)

    w["wg"] = w_in[:, o_ga:o_gb + D_MODEL].astype(BF16)
    w["womla"] = w_o_mla.astype(BF16)
    w["wofox"] = w_o_fox.astype(BF16)
    w["wout"] = w_out.astype(BF16)
    wr = jnp.concatenate([w_router, jnp.zeros((D_MODEL, LANES - N_EXPERTS), F32)], 1)
    w["wrhi"] = wr.astype(BF16)
    w["wrlo"] = (wr - w["wrhi"].astype(F32)).astype(BF16)
    w["wsg"] = w_sh_gate.astype(BF16)
    w["wsu"] = w_sh_up.astype(BF16)
    w["wsd"] = w_sh_down.astype(BF16)
    return w


def _rope_freq_row():
    half = MLA_ROPE // 2
    inv = np.power(ROPE_THETA, -np.arange(half, dtype=np.float32) / half).astype(np.float32)
    row = np.zeros((1, LANES), np.float32)
    row[0, MLA_NOPE:MLA_NOPE + half] = inv
    row[0, MLA_NOPE + half:MLA_NOPE + 2 * half] = inv
    return jnp.asarray(row)


def kernel(x, c, positions, w_mod, b_mod, g_mix_norm, w_in, b_forget, g_q_lat, w_q_up, g_kv_lat, w_kv_up,
           w_o_mla, w_o_fox, w_out, g_ffn_norm, w_router, b_router, w_exp_gate, w_exp_up, w_exp_down,
           w_sh_gate, w_sh_up, w_sh_down, g_final):
    batch, seq, d = x.shape
    assert d == D_MODEL and w_mod.shape[0] == 1
    t = batch * seq
    tm = min(512, seq)
    bq = min(1024, seq)
    tr = min(512, t)
    tc = min(256, seq)
    assert seq % tm == 0 and seq % bq == 0 and t % tr == 0 and seq % tc == 0 and batch <= 8
    assert t % (SC_CHUNK * SC_WORKERS) == 0

    w = _prep_weights(w_in[0], b_forget[0], g_q_lat[0], w_q_up[0], g_kv_lat[0], w_kv_up[0], w_o_mla[0],
                      w_o_fox[0], w_out[0], w_router[0], w_sh_gate[0], w_sh_up[0], w_sh_down[0])

    c8 = jnp.zeros((8, D_MODEL), F32).at[:batch].set(c)
    mod = _mod(c8, w_mod[0], b_mod)
    mod8 = jnp.zeros((batch, 8, D_MODEL), F32).at[:, :N_MOD].set(mod[:batch].reshape(batch, N_MOD, D_MODEL))

    x2 = x.reshape(t, D_MODEL)
    fdec = _fox_decay(x2, mod8, g_mix_norm, w["wfl_hi"], w["wfl_lo"], w["bfl"], seq, tm)
    q_all, k_all, v_all, sg = _in_proj(x2, mod8, g_mix_norm, fdec, positions.reshape(t, 1),
                                       _rope_freq_row(), w, batch, seq, tm)
    o = _attention(q_all, k_all, v_all, bq)
    x1, hpk, logits = _out_proj(o.reshape(t, D_MODEL), sg, x2, mod8, g_ffn_norm, w, seq, tm)

    eidx, rank, gate, counts = _route(logits[:, :N_EXPERTS].T, b_router.reshape(N_EXPERTS, 1), tr)

    cnt = counts[:, 0]
    padded = (cnt + ROW_BLOCK - 1) // ROW_BLOCK * ROW_BLOCK
    pend = jnp.cumsum(padded)
    pstart = pend - padded
    n_blocks = t * TOP_K // ROW_BLOCK + N_EXPERTS
    n_rows = n_blocks * ROW_BLOCK
    block_row = jnp.arange(n_blocks, dtype=I32) * ROW_BLOCK
    block_e = jnp.minimum(jnp.sum(pend[None, :] <= block_row[:, None], axis=1), N_EXPERTS - 1).astype(I32)
    n_used = (pend[-1:] // ROW_BLOCK).astype(I32)
    block_valid = jnp.clip((pstart + cnt)[block_e] - block_row, 0, ROW_BLOCK).astype(I32)
    block_first = jnp.concatenate([jnp.ones((1,), I32), (block_e[1:] != block_e[:-1]).astype(I32)])

    dest = _dest(pstart.astype(I32), eidx, rank, tr)
    dest3 = dest.reshape(SLOTS, t // SC_CHUNK, SC_CHUNK).transpose(1, 0, 2)
    xs = _dispatch(hpk, dest3, n_rows)
    ys = _experts(block_e, block_valid, block_first, n_used, xs, w_exp_gate[0], w_exp_up[0], w_exp_down[0])
    yk = _gather_back(ys, dest3)
    out = _combine(yk, gate.T, hpk, x1, mod8, w, g_final.reshape(1, D_MODEL), seq, tc)
    return out.reshape(batch, seq, D_MODEL)
```

```python
import functools
import math

import numpy as np
import jax
import jax.numpy as jnp
from jax import lax
from jax.experimental import pallas as pl
from jax.experimental.pallas import tpu as pltpu
from jax.experimental.pallas import tpu_sc as plsc

F32 = jnp.float32
BF16 = jnp.bfloat16
I32 = jnp.int32
U32 = jnp.uint32

D_MODEL = 1024
MLA_HEADS = 8
MLA_Q_LORA = 256
MLA_KV_LORA = 128
MLA_NOPE = 64
MLA_ROPE = 32
MLA_V = 64
MLA_QK = MLA_NOPE + MLA_ROPE
ROPE_THETA = 10000.0
FOX_HEADS = 8
FOX_HEAD_DIM = 64
FOX_WIDTH = FOX_HEADS * FOX_HEAD_DIM
N_HEADS = MLA_HEADS + FOX_HEADS
N_EXPERTS = 64
N_GROUPS = 8
GROUP_SIZE = N_EXPERTS // N_GROUPS
TOPK_GROUPS = 4
TOP_K = 6
D_EXPERT = 256
ROUTED_SCALE = 2.5
N_MOD = 6
NORM_EPS = 1e-6
NEG_INF = -1e30
LOG2E = math.log2(math.e)

LANES = 128
HEAD_PAD = 128
ROW_BLOCK = 512
SLOTS = 8
SC_CORES = 2
SC_SUBCORES = 16
SC_WORKERS = SC_CORES * SC_SUBCORES
SC_CHUNK = 32
N_PARTS = 2
VMEM_LIMIT = 56 * 1024 * 1024

FQ_COL = FOX_HEAD_DIM
FK_COL = FOX_HEAD_DIM + 3


def _cparams(sem, vmem=VMEM_LIMIT):
    return pltpu.CompilerParams(dimension_semantics=sem, vmem_limit_bytes=vmem)


def _const_spec(shape):
    nd = len(shape)
    return pl.BlockSpec(shape, lambda *_: (0,) * nd)


def _rms(x):
    return x * lax.rsqrt(jnp.mean(x * x, axis=-1, keepdims=True) + NORM_EPS)


def _split3(x):
    hi = x.astype(BF16)
    r = x - hi.astype(F32)
    mid = r.astype(BF16)
    lo = (r - mid.astype(F32)).astype(BF16)
    return hi, mid, lo


def _dot(a, b):
    return jnp.dot(a, b, preferred_element_type=F32)


def _modulated_norm(x, gain, mod, shift_row, scale_row):
    shift = mod[shift_row:shift_row + 1, :]
    scale = mod[scale_row:scale_row + 1, :]
    return _rms(x) * gain * (1.0 + scale) + shift


def _mod_body(c_ref, w_ref, b_ref, o_ref):
    c = c_ref[...]
    cond = c * jax.nn.sigmoid(c)
    o_ref[...] = _dot(cond.astype(BF16), w_ref[...].astype(BF16)) + b_ref[...]


def _mod(c8, w_mod, b_mod):
    n = w_mod.shape[1]
    tn = D_MODEL
    return pl.pallas_call(
        _mod_body,
        out_shape=jax.ShapeDtypeStruct((8, n), F32),
        grid=(n // tn,),
        in_specs=[_const_spec((8, D_MODEL)),
                  pl.BlockSpec((D_MODEL, tn), lambda j: (0, j)),
                  pl.BlockSpec((1, tn), lambda j: (0, j))],
        out_specs=pl.BlockSpec((8, tn), lambda j: (0, j)),
        compiler_params=_cparams(("parallel",)),
        name="mod",
    )(c8, w_mod, b_mod)


def _decay_body(tiles_per_seq, x_ref, mod_ref, g_ref, whi_ref, wlo_ref, b_ref, o_ref, carry_ref):
    i = pl.program_id(0)

    @pl.when(i % tiles_per_seq == 0)
    def _():
        carry_ref[...] = jnp.zeros_like(carry_ref)

    h = _modulated_norm(x_ref[...], g_ref[...], mod_ref[0], 0, 1)
    hhi = h.astype(BF16)
    hlo = (h - hhi.astype(F32)).astype(BF16)
    z = _dot(hhi, whi_ref[...]) + _dot(hhi, wlo_ref[...]) + _dot(hlo, whi_ref[...]) + b_ref[...]
    logf = jnp.minimum(z, 0.0) - jnp.log1p(jnp.exp(-jnp.abs(z)))
    tm = logf.shape[0]
    row = lax.broadcasted_iota(I32, (tm, tm), 0)
    col = lax.broadcasted_iota(I32, (tm, tm), 1)
    tri = jnp.where(col <= row, 1.0, 0.0).astype(BF16)
    hi, mid, lo = _split3(logf)
    cum = _dot(tri, hi) + _dot(tri, mid) + _dot(tri, lo) + carry_ref[...]
    o_ref[...] = cum
    carry_ref[...] = cum[tm - 1:tm, :]


def _fox_decay(x2, mod8, g_mix, wfl_hi, wfl_lo, bfl, seq, tm):
    t = x2.shape[0]
    tps = seq // tm
    return pl.pallas_call(
        functools.partial(_decay_body, tps),
        out_shape=jax.ShapeDtypeStruct((t, LANES), F32),
        grid=(t // tm,),
        in_specs=[pl.BlockSpec((tm, D_MODEL), lambda i: (i, 0)),
                  pl.BlockSpec((1, 8, D_MODEL), lambda i: (i // tps, 0, 0)),
                  _const_spec((1, D_MODEL)),
                  _const_spec((D_MODEL, LANES)),
                  _const_spec((D_MODEL, LANES)),
                  _const_spec((1, LANES))],
        out_specs=pl.BlockSpec((tm, LANES), lambda i: (i, 0)),
        scratch_shapes=[pltpu.VMEM((1, LANES), F32)],
        compiler_params=_cparams(("arbitrary",)),
        name="fox_decay",
    )(x2, mod8, g_mix, wfl_hi, wfl_lo, bfl)


def _in_proj_body(x_ref, mod_ref, g_ref, f_ref, pos_ref, freq_ref,
                  wlat_ref, wkr_ref, gq_ref, gkv_ref, wqa_ref, wqb_ref, wka_ref, wva_ref,
                  wfq_ref, wfk_ref, wfv_ref, pall_ref, cq_ref, ck_ref, wg_ref,
                  q_ref, k_ref, v_ref, sg_ref):
    h = _modulated_norm(x_ref[...], g_ref[...], mod_ref[0], 0, 1)
    hb = h.astype(BF16)

    lat = _dot(hb, wlat_ref[...])
    qn = (_rms(lat[:, :MLA_Q_LORA]) * gq_ref[...]).astype(BF16)
    kvn = (_rms(lat[:, MLA_Q_LORA:]) * gkv_ref[...]).astype(BF16)
    ang = pos_ref[...].astype(F32) * freq_ref[...]
    cs = jnp.cos(ang)
    sn = jnp.sin(ang)
    kr = _dot(hb, wkr_ref[...])
    kpe = kr[:, :HEAD_PAD] * cs + kr[:, HEAD_PAD:] * sn
    qa = _dot(qn, wqa_ref[...])
    qb = _dot(qn, wqb_ref[...])
    ka = _dot(kvn, wka_ref[...])
    va = _dot(kvn, wva_ref[...]).astype(BF16)
    for hd in range(MLA_HEADS):
        sl = slice(hd * HEAD_PAD, (hd + 1) * HEAD_PAD)
        q_ref[0, hd] = (qa[:, sl] * cs + qb[:, sl] * sn).astype(BF16)
        k_ref[0, hd] = (ka[:, sl] + kpe).astype(BF16)
    for hp in range(MLA_HEADS // 2):
        v_ref[0, hp] = va[:, hp * 2 * MLA_V:(hp + 1) * 2 * MLA_V]

    hi, mid, lo = _split3(f_ref[...] * LOG2E)
    lane = lax.broadcasted_iota(I32, hi.shape, 1)
    f3 = jnp.where(lane < FOX_HEADS, hi, jnp.where(lane < 2 * FOX_HEADS, mid, lo))
    fp = _dot(f3, pall_ref[...])
    half = FOX_HEADS // 2 * 2 * HEAD_PAD
    for hp in range(FOX_HEADS // 2):
        sl = slice(hp * 2 * HEAD_PAD, (hp + 1) * 2 * HEAD_PAD)
        q2 = (_dot(hb, wfq_ref[hp]) + fp[:, sl] + cq_ref[...]).astype(BF16)
        k2 = (_dot(hb, wfk_ref[hp]) + fp[:, half + hp * 2 * HEAD_PAD:half + (hp + 1) * 2 * HEAD_PAD]
              + ck_ref[...]).astype(BF16)
        for j in range(2):
            hd = MLA_HEADS + 2 * hp + j
            q_ref[0, hd] = q2[:, j * HEAD_PAD:(j + 1) * HEAD_PAD]
            k_ref[0, hd] = k2[:, j * HEAD_PAD:(j + 1) * HEAD_PAD]
    vf = _dot(hb, wfv_ref[...]).astype(BF16)
    for hp in range(FOX_HEADS // 2):
        v_ref[0, MLA_HEADS // 2 + hp] = vf[:, hp * 2 * FOX_HEAD_DIM:(hp + 1) * 2 * FOX_HEAD_DIM]

    sg_ref[...] = jax.nn.sigmoid(_dot(hb, wg_ref[...])).astype(BF16)


def _in_proj(x2, mod8, g_mix, fdec, pos, freq, w, batch, seq, tm):
    t = x2.shape[0]
    tps = seq // tm
    consts = [w["wlat"], w["wkr"], w["gq"], w["gkv"], w["wqa"], w["wqb"], w["wka"], w["wva"],
              w["wfq"], w["wfk"], w["wfv"], w["pall"], w["cq"], w["ck"], w["wg"]]
    head_shape = jax.ShapeDtypeStruct((batch, N_HEADS, seq, HEAD_PAD), BF16)
    head_spec = pl.BlockSpec((1, N_HEADS, tm, HEAD_PAD), lambda i: (i // tps, 0, i % tps, 0))
    pair_shape = jax.ShapeDtypeStruct((batch, N_HEADS // 2, seq, 2 * MLA_V), BF16)
    pair_spec = pl.BlockSpec((1, N_HEADS // 2, tm, 2 * MLA_V), lambda i: (i // tps, 0, i % tps, 0))
    return pl.pallas_call(
        _in_proj_body,
        out_shape=(head_shape, head_shape, pair_shape,
                   jax.ShapeDtypeStruct((t, 2 * D_MODEL), BF16)),
        grid=(t // tm,),
        in_specs=[pl.BlockSpec((tm, D_MODEL), lambda i: (i, 0)),
                  pl.BlockSpec((1, 8, D_MODEL), lambda i: (i // tps, 0, 0)),
                  _const_spec((1, D_MODEL)),
                  pl.BlockSpec((tm, LANES), lambda i: (i, 0)),
                  pl.BlockSpec((tm, 1), lambda i: (i, 0)),
                  _const_spec((1, LANES))] + [_const_spec(a.shape) for a in consts],
        out_specs=(head_spec, head_spec, pair_spec,
                   pl.BlockSpec((tm, 2 * D_MODEL), lambda i: (i, 0))),
        compiler_params=_cparams(("parallel",)),
        name="in_proj",
    )(x2, mod8, g_mix, fdec, pos, freq, *consts)


def _attn_body(bq, sq, q_ref, k_ref, v_ref, o_ref, s_scr, p_scr, acc_scr):
    qi = pl.program_id(2)
    bk = sq
    n_heads = q_ref.shape[1]
    n_sub = bq // sq
    assert n_sub % 2 == 0
    chains = [(hh, u) for hh in range(n_heads) for u in range(n_sub)]
    n_chains = len(chains)
    n_main = qi * n_sub

    def chunk_start(j):
        return pl.multiple_of(jnp.maximum(j, 0) * bk, bk)

    def scores(c, j, par):
        hh, u = chains[c]
        k = k_ref[0, hh, pl.ds(chunk_start(j), bk), :]
        q = q_ref[0, hh, u * sq:(u + 1) * sq, :]
        s = lax.dot_general(k, q, (((1,), (1,)), ((), ())), preferred_element_type=F32)
        s_scr[par, c] = s
        return jnp.max(s, axis=0, keepdims=True)

    def values(c, j, par, alpha):
        hh = chains[c][0]
        v = v_ref[0, 0, pl.ds(chunk_start(j), bk), hh * MLA_V:(hh + 1) * MLA_V]
        pv = lax.dot_general(v, p_scr[par, c], (((0,), (0,)), ((), ())), preferred_element_type=F32)
        acc_scr[c] = alpha * acc_scr[c] + pv

    def softmax(c, par, m, l, smax, masked):
        s = s_scr[par, c]
        if masked:
            key = lax.broadcasted_iota(I32, (bk, sq), 0)
            qry = lax.broadcasted_iota(I32, (bk, sq), 1)
            s = jnp.where(key <= qry, s, NEG_INF)
            smax = jnp.max(s, axis=0, keepdims=True)
        m_new = jnp.maximum(m, smax)
        alpha = jnp.exp2(m - m_new)
        p = jnp.exp2(s - m_new)
        p_scr[par, c] = p.astype(BF16)
        return m_new, alpha * l + jnp.sum(p, axis=0, keepdims=True), alpha

    def stage(j, par, state, active, has_next, masked_of):
        nxt = {c: scores(c, j + 1, 1 - par) if has_next(c) else state[c][2] for c in active}
        for c in active:
            values(c, j - 1, 1 - par, state[c][3])
        out = list(state)
        for c in active:
            m, l, smax, _ = state[c]
            m_new, l_new, alpha = softmax(c, par, m, l, smax, masked_of(c))
            out[c] = (m_new, l_new, nxt[c], alpha)
        return out

    p_scr[1] = jnp.zeros(p_scr.shape[1:], BF16)
    acc_scr[...] = jnp.zeros_like(acc_scr)
    state = [(jnp.full((1, sq), NEG_INF, F32), jnp.zeros((1, sq), F32), scores(c, 0, 0), jnp.ones((1, sq), F32))
             for c in range(n_chains)]

    def step(jj, st):
        for par in range(2):
            st = stage(2 * jj + par, par, list(st), range(n_chains), lambda c: True, lambda c: False)
        return tuple(st)

    state = list(lax.fori_loop(0, n_main // 2, step, tuple(state)))

    for t in range(n_sub):
        active = [c for c, (_, u) in enumerate(chains) if u >= t]
        state = stage(n_main + t, t % 2, state, active, lambda c, t=t: chains[c][1] > t,
                      lambda c, t=t: chains[c][1] == t)

    heads = []
    for hh in range(n_heads):
        parts = []
        for u in range(n_sub):
            c = hh * n_sub + u
            values(c, n_main + u, u % 2, state[c][3])
            parts.append(acc_scr[c] / state[c][1])
        heads.append(jnp.concatenate(parts, axis=1))
    o_ref[0] = jnp.concatenate(heads, axis=0).T.astype(BF16)


def _attention(q_all, k_all, v_all, bq):
    batch, _, seq, _ = q_all.shape
    sq = min(256, bq)
    n_chains = 2 * (bq // sq)
    return pl.pallas_call(
        functools.partial(_attn_body, bq, sq),
        out_shape=jax.ShapeDtypeStruct((batch, seq, N_HEADS * MLA_V), BF16),
        grid=(batch, N_HEADS // 2, seq // bq),
        in_specs=[pl.BlockSpec((1, 2, bq, HEAD_PAD), lambda b, hp, qi: (b, hp, qi, 0)),
                  pl.BlockSpec((1, 2, seq, HEAD_PAD), lambda b, hp, qi: (b, hp, 0, 0)),
                  pl.BlockSpec((1, 1, seq, 2 * MLA_V), lambda b, hp, qi: (b, hp, 0, 0))],
        out_specs=pl.BlockSpec((1, bq, 2 * MLA_V), lambda b, hp, qi: (b, qi, hp)),
        scratch_shapes=[pltpu.VMEM((2, n_chains, sq, sq), F32),
                        pltpu.VMEM((2, n_chains, sq, sq), BF16),
                        pltpu.VMEM((n_chains, MLA_V, sq), F32)],
        compiler_params=_cparams(("parallel", "parallel", "arbitrary")),
        name="attention",
    )(q_all, k_all, v_all)


def _pack_halves(x):
    w = x.shape[1] // 2
    a = lax.bitcast_convert_type(x[:, :w].astype(BF16).astype(F32), U32)
    b = lax.bitcast_convert_type(x[:, w:].astype(BF16).astype(F32), U32)
    return a | (b >> 16)


def _unpack_halves(p):
    a = lax.bitcast_convert_type(p & jnp.uint32(0xFFFF0000), F32)
    b = lax.bitcast_convert_type(p << 16, F32)
    return a, b


def _out_proj_body(o_ref, sg_ref, x_ref, mod_ref, gffn_ref, womla_ref, wofox_ref, wout_ref,
                   wrhi_ref, wrlo_ref, x1_ref, hp_ref, lg_ref):
    o = o_ref[...]
    half = o.shape[1] // 2
    mo = _dot(o[:, :half], womla_ref[...])
    fo = _dot(o[:, half:], wofox_ref[...])
    sg = sg_ref[...]
    merged = sg[:, :D_MODEL].astype(F32) * mo + sg[:, D_MODEL:].astype(F32) * fo
    mix = _dot(merged.astype(BF16), wout_ref[...])
    mod = mod_ref[0]
    x1 = x_ref[...] + mod[2:3, :] * mix
    x1_ref[...] = x1
    h2 = _modulated_norm(x1, gffn_ref[...], mod, 3, 4)
    hhi = h2.astype(BF16)
    hlo = (h2 - hhi.astype(F32)).astype(BF16)
    lg_ref[...] = _dot(hhi, wrhi_ref[...]) + _dot(hhi, wrlo_ref[...]) + _dot(hlo, wrhi_ref[...])
    hp_ref[...] = _pack_halves(h2)


def _out_proj(o2, sg, x2, mod8, g_ffn, w, seq, tm):
    t = x2.shape[0]
    tps = seq // tm
    consts = [w["womla"], w["wofox"], w["wout"], w["wrhi"], w["wrlo"]]
    return pl.pallas_call(
        _out_proj_body,
        out_shape=(jax.ShapeDtypeStruct((t, D_MODEL), F32),
                   jax.ShapeDtypeStruct((t, D_MODEL // 2), U32),
                   jax.ShapeDtypeStruct((t, LANES), F32)),
        grid=(t // tm,),
        in_specs=[pl.BlockSpec((tm, D_MODEL), lambda i: (i, 0)),
                  pl.BlockSpec((tm, 2 * D_MODEL), lambda i: (i, 0)),
                  pl.BlockSpec((tm, D_MODEL), lambda i: (i, 0)),
                  pl.BlockSpec((1, 8, D_MODEL), lambda i: (i // tps, 0, 0)),
                  _const_spec((1, D_MODEL))] + [_const_spec(a.shape) for a in consts],
        out_specs=(pl.BlockSpec((tm, D_MODEL), lambda i: (i, 0)),
                   pl.BlockSpec((tm, D_MODEL // 2), lambda i: (i, 0)),
                   pl.BlockSpec((tm, LANES), lambda i: (i, 0))),
        compiler_params=_cparams(("parallel",)),
        name="out_proj",
    )(o2, sg, x2, mod8, g_ffn, *consts)


def _route_body(lt_ref, b_ref, eidx_ref, rank_ref, gate_ref, cnt_ref, carry_ref):
    i = pl.program_id(0)

    @pl.when(i == 0)
    def _():
        carry_ref[...] = jnp.zeros_like(carry_ref)

    s = jax.nn.sigmoid(lt_ref[...])
    c = s + b_ref[...]
    tr = s.shape[1]
    sub = lax.broadcasted_iota(I32, (GROUP_SIZE, tr), 0).astype(F32)

    gs = []
    for g in range(N_GROUPS):
        cg = c[g * GROUP_SIZE:(g + 1) * GROUP_SIZE, :]
        m1 = jnp.max(cg, axis=0, keepdims=True)
        i1 = jnp.min(jnp.where(cg == m1, sub, float(GROUP_SIZE)), axis=0, keepdims=True)
        m2 = jnp.max(jnp.where(sub == i1, NEG_INF, cg), axis=0, keepdims=True)
        gs.append(m1 + m2)

    masked = []
    for g in range(N_GROUPS):
        beats = jnp.zeros_like(gs[g])
        for o in range(N_GROUPS):
            if o == g:
                continue
            better = (gs[o] >= gs[g]) if o < g else (gs[o] > gs[g])
            beats = beats + jnp.where(better, 1.0, 0.0)
        keep = beats < float(TOPK_GROUPS)
        cg = c[g * GROUP_SIZE:(g + 1) * GROUP_SIZE, :]
        masked.append(jnp.where(keep, cg, NEG_INF))
    mc = jnp.concatenate(masked, axis=0)

    eio = lax.broadcasted_iota(I32, (N_EXPERTS, tr), 0).astype(F32)
    picks = []
    selected = jnp.zeros((N_EXPERTS, tr), F32)
    for _ in range(TOP_K):
        m = jnp.max(mc, axis=0, keepdims=True)
        idx = jnp.min(jnp.where(mc == m, eio, float(N_EXPERTS)), axis=0, keepdims=True)
        hit = eio == idx
        picks.append(idx)
        selected = jnp.where(hit, 1.0, selected)
        mc = jnp.where(hit, -3.0e38, mc)

    ssum = jnp.sum(selected * s, axis=0, keepdims=True)
    gate_full = selected * s / ssum * ROUTED_SCALE

    r_io = lax.broadcasted_iota(I32, (tr, tr), 0)
    c_io = lax.broadcasted_iota(I32, (tr, tr), 1)
    upper = jnp.where(r_io < c_io, 1.0, 0.0).astype(BF16)
    before = _dot(selected.astype(BF16), upper) + carry_ref[...]
    carry_new = carry_ref[...] + jnp.sum(selected, axis=1, keepdims=True)
    carry_ref[...] = carry_new
    cnt_ref[...] = jnp.broadcast_to(carry_new, cnt_ref.shape).astype(I32)

    for r in range(SLOTS):
        if r < TOP_K:
            hit = eio == picks[r]
            eidx_ref[r:r + 1, :] = picks[r].astype(I32)
            rank_ref[r:r + 1, :] = jnp.sum(jnp.where(hit, before, 0.0), axis=0, keepdims=True).astype(I32)
            gate_ref[r:r + 1, :] = jnp.sum(jnp.where(hit, gate_full, 0.0), axis=0, keepdims=True)
        else:
            eidx_ref[r:r + 1, :] = jnp.zeros((1, tr), I32)
            rank_ref[r:r + 1, :] = jnp.zeros((1, tr), I32)
            gate_ref[r:r + 1, :] = jnp.zeros((1, tr), F32)


def _route(logits_t, bias_col, tr):
    t = logits_t.shape[1]
    slot_i = jax.ShapeDtypeStruct((SLOTS, t), I32)
    slot_spec = pl.BlockSpec((SLOTS, tr), lambda i: (0, i))
    return pl.pallas_call(
        _route_body,
        out_shape=(slot_i, slot_i, jax.ShapeDtypeStruct((SLOTS, t), F32),
                   jax.ShapeDtypeStruct((N_EXPERTS, LANES), I32)),
        grid=(t // tr,),
        in_specs=[pl.BlockSpec((N_EXPERTS, tr), lambda i: (0, i)),
                  _const_spec((N_EXPERTS, 1))],
        out_specs=(slot_spec, slot_spec, slot_spec, _const_spec((N_EXPERTS, LANES))),
        scratch_shapes=[pltpu.VMEM((N_EXPERTS, 1), F32)],
        compiler_params=_cparams(("arbitrary",)),
        name="route",
    )(logits_t, bias_col)


def _dest_body(pstart_ref, eidx_ref, rank_ref, o_ref):
    e = eidx_ref[...]
    d = rank_ref[...]
    for j in range(N_EXPERTS):
        d = d + jnp.where(e == j, pstart_ref[j], 0)
    o_ref[...] = d


def _dest(pstart, eidx, rank, tr):
    t = eidx.shape[1]
    spec = pl.BlockSpec((SLOTS, tr), lambda i, ps: (0, i))
    return pl.pallas_call(
        _dest_body,
        out_shape=jax.ShapeDtypeStruct((SLOTS, t), I32),
        grid_spec=pltpu.PrefetchScalarGridSpec(
            num_scalar_prefetch=1, grid=(t // tr,), in_specs=[spec, spec], out_specs=spec),
        compiler_params=_cparams(("parallel",)),
        name="dest",
    )(pstart, eidx, rank)


def _sc_mesh():
    return plsc.VectorSubcoreMesh(core_axis_name="c", subcore_axis_name="s")


def _sc_worker_id():
    return lax.axis_index("s") * SC_CORES + lax.axis_index("c")


def _dispatch(hpk, dest3, n_rows):
    _, w = hpk.shape
    n_chunks, _, c = dest3.shape
    per_worker = n_chunks // SC_WORKERS

    def body(h_hbm, d_hbm, xs_hbm, idx_v, rows_v, sem):
        wid = _sc_worker_id()

        @pl.loop(0, per_worker)
        def _(i):
            ch = wid * per_worker + i
            pltpu.sync_copy(d_hbm.at[ch], idx_v)
            pltpu.sync_copy(h_hbm.at[pl.ds(ch * c, c)], rows_v)
            copies = [pltpu.async_copy(rows_v, xs_hbm.at[idx_v.at[k]], sem) for k in range(TOP_K)]
            for cp in copies:
                cp.wait()

    return pl.kernel(
        body, mesh=_sc_mesh(),
        out_type=jax.ShapeDtypeStruct((n_rows, w), U32),
        scratch_types=[pltpu.VMEM((SLOTS, c), I32), pltpu.VMEM((c, w), U32), pltpu.SemaphoreType.DMA],
        name="dispatch",
    )(hpk, dest3)


def _experts_body(n_sub, be_ref, nv_ref, first_ref, nu_ref, xs_ref, wg_ref, wu_ref, wd_ref, ys_ref,
                  wgb_ref, wub_ref, wdb_ref):
    del be_ref
    i = pl.program_id(0)

    @pl.when(first_ref[i] == 1)
    def _():
        wgb_ref[...] = wg_ref[0].astype(BF16)
        wub_ref[...] = wu_ref[0].astype(BF16)
        wdb_ref[...] = wd_ref[0].astype(BF16)

    @pl.when(i < nu_ref[0])
    def _():
        half = D_MODEL // 2
        sub = ROW_BLOCK // n_sub
        row = lax.broadcasted_iota(I32, (sub, xs_ref.shape[1]), 0)
        gu = []
        for s in range(n_sub):
            x = jnp.where(row + s * sub < nv_ref[i], xs_ref[s * sub:(s + 1) * sub, :], jnp.uint32(0))
            xa, xb = _unpack_halves(x)
            xa = xa.astype(BF16)
            xb = xb.astype(BF16)
            g = _dot(xa, wgb_ref[:half, :]) + _dot(xb, wgb_ref[half:, :])
            u = _dot(xa, wub_ref[:half, :]) + _dot(xb, wub_ref[half:, :])
            gu.append((g, u))
        for s, (g, u) in enumerate(gu):
            hb = (g * jax.nn.sigmoid(g) * u).astype(BF16)
            ys_ref[s * sub:(s + 1) * sub, :] = _pack_halves(_dot(hb, wdb_ref[...]))

    @pl.when(i >= nu_ref[0])
    def _():
        ys_ref[...] = jnp.zeros_like(ys_ref)


def _experts(block_e, block_valid, block_first, n_used, xs, wg, wu, wd):
    n_rows, w = xs.shape
    n_blocks = n_rows // ROW_BLOCK

    def row_map(i, be, nv, bf, nu):
        return (jnp.minimum(i, nu[0] - 1), 0)

    def w_map(i, be, nv, bf, nu):
        return (be[i], 0, 0)

    return pl.pallas_call(
        functools.partial(_experts_body, ROW_BLOCK // 256),
        out_shape=jax.ShapeDtypeStruct((n_rows, w), U32),
        grid_spec=pltpu.PrefetchScalarGridSpec(
            num_scalar_prefetch=4, grid=(n_blocks,),
            in_specs=[pl.BlockSpec((ROW_BLOCK, w), row_map),
                      pl.BlockSpec((1, D_MODEL, D_EXPERT), w_map),
                      pl.BlockSpec((1, D_MODEL, D_EXPERT), w_map),
                      pl.BlockSpec((1, D_EXPERT, D_MODEL), w_map)],
            out_specs=pl.BlockSpec((ROW_BLOCK, w), lambda i, be, nv, bf, nu: (i, 0)),
            scratch_shapes=[pltpu.VMEM((D_MODEL, D_EXPERT), BF16), pltpu.VMEM((D_MODEL, D_EXPERT), BF16),
                            pltpu.VMEM((D_EXPERT, D_MODEL), BF16)]),
        compiler_params=_cparams(("arbitrary",)),
        name="experts",
    )(block_e, block_valid, block_first, n_used, xs, wg, wu, wd)


def _gather_back(ys, dest3):
    _, w = ys.shape
    n_chunks, _, c = dest3.shape
    per_worker = n_chunks // SC_WORKERS

    def body(ys_hbm, d_hbm, yk_hbm, idx_v, rows_v, gsem, osem):
        wid = _sc_worker_id()

        @pl.loop(0, per_worker)
        def _(i):
            ch = wid * per_worker + i
            pltpu.sync_copy(d_hbm.at[ch], idx_v)
            gathers = [pltpu.async_copy(ys_hbm.at[idx_v.at[k]], rows_v.at[k], gsem) for k in range(TOP_K)]
            for g in gathers:
                g.wait()
            outs = [pltpu.async_copy(rows_v.at[k], yk_hbm.at[k, pl.ds(ch * c, c)], osem) for k in range(TOP_K)]
            for o in outs:
                o.wait()

    return pl.kernel(
        body, mesh=_sc_mesh(),
        out_type=jax.ShapeDtypeStruct((TOP_K, n_chunks * c, w), U32),
        scratch_types=[pltpu.VMEM((SLOTS, c), I32), pltpu.VMEM((TOP_K, c, w), U32),
                       pltpu.SemaphoreType.DMA, pltpu.SemaphoreType.DMA],
        name="gather_back",
    )(ys, dest3)


def _shared_body(hp_ref, x1_ref, mod_ref, wsg_ref, wsu_ref, wsd_ref, o_ref):
    ha, hb = _unpack_halves(hp_ref[...])
    ha = ha.astype(BF16)
    hb = hb.astype(BF16)
    half = D_MODEL // 2
    g = _dot(ha, wsg_ref[:half, :]) + _dot(hb, wsg_ref[half:, :])
    u = _dot(ha, wsu_ref[:half, :]) + _dot(hb, wsu_ref[half:, :])
    shared = _dot((g * jax.nn.sigmoid(g) * u).astype(BF16), wsd_ref[...])
    o_ref[...] = x1_ref[...] + mod_ref[0][5:6, :] * shared


def _shared(hpk, x1, mod8, w, seq, tm):
    t, wd = hpk.shape
    tps = seq // tm
    return pl.pallas_call(
        _shared_body,
        out_shape=jax.ShapeDtypeStruct((t, D_MODEL), F32),
        grid=(t // tm,),
        in_specs=[pl.BlockSpec((tm, wd), lambda i: (i, 0)),
                  pl.BlockSpec((tm, D_MODEL), lambda i: (i, 0)),
                  pl.BlockSpec((1, 8, D_MODEL), lambda i: (i // tps, 0, 0)),
                  _const_spec(w["wsg"].shape), _const_spec(w["wsu"].shape), _const_spec(w["wsd"].shape)],
        out_specs=pl.BlockSpec((tm, D_MODEL), lambda i: (i, 0)),
        compiler_params=_cparams(("parallel",)),
        name="shared",
    )(hpk, x1, mod8, w["wsg"], w["wsu"], w["wsd"])


def _combine_body(yk_ref, gate_ref, base_ref, mod_ref, gfin_ref, *rest):
    o_ref = rest[-1]
    gate = gate_ref[...]
    half = D_MODEL // 2
    ra = jnp.zeros((gate.shape[0], half), F32)
    rb = jnp.zeros((gate.shape[0], half), F32)
    for k in range(TOP_K):
        ya, yb = _unpack_halves(yk_ref[k])
        gk = gate[:, k:k + 1]
        ra = ra + gk * ya
        rb = rb + gk * yb
    routed = jnp.concatenate([ra, rb], axis=1)
    x2 = base_ref[...] + mod_ref[0][5:6, :] * routed
    o_ref[...] = _rms(x2) * gfin_ref[...]


def _combine(yk, gate_t, base, mod8, g_final, seq, tq, part, prev_out):
    t = base.shape[0]
    _, t_part, wd = yk.shape
    tps = seq // tq
    off = part * (t_part // tq)
    in_specs = [pl.BlockSpec((TOP_K, tq, wd), lambda i: (0, i, 0)),
                pl.BlockSpec((tq, SLOTS), lambda i: (i + off, 0)),
                pl.BlockSpec((tq, D_MODEL), lambda i: (i + off, 0)),
                pl.BlockSpec((1, 8, D_MODEL), lambda i: ((i + off) // tps, 0, 0)),
                _const_spec((1, D_MODEL))]
    args = [yk, gate_t, base, mod8, g_final]
    aliases = {}
    if prev_out is not None:
        in_specs.append(pl.BlockSpec(memory_space=pl.ANY))
        args.append(prev_out)
        aliases = {len(args) - 1: 0}
    return pl.pallas_call(
        _combine_body,
        out_shape=jax.ShapeDtypeStruct((t, D_MODEL), F32),
        grid=(t_part // tq,),
        in_specs=in_specs,
        out_specs=pl.BlockSpec((tq, D_MODEL), lambda i: (i + off, 0)),
        input_output_aliases=aliases,
        compiler_params=_cparams(("parallel",)),
        name="combine",
    )(*args)


def _prep_weights(w_in, b_forget, g_q_lat, w_q_up, g_kv_lat, w_kv_up, w_o_mla, w_o_fox, w_out,
                  w_router, w_sh_gate, w_sh_up, w_sh_down):
    o_q, o_kv, o_kr = 0, MLA_Q_LORA, MLA_Q_LORA + MLA_KV_LORA
    o_fq = o_kr + MLA_ROPE
    o_fk, o_fv = o_fq + FOX_WIDTH, o_fq + 2 * FOX_WIDTH
    o_fl = o_fq + 3 * FOX_WIDTH
    o_ga = o_fl + FOX_HEADS
    o_gb = o_ga + D_MODEL
    w = {}
    w["wlat"] = w_in[:, o_q:o_kr].astype(BF16)
    half = MLA_ROPE // 2

    def rope_pair(cols):
        x1, x2 = cols[..., :half], cols[..., half:]
        z = jnp.zeros(cols.shape[:-1] + (MLA_NOPE,), cols.dtype)
        zt = jnp.zeros(cols.shape[:-1] + (HEAD_PAD - MLA_QK,), cols.dtype)
        plain = jnp.concatenate([z, x1, x2, zt], axis=-1)
        rot = jnp.concatenate([z, -x2, x1, zt], axis=-1)
        return plain, rot

    kr_plain, kr_rot = rope_pair(w_in[:, o_kr:o_fq])
    w["wkr"] = jnp.concatenate([kr_plain, kr_rot], axis=1).astype(BF16)
    w["gq"] = g_q_lat.reshape(1, -1)
    w["gkv"] = g_kv_lat.reshape(1, -1)

    scale_a = LOG2E / math.sqrt(MLA_QK)
    wq = (w_q_up * scale_a).reshape(MLA_Q_LORA, MLA_HEADS, MLA_QK)
    q_plain, q_rot = rope_pair(wq[..., MLA_NOPE:])
    nope = jnp.concatenate([wq[..., :MLA_NOPE], jnp.zeros((MLA_Q_LORA, MLA_HEADS, HEAD_PAD - MLA_NOPE), F32)], -1)
    w["wqa"] = (nope + q_plain).reshape(MLA_Q_LORA, -1).astype(BF16)
    w["wqb"] = q_rot.reshape(MLA_Q_LORA, -1).astype(BF16)

    wkv = w_kv_up.reshape(MLA_KV_LORA, MLA_HEADS, MLA_NOPE + MLA_V)
    zpad = jnp.zeros((MLA_KV_LORA, MLA_HEADS, HEAD_PAD - MLA_NOPE), F32)
    w["wka"] = jnp.concatenate([wkv[..., :MLA_NOPE], zpad], -1).reshape(MLA_KV_LORA, -1).astype(BF16)

    w["wva"] = wkv[..., MLA_NOPE:].reshape(MLA_KV_LORA, -1).astype(BF16)

    def fox_heads(cols, scale):
        v = (cols * scale).reshape(D_MODEL, FOX_HEADS, FOX_HEAD_DIM)
        v = jnp.concatenate([v, jnp.zeros((D_MODEL, FOX_HEADS, HEAD_PAD - FOX_HEAD_DIM), F32)], -1)
        return v.reshape(D_MODEL, FOX_HEADS // 2, 2 * HEAD_PAD).transpose(1, 0, 2).astype(BF16)

    w["wfq"] = fox_heads(w_in[:, o_fq:o_fk], LOG2E / math.sqrt(FOX_HEAD_DIM))
    w["wfk"] = fox_heads(w_in[:, o_fk:o_fv], 1.0)
    w["wfv"] = w_in[:, o_fv:o_fl].astype(BF16)

    wfl = jnp.concatenate([w_in[:, o_fl:o_ga]] * 3 + [jnp.zeros((D_MODEL, LANES - 3 * FOX_HEADS), F32)], 1)
    w["wfl_hi"] = wfl.astype(BF16)
    w["wfl_lo"] = (wfl - w["wfl_hi"].astype(F32)).astype(BF16)
    w["bfl"] = jnp.concatenate([b_forget] * 3 + [jnp.zeros((LANES - 3 * FOX_HEADS,), F32)]).reshape(1, LANES)

    pall = np.zeros((LANES, 2 * FOX_HEADS * HEAD_PAD), np.float32)
    cq = np.zeros((1, 2 * HEAD_PAD), np.float32)
    ck = np.zeros((1, 2 * HEAD_PAD), np.float32)
    for hd in range(FOX_HEADS):
        for term in range(3):
            pall[term * FOX_HEADS + hd, hd * HEAD_PAD + FQ_COL + term] = 1.0
            pall[term * FOX_HEADS + hd, FOX_HEADS * HEAD_PAD + hd * HEAD_PAD + FK_COL + term] = -1.0
    for j in range(2):
        cq[0, j * HEAD_PAD + FK_COL:j * HEAD_PAD + FK_COL + 3] = 1.0
        ck[0, j * HEAD_PAD + FQ_COL:j * HEAD_PAD + FQ_COL + 3] = 1.0
    w["pall"] = jnp.asarray(pall, BF16)
    w["cq"] = jnp.asarray(cq)
    w["ck"] = jnp.asarray(ck)

    w["wg"] = w_in[:, o_ga:o_gb + D_MODEL].astype(BF16)
    w["womla"] = w_o_mla.astype(BF16)
    w["wofox"] = w_o_fox.astype(BF16)
    w["wout"] = w_out.astype(BF16)
    wr = jnp.concatenate([w_router, jnp.zeros((D_MODEL, LANES - N_EXPERTS), F32)], 1)
    w["wrhi"] = wr.astype(BF16)
    w["wrlo"] = (wr - w["wrhi"].astype(F32)).astype(BF16)
    w["wsg"] = w_sh_gate.astype(BF16)
    w["wsu"] = w_sh_up.astype(BF16)
    w["wsd"] = w_sh_down.astype(BF16)
    return w


def _rope_freq_row():
    half = MLA_ROPE // 2
    inv = np.power(ROPE_THETA, -np.arange(half, dtype=np.float32) / half).astype(np.float32)
    row = np.zeros((1, LANES), np.float32)
    row[0, MLA_NOPE:MLA_NOPE + half] = inv
    row[0, MLA_NOPE + half:MLA_NOPE + 2 * half] = inv
    return jnp.asarray(row)


def kernel(x, c, positions, w_mod, b_mod, g_mix_norm, w_in, b_forget, g_q_lat, w_q_up, g_kv_lat, w_kv_up,
           w_o_mla, w_o_fox, w_out, g_ffn_norm, w_router, b_router, w_exp_gate, w_exp_up, w_exp_down,
           w_sh_gate, w_sh_up, w_sh_down, g_final):
    batch, seq, d = x.shape
    assert d == D_MODEL and w_mod.shape[0] == 1
    t = batch * seq
    tm = min(512, seq)
    bq = min(1024, seq)
    tr = min(512, t)
    tc = min(256, seq)
    assert seq % tm == 0 and seq % bq == 0 and t % tr == 0 and seq % tc == 0 and batch <= 8
    assert t % (SC_CHUNK * SC_WORKERS * N_PARTS) == 0 and (t // N_PARTS) % tc == 0

    w = _prep_weights(w_in[0], b_forget[0], g_q_lat[0], w_q_up[0], g_kv_lat[0], w_kv_up[0], w_o_mla[0],
                      w_o_fox[0], w_out[0], w_router[0], w_sh_gate[0], w_sh_up[0], w_sh_down[0])

    c8 = jnp.zeros((8, D_MODEL), F32).at[:batch].set(c)
    mod = _mod(c8, w_mod[0], b_mod)
    mod8 = jnp.zeros((batch, 8, D_MODEL), F32).at[:, :N_MOD].set(mod[:batch].reshape(batch, N_MOD, D_MODEL))

    x2 = x.reshape(t, D_MODEL)
    fdec = _fox_decay(x2, mod8, g_mix_norm, w["wfl_hi"], w["wfl_lo"], w["bfl"], seq, tm)
    q_all, k_all, v_all, sg = _in_proj(x2, mod8, g_mix_norm, fdec, positions.reshape(t, 1),
                                       _rope_freq_row(), w, batch, seq, tm)
    o = _attention(q_all, k_all, v_all, bq)
    x1, hpk, logits = _out_proj(o.reshape(t, D_MODEL), sg, x2, mod8, g_ffn_norm, w, seq, tm)

    base = _shared(hpk, x1, mod8, w, seq, tm)
    eidx, rank, gate, counts = _route(logits[:, :N_EXPERTS].T, b_router.reshape(N_EXPERTS, 1), tr)

    cnt = counts[:, 0]
    padded = (cnt + ROW_BLOCK - 1) // ROW_BLOCK * ROW_BLOCK
    pend = jnp.cumsum(padded)
    pstart = pend - padded
    n_blocks = t * TOP_K // ROW_BLOCK + N_EXPERTS
    n_rows = n_blocks * ROW_BLOCK
    block_row = jnp.arange(n_blocks, dtype=I32) * ROW_BLOCK
    block_e = jnp.minimum(jnp.sum(pend[None, :] <= block_row[:, None], axis=1), N_EXPERTS - 1).astype(I32)
    n_used = (pend[-1:] // ROW_BLOCK).astype(I32)
    block_valid = jnp.clip((pstart + cnt)[block_e] - block_row, 0, ROW_BLOCK).astype(I32)
    block_first = jnp.concatenate([jnp.ones((1,), I32), (block_e[1:] != block_e[:-1]).astype(I32)])

    dest = _dest(pstart.astype(I32), eidx, rank, tr)
    dest3 = dest.reshape(SLOTS, t // SC_CHUNK, SC_CHUNK).transpose(1, 0, 2)
    xs = _dispatch(hpk, dest3, n_rows)
    ys = _experts(block_e, block_valid, block_first, n_used, xs, w_exp_gate[0], w_exp_up[0], w_exp_down[0])
    gate_t = gate.T
    out = None
    n_chunks = dest3.shape[0]
    for part in range(N_PARTS):
        sl = slice(part * n_chunks // N_PARTS, (part + 1) * n_chunks // N_PARTS)
        yk = _gather_back(ys, dest3[sl])
        out = _combine(yk, gate_t, base, mod8, g_final.reshape(1, D_MODEL), seq, tc, part, out)
    return out.reshape(batch, seq, D_MODEL)
```

```python
import functools
import math

import numpy as np
import jax
import jax.numpy as jnp
from jax import lax
from jax.experimental import pallas as pl
from jax.experimental.pallas import tpu as pltpu
from jax.experimental.pallas import tpu_sc as plsc

F32 = jnp.float32
BF16 = jnp.bfloat16
I32 = jnp.int32
U32 = jnp.uint32

D_MODEL = 1024
MLA_HEADS = 8
MLA_Q_LORA = 256
MLA_KV_LORA = 128
MLA_NOPE = 64
MLA_ROPE = 32
MLA_V = 64
MLA_QK = MLA_NOPE + MLA_ROPE
ROPE_THETA = 10000.0
FOX_HEADS = 8
FOX_HEAD_DIM = 64
FOX_WIDTH = FOX_HEADS * FOX_HEAD_DIM
N_HEADS = MLA_HEADS + FOX_HEADS
N_EXPERTS = 64
N_GROUPS = 8
GROUP_SIZE = N_EXPERTS // N_GROUPS
TOPK_GROUPS = 4
TOP_K = 6
D_EXPERT = 256
ROUTED_SCALE = 2.5
N_MOD = 6
NORM_EPS = 1e-6
NEG_INF = -1e30
LOG2E = math.log2(math.e)

LANES = 128
HEAD_PAD = 128
V_ROWS = MLA_V + 16
ROW_BLOCK = 512
SLOTS = 8
SC_CORES = 2
SC_SUBCORES = 16
SC_WORKERS = SC_CORES * SC_SUBCORES
SC_CHUNK = 32
VMEM_LIMIT = 56 * 1024 * 1024

FQ_COL = FOX_HEAD_DIM
FK_COL = FOX_HEAD_DIM + 3


def _cparams(sem, vmem=VMEM_LIMIT):
    return pltpu.CompilerParams(dimension_semantics=sem, vmem_limit_bytes=vmem)


def _const_spec(shape):
    nd = len(shape)
    return pl.BlockSpec(shape, lambda *_: (0,) * nd)


def _rms(x):
    return x * lax.rsqrt(jnp.mean(x * x, axis=-1, keepdims=True) + NORM_EPS)


def _split3(x):
    hi = x.astype(BF16)
    r = x - hi.astype(F32)
    mid = r.astype(BF16)
    lo = (r - mid.astype(F32)).astype(BF16)
    return hi, mid, lo


def _dot(a, b):
    return jnp.dot(a, b, preferred_element_type=F32)


def _modulated_norm(x, gain, mod, shift_row, scale_row):
    shift = mod[shift_row:shift_row + 1, :]
    scale = mod[scale_row:scale_row + 1, :]
    return _rms(x) * gain * (1.0 + scale) + shift


def _mod_body(c_ref, w_ref, b_ref, o_ref):
    c = c_ref[...]
    cond = c * jax.nn.sigmoid(c)
    o_ref[...] = _dot(cond.astype(BF16), w_ref[...].astype(BF16)) + b_ref[...]


def _mod(c8, w_mod, b_mod):
    n = w_mod.shape[1]
    tn = D_MODEL
    return pl.pallas_call(
        _mod_body,
        out_shape=jax.ShapeDtypeStruct((8, n), F32),
        grid=(n // tn,),
        in_specs=[_const_spec((8, D_MODEL)),
                  pl.BlockSpec((D_MODEL, tn), lambda j: (0, j)),
                  pl.BlockSpec((1, tn), lambda j: (0, j))],
        out_specs=pl.BlockSpec((8, tn), lambda j: (0, j)),
        compiler_params=_cparams(("parallel",)),
        name="mod",
    )(c8, w_mod, b_mod)


def _decay_body(tiles_per_seq, x_ref, mod_ref, g_ref, whi_ref, wlo_ref, b_ref, o_ref, carry_ref):
    i = pl.program_id(0)

    @pl.when(i % tiles_per_seq == 0)
    def _():
        carry_ref[...] = jnp.zeros_like(carry_ref)

    h = _modulated_norm(x_ref[...], g_ref[...], mod_ref[0], 0, 1)
    hhi = h.astype(BF16)
    hlo = (h - hhi.astype(F32)).astype(BF16)
    z = _dot(hhi, whi_ref[...]) + _dot(hhi, wlo_ref[...]) + _dot(hlo, whi_ref[...]) + b_ref[...]
    logf = jnp.minimum(z, 0.0) - jnp.log1p(jnp.exp(-jnp.abs(z)))
    tm = logf.shape[0]
    row = lax.broadcasted_iota(I32, (tm, tm), 0)
    col = lax.broadcasted_iota(I32, (tm, tm), 1)
    tri = jnp.where(col <= row, 1.0, 0.0).astype(BF16)
    hi, mid, lo = _split3(logf)
    cum = _dot(tri, hi) + _dot(tri, mid) + _dot(tri, lo) + carry_ref[...]
    o_ref[...] = cum
    carry_ref[...] = cum[tm - 1:tm, :]


def _fox_decay(x2, mod8, g_mix, wfl_hi, wfl_lo, bfl, seq, tm):
    t = x2.shape[0]
    tps = seq // tm
    return pl.pallas_call(
        functools.partial(_decay_body, tps),
        out_shape=jax.ShapeDtypeStruct((t, LANES), F32),
        grid=(t // tm,),
        in_specs=[pl.BlockSpec((tm, D_MODEL), lambda i: (i, 0)),
                  pl.BlockSpec((1, 8, D_MODEL), lambda i: (i // tps, 0, 0)),
                  _const_spec((1, D_MODEL)),
                  _const_spec((D_MODEL, LANES)),
                  _const_spec((D_MODEL, LANES)),
                  _const_spec((1, LANES))],
        out_specs=pl.BlockSpec((tm, LANES), lambda i: (i, 0)),
        scratch_shapes=[pltpu.VMEM((1, LANES), F32)],
        compiler_params=_cparams(("arbitrary",)),
        name="fox_decay",
    )(x2, mod8, g_mix, wfl_hi, wfl_lo, bfl)


def _in_proj_body(x_ref, mod_ref, g_ref, f_ref, pos_ref, freq_ref,
                  wlat_ref, wkr_ref, gq_ref, gkv_ref, wqa_ref, wqb_ref, wka_ref, wva_ref,
                  wfq_ref, wfk_ref, wfv_ref, pall_ref, cq_ref, ck_ref, wg_ref,
                  q_ref, k_ref, v_ref, sg_ref):
    h = _modulated_norm(x_ref[...], g_ref[...], mod_ref[0], 0, 1)
    hb = h.astype(BF16)

    lat = _dot(hb, wlat_ref[...])
    qn = (_rms(lat[:, :MLA_Q_LORA]) * gq_ref[...]).astype(BF16)
    kvn = (_rms(lat[:, MLA_Q_LORA:]) * gkv_ref[...]).astype(BF16)
    ang = pos_ref[...].astype(F32) * freq_ref[...]
    cs = jnp.cos(ang)
    sn = jnp.sin(ang)
    kr = _dot(hb, wkr_ref[...])
    kpe = kr[:, :HEAD_PAD] * cs + kr[:, HEAD_PAD:] * sn
    qa = _dot(qn, wqa_ref[...])
    qb = _dot(qn, wqb_ref[...])
    ka = _dot(kvn, wka_ref[...])
    va = _dot(kvn, wva_ref[...])
    for hd in range(MLA_HEADS):
        sl = slice(hd * HEAD_PAD, (hd + 1) * HEAD_PAD)
        q_ref[0, hd] = (qa[:, sl] * cs + qb[:, sl] * sn).astype(BF16)
        k_ref[0, hd] = (ka[:, sl] + kpe).astype(BF16)
    ones_rows = jnp.ones((V_ROWS - MLA_V, va.shape[0]), BF16)

    def put_values(first_head, vals):
        for hp in range(vals.shape[1] // (2 * MLA_V)):
            vt = vals[:, hp * 2 * MLA_V:(hp + 1) * 2 * MLA_V].T.astype(BF16)
            for j in range(2):
                v_ref[0, first_head + 2 * hp + j, :MLA_V, :] = vt[j * MLA_V:(j + 1) * MLA_V, :]
                v_ref[0, first_head + 2 * hp + j, MLA_V:, :] = ones_rows

    put_values(0, va)

    hi, mid, lo = _split3(f_ref[...] * LOG2E)
    lane = lax.broadcasted_iota(I32, hi.shape, 1)
    f3 = jnp.where(lane < FOX_HEADS, hi, jnp.where(lane < 2 * FOX_HEADS, mid, lo))
    fp = _dot(f3, pall_ref[...])
    half = FOX_HEADS // 2 * 2 * HEAD_PAD
    for hp in range(FOX_HEADS // 2):
        sl = slice(hp * 2 * HEAD_PAD, (hp + 1) * 2 * HEAD_PAD)
        q2 = (_dot(hb, wfq_ref[hp]) + fp[:, sl] + cq_ref[...]).astype(BF16)
        k2 = (_dot(hb, wfk_ref[hp]) + fp[:, half + hp * 2 * HEAD_PAD:half + (hp + 1) * 2 * HEAD_PAD]
              + ck_ref[...]).astype(BF16)
        for j in range(2):
            hd = MLA_HEADS + 2 * hp + j
            q_ref[0, hd] = q2[:, j * HEAD_PAD:(j + 1) * HEAD_PAD]
            k_ref[0, hd] = k2[:, j * HEAD_PAD:(j + 1) * HEAD_PAD]
    put_values(MLA_HEADS, _dot(hb, wfv_ref[...]))

    sg_ref[...] = jax.nn.sigmoid(_dot(hb, wg_ref[...])).astype(BF16)


def _in_proj(x2, mod8, g_mix, fdec, pos, freq, w, batch, seq, tm):
    t = x2.shape[0]
    tps = seq // tm
    consts = [w["wlat"], w["wkr"], w["gq"], w["gkv"], w["wqa"], w["wqb"], w["wka"], w["wva"],
              w["wfq"], w["wfk"], w["wfv"], w["pall"], w["cq"], w["ck"], w["wg"]]
    head_shape = jax.ShapeDtypeStruct((batch, N_HEADS, seq, HEAD_PAD), BF16)
    head_spec = pl.BlockSpec((1, N_HEADS, tm, HEAD_PAD), lambda i: (i // tps, 0, i % tps, 0))
    pair_shape = jax.ShapeDtypeStruct((batch, N_HEADS, V_ROWS, seq), BF16)
    pair_spec = pl.BlockSpec((1, N_HEADS, V_ROWS, tm), lambda i: (i // tps, 0, 0, i % tps))
    return pl.pallas_call(
        _in_proj_body,
        out_shape=(head_shape, head_shape, pair_shape,
                   jax.ShapeDtypeStruct((t, 2 * D_MODEL), BF16)),
        grid=(t // tm,),
        in_specs=[pl.BlockSpec((tm, D_MODEL), lambda i: (i, 0)),
                  pl.BlockSpec((1, 8, D_MODEL), lambda i: (i // tps, 0, 0)),
                  _const_spec((1, D_MODEL)),
                  pl.BlockSpec((tm, LANES), lambda i: (i, 0)),
                  pl.BlockSpec((tm, 1), lambda i: (i, 0)),
                  _const_spec((1, LANES))] + [_const_spec(a.shape) for a in consts],
        out_specs=(head_spec, head_spec, pair_spec,
                   pl.BlockSpec((tm, 2 * D_MODEL), lambda i: (i, 0))),
        compiler_params=_cparams(("parallel",)),
        name="in_proj",
    )(x2, mod8, g_mix, fdec, pos, freq, *consts)


def _attn_body(bq, sq, q_ref, k_ref, v_ref, o_ref, s_scr, p_scr, acc_scr):
    qi = pl.program_id(2)
    bk = sq
    n_heads = q_ref.shape[1]
    n_sub = bq // sq
    assert n_sub % 2 == 0
    chains = [(hh, u) for hh in range(n_heads) for u in range(n_sub)]
    n_chains = len(chains)
    n_main = qi * n_sub

    def chunk_start(j):
        return pl.multiple_of(jnp.maximum(j, 0) * bk, bk)

    def scores(c, j, par):
        hh, u = chains[c]
        k = k_ref[0, hh, pl.ds(chunk_start(j), bk), :]
        q = q_ref[0, hh, u * sq:(u + 1) * sq, :]
        s = lax.dot_general(k, q, (((1,), (1,)), ((), ())), preferred_element_type=F32)
        s_scr[par, c] = s
        return jnp.max(s, axis=0, keepdims=True)

    def values(c, j, par, alpha):
        vt = v_ref[0, chains[c][0], :, pl.ds(chunk_start(j), bk)]
        acc_scr[c] = alpha * acc_scr[c] + _dot(vt, p_scr[par, c])

    def softmax(c, par, m, smax, masked):
        s = s_scr[par, c]
        if masked:
            key = lax.broadcasted_iota(I32, (bk, sq), 0)
            qry = lax.broadcasted_iota(I32, (bk, sq), 1)
            s = jnp.where(key <= qry, s, NEG_INF)
            smax = jnp.max(s, axis=0, keepdims=True)
        m_new = jnp.maximum(m, smax)
        p_scr[par, c] = jnp.exp2((s - m_new).astype(BF16))
        return m_new, jnp.exp2(m - m_new)

    def stage(j, par, state, active, has_next, masked_of):
        nxt = {c: scores(c, j + 1, 1 - par) if has_next(c) else state[c][1] for c in active}
        for c in active:
            values(c, j - 1, 1 - par, state[c][2])
        out = list(state)
        for c in active:
            m, smax, _ = state[c]
            m_new, alpha = softmax(c, par, m, smax, masked_of(c))
            out[c] = (m_new, nxt[c], alpha)
        return out

    p_scr[1] = jnp.zeros(p_scr.shape[1:], BF16)
    acc_scr[...] = jnp.zeros_like(acc_scr)
    state = [(jnp.full((1, sq), NEG_INF, F32), scores(c, 0, 0), jnp.ones((1, sq), F32)) for c in range(n_chains)]

    def step(jj, st):
        for par in range(2):
            st = stage(2 * jj + par, par, list(st), range(n_chains), lambda c: True, lambda c: False)
        return tuple(st)

    state = list(lax.fori_loop(0, n_main // 2, step, tuple(state)))

    for t in range(n_sub):
        active = [c for c, (_, u) in enumerate(chains) if u >= t]
        state = stage(n_main + t, t % 2, state, active, lambda c, t=t: chains[c][1] > t,
                      lambda c, t=t: chains[c][1] == t)

    heads = []
    for hh in range(n_heads):
        parts = []
        for u in range(n_sub):
            c = hh * n_sub + u
            values(c, n_main + u, u % 2, state[c][2])
            parts.append(acc_scr[c, :MLA_V, :] / acc_scr[c, MLA_V:MLA_V + 1, :])
        heads.append(jnp.concatenate(parts, axis=1))
    o_ref[0] = jnp.concatenate(heads, axis=0).T.astype(BF16)


def _attention(q_all, k_all, v_all, bq):
    batch, _, seq, _ = q_all.shape
    sq = min(256, bq)
    n_chains = 2 * (bq // sq)
    return pl.pallas_call(
        functools.partial(_attn_body, bq, sq),
        out_shape=jax.ShapeDtypeStruct((batch, seq, N_HEADS * MLA_V), BF16),
        grid=(batch, N_HEADS // 2, seq // bq),
        in_specs=[pl.BlockSpec((1, 2, bq, HEAD_PAD), lambda b, hp, qi: (b, hp, qi, 0)),
                  pl.BlockSpec((1, 2, seq, HEAD_PAD), lambda b, hp, qi: (b, hp, 0, 0)),
                  pl.BlockSpec((1, 2, V_ROWS, seq), lambda b, hp, qi: (b, hp, 0, 0))],
        out_specs=pl.BlockSpec((1, bq, 2 * MLA_V), lambda b, hp, qi: (b, qi, hp)),
        scratch_shapes=[pltpu.VMEM((2, n_chains, sq, sq), F32),
                        pltpu.VMEM((2, n_chains, sq, sq), BF16),
                        pltpu.VMEM((n_chains, V_ROWS, sq), F32)],
        compiler_params=_cparams(("parallel", "parallel", "arbitrary")),
        name="attention",
    )(q_all, k_all, v_all)


def _pack_halves(x):
    w = x.shape[1] // 2
    a = lax.bitcast_convert_type(x[:, :w].astype(BF16).astype(F32), U32)
    b = lax.bitcast_convert_type(x[:, w:].astype(BF16).astype(F32), U32)
    return a | (b >> 16)


def _unpack_halves(p):
    a = lax.bitcast_convert_type(p & jnp.uint32(0xFFFF0000), F32)
    b = lax.bitcast_convert_type(p << 16, F32)
    return a, b


def _out_proj_body(o_ref, sg_ref, x_ref, mod_ref, gffn_ref, womla_ref, wofox_ref, wout_ref,
                   wrhi_ref, wrlo_ref, x1_ref, hp_ref, lg_ref):
    o = o_ref[...]
    half = o.shape[1] // 2
    mo = _dot(o[:, :half], womla_ref[...])
    fo = _dot(o[:, half:], wofox_ref[...])
    sg = sg_ref[...]
    merged = sg[:, :D_MODEL].astype(F32) * mo + sg[:, D_MODEL:].astype(F32) * fo
    mix = _dot(merged.astype(BF16), wout_ref[...])
    mod = mod_ref[0]
    x1 = x_ref[...] + mod[2:3, :] * mix
    x1_ref[...] = x1
    h2 = _modulated_norm(x1, gffn_ref[...], mod, 3, 4)
    hhi = h2.astype(BF16)
    hlo = (h2 - hhi.astype(F32)).astype(BF16)
    lg_ref[...] = _dot(hhi, wrhi_ref[...]) + _dot(hhi, wrlo_ref[...]) + _dot(hlo, wrhi_ref[...])
    hp_ref[...] = _pack_halves(h2)


def _out_proj(o2, sg, x2, mod8, g_ffn, w, seq, tm):
    t = x2.shape[0]
    tps = seq // tm
    consts = [w["womla"], w["wofox"], w["wout"], w["wrhi"], w["wrlo"]]
    return pl.pallas_call(
        _out_proj_body,
        out_shape=(jax.ShapeDtypeStruct((t, D_MODEL), F32),
                   jax.ShapeDtypeStruct((t, D_MODEL // 2), U32),
                   jax.ShapeDtypeStruct((t, LANES), F32)),
        grid=(t // tm,),
        in_specs=[pl.BlockSpec((tm, D_MODEL), lambda i: (i, 0)),
                  pl.BlockSpec((tm, 2 * D_MODEL), lambda i: (i, 0)),
                  pl.BlockSpec((tm, D_MODEL), lambda i: (i, 0)),
                  pl.BlockSpec((1, 8, D_MODEL), lambda i: (i // tps, 0, 0)),
                  _const_spec((1, D_MODEL))] + [_const_spec(a.shape) for a in consts],
        out_specs=(pl.BlockSpec((tm, D_MODEL), lambda i: (i, 0)),
                   pl.BlockSpec((tm, D_MODEL // 2), lambda i: (i, 0)),
                   pl.BlockSpec((tm, LANES), lambda i: (i, 0))),
        compiler_params=_cparams(("parallel",)),
        name="out_proj",
    )(o2, sg, x2, mod8, g_ffn, *consts)


def _route_body(lt_ref, b_ref, eidx_ref, rank_ref, gate_ref, cnt_ref, carry_ref):
    i = pl.program_id(0)

    @pl.when(i == 0)
    def _():
        carry_ref[...] = jnp.zeros_like(carry_ref)

    s = jax.nn.sigmoid(lt_ref[...])
    c = s + b_ref[...]
    tr = s.shape[1]
    sub = lax.broadcasted_iota(I32, (GROUP_SIZE, tr), 0).astype(F32)

    gs = []
    for g in range(N_GROUPS):
        cg = c[g * GROUP_SIZE:(g + 1) * GROUP_SIZE, :]
        m1 = jnp.max(cg, axis=0, keepdims=True)
        i1 = jnp.min(jnp.where(cg == m1, sub, float(GROUP_SIZE)), axis=0, keepdims=True)
        m2 = jnp.max(jnp.where(sub == i1, NEG_INF, cg), axis=0, keepdims=True)
        gs.append(m1 + m2)

    masked = []
    for g in range(N_GROUPS):
        beats = jnp.zeros_like(gs[g])
        for o in range(N_GROUPS):
            if o == g:
                continue
            better = (gs[o] >= gs[g]) if o < g else (gs[o] > gs[g])
            beats = beats + jnp.where(better, 1.0, 0.0)
        keep = beats < float(TOPK_GROUPS)
        cg = c[g * GROUP_SIZE:(g + 1) * GROUP_SIZE, :]
        masked.append(jnp.where(keep, cg, NEG_INF))
    mc = jnp.concatenate(masked, axis=0)

    eio = lax.broadcasted_iota(I32, (N_EXPERTS, tr), 0).astype(F32)
    picks = []
    selected = jnp.zeros((N_EXPERTS, tr), F32)
    for _ in range(TOP_K):
        m = jnp.max(mc, axis=0, keepdims=True)
        idx = jnp.min(jnp.where(mc == m, eio, float(N_EXPERTS)), axis=0, keepdims=True)
        hit = eio == idx
        picks.append(idx)
        selected = jnp.where(hit, 1.0, selected)
        mc = jnp.where(hit, -3.0e38, mc)

    ssum = jnp.sum(selected * s, axis=0, keepdims=True)
    gate_full = selected * s / ssum * ROUTED_SCALE

    r_io = lax.broadcasted_iota(I32, (tr, tr), 0)
    c_io = lax.broadcasted_iota(I32, (tr, tr), 1)
    upper = jnp.where(r_io < c_io, 1.0, 0.0).astype(BF16)
    before = _dot(selected.astype(BF16), upper) + carry_ref[...]
    carry_new = carry_ref[...] + jnp.sum(selected, axis=1, keepdims=True)
    carry_ref[...] = carry_new
    cnt_ref[...] = jnp.broadcast_to(carry_new, cnt_ref.shape).astype(I32)

    for r in range(SLOTS):
        if r < TOP_K:
            hit = eio == picks[r]
            eidx_ref[r:r + 1, :] = picks[r].astype(I32)
            rank_ref[r:r + 1, :] = jnp.sum(jnp.where(hit, before, 0.0), axis=0, keepdims=True).astype(I32)
            gate_ref[r:r + 1, :] = jnp.sum(jnp.where(hit, gate_full, 0.0), axis=0, keepdims=True)
        else:
            eidx_ref[r:r + 1, :] = jnp.zeros((1, tr), I32)
            rank_ref[r:r + 1, :] = jnp.zeros((1, tr), I32)
            gate_ref[r:r + 1, :] = jnp.zeros((1, tr), F32)


def _route(logits_t, bias_col, tr):
    t = logits_t.shape[1]
    slot_i = jax.ShapeDtypeStruct((SLOTS, t), I32)
    slot_spec = pl.BlockSpec((SLOTS, tr), lambda i: (0, i))
    return pl.pallas_call(
        _route_body,
        out_shape=(slot_i, slot_i, jax.ShapeDtypeStruct((SLOTS, t), F32),
                   jax.ShapeDtypeStruct((N_EXPERTS, LANES), I32)),
        grid=(t // tr,),
        in_specs=[pl.BlockSpec((N_EXPERTS, tr), lambda i: (0, i)),
                  _const_spec((N_EXPERTS, 1))],
        out_specs=(slot_spec, slot_spec, slot_spec, _const_spec((N_EXPERTS, LANES))),
        scratch_shapes=[pltpu.VMEM((N_EXPERTS, 1), F32)],
        compiler_params=_cparams(("arbitrary",)),
        name="route",
    )(logits_t, bias_col)


def _dest_body(pstart_ref, eidx_ref, rank_ref, o_ref):
    e = eidx_ref[...]
    d = rank_ref[...]
    for j in range(N_EXPERTS):
        d = d + jnp.where(e == j, pstart_ref[j], 0)
    o_ref[...] = d


def _dest(pstart, eidx, rank, tr):
    t = eidx.shape[1]
    spec = pl.BlockSpec((SLOTS, tr), lambda i, ps: (0, i))
    return pl.pallas_call(
        _dest_body,
        out_shape=jax.ShapeDtypeStruct((SLOTS, t), I32),
        grid_spec=pltpu.PrefetchScalarGridSpec(
            num_scalar_prefetch=1, grid=(t // tr,), in_specs=[spec, spec], out_specs=spec),
        compiler_params=_cparams(("parallel",)),
        name="dest",
    )(pstart, eidx, rank)


def _sc_mesh():
    return plsc.VectorSubcoreMesh(core_axis_name="c", subcore_axis_name="s")


def _sc_worker_id():
    return lax.axis_index("s") * SC_CORES + lax.axis_index("c")


def _dispatch(hpk, dest3, n_rows):
    _, w = hpk.shape
    n_chunks, _, c = dest3.shape
    per_worker = n_chunks // SC_WORKERS

    def body(h_hbm, d_hbm, xs_hbm, idx_v, rows_v, sem):
        wid = _sc_worker_id()

        @pl.loop(0, per_worker)
        def _(i):
            ch = wid * per_worker + i
            pltpu.sync_copy(d_hbm.at[ch], idx_v)
            pltpu.sync_copy(h_hbm.at[pl.ds(ch * c, c)], rows_v)
            copies = [pltpu.async_copy(rows_v, xs_hbm.at[idx_v.at[k]], sem) for k in range(TOP_K)]
            for cp in copies:
                cp.wait()

    return pl.kernel(
        body, mesh=_sc_mesh(),
        out_type=jax.ShapeDtypeStruct((n_rows, w), U32),
        scratch_types=[pltpu.VMEM((SLOTS, c), I32), pltpu.VMEM((c, w), U32), pltpu.SemaphoreType.DMA],
        name="dispatch",
    )(hpk, dest3)


def _experts_body(n_sub, be_ref, nv_ref, first_ref, nu_ref, xs_ref, wg_ref, wu_ref, wd_ref, ys_ref,
                  wgb_ref, wub_ref, wdb_ref):
    del be_ref
    i = pl.program_id(0)

    @pl.when(first_ref[i] == 1)
    def _():
        wgb_ref[...] = wg_ref[0].astype(BF16)
        wub_ref[...] = wu_ref[0].astype(BF16)
        wdb_ref[...] = wd_ref[0].astype(BF16)

    @pl.when(i < nu_ref[0])
    def _():
        half = D_MODEL // 2
        sub = ROW_BLOCK // n_sub
        row = lax.broadcasted_iota(I32, (sub, xs_ref.shape[1]), 0)
        gu = []
        for s in range(n_sub):
            x = jnp.where(row + s * sub < nv_ref[i], xs_ref[s * sub:(s + 1) * sub, :], jnp.uint32(0))
            xa, xb = _unpack_halves(x)
            xa = xa.astype(BF16)
            xb = xb.astype(BF16)
            g = _dot(xa, wgb_ref[:half, :]) + _dot(xb, wgb_ref[half:, :])
            u = _dot(xa, wub_ref[:half, :]) + _dot(xb, wub_ref[half:, :])
            gu.append((g, u))
        for s, (g, u) in enumerate(gu):
            hb = (g * jax.nn.sigmoid(g) * u).astype(BF16)
            ys_ref[s * sub:(s + 1) * sub, :] = _pack_halves(_dot(hb, wdb_ref[...]))

    @pl.when(i >= nu_ref[0])
    def _():
        ys_ref[...] = jnp.zeros_like(ys_ref)


def _experts(block_e, block_valid, block_first, n_used, xs, wg, wu, wd):
    n_rows, w = xs.shape
    n_blocks = n_rows // ROW_BLOCK

    def row_map(i, be, nv, bf, nu):
        return (jnp.minimum(i, nu[0] - 1), 0)

    def w_map(i, be, nv, bf, nu):
        return (be[i], 0, 0)

    return pl.pallas_call(
        functools.partial(_experts_body, ROW_BLOCK // 256),
        out_shape=jax.ShapeDtypeStruct((n_rows, w), U32),
        grid_spec=pltpu.PrefetchScalarGridSpec(
            num_scalar_prefetch=4, grid=(n_blocks,),
            in_specs=[pl.BlockSpec((ROW_BLOCK, w), row_map),
                      pl.BlockSpec((1, D_MODEL, D_EXPERT), w_map),
                      pl.BlockSpec((1, D_MODEL, D_EXPERT), w_map),
                      pl.BlockSpec((1, D_EXPERT, D_MODEL), w_map)],
            out_specs=pl.BlockSpec((ROW_BLOCK, w), lambda i, be, nv, bf, nu: (i, 0)),
            scratch_shapes=[pltpu.VMEM((D_MODEL, D_EXPERT), BF16), pltpu.VMEM((D_MODEL, D_EXPERT), BF16),
                            pltpu.VMEM((D_EXPERT, D_MODEL), BF16)]),
        compiler_params=_cparams(("arbitrary",)),
        name="experts",
    )(block_e, block_valid, block_first, n_used, xs, wg, wu, wd)


def _gather_back(ys, dest3):
    _, w = ys.shape
    n_chunks, _, c = dest3.shape
    per_worker = n_chunks // SC_WORKERS

    def body(ys_hbm, d_hbm, yk_hbm, idx_v, rows_v, gsem, osem):
        wid = _sc_worker_id()

        @pl.loop(0, per_worker)
        def _(i):
            ch = wid * per_worker + i
            pltpu.sync_copy(d_hbm.at[ch], idx_v)
            gathers = [pltpu.async_copy(ys_hbm.at[idx_v.at[k]], rows_v.at[k], gsem) for k in range(TOP_K)]
            for g in gathers:
                g.wait()
            outs = [pltpu.async_copy(rows_v.at[k], yk_hbm.at[k, pl.ds(ch * c, c)], osem) for k in range(TOP_K)]
            for o in outs:
                o.wait()

    return pl.kernel(
        body, mesh=_sc_mesh(),
        out_type=jax.ShapeDtypeStruct((TOP_K, n_chunks * c, w), U32),
        scratch_types=[pltpu.VMEM((SLOTS, c), I32), pltpu.VMEM((TOP_K, c, w), U32),
                       pltpu.SemaphoreType.DMA, pltpu.SemaphoreType.DMA],
        name="gather_back",
    )(ys, dest3)


def _combine_body(yk_ref, gate_ref, hp_ref, x1_ref, mod_ref, wsg_ref, wsu_ref, wsd_ref, gfin_ref, o_ref):
    ha, hb = _unpack_halves(hp_ref[...])
    ha = ha.astype(BF16)
    hb = hb.astype(BF16)
    half = D_MODEL // 2
    g = _dot(ha, wsg_ref[:half, :]) + _dot(hb, wsg_ref[half:, :])
    u = _dot(ha, wsu_ref[:half, :]) + _dot(hb, wsu_ref[half:, :])
    shared = _dot((g * jax.nn.sigmoid(g) * u).astype(BF16), wsd_ref[...])

    gate = gate_ref[...]
    ra = jnp.zeros(ha.shape, F32)
    rb = jnp.zeros(ha.shape, F32)
    for k in range(TOP_K):
        ya, yb = _unpack_halves(yk_ref[k])
        gk = gate[:, k:k + 1]
        ra = ra + gk * ya
        rb = rb + gk * yb
    moe = shared + jnp.concatenate([ra, rb], axis=1)
    mod = mod_ref[0]
    x2 = x1_ref[...] + mod[5:6, :] * moe
    o_ref[...] = _rms(x2) * gfin_ref[...]


def _combine(yk, gate_t, hpk, x1, mod8, w, g_final, seq, tq):
    t, wd = hpk.shape
    tps = seq // tq
    return pl.pallas_call(
        _combine_body,
        out_shape=jax.ShapeDtypeStruct((t, D_MODEL), F32),
        grid=(t // tq,),
        in_specs=[pl.BlockSpec((TOP_K, tq, wd), lambda i: (0, i, 0)),
                  pl.BlockSpec((tq, SLOTS), lambda i: (i, 0)),
                  pl.BlockSpec((tq, wd), lambda i: (i, 0)),
                  pl.BlockSpec((tq, D_MODEL), lambda i: (i, 0)),
                  pl.BlockSpec((1, 8, D_MODEL), lambda i: (i // tps, 0, 0)),
                  _const_spec(w["wsg"].shape), _const_spec(w["wsu"].shape), _const_spec(w["wsd"].shape),
                  _const_spec((1, D_MODEL))],
        out_specs=pl.BlockSpec((tq, D_MODEL), lambda i: (i, 0)),
        compiler_params=_cparams(("parallel",)),
        name="combine",
    )(yk, gate_t, hpk, x1, mod8, w["wsg"], w["wsu"], w["wsd"], g_final)


def _prep_weights(w_in, b_forget, g_q_lat, w_q_up, g_kv_lat, w_kv_up, w_o_mla, w_o_fox, w_out,
                  w_router, w_sh_gate, w_sh_up, w_sh_down):
    o_q, o_kv, o_kr = 0, MLA_Q_LORA, MLA_Q_LORA + MLA_KV_LORA
    o_fq = o_kr + MLA_ROPE
    o_fk, o_fv = o_fq + FOX_WIDTH, o_fq + 2 * FOX_WIDTH
    o_fl = o_fq + 3 * FOX_WIDTH
    o_ga = o_fl + FOX_HEADS
    o_gb = o_ga + D_MODEL
    w = {}
    w["wlat"] = w_in[:, o_q:o_kr].astype(BF16)
    half = MLA_ROPE // 2

    def rope_pair(cols):
        x1, x2 = cols[..., :half], cols[..., half:]
        z = jnp.zeros(cols.shape[:-1] + (MLA_NOPE,), cols.dtype)
        zt = jnp.zeros(cols.shape[:-1] + (HEAD_PAD - MLA_QK,), cols.dtype)
        plain = jnp.concatenate([z, x1, x2, zt], axis=-1)
        rot = jnp.concatenate([z, -x2, x1, zt], axis=-1)
        return plain, rot

    kr_plain, kr_rot = rope_pair(w_in[:, o_kr:o_fq])
    w["wkr"] = jnp.concatenate([kr_plain, kr_rot], axis=1).astype(BF16)
    w["gq"] = g_q_lat.reshape(1, -1)
    w["gkv"] = g_kv_lat.reshape(1, -1)

    scale_a = LOG2E / math.sqrt(MLA_QK)
    wq = (w_q_up * scale_a).reshape(MLA_Q_LORA, MLA_HEADS, MLA_QK)
    q_plain, q_rot = rope_pair(wq[..., MLA_NOPE:])
    nope = jnp.concatenate([wq[..., :MLA_NOPE], jnp.zeros((MLA_Q_LORA, MLA_HEADS, HEAD_PAD - MLA_NOPE), F32)], -1)
    w["wqa"] = (nope + q_plain).reshape(MLA_Q_LORA, -1).astype(BF16)
    w["wqb"] = q_rot.reshape(MLA_Q_LORA, -1).astype(BF16)

    wkv = w_kv_up.reshape(MLA_KV_LORA, MLA_HEADS, MLA_NOPE + MLA_V)
    zpad = jnp.zeros((MLA_KV_LORA, MLA_HEADS, HEAD_PAD - MLA_NOPE), F32)
    w["wka"] = jnp.concatenate([wkv[..., :MLA_NOPE], zpad], -1).reshape(MLA_KV_LORA, -1).astype(BF16)

    w["wva"] = wkv[..., MLA_NOPE:].reshape(MLA_KV_LORA, -1).astype(BF16)

    def fox_heads(cols, scale):
        v = (cols * scale).reshape(D_MODEL, FOX_HEADS, FOX_HEAD_DIM)
        v = jnp.concatenate([v, jnp.zeros((D_MODEL, FOX_HEADS, HEAD_PAD - FOX_HEAD_DIM), F32)], -1)
        return v.reshape(D_MODEL, FOX_HEADS // 2, 2 * HEAD_PAD).transpose(1, 0, 2).astype(BF16)

    w["wfq"] = fox_heads(w_in[:, o_fq:o_fk], LOG2E / math.sqrt(FOX_HEAD_DIM))
    w["wfk"] = fox_heads(w_in[:, o_fk:o_fv], 1.0)
    w["wfv"] = w_in[:, o_fv:o_fl].astype(BF16)

    wfl = jnp.concatenate([w_in[:, o_fl:o_ga]] * 3 + [jnp.zeros((D_MODEL, LANES - 3 * FOX_HEADS), F32)], 1)
    w["wfl_hi"] = wfl.astype(BF16)
    w["wfl_lo"] = (wfl - w["wfl_hi"].astype(F32)).astype(BF16)
    w["bfl"] = jnp.concatenate([b_forget] * 3 + [jnp.zeros((LANES - 3 * FOX_HEADS,), F32)]).reshape(1, LANES)

    pall = np.zeros((LANES, 2 * FOX_HEADS * HEAD_PAD), np.float32)
    cq = np.zeros((1, 2 * HEAD_PAD), np.float32)
    ck = np.zeros((1, 2 * HEAD_PAD), np.float32)
    for hd in range(FOX_HEADS):
        for term in range(3):
            pall[term * FOX_HEADS + hd, hd * HEAD_PAD + FQ_COL + term] = 1.0
            pall[term * FOX_HEADS + hd, FOX_HEADS * HEAD_PAD + hd * HEAD_PAD + FK_COL + term] = -1.0
    for j in range(2):
        cq[0, j * HEAD_PAD + FK_COL:j * HEAD_PAD + FK_COL + 3] = 1.0
        ck[0, j * HEAD_PAD + FQ_COL:j * HEAD_PAD + FQ_COL + 3] = 1.0
    w["pall"] = jnp.asarray(pall, BF16)
    w["cq"] = jnp.asarray(cq)
    w["ck"] = jnp.asarray(ck)

    w["wg"] = w_in[:, o_ga:o_gb + D_MODEL].astype(BF16)
    w["womla"] = w_o_mla.astype(BF16)
    w["wofox"] = w_o_fox.astype(BF16)
    w["wout"] = w_out.astype(BF16)
    wr = jnp.concatenate([w_router, jnp.zeros((D_MODEL, LANES - N_EXPERTS), F32)], 1)
    w["wrhi"] = wr.astype(BF16)
    w["wrlo"] = (wr - w["wrhi"].astype(F32)).astype(BF16)
    w["wsg"] = w_sh_gate.astype(BF16)
    w["wsu"] = w_sh_up.astype(BF16)
    w["wsd"] = w_sh_down.astype(BF16)
    return w


def _rope_freq_row():
    half = MLA_ROPE // 2
    inv = np.power(ROPE_THETA, -np.arange(half, dtype=np.float32) / half).astype(np.float32)
    row = np.zeros((1, LANES), np.float32)
    row[0, MLA_NOPE:MLA_NOPE + half] = inv
    row[0, MLA_NOPE + half:MLA_NOPE + 2 * half] = inv
    return jnp.asarray(row)


def kernel(x, c, positions, w_mod, b_mod, g_mix_norm, w_in, b_forget, g_q_lat, w_q_up, g_kv_lat, w_kv_up,
           w_o_mla, w_o_fox, w_out, g_ffn_norm, w_router, b_router, w_exp_gate, w_exp_up, w_exp_down,
           w_sh_gate, w_sh_up, w_sh_down, g_final):
    batch, seq, d = x.shape
    assert d == D_MODEL and w_mod.shape[0] == 1
    t = batch * seq
    tm = min(512, seq)
    bq = min(1024, seq)
    tr = min(512, t)
    tc = min(256, seq)
    assert seq % tm == 0 and seq % bq == 0 and t % tr == 0 and seq % tc == 0 and batch <= 8
    assert t % (SC_CHUNK * SC_WORKERS) == 0

    w = _prep_weights(w_in[0], b_forget[0], g_q_lat[0], w_q_up[0], g_kv_lat[0], w_kv_up[0], w_o_mla[0],
                      w_o_fox[0], w_out[0], w_router[0], w_sh_gate[0], w_sh_up[0], w_sh_down[0])

    c8 = jnp.zeros((8, D_MODEL), F32).at[:batch].set(c)
    mod = _mod(c8, w_mod[0], b_mod)
    mod8 = jnp.zeros((batch, 8, D_MODEL), F32).at[:, :N_MOD].set(mod[:batch].reshape(batch, N_MOD, D_MODEL))

    x2 = x.reshape(t, D_MODEL)
    fdec = _fox_decay(x2, mod8, g_mix_norm, w["wfl_hi"], w["wfl_lo"], w["bfl"], seq, tm)
    q_all, k_all, v_all, sg = _in_proj(x2, mod8, g_mix_norm, fdec, positions.reshape(t, 1),
                                       _rope_freq_row(), w, batch, seq, tm)
    o = _attention(q_all, k_all, v_all, bq)
    x1, hpk, logits = _out_proj(o.reshape(t, D_MODEL), sg, x2, mod8, g_ffn_norm, w, seq, tm)

    eidx, rank, gate, counts = _route(logits[:, :N_EXPERTS].T, b_router.reshape(N_EXPERTS, 1), tr)

    cnt = counts[:, 0]
    padded = (cnt + ROW_BLOCK - 1) // ROW_BLOCK * ROW_BLOCK
    pend = jnp.cumsum(padded)
    pstart = pend - padded
    n_blocks = t * TOP_K // ROW_BLOCK + N_EXPERTS
    n_rows = n_blocks * ROW_BLOCK
    block_row = jnp.arange(n_blocks, dtype=I32) * ROW_BLOCK
    block_e = jnp.minimum(jnp.sum(pend[None, :] <= block_row[:, None], axis=1), N_EXPERTS - 1).astype(I32)
    n_used = (pend[-1:] // ROW_BLOCK).astype(I32)
    block_valid = jnp.clip((pstart + cnt)[block_e] - block_row, 0, ROW_BLOCK).astype(I32)
    block_first = jnp.concatenate([jnp.ones((1,), I32), (block_e[1:] != block_e[:-1]).astype(I32)])

    dest = _dest(pstart.astype(I32), eidx, rank, tr)
    dest3 = dest.reshape(SLOTS, t // SC_CHUNK, SC_CHUNK).transpose(1, 0, 2)
    xs = _dispatch(hpk, dest3, n_rows)
    ys = _experts(block_e, block_valid, block_first, n_used, xs, w_exp_gate[0], w_exp_up[0], w_exp_down[0])
    yk = _gather_back(ys, dest3)
    out = _combine(yk, gate.T, hpk, x1, mod8, w, g_final.reshape(1, D_MODEL), seq, tc)
    return out.reshape(batch, seq, D_MODEL)
```

```python
import functools
import math

import numpy as np
import jax
import jax.numpy as jnp
from jax import lax
from jax.experimental import pallas as pl
from jax.experimental.pallas import tpu as pltpu
from jax.experimental.pallas import tpu_sc as plsc

F32 = jnp.float32
BF16 = jnp.bfloat16
I32 = jnp.int32
U32 = jnp.uint32

D_MODEL = 1024
MLA_HEADS = 8
MLA_Q_LORA = 256
MLA_KV_LORA = 128
MLA_NOPE = 64
MLA_ROPE = 32
MLA_V = 64
MLA_QK = MLA_NOPE + MLA_ROPE
ROPE_THETA = 10000.0
FOX_HEADS = 8
FOX_HEAD_DIM = 64
FOX_WIDTH = FOX_HEADS * FOX_HEAD_DIM
N_HEADS = MLA_HEADS + FOX_HEADS
N_EXPERTS = 64
N_GROUPS = 8
GROUP_SIZE = N_EXPERTS // N_GROUPS
TOPK_GROUPS = 4
TOP_K = 6
D_EXPERT = 256
ROUTED_SCALE = 2.5
N_MOD = 6
NORM_EPS = 1e-6
NEG_INF = -1e30
LOG2E = math.log2(math.e)

LANES = 128
HEAD_PAD = 128
V_ROWS = MLA_V + 16
ROW_BLOCK = 512
SLOTS = 8
SC_CORES = 2
SC_SUBCORES = 16
SC_WORKERS = SC_CORES * SC_SUBCORES
SC_CHUNK = 32
VMEM_LIMIT = 56 * 1024 * 1024

FQ_COL = FOX_HEAD_DIM
FK_COL = FOX_HEAD_DIM + 3


def _cparams(sem, vmem=VMEM_LIMIT):
    return pltpu.CompilerParams(dimension_semantics=sem, vmem_limit_bytes=vmem)


def _const_spec(shape):
    nd = len(shape)
    return pl.BlockSpec(shape, lambda *_: (0,) * nd)


def _rms(x):
    return x * lax.rsqrt(jnp.mean(x * x, axis=-1, keepdims=True) + NORM_EPS)


def _split3(x):
    hi = x.astype(BF16)
    r = x - hi.astype(F32)
    mid = r.astype(BF16)
    lo = (r - mid.astype(F32)).astype(BF16)
    return hi, mid, lo


def _dot(a, b):
    return jnp.dot(a, b, preferred_element_type=F32)


def _modulated_norm(x, gain, mod, shift_row, scale_row):
    shift = mod[shift_row:shift_row + 1, :]
    scale = mod[scale_row:scale_row + 1, :]
    return _rms(x) * gain * (1.0 + scale) + shift


def _mod_body(c_ref, w_ref, b_ref, o_ref):
    c = c_ref[...]
    cond = c * jax.nn.sigmoid(c)
    o_ref[...] = _dot(cond.astype(BF16), w_ref[...].astype(BF16)) + b_ref[...]


def _mod(c8, w_mod, b_mod):
    n = w_mod.shape[1]
    tn = D_MODEL
    return pl.pallas_call(
        _mod_body,
        out_shape=jax.ShapeDtypeStruct((8, n), F32),
        grid=(n // tn,),
        in_specs=[_const_spec((8, D_MODEL)),
                  pl.BlockSpec((D_MODEL, tn), lambda j: (0, j)),
                  pl.BlockSpec((1, tn), lambda j: (0, j))],
        out_specs=pl.BlockSpec((8, tn), lambda j: (0, j)),
        compiler_params=_cparams(("parallel",)),
        name="mod",
    )(c8, w_mod, b_mod)


def _decay_body(tiles_per_seq, x_ref, mod_ref, g_ref, whi_ref, wlo_ref, b_ref, o_ref, carry_ref):
    i = pl.program_id(0)

    @pl.when(i % tiles_per_seq == 0)
    def _():
        carry_ref[...] = jnp.zeros_like(carry_ref)

    h = _modulated_norm(x_ref[...], g_ref[...], mod_ref[0], 0, 1)
    hhi = h.astype(BF16)
    hlo = (h - hhi.astype(F32)).astype(BF16)
    z = _dot(hhi, whi_ref[...]) + _dot(hhi, wlo_ref[...]) + _dot(hlo, whi_ref[...]) + b_ref[...]
    logf = jnp.minimum(z, 0.0) - jnp.log1p(jnp.exp(-jnp.abs(z)))
    tm = logf.shape[0]
    row = lax.broadcasted_iota(I32, (tm, tm), 0)
    col = lax.broadcasted_iota(I32, (tm, tm), 1)
    tri = jnp.where(col <= row, 1.0, 0.0).astype(BF16)
    hi, mid, lo = _split3(logf)
    cum = _dot(tri, hi) + _dot(tri, mid) + _dot(tri, lo) + carry_ref[...]
    o_ref[...] = cum
    carry_ref[...] = cum[tm - 1:tm, :]


def _fox_decay(x2, mod8, g_mix, wfl_hi, wfl_lo, bfl, seq, tm):
    t = x2.shape[0]
    tps = seq // tm
    return pl.pallas_call(
        functools.partial(_decay_body, tps),
        out_shape=jax.ShapeDtypeStruct((t, LANES), F32),
        grid=(t // tm,),
        in_specs=[pl.BlockSpec((tm, D_MODEL), lambda i: (i, 0)),
                  pl.BlockSpec((1, 8, D_MODEL), lambda i: (i // tps, 0, 0)),
                  _const_spec((1, D_MODEL)),
                  _const_spec((D_MODEL, LANES)),
                  _const_spec((D_MODEL, LANES)),
                  _const_spec((1, LANES))],
        out_specs=pl.BlockSpec((tm, LANES), lambda i: (i, 0)),
        scratch_shapes=[pltpu.VMEM((1, LANES), F32)],
        compiler_params=_cparams(("arbitrary",)),
        name="fox_decay",
    )(x2, mod8, g_mix, wfl_hi, wfl_lo, bfl)


def _in_proj_body(x_ref, mod_ref, g_ref, f_ref, pos_ref, freq_ref,
                  wlat_ref, wkr_ref, gq_ref, gkv_ref, wqa_ref, wqb_ref, wka_ref, wva_ref,
                  wfq_ref, wfk_ref, wfv_ref, pall_ref, cq_ref, ck_ref, wg_ref,
                  q_ref, k_ref, v_ref, sg_ref):
    h = _modulated_norm(x_ref[...], g_ref[...], mod_ref[0], 0, 1)
    hb = h.astype(BF16)

    lat = _dot(hb, wlat_ref[...])
    qn = (_rms(lat[:, :MLA_Q_LORA]) * gq_ref[...]).astype(BF16)
    kvn = (_rms(lat[:, MLA_Q_LORA:]) * gkv_ref[...]).astype(BF16)
    ang = pos_ref[...].astype(F32) * freq_ref[...]
    cs = jnp.cos(ang)
    sn = jnp.sin(ang)
    kr = _dot(hb, wkr_ref[...])
    kpe = kr[:, :HEAD_PAD] * cs + kr[:, HEAD_PAD:] * sn
    qa = _dot(qn, wqa_ref[...])
    qb = _dot(qn, wqb_ref[...])
    ka = _dot(kvn, wka_ref[...])
    va = _dot(kvn, wva_ref[...])
    for hd in range(MLA_HEADS):
        sl = slice(hd * HEAD_PAD, (hd + 1) * HEAD_PAD)
        q_ref[0, hd] = (qa[:, sl] * cs + qb[:, sl] * sn).astype(BF16)
        k_ref[0, hd] = (ka[:, sl] + kpe).astype(BF16)
    ones_rows = jnp.ones((V_ROWS - MLA_V, va.shape[0]), BF16)

    def put_values(first_head, vals):
        for hp in range(vals.shape[1] // (2 * MLA_V)):
            vt = vals[:, hp * 2 * MLA_V:(hp + 1) * 2 * MLA_V].T.astype(BF16)
            for j in range(2):
                v_ref[0, first_head + 2 * hp + j, :MLA_V, :] = vt[j * MLA_V:(j + 1) * MLA_V, :]
                v_ref[0, first_head + 2 * hp + j, MLA_V:, :] = ones_rows

    put_values(0, va)

    hi, mid, lo = _split3(f_ref[...] * LOG2E)
    lane = lax.broadcasted_iota(I32, hi.shape, 1)
    f3 = jnp.where(lane < FOX_HEADS, hi, jnp.where(lane < 2 * FOX_HEADS, mid, lo))
    fp = _dot(f3, pall_ref[...])
    half = FOX_HEADS // 2 * 2 * HEAD_PAD
    for hp in range(FOX_HEADS // 2):
        sl = slice(hp * 2 * HEAD_PAD, (hp + 1) * 2 * HEAD_PAD)
        q2 = (_dot(hb, wfq_ref[hp]) + fp[:, sl] + cq_ref[...]).astype(BF16)
        k2 = (_dot(hb, wfk_ref[hp]) + fp[:, half + hp * 2 * HEAD_PAD:half + (hp + 1) * 2 * HEAD_PAD]
              + ck_ref[...]).astype(BF16)
        for j in range(2):
            hd = MLA_HEADS + 2 * hp + j
            q_ref[0, hd] = q2[:, j * HEAD_PAD:(j + 1) * HEAD_PAD]
            k_ref[0, hd] = k2[:, j * HEAD_PAD:(j + 1) * HEAD_PAD]
    put_values(MLA_HEADS, _dot(hb, wfv_ref[...]))

    sg_ref[...] = jax.nn.sigmoid(_dot(hb, wg_ref[...])).astype(BF16)


def _in_proj(x2, mod8, g_mix, fdec, pos, freq, w, batch, seq, tm):
    t = x2.shape[0]
    tps = seq // tm
    consts = [w["wlat"], w["wkr"], w["gq"], w["gkv"], w["wqa"], w["wqb"], w["wka"], w["wva"],
              w["wfq"], w["wfk"], w["wfv"], w["pall"], w["cq"], w["ck"], w["wg"]]
    head_shape = jax.ShapeDtypeStruct((batch, N_HEADS, seq, HEAD_PAD), BF16)
    head_spec = pl.BlockSpec((1, N_HEADS, tm, HEAD_PAD), lambda i: (i // tps, 0, i % tps, 0))
    pair_shape = jax.ShapeDtypeStruct((batch, N_HEADS, V_ROWS, seq), BF16)
    pair_spec = pl.BlockSpec((1, N_HEADS, V_ROWS, tm), lambda i: (i // tps, 0, 0, i % tps))
    return pl.pallas_call(
        _in_proj_body,
        out_shape=(head_shape, head_shape, pair_shape,
                   jax.ShapeDtypeStruct((t, 2 * D_MODEL), BF16)),
        grid=(t // tm,),
        in_specs=[pl.BlockSpec((tm, D_MODEL), lambda i: (i, 0)),
                  pl.BlockSpec((1, 8, D_MODEL), lambda i: (i // tps, 0, 0)),
                  _const_spec((1, D_MODEL)),
                  pl.BlockSpec((tm, LANES), lambda i: (i, 0)),
                  pl.BlockSpec((tm, 1), lambda i: (i, 0)),
                  _const_spec((1, LANES))] + [_const_spec(a.shape) for a in consts],
        out_specs=(head_spec, head_spec, pair_spec,
                   pl.BlockSpec((tm, 2 * D_MODEL), lambda i: (i, 0))),
        compiler_params=_cparams(("parallel",)),
        name="in_proj",
    )(x2, mod8, g_mix, fdec, pos, freq, *consts)


def _attn_body(bq, sq, q_ref, k_ref, v_ref, o_ref, s_scr, p_scr, acc_scr):
    qi = pl.program_id(2)
    bk = sq
    n_heads = q_ref.shape[1]
    n_sub = bq // sq
    assert n_sub % 2 == 0
    chains = [(hh, u) for hh in range(n_heads) for u in range(n_sub)]
    n_chains = len(chains)
    n_main = qi * n_sub

    def chunk_start(j):
        return pl.multiple_of(jnp.maximum(j, 0) * bk, bk)

    def scores(c, j, par):
        hh, u = chains[c]
        k = k_ref[0, hh, pl.ds(chunk_start(j), bk), :]
        q = q_ref[0, hh, u * sq:(u + 1) * sq, :]
        s = lax.dot_general(k, q, (((1,), (1,)), ((), ())), preferred_element_type=F32)
        s_scr[par, c] = s
        return jnp.max(s, axis=0, keepdims=True)

    def values(c, j, par, alpha):
        vt = v_ref[0, chains[c][0], :, pl.ds(chunk_start(j), bk)]
        acc_scr[c] = alpha * acc_scr[c] + _dot(vt, p_scr[par, c])

    def softmax(c, par, m, smax, masked):
        s = s_scr[par, c]
        if masked:
            key = lax.broadcasted_iota(I32, (bk, sq), 0)
            qry = lax.broadcasted_iota(I32, (bk, sq), 1)
            s = jnp.where(key <= qry, s, NEG_INF)
            smax = jnp.max(s, axis=0, keepdims=True)
        m_new = jnp.maximum(m, smax)
        p_scr[par, c] = jnp.exp2((s - m_new).astype(BF16))
        return m_new, jnp.exp2(m - m_new)

    def stage(j, par, state, active, has_next, masked_of):
        nxt = {c: scores(c, j + 1, 1 - par) if has_next(c) else state[c][1] for c in active}
        for c in active:
            values(c, j - 1, 1 - par, state[c][2])
        out = list(state)
        for c in active:
            m, smax, _ = state[c]
            m_new, alpha = softmax(c, par, m, smax, masked_of(c))
            out[c] = (m_new, nxt[c], alpha)
        return out

    p_scr[1] = jnp.zeros(p_scr.shape[1:], BF16)
    acc_scr[...] = jnp.zeros_like(acc_scr)
    state = [(jnp.full((1, sq), NEG_INF, F32), scores(c, 0, 0), jnp.ones((1, sq), F32)) for c in range(n_chains)]

    def step(jj, st):
        for par in range(2):
            st = stage(2 * jj + par, par, list(st), range(n_chains), lambda c: True, lambda c: False)
        return tuple(st)

    state = list(lax.fori_loop(0, n_main // 2, step, tuple(state)))

    for t in range(n_sub):
        active = [c for c, (_, u) in enumerate(chains) if u >= t]
        state = stage(n_main + t, t % 2, state, active, lambda c, t=t: chains[c][1] > t,
                      lambda c, t=t: chains[c][1] == t)

    heads = []
    for hh in range(n_heads):
        parts = []
        for u in range(n_sub):
            c = hh * n_sub + u
            values(c, n_main + u, u % 2, state[c][2])
            parts.append(acc_scr[c, :MLA_V, :] / acc_scr[c, MLA_V:MLA_V + 1, :])
        heads.append(jnp.concatenate(parts, axis=1))
    o_ref[0] = jnp.concatenate(heads, axis=0).T.astype(BF16)


def _attention(q_all, k_all, v_all, bq):
    batch, _, seq, _ = q_all.shape
    sq = min(256, bq)
    n_chains = 2 * (bq // sq)
    return pl.pallas_call(
        functools.partial(_attn_body, bq, sq),
        out_shape=jax.ShapeDtypeStruct((batch, seq, N_HEADS * MLA_V), BF16),
        grid=(batch, N_HEADS // 2, seq // bq),
        in_specs=[pl.BlockSpec((1, 2, bq, HEAD_PAD), lambda b, hp, qi: (b, hp, qi, 0)),
                  pl.BlockSpec((1, 2, seq, HEAD_PAD), lambda b, hp, qi: (b, hp, 0, 0)),
                  pl.BlockSpec((1, 2, V_ROWS, seq), lambda b, hp, qi: (b, hp, 0, 0))],
        out_specs=pl.BlockSpec((1, bq, 2 * MLA_V), lambda b, hp, qi: (b, qi, hp)),
        scratch_shapes=[pltpu.VMEM((2, n_chains, sq, sq), F32),
                        pltpu.VMEM((2, n_chains, sq, sq), BF16),
                        pltpu.VMEM((n_chains, V_ROWS, sq), F32)],
        compiler_params=_cparams(("parallel", "parallel", "arbitrary")),
        name="attention",
    )(q_all, k_all, v_all)


def _pack_halves(x):
    w = x.shape[1] // 2
    a = lax.bitcast_convert_type(x[:, :w].astype(BF16).astype(F32), U32)
    b = lax.bitcast_convert_type(x[:, w:].astype(BF16).astype(F32), U32)
    return a | (b >> 16)


def _unpack_halves(p):
    a = lax.bitcast_convert_type(p & jnp.uint32(0xFFFF0000), F32)
    b = lax.bitcast_convert_type(p << 16, F32)
    return a, b


def _out_proj_body(o_ref, sg_ref, x_ref, mod_ref, gffn_ref, womla_ref, wofox_ref, wout_ref,
                   wrhi_ref, wrlo_ref, x1_ref, hp_ref, lg_ref):
    o = o_ref[...]
    half = o.shape[1] // 2
    mo = _dot(o[:, :half], womla_ref[...])
    fo = _dot(o[:, half:], wofox_ref[...])
    sg = sg_ref[...]
    merged = sg[:, :D_MODEL].astype(F32) * mo + sg[:, D_MODEL:].astype(F32) * fo
    mix = _dot(merged.astype(BF16), wout_ref[...])
    mod = mod_ref[0]
    x1 = x_ref[...] + mod[2:3, :] * mix
    x1_ref[...] = x1
    h2 = _modulated_norm(x1, gffn_ref[...], mod, 3, 4)
    hhi = h2.astype(BF16)
    hlo = (h2 - hhi.astype(F32)).astype(BF16)
    lg_ref[...] = _dot(hhi, wrhi_ref[...]) + _dot(hhi, wrlo_ref[...]) + _dot(hlo, wrhi_ref[...])
    hp_ref[...] = _pack_halves(h2)


def _out_proj(o2, sg, x2, mod8, g_ffn, w, seq, tm):
    t = x2.shape[0]
    tps = seq // tm
    consts = [w["womla"], w["wofox"], w["wout"], w["wrhi"], w["wrlo"]]
    return pl.pallas_call(
        _out_proj_body,
        out_shape=(jax.ShapeDtypeStruct((t, D_MODEL), F32),
                   jax.ShapeDtypeStruct((t, D_MODEL // 2), U32),
                   jax.ShapeDtypeStruct((t, LANES), F32)),
        grid=(t // tm,),
        in_specs=[pl.BlockSpec((tm, D_MODEL), lambda i: (i, 0)),
                  pl.BlockSpec((tm, 2 * D_MODEL), lambda i: (i, 0)),
                  pl.BlockSpec((tm, D_MODEL), lambda i: (i, 0)),
                  pl.BlockSpec((1, 8, D_MODEL), lambda i: (i // tps, 0, 0)),
                  _const_spec((1, D_MODEL))] + [_const_spec(a.shape) for a in consts],
        out_specs=(pl.BlockSpec((tm, D_MODEL), lambda i: (i, 0)),
                   pl.BlockSpec((tm, D_MODEL // 2), lambda i: (i, 0)),
                   pl.BlockSpec((tm, LANES), lambda i: (i, 0))),
        compiler_params=_cparams(("parallel",)),
        name="out_proj",
    )(o2, sg, x2, mod8, g_ffn, *consts)


def _route_body(lt_ref, b_ref, eidx_ref, rank_ref, gate_ref, cnt_ref, carry_ref):
    i = pl.program_id(0)

    @pl.when(i == 0)
    def _():
        carry_ref[...] = jnp.zeros_like(carry_ref)

    s = jax.nn.sigmoid(lt_ref[...])
    c = s + b_ref[...]
    tr = s.shape[1]
    sub = lax.broadcasted_iota(I32, (GROUP_SIZE, tr), 0).astype(F32)

    gs = []
    for g in range(N_GROUPS):
        cg = c[g * GROUP_SIZE:(g + 1) * GROUP_SIZE, :]
        m1 = jnp.max(cg, axis=0, keepdims=True)
        i1 = jnp.min(jnp.where(cg == m1, sub, float(GROUP_SIZE)), axis=0, keepdims=True)
        m2 = jnp.max(jnp.where(sub == i1, NEG_INF, cg), axis=0, keepdims=True)
        gs.append(m1 + m2)

    masked = []
    for g in range(N_GROUPS):
        beats = jnp.zeros_like(gs[g])
        for o in range(N_GROUPS):
            if o == g:
                continue
            better = (gs[o] >= gs[g]) if o < g else (gs[o] > gs[g])
            beats = beats + jnp.where(better, 1.0, 0.0)
        keep = beats < float(TOPK_GROUPS)
        cg = c[g * GROUP_SIZE:(g + 1) * GROUP_SIZE, :]
        masked.append(jnp.where(keep, cg, NEG_INF))
    mc = jnp.concatenate(masked, axis=0)

    eio = lax.broadcasted_iota(I32, (N_EXPERTS, tr), 0).astype(F32)
    picks = []
    selected = jnp.zeros((N_EXPERTS, tr), F32)
    for _ in range(TOP_K):
        m = jnp.max(mc, axis=0, keepdims=True)
        idx = jnp.min(jnp.where(mc == m, eio, float(N_EXPERTS)), axis=0, keepdims=True)
        hit = eio == idx
        picks.append(idx)
        selected = jnp.where(hit, 1.0, selected)
        mc = jnp.where(hit, -3.0e38, mc)

    ssum = jnp.sum(selected * s, axis=0, keepdims=True)
    gate_full = selected * s / ssum * ROUTED_SCALE

    r_io = lax.broadcasted_iota(I32, (tr, tr), 0)
    c_io = lax.broadcasted_iota(I32, (tr, tr), 1)
    upper = jnp.where(r_io < c_io, 1.0, 0.0).astype(BF16)
    before = _dot(selected.astype(BF16), upper) + carry_ref[...]
    carry_new = carry_ref[...] + jnp.sum(selected, axis=1, keepdims=True)
    carry_ref[...] = carry_new
    cnt_ref[...] = jnp.broadcast_to(carry_new, cnt_ref.shape).astype(I32)

    for r in range(SLOTS):
        if r < TOP_K:
            hit = eio == picks[r]
            eidx_ref[r:r + 1, :] = picks[r].astype(I32)
            rank_ref[r:r + 1, :] = jnp.sum(jnp.where(hit, before, 0.0), axis=0, keepdims=True).astype(I32)
            gate_ref[r:r + 1, :] = jnp.sum(jnp.where(hit, gate_full, 0.0), axis=0, keepdims=True)
        else:
            eidx_ref[r:r + 1, :] = jnp.zeros((1, tr), I32)
            rank_ref[r:r + 1, :] = jnp.zeros((1, tr), I32)
            gate_ref[r:r + 1, :] = jnp.zeros((1, tr), F32)


def _route(logits_t, bias_col, tr):
    t = logits_t.shape[1]
    slot_i = jax.ShapeDtypeStruct((SLOTS, t), I32)
    slot_spec = pl.BlockSpec((SLOTS, tr), lambda i: (0, i))
    return pl.pallas_call(
        _route_body,
        out_shape=(slot_i, slot_i, jax.ShapeDtypeStruct((SLOTS, t), F32),
                   jax.ShapeDtypeStruct((N_EXPERTS, LANES), I32)),
        grid=(t // tr,),
        in_specs=[pl.BlockSpec((N_EXPERTS, tr), lambda i: (0, i)),
                  _const_spec((N_EXPERTS, 1))],
        out_specs=(slot_spec, slot_spec, slot_spec, _const_spec((N_EXPERTS, LANES))),
        scratch_shapes=[pltpu.VMEM((N_EXPERTS, 1), F32)],
        compiler_params=_cparams(("arbitrary",)),
        name="route",
    )(logits_t, bias_col)


def _dest_body(pstart_ref, eidx_ref, rank_ref, o_ref):
    e = eidx_ref[...]
    d = rank_ref[...]
    for j in range(N_EXPERTS):
        d = d + jnp.where(e == j, pstart_ref[j], 0)
    o_ref[...] = d


def _dest(pstart, eidx, rank, tr):
    t = eidx.shape[1]
    spec = pl.BlockSpec((SLOTS, tr), lambda i, ps: (0, i))
    return pl.pallas_call(
        _dest_body,
        out_shape=jax.ShapeDtypeStruct((SLOTS, t), I32),
        grid_spec=pltpu.PrefetchScalarGridSpec(
            num_scalar_prefetch=1, grid=(t // tr,), in_specs=[spec, spec], out_specs=spec),
        compiler_params=_cparams(("parallel",)),
        name="dest",
    )(pstart, eidx, rank)


def _sc_mesh():
    return plsc.VectorSubcoreMesh(core_axis_name="c", subcore_axis_name="s")


def _sc_worker_id():
    return lax.axis_index("s") * SC_CORES + lax.axis_index("c")


def _dispatch(hpk, dest3, n_rows):
    _, w = hpk.shape
    n_chunks, _, c = dest3.shape
    per_worker = n_chunks // SC_WORKERS

    def body(h_hbm, d_hbm, xs_hbm, idx_v, rows_v, sem):
        wid = _sc_worker_id()

        @pl.loop(0, per_worker)
        def _(i):
            ch = wid * per_worker + i
            pltpu.sync_copy(d_hbm.at[ch], idx_v)
            pltpu.sync_copy(h_hbm.at[pl.ds(ch * c, c)], rows_v)
            copies = [pltpu.async_copy(rows_v, xs_hbm.at[idx_v.at[k]], sem) for k in range(TOP_K)]
            for cp in copies:
                cp.wait()

    return pl.kernel(
        body, mesh=_sc_mesh(),
        out_type=jax.ShapeDtypeStruct((n_rows, w), U32),
        scratch_types=[pltpu.VMEM((SLOTS, c), I32), pltpu.VMEM((c, w), U32), pltpu.SemaphoreType.DMA],
        name="dispatch",
    )(hpk, dest3)


def _experts_body(n_sub, first_ref, nblk_ref, cnt_ref, xs_hbm, wg_ref, wu_ref, wd_ref, ys_hbm,
                  wgb_ref, wub_ref, wdb_ref, xbuf, ybuf, xsem, ysem):
    e = pl.program_id(0)
    nb = nblk_ref[e]
    b0 = first_ref[e]
    cnt = cnt_ref[e]
    w = xbuf.shape[2]

    def x_copy(j, slot):
        start = pl.multiple_of((b0 + j) * ROW_BLOCK, ROW_BLOCK)
        return pltpu.make_async_copy(xs_hbm.at[pl.ds(start, ROW_BLOCK), :], xbuf.at[slot], xsem.at[slot])

    def y_copy(j, slot):
        start = pl.multiple_of((b0 + j) * ROW_BLOCK, ROW_BLOCK)
        return pltpu.make_async_copy(ybuf.at[slot], ys_hbm.at[pl.ds(start, ROW_BLOCK), :], ysem.at[slot])

    @pl.when(nb > 0)
    def _():
        x_copy(0, 0).start()

    wgb_ref[...] = wg_ref[0].astype(BF16)
    wub_ref[...] = wu_ref[0].astype(BF16)
    wdb_ref[...] = wd_ref[0].astype(BF16)

    half = D_MODEL // 2
    sub = ROW_BLOCK // n_sub
    row = lax.broadcasted_iota(I32, (sub, w), 0)

    def block(j, slot):
        x_copy(j, slot).wait()

        @pl.when(j + 1 < nb)
        def _():
            x_copy(j + 1, 1 - slot).start()

        @pl.when(j >= 2)
        def _():
            y_copy(j - 2, slot).wait()

        valid = cnt - j * ROW_BLOCK
        gu = []
        for s in range(n_sub):
            x = jnp.where(row + s * sub < valid, xbuf[slot, s * sub:(s + 1) * sub, :], jnp.uint32(0))
            xa, xb = _unpack_halves(x)
            xa = xa.astype(BF16)
            xb = xb.astype(BF16)
            g = _dot(xa, wgb_ref[:half, :]) + _dot(xb, wgb_ref[half:, :])
            u = _dot(xa, wub_ref[:half, :]) + _dot(xb, wub_ref[half:, :])
            gu.append((g, u))
        for s, (g, u) in enumerate(gu):
            hb = (g * jax.nn.sigmoid(g) * u).astype(BF16)
            ybuf[slot, s * sub:(s + 1) * sub, :] = _pack_halves(_dot(hb, wdb_ref[...]))
        y_copy(j, slot).start()

    def pair(jj, _):
        for slot in range(2):
            j = 2 * jj + slot

            @pl.when(j < nb)
            def _():
                block(j, slot)
        return 0

    lax.fori_loop(0, (nb + 1) // 2, pair, 0)

    for slot in range(2):
        @pl.when(nb > slot)
        def _():
            y_copy(0, slot).wait()

    @pl.when(e == pl.num_programs(0) - 1)
    def _():
        n_blocks = ys_hbm.shape[0] // ROW_BLOCK
        first_tail = b0 + nb
        ybuf[0] = jnp.zeros(ybuf.shape[1:], U32)

        def tail_copy(b):
            start = pl.multiple_of(b * ROW_BLOCK, ROW_BLOCK)
            return pltpu.make_async_copy(ybuf.at[0], ys_hbm.at[pl.ds(start, ROW_BLOCK), :], ysem.at[0])

        lax.fori_loop(first_tail, n_blocks, lambda b, c: (tail_copy(b).start(), c)[1], 0)
        lax.fori_loop(first_tail, n_blocks, lambda b, c: (tail_copy(b).wait(), c)[1], 0)


def _experts(first_blk, n_blk, cnt, xs, wg, wu, wd):
    n_rows, w = xs.shape

    def w_map(e, fb, nb, ct):
        return (e, 0, 0)

    return pl.pallas_call(
        functools.partial(_experts_body, ROW_BLOCK // 256),
        out_shape=jax.ShapeDtypeStruct((n_rows, w), U32),
        grid_spec=pltpu.PrefetchScalarGridSpec(
            num_scalar_prefetch=3, grid=(N_EXPERTS,),
            in_specs=[pl.BlockSpec(memory_space=pl.ANY),
                      pl.BlockSpec((1, D_MODEL, D_EXPERT), w_map),
                      pl.BlockSpec((1, D_MODEL, D_EXPERT), w_map),
                      pl.BlockSpec((1, D_EXPERT, D_MODEL), w_map)],
            out_specs=pl.BlockSpec(memory_space=pl.ANY),
            scratch_shapes=[pltpu.VMEM((D_MODEL, D_EXPERT), BF16), pltpu.VMEM((D_MODEL, D_EXPERT), BF16),
                            pltpu.VMEM((D_EXPERT, D_MODEL), BF16),
                            pltpu.VMEM((2, ROW_BLOCK, w), U32), pltpu.VMEM((2, ROW_BLOCK, w), U32),
                            pltpu.SemaphoreType.DMA((2,)), pltpu.SemaphoreType.DMA((2,))]),
        compiler_params=_cparams(("arbitrary",)),
        name="experts",
    )(first_blk, n_blk, cnt, xs, wg, wu, wd)


def _gather_back(ys, dest3):
    _, w = ys.shape
    n_chunks, _, c = dest3.shape
    per_worker = n_chunks // SC_WORKERS

    def body(ys_hbm, d_hbm, yk_hbm, idx_v, rows_v, gsem, osem):
        wid = _sc_worker_id()

        @pl.loop(0, per_worker)
        def _(i):
            ch = wid * per_worker + i
            pltpu.sync_copy(d_hbm.at[ch], idx_v)
            gathers = [pltpu.async_copy(ys_hbm.at[idx_v.at[k]], rows_v.at[k], gsem) for k in range(TOP_K)]
            for g in gathers:
                g.wait()
            outs = [pltpu.async_copy(rows_v.at[k], yk_hbm.at[k, pl.ds(ch * c, c)], osem) for k in range(TOP_K)]
            for o in outs:
                o.wait()

    return pl.kernel(
        body, mesh=_sc_mesh(),
        out_type=jax.ShapeDtypeStruct((TOP_K, n_chunks * c, w), U32),
        scratch_types=[pltpu.VMEM((SLOTS, c), I32), pltpu.VMEM((TOP_K, c, w), U32),
                       pltpu.SemaphoreType.DMA, pltpu.SemaphoreType.DMA],
        name="gather_back",
    )(ys, dest3)


def _combine_body(yk_ref, gate_ref, hp_ref, x1_ref, mod_ref, wsg_ref, wsu_ref, wsd_ref, gfin_ref, o_ref):
    ha, hb = _unpack_halves(hp_ref[...])
    ha = ha.astype(BF16)
    hb = hb.astype(BF16)
    half = D_MODEL // 2
    g = _dot(ha, wsg_ref[:half, :]) + _dot(hb, wsg_ref[half:, :])
    u = _dot(ha, wsu_ref[:half, :]) + _dot(hb, wsu_ref[half:, :])
    shared = _dot((g * jax.nn.sigmoid(g) * u).astype(BF16), wsd_ref[...])

    gate = gate_ref[...]
    ra = jnp.zeros(ha.shape, F32)
    rb = jnp.zeros(ha.shape, F32)
    for k in range(TOP_K):
        ya, yb = _unpack_halves(yk_ref[k])
        gk = gate[:, k:k + 1]
        ra = ra + gk * ya
        rb = rb + gk * yb
    moe = shared + jnp.concatenate([ra, rb], axis=1)
    mod = mod_ref[0]
    x2 = x1_ref[...] + mod[5:6, :] * moe
    o_ref[...] = _rms(x2) * gfin_ref[...]


def _combine(yk, gate_t, hpk, x1, mod8, w, g_final, seq, tq):
    t, wd = hpk.shape
    tps = seq // tq
    return pl.pallas_call(
        _combine_body,
        out_shape=jax.ShapeDtypeStruct((t, D_MODEL), F32),
        grid=(t // tq,),
        in_specs=[pl.BlockSpec((TOP_K, tq, wd), lambda i: (0, i, 0)),
                  pl.BlockSpec((tq, SLOTS), lambda i: (i, 0)),
                  pl.BlockSpec((tq, wd), lambda i: (i, 0)),
                  pl.BlockSpec((tq, D_MODEL), lambda i: (i, 0)),
                  pl.BlockSpec((1, 8, D_MODEL), lambda i: (i // tps, 0, 0)),
                  _const_spec(w["wsg"].shape), _const_spec(w["wsu"].shape), _const_spec(w["wsd"].shape),
                  _const_spec((1, D_MODEL))],
        out_specs=pl.BlockSpec((tq, D_MODEL), lambda i: (i, 0)),
        compiler_params=_cparams(("parallel",)),
        name="combine",
    )(yk, gate_t, hpk, x1, mod8, w["wsg"], w["wsu"], w["wsd"], g_final)


def _prep_weights(w_in, b_forget, g_q_lat, w_q_up, g_kv_lat, w_kv_up, w_o_mla, w_o_fox, w_out,
                  w_router, w_sh_gate, w_sh_up, w_sh_down):
    o_q, o_kv, o_kr = 0, MLA_Q_LORA, MLA_Q_LORA + MLA_KV_LORA
    o_fq = o_kr + MLA_ROPE
    o_fk, o_fv = o_fq + FOX_WIDTH, o_fq + 2 * FOX_WIDTH
    o_fl = o_fq + 3 * FOX_WIDTH
    o_ga = o_fl + FOX_HEADS
    o_gb = o_ga + D_MODEL
    w = {}
    w["wlat"] = w_in[:, o_q:o_kr].astype(BF16)
    half = MLA_ROPE // 2

    def rope_pair(cols):
        x1, x2 = cols[..., :half], cols[..., half:]
        z = jnp.zeros(cols.shape[:-1] + (MLA_NOPE,), cols.dtype)
        zt = jnp.zeros(cols.shape[:-1] + (HEAD_PAD - MLA_QK,), cols.dtype)
        plain = jnp.concatenate([z, x1, x2, zt], axis=-1)
        rot = jnp.concatenate([z, -x2, x1, zt], axis=-1)
        return plain, rot

    kr_plain, kr_rot = rope_pair(w_in[:, o_kr:o_fq])
    w["wkr"] = jnp.concatenate([kr_plain, kr_rot], axis=1).astype(BF16)
    w["gq"] = g_q_lat.reshape(1, -1)
    w["gkv"] = g_kv_lat.reshape(1, -1)

    scale_a = LOG2E / math.sqrt(MLA_QK)
    wq = (w_q_up * scale_a).reshape(MLA_Q_LORA, MLA_HEADS, MLA_QK)
    q_plain, q_rot = rope_pair(wq[..., MLA_NOPE:])
    nope = jnp.concatenate([wq[..., :MLA_NOPE], jnp.zeros((MLA_Q_LORA, MLA_HEADS, HEAD_PAD - MLA_NOPE), F32)], -1)
    w["wqa"] = (nope + q_plain).reshape(MLA_Q_LORA, -1).astype(BF16)
    w["wqb"] = q_rot.reshape(MLA_Q_LORA, -1).astype(BF16)

    wkv = w_kv_up.reshape(MLA_KV_LORA, MLA_HEADS, MLA_NOPE + MLA_V)
    zpad = jnp.zeros((MLA_KV_LORA, MLA_HEADS, HEAD_PAD - MLA_NOPE), F32)
    w["wka"] = jnp.concatenate([wkv[..., :MLA_NOPE], zpad], -1).reshape(MLA_KV_LORA, -1).astype(BF16)

    w["wva"] = wkv[..., MLA_NOPE:].reshape(MLA_KV_LORA, -1).astype(BF16)

    def fox_heads(cols, scale):
        v = (cols * scale).reshape(D_MODEL, FOX_HEADS, FOX_HEAD_DIM)
        v = jnp.concatenate([v, jnp.zeros((D_MODEL, FOX_HEADS, HEAD_PAD - FOX_HEAD_DIM), F32)], -1)
        return v.reshape(D_MODEL, FOX_HEADS // 2, 2 * HEAD_PAD).transpose(1, 0, 2).astype(BF16)

    w["wfq"] = fox_heads(w_in[:, o_fq:o_fk], LOG2E / math.sqrt(FOX_HEAD_DIM))
    w["wfk"] = fox_heads(w_in[:, o_fk:o_fv], 1.0)
    w["wfv"] = w_in[:, o_fv:o_fl].astype(BF16)

    wfl = jnp.concatenate([w_in[:, o_fl:o_ga]] * 3 + [jnp.zeros((D_MODEL, LANES - 3 * FOX_HEADS), F32)], 1)
    w["wfl_hi"] = wfl.astype(BF16)
    w["wfl_lo"] = (wfl - w["wfl_hi"].astype(F32)).astype(BF16)
    w["bfl"] = jnp.concatenate([b_forget] * 3 + [jnp.zeros((LANES - 3 * FOX_HEADS,), F32)]).reshape(1, LANES)

    pall = np.zeros((LANES, 2 * FOX_HEADS * HEAD_PAD), np.float32)
    cq = np.zeros((1, 2 * HEAD_PAD), np.float32)
    ck = np.zeros((1, 2 * HEAD_PAD), np.float32)
    for hd in range(FOX_HEADS):
        for term in range(3):
            pall[term * FOX_HEADS + hd, hd * HEAD_PAD + FQ_COL + term] = 1.0
            pall[term * FOX_HEADS + hd, FOX_HEADS * HEAD_PAD + hd * HEAD_PAD + FK_COL + term] = -1.0
    for j in range(2):
        cq[0, j * HEAD_PAD + FK_COL:j * HEAD_PAD + FK_COL + 3] = 1.0
        ck[0, j * HEAD_PAD + FQ_COL:j * HEAD_PAD + FQ_COL + 3] = 1.0
    w["pall"] = jnp.asarray(pall, BF16)
    w["cq"] = jnp.asarray(cq)
    w["ck"] = jnp.asarray(ck)

    w["wg"] = w_in[:, o_ga:o_gb + D_MODEL].astype(BF16)
    w["womla"] = w_o_mla.astype(BF16)
    w["wofox"] = w_o_fox.astype(BF16)
    w["wout"] = w_out.astype(BF16)
    wr = jnp.concatenate([w_router, jnp.zeros((D_MODEL, LANES - N_EXPERTS), F32)], 1)
    w["wrhi"] = wr.astype(BF16)
    w["wrlo"] = (wr - w["wrhi"].astype(F32)).astype(BF16)
    w["wsg"] = w_sh_gate.astype(BF16)
    w["wsu"] = w_sh_up.astype(BF16)
    w["wsd"] = w_sh_down.astype(BF16)
    return w


def _rope_freq_row():
    half = MLA_ROPE // 2
    inv = np.power(ROPE_THETA, -np.arange(half, dtype=np.float32) / half).astype(np.float32)
    row = np.zeros((1, LANES), np.float32)
    row[0, MLA_NOPE:MLA_NOPE + half] = inv
    row[0, MLA_NOPE + half:MLA_NOPE + 2 * half] = inv
    return jnp.asarray(row)


def kernel(x, c, positions, w_mod, b_mod, g_mix_norm, w_in, b_forget, g_q_lat, w_q_up, g_kv_lat, w_kv_up,
           w_o_mla, w_o_fox, w_out, g_ffn_norm, w_router, b_router, w_exp_gate, w_exp_up, w_exp_down,
           w_sh_gate, w_sh_up, w_sh_down, g_final):
    batch, seq, d = x.shape
    assert d == D_MODEL and w_mod.shape[0] == 1
    t = batch * seq
    tm = min(512, seq)
    bq = min(1024, seq)
    tr = min(512, t)
    tc = min(256, seq)
    assert seq % tm == 0 and seq % bq == 0 and t % tr == 0 and seq % tc == 0 and batch <= 8
    assert t % (SC_CHUNK * SC_WORKERS) == 0

    w = _prep_weights(w_in[0], b_forget[0], g_q_lat[0], w_q_up[0], g_kv_lat[0], w_kv_up[0], w_o_mla[0],
                      w_o_fox[0], w_out[0], w_router[0], w_sh_gate[0], w_sh_up[0], w_sh_down[0])

    c8 = jnp.zeros((8, D_MODEL), F32).at[:batch].set(c)
    mod = _mod(c8, w_mod[0], b_mod)
    mod8 = jnp.zeros((batch, 8, D_MODEL), F32).at[:, :N_MOD].set(mod[:batch].reshape(batch, N_MOD, D_MODEL))

    x2 = x.reshape(t, D_MODEL)
    fdec = _fox_decay(x2, mod8, g_mix_norm, w["wfl_hi"], w["wfl_lo"], w["bfl"], seq, tm)
    q_all, k_all, v_all, sg = _in_proj(x2, mod8, g_mix_norm, fdec, positions.reshape(t, 1),
                                       _rope_freq_row(), w, batch, seq, tm)
    o = _attention(q_all, k_all, v_all, bq)
    x1, hpk, logits = _out_proj(o.reshape(t, D_MODEL), sg, x2, mod8, g_ffn_norm, w, seq, tm)

    eidx, rank, gate, counts = _route(logits[:, :N_EXPERTS].T, b_router.reshape(N_EXPERTS, 1), tr)

    cnt = counts[:, 0]
    padded = (cnt + ROW_BLOCK - 1) // ROW_BLOCK * ROW_BLOCK
    pend = jnp.cumsum(padded)
    pstart = pend - padded
    n_blocks = t * TOP_K // ROW_BLOCK + N_EXPERTS
    n_rows = n_blocks * ROW_BLOCK

    dest = _dest(pstart.astype(I32), eidx, rank, tr)
    dest3 = dest.reshape(SLOTS, t // SC_CHUNK, SC_CHUNK).transpose(1, 0, 2)
    xs = _dispatch(hpk, dest3, n_rows)
    ys = _experts((pstart // ROW_BLOCK).astype(I32), (padded // ROW_BLOCK).astype(I32), cnt.astype(I32), xs,
                  w_exp_gate[0], w_exp_up[0], w_exp_down[0])
    yk = _gather_back(ys, dest3)
    out = _combine(yk, gate.T, hpk, x1, mod8, w, g_final.reshape(1, D_MODEL), seq, tc)
    return out.reshape(batch, seq, D_MODEL)
```

```python
import functools
import math

import numpy as np
import jax
import jax.numpy as jnp
from jax import lax
from jax.experimental import pallas as pl
from jax.experimental.pallas import tpu as pltpu
from jax.experimental.pallas import tpu_sc as plsc

F32 = jnp.float32
BF16 = jnp.bfloat16
I32 = jnp.int32
U32 = jnp.uint32

D_MODEL = 1024
MLA_HEADS = 8
MLA_Q_LORA = 256
MLA_KV_LORA = 128
MLA_NOPE = 64
MLA_ROPE = 32
MLA_V = 64
MLA_QK = MLA_NOPE + MLA_ROPE
ROPE_THETA = 10000.0
FOX_HEADS = 8
FOX_HEAD_DIM = 64
FOX_WIDTH = FOX_HEADS * FOX_HEAD_DIM
N_HEADS = MLA_HEADS + FOX_HEADS
N_EXPERTS = 64
N_GROUPS = 8
GROUP_SIZE = N_EXPERTS // N_GROUPS
TOPK_GROUPS = 4
TOP_K = 6
D_EXPERT = 256
ROUTED_SCALE = 2.5
N_MOD = 6
NORM_EPS = 1e-6
NEG_INF = -1e30
LOG2E = math.log2(math.e)

LANES = 128
HEAD_PAD = 128
V_ROWS = MLA_V + 16
ROW_BLOCK = 512
SLOTS = 8
SC_CORES = 2
SC_SUBCORES = 16
SC_WORKERS = SC_CORES * SC_SUBCORES
SC_CHUNK = 32
VMEM_LIMIT = 56 * 1024 * 1024

FQ_COL = FOX_HEAD_DIM
FK_COL = FOX_HEAD_DIM + 3


def _cparams(sem, vmem=VMEM_LIMIT):
    return pltpu.CompilerParams(dimension_semantics=sem, vmem_limit_bytes=vmem)


def _const_spec(shape):
    nd = len(shape)
    return pl.BlockSpec(shape, lambda *_: (0,) * nd)


def _rms(x):
    return x * lax.rsqrt(jnp.mean(x * x, axis=-1, keepdims=True) + NORM_EPS)


def _split3(x):
    hi = x.astype(BF16)
    r = x - hi.astype(F32)
    mid = r.astype(BF16)
    lo = (r - mid.astype(F32)).astype(BF16)
    return hi, mid, lo


def _dot(a, b):
    return jnp.dot(a, b, preferred_element_type=F32)


def _modulated_norm(x, gain, mod, shift_row, scale_row):
    shift = mod[shift_row:shift_row + 1, :]
    scale = mod[scale_row:scale_row + 1, :]
    return _rms(x) * gain * (1.0 + scale) + shift


def _mod_body(c_ref, w_ref, b_ref, o_ref):
    c = c_ref[...]
    cond = c * jax.nn.sigmoid(c)
    o_ref[...] = _dot(cond.astype(BF16), w_ref[...].astype(BF16)) + b_ref[...]


def _mod(c8, w_mod, b_mod):
    n = w_mod.shape[1]
    tn = D_MODEL
    return pl.pallas_call(
        _mod_body,
        out_shape=jax.ShapeDtypeStruct((8, n), F32),
        grid=(n // tn,),
        in_specs=[_const_spec((8, D_MODEL)),
                  pl.BlockSpec((D_MODEL, tn), lambda j: (0, j)),
                  pl.BlockSpec((1, tn), lambda j: (0, j))],
        out_specs=pl.BlockSpec((8, tn), lambda j: (0, j)),
        compiler_params=_cparams(("parallel",)),
        name="mod",
    )(c8, w_mod, b_mod)


def _decay_body(tiles_per_seq, x_ref, mod_ref, g_ref, whi_ref, wlo_ref, b_ref, o_ref, carry_ref):
    i = pl.program_id(0)

    @pl.when(i % tiles_per_seq == 0)
    def _():
        carry_ref[...] = jnp.zeros_like(carry_ref)

    h = _modulated_norm(x_ref[...], g_ref[...], mod_ref[0], 0, 1)
    hhi = h.astype(BF16)
    hlo = (h - hhi.astype(F32)).astype(BF16)
    z = _dot(hhi, whi_ref[...]) + _dot(hhi, wlo_ref[...]) + _dot(hlo, whi_ref[...]) + b_ref[...]
    logf = jnp.minimum(z, 0.0) - jnp.log1p(jnp.exp(-jnp.abs(z)))
    tm = logf.shape[0]
    row = lax.broadcasted_iota(I32, (tm, tm), 0)
    col = lax.broadcasted_iota(I32, (tm, tm), 1)
    tri = jnp.where(col <= row, 1.0, 0.0).astype(BF16)
    hi, mid, lo = _split3(logf)
    cum = _dot(tri, hi) + _dot(tri, mid) + _dot(tri, lo) + carry_ref[...]
    o_ref[...] = cum
    carry_ref[...] = cum[tm - 1:tm, :]


def _fox_decay(x2, mod8, g_mix, wfl_hi, wfl_lo, bfl, seq, tm):
    t = x2.shape[0]
    tps = seq // tm
    return pl.pallas_call(
        functools.partial(_decay_body, tps),
        out_shape=jax.ShapeDtypeStruct((t, LANES), F32),
        grid=(t // tm,),
        in_specs=[pl.BlockSpec((tm, D_MODEL), lambda i: (i, 0)),
                  pl.BlockSpec((1, 8, D_MODEL), lambda i: (i // tps, 0, 0)),
                  _const_spec((1, D_MODEL)),
                  _const_spec((D_MODEL, LANES)),
                  _const_spec((D_MODEL, LANES)),
                  _const_spec((1, LANES))],
        out_specs=pl.BlockSpec((tm, LANES), lambda i: (i, 0)),
        scratch_shapes=[pltpu.VMEM((1, LANES), F32)],
        compiler_params=_cparams(("arbitrary",)),
        name="fox_decay",
    )(x2, mod8, g_mix, wfl_hi, wfl_lo, bfl)


def _in_proj_body(x_ref, mod_ref, g_ref, f_ref, pos_ref, freq_ref,
                  wlat_ref, wkr_ref, gq_ref, gkv_ref, wqa_ref, wqb_ref, wka_ref, wva_ref,
                  wfq_ref, wfk_ref, wfv_ref, pall_ref, cq_ref, ck_ref, wg_ref,
                  q_ref, k_ref, v_ref, sg_ref):
    h = _modulated_norm(x_ref[...], g_ref[...], mod_ref[0], 0, 1)
    hb = h.astype(BF16)

    lat = _dot(hb, wlat_ref[...])
    qn = (_rms(lat[:, :MLA_Q_LORA]) * gq_ref[...]).astype(BF16)
    kvn = (_rms(lat[:, MLA_Q_LORA:]) * gkv_ref[...]).astype(BF16)
    ang = pos_ref[...].astype(F32) * freq_ref[...]
    cs = jnp.cos(ang)
    sn = jnp.sin(ang)
    kr = _dot(hb, wkr_ref[...])
    kpe = kr[:, :HEAD_PAD] * cs + kr[:, HEAD_PAD:] * sn
    qa = _dot(qn, wqa_ref[...])
    qb = _dot(qn, wqb_ref[...])
    ka = _dot(kvn, wka_ref[...])
    va = _dot(kvn, wva_ref[...])
    for hd in range(MLA_HEADS):
        sl = slice(hd * HEAD_PAD, (hd + 1) * HEAD_PAD)
        q_ref[0, hd] = (qa[:, sl] * cs + qb[:, sl] * sn).astype(BF16)
        k_ref[0, hd] = (ka[:, sl] + kpe).astype(BF16)
    ones_rows = jnp.ones((V_ROWS - MLA_V, va.shape[0]), BF16)

    def put_values(first_head, vals):
        for hp in range(vals.shape[1] // (2 * MLA_V)):
            vt = vals[:, hp * 2 * MLA_V:(hp + 1) * 2 * MLA_V].T.astype(BF16)
            for j in range(2):
                v_ref[0, first_head + 2 * hp + j, :MLA_V, :] = vt[j * MLA_V:(j + 1) * MLA_V, :]
                v_ref[0, first_head + 2 * hp + j, MLA_V:, :] = ones_rows

    put_values(0, va)

    hi, mid, lo = _split3(f_ref[...] * LOG2E)
    lane = lax.broadcasted_iota(I32, hi.shape, 1)
    f3 = jnp.where(lane < FOX_HEADS, hi, jnp.where(lane < 2 * FOX_HEADS, mid, lo))
    fp = _dot(f3, pall_ref[...])
    half = FOX_HEADS // 2 * 2 * HEAD_PAD
    for hp in range(FOX_HEADS // 2):
        sl = slice(hp * 2 * HEAD_PAD, (hp + 1) * 2 * HEAD_PAD)
        q2 = (_dot(hb, wfq_ref[hp]) + fp[:, sl] + cq_ref[...]).astype(BF16)
        k2 = (_dot(hb, wfk_ref[hp]) + fp[:, half + hp * 2 * HEAD_PAD:half + (hp + 1) * 2 * HEAD_PAD]
              + ck_ref[...]).astype(BF16)
        for j in range(2):
            hd = MLA_HEADS + 2 * hp + j
            q_ref[0, hd] = q2[:, j * HEAD_PAD:(j + 1) * HEAD_PAD]
            k_ref[0, hd] = k2[:, j * HEAD_PAD:(j + 1) * HEAD_PAD]
    put_values(MLA_HEADS, _dot(hb, wfv_ref[...]))

    sg_ref[...] = jax.nn.sigmoid(_dot(hb, wg_ref[...])).astype(BF16)


def _in_proj(x2, mod8, g_mix, fdec, pos, freq, w, batch, seq, tm):
    t = x2.shape[0]
    tps = seq // tm
    consts = [w["wlat"], w["wkr"], w["gq"], w["gkv"], w["wqa"], w["wqb"], w["wka"], w["wva"],
              w["wfq"], w["wfk"], w["wfv"], w["pall"], w["cq"], w["ck"], w["wg"]]
    head_shape = jax.ShapeDtypeStruct((batch, N_HEADS, seq, HEAD_PAD), BF16)
    head_spec = pl.BlockSpec((1, N_HEADS, tm, HEAD_PAD), lambda i: (i // tps, 0, i % tps, 0))
    pair_shape = jax.ShapeDtypeStruct((batch, N_HEADS, V_ROWS, seq), BF16)
    pair_spec = pl.BlockSpec((1, N_HEADS, V_ROWS, tm), lambda i: (i // tps, 0, 0, i % tps))
    return pl.pallas_call(
        _in_proj_body,
        out_shape=(head_shape, head_shape, pair_shape,
                   jax.ShapeDtypeStruct((t, 2 * D_MODEL), BF16)),
        grid=(t // tm,),
        in_specs=[pl.BlockSpec((tm, D_MODEL), lambda i: (i, 0)),
                  pl.BlockSpec((1, 8, D_MODEL), lambda i: (i // tps, 0, 0)),
                  _const_spec((1, D_MODEL)),
                  pl.BlockSpec((tm, LANES), lambda i: (i, 0)),
                  pl.BlockSpec((tm, 1), lambda i: (i, 0)),
                  _const_spec((1, LANES))] + [_const_spec(a.shape) for a in consts],
        out_specs=(head_spec, head_spec, pair_spec,
                   pl.BlockSpec((tm, 2 * D_MODEL), lambda i: (i, 0))),
        compiler_params=_cparams(("parallel",)),
        name="in_proj",
    )(x2, mod8, g_mix, fdec, pos, freq, *consts)


def _attn_body(bq, sq, q_ref, k_ref, v_ref, o_ref, s_scr, p_scr, acc_scr):
    qi = pl.program_id(2)
    bk = sq
    n_heads = q_ref.shape[1]
    n_sub = bq // sq
    assert n_sub % 2 == 0
    chains = [(hh, u) for hh in range(n_heads) for u in range(n_sub)]
    n_chains = len(chains)
    n_main = qi * n_sub

    def chunk_start(j):
        return pl.multiple_of(jnp.maximum(j, 0) * bk, bk)

    def scores(c, j, par):
        hh, u = chains[c]
        k = k_ref[0, hh, pl.ds(chunk_start(j), bk), :]
        q = q_ref[0, hh, u * sq:(u + 1) * sq, :]
        s = lax.dot_general(k, q, (((1,), (1,)), ((), ())), preferred_element_type=F32)
        s_scr[par, c] = s
        return jnp.max(s, axis=0, keepdims=True)

    def values(c, j, par, alpha):
        vt = v_ref[0, chains[c][0], :, pl.ds(chunk_start(j), bk)]
        acc_scr[c] = alpha * acc_scr[c] + _dot(vt, p_scr[par, c])

    def softmax(c, par, m, smax, masked):
        s = s_scr[par, c]
        if masked:
            key = lax.broadcasted_iota(I32, (bk, sq), 0)
            qry = lax.broadcasted_iota(I32, (bk, sq), 1)
            s = jnp.where(key <= qry, s, NEG_INF)
            smax = jnp.max(s, axis=0, keepdims=True)
        m_new = jnp.maximum(m, smax)
        p_scr[par, c] = jnp.exp2((s - m_new).astype(BF16))
        return m_new, jnp.exp2(m - m_new)

    def stage(j, par, state, active, has_next, masked_of):
        nxt = {c: scores(c, j + 1, 1 - par) if has_next(c) else state[c][1] for c in active}
        for c in active:
            values(c, j - 1, 1 - par, state[c][2])
        out = list(state)
        for c in active:
            m, smax, _ = state[c]
            m_new, alpha = softmax(c, par, m, smax, masked_of(c))
            out[c] = (m_new, nxt[c], alpha)
        return out

    p_scr[1] = jnp.zeros(p_scr.shape[1:], BF16)
    acc_scr[...] = jnp.zeros_like(acc_scr)
    state = [(jnp.full((1, sq), NEG_INF, F32), scores(c, 0, 0), jnp.ones((1, sq), F32)) for c in range(n_chains)]

    def step(jj, st):
        for par in range(2):
            st = stage(2 * jj + par, par, list(st), range(n_chains), lambda c: True, lambda c: False)
        return tuple(st)

    state = list(lax.fori_loop(0, n_main // 2, step, tuple(state)))

    for t in range(n_sub):
        active = [c for c, (_, u) in enumerate(chains) if u >= t]
        state = stage(n_main + t, t % 2, state, active, lambda c, t=t: chains[c][1] > t,
                      lambda c, t=t: chains[c][1] == t)

    heads = []
    for hh in range(n_heads):
        parts = []
        for u in range(n_sub):
            c = hh * n_sub + u
            values(c, n_main + u, u % 2, state[c][2])
            parts.append(acc_scr[c, :MLA_V, :] / acc_scr[c, MLA_V:MLA_V + 1, :])
        heads.append(jnp.concatenate(parts, axis=1))
    o_ref[0] = jnp.concatenate(heads, axis=0).T.astype(BF16)


def _attention(q_all, k_all, v_all, bq):
    batch, _, seq, _ = q_all.shape
    sq = min(256, bq)
    n_chains = 2 * (bq // sq)
    return pl.pallas_call(
        functools.partial(_attn_body, bq, sq),
        out_shape=jax.ShapeDtypeStruct((batch, seq, N_HEADS * MLA_V), BF16),
        grid=(batch, N_HEADS // 2, seq // bq),
        in_specs=[pl.BlockSpec((1, 2, bq, HEAD_PAD), lambda b, hp, qi: (b, hp, qi, 0)),
                  pl.BlockSpec((1, 2, seq, HEAD_PAD), lambda b, hp, qi: (b, hp, 0, 0)),
                  pl.BlockSpec((1, 2, V_ROWS, seq), lambda b, hp, qi: (b, hp, 0, 0))],
        out_specs=pl.BlockSpec((1, bq, 2 * MLA_V), lambda b, hp, qi: (b, qi, hp)),
        scratch_shapes=[pltpu.VMEM((2, n_chains, sq, sq), F32),
                        pltpu.VMEM((2, n_chains, sq, sq), BF16),
                        pltpu.VMEM((n_chains, V_ROWS, sq), F32)],
        compiler_params=_cparams(("parallel", "parallel", "arbitrary")),
        name="attention",
    )(q_all, k_all, v_all)


def _pack_halves(x):
    w = x.shape[1] // 2
    a = lax.bitcast_convert_type(x[:, :w].astype(BF16).astype(F32), U32)
    b = lax.bitcast_convert_type(x[:, w:].astype(BF16).astype(F32), U32)
    return a | (b >> 16)


def _unpack_halves(p):
    a = lax.bitcast_convert_type(p & jnp.uint32(0xFFFF0000), F32)
    b = lax.bitcast_convert_type(p << 16, F32)
    return a, b


def _out_proj_body(o_ref, sg_ref, x_ref, mod_ref, gffn_ref, womla_ref, wofox_ref, wout_ref,
                   wrhi_ref, wrlo_ref, x1_ref, hp_ref, lg_ref):
    o = o_ref[...]
    half = o.shape[1] // 2
    mo = _dot(o[:, :half], womla_ref[...])
    fo = _dot(o[:, half:], wofox_ref[...])
    sg = sg_ref[...]
    merged = sg[:, :D_MODEL].astype(F32) * mo + sg[:, D_MODEL:].astype(F32) * fo
    mix = _dot(merged.astype(BF16), wout_ref[...])
    mod = mod_ref[0]
    x1 = x_ref[...] + mod[2:3, :] * mix
    x1_ref[...] = x1
    h2 = _modulated_norm(x1, gffn_ref[...], mod, 3, 4)
    hhi = h2.astype(BF16)
    hlo = (h2 - hhi.astype(F32)).astype(BF16)
    lg_ref[...] = _dot(hhi, wrhi_ref[...]) + _dot(hhi, wrlo_ref[...]) + _dot(hlo, wrhi_ref[...])
    hp_ref[...] = _pack_halves(h2)


def _out_proj(o2, sg, x2, mod8, g_ffn, w, seq, tm):
    t = x2.shape[0]
    tps = seq // tm
    consts = [w["womla"], w["wofox"], w["wout"], w["wrhi"], w["wrlo"]]
    return pl.pallas_call(
        _out_proj_body,
        out_shape=(jax.ShapeDtypeStruct((t, D_MODEL), F32),
                   jax.ShapeDtypeStruct((t, D_MODEL // 2), U32),
                   jax.ShapeDtypeStruct((t, LANES), F32)),
        grid=(t // tm,),
        in_specs=[pl.BlockSpec((tm, D_MODEL), lambda i: (i, 0)),
                  pl.BlockSpec((tm, 2 * D_MODEL), lambda i: (i, 0)),
                  pl.BlockSpec((tm, D_MODEL), lambda i: (i, 0)),
                  pl.BlockSpec((1, 8, D_MODEL), lambda i: (i // tps, 0, 0)),
                  _const_spec((1, D_MODEL))] + [_const_spec(a.shape) for a in consts],
        out_specs=(pl.BlockSpec((tm, D_MODEL), lambda i: (i, 0)),
                   pl.BlockSpec((tm, D_MODEL // 2), lambda i: (i, 0)),
                   pl.BlockSpec((tm, LANES), lambda i: (i, 0))),
        compiler_params=_cparams(("parallel",)),
        name="out_proj",
    )(o2, sg, x2, mod8, g_ffn, *consts)


def _route_body(lt_ref, b_ref, eidx_ref, rank_ref, gate_ref, cnt_ref, carry_ref):
    i = pl.program_id(0)

    @pl.when(i == 0)
    def _():
        carry_ref[...] = jnp.zeros_like(carry_ref)

    s = jax.nn.sigmoid(lt_ref[...])
    c = s + b_ref[...]
    tr = s.shape[1]
    sub = lax.broadcasted_iota(I32, (GROUP_SIZE, tr), 0).astype(F32)

    gs = []
    for g in range(N_GROUPS):
        cg = c[g * GROUP_SIZE:(g + 1) * GROUP_SIZE, :]
        m1 = jnp.max(cg, axis=0, keepdims=True)
        i1 = jnp.min(jnp.where(cg == m1, sub, float(GROUP_SIZE)), axis=0, keepdims=True)
        m2 = jnp.max(jnp.where(sub == i1, NEG_INF, cg), axis=0, keepdims=True)
        gs.append(m1 + m2)

    masked = []
    for g in range(N_GROUPS):
        beats = jnp.zeros_like(gs[g])
        for o in range(N_GROUPS):
            if o == g:
                continue
            better = (gs[o] >= gs[g]) if o < g else (gs[o] > gs[g])
            beats = beats + jnp.where(better, 1.0, 0.0)
        keep = beats < float(TOPK_GROUPS)
        cg = c[g * GROUP_SIZE:(g + 1) * GROUP_SIZE, :]
        masked.append(jnp.where(keep, cg, NEG_INF))
    mc = jnp.concatenate(masked, axis=0)

    eio = lax.broadcasted_iota(I32, (N_EXPERTS, tr), 0).astype(F32)
    picks = []
    selected = jnp.zeros((N_EXPERTS, tr), F32)
    for _ in range(TOP_K):
        m = jnp.max(mc, axis=0, keepdims=True)
        idx = jnp.min(jnp.where(mc == m, eio, float(N_EXPERTS)), axis=0, keepdims=True)
        hit = eio == idx
        picks.append(idx)
        selected = jnp.where(hit, 1.0, selected)
        mc = jnp.where(hit, -3.0e38, mc)

    ssum = jnp.sum(selected * s, axis=0, keepdims=True)
    gate_full = selected * s / ssum * ROUTED_SCALE

    r_io = lax.broadcasted_iota(I32, (tr, tr), 0)
    c_io = lax.broadcasted_iota(I32, (tr, tr), 1)
    upper = jnp.where(r_io < c_io, 1.0, 0.0).astype(BF16)
    before = _dot(selected.astype(BF16), upper) + carry_ref[...]
    carry_new = carry_ref[...] + jnp.sum(selected, axis=1, keepdims=True)
    carry_ref[...] = carry_new
    cnt_ref[...] = jnp.broadcast_to(carry_new, cnt_ref.shape).astype(I32)

    for r in range(SLOTS):
        if r < TOP_K:
            hit = eio == picks[r]
            eidx_ref[r:r + 1, :] = picks[r].astype(I32)
            rank_ref[r:r + 1, :] = jnp.sum(jnp.where(hit, before, 0.0), axis=0, keepdims=True).astype(I32)
            gate_ref[r:r + 1, :] = jnp.sum(jnp.where(hit, gate_full, 0.0), axis=0, keepdims=True)
        else:
            eidx_ref[r:r + 1, :] = jnp.zeros((1, tr), I32)
            rank_ref[r:r + 1, :] = jnp.zeros((1, tr), I32)
            gate_ref[r:r + 1, :] = jnp.zeros((1, tr), F32)


def _route(logits_t, bias_col, tr):
    t = logits_t.shape[1]
    slot_i = jax.ShapeDtypeStruct((SLOTS, t), I32)
    slot_spec = pl.BlockSpec((SLOTS, tr), lambda i: (0, i))
    return pl.pallas_call(
        _route_body,
        out_shape=(slot_i, slot_i, jax.ShapeDtypeStruct((SLOTS, t), F32),
                   jax.ShapeDtypeStruct((N_EXPERTS, LANES), I32)),
        grid=(t // tr,),
        in_specs=[pl.BlockSpec((N_EXPERTS, tr), lambda i: (0, i)),
                  _const_spec((N_EXPERTS, 1))],
        out_specs=(slot_spec, slot_spec, slot_spec, _const_spec((N_EXPERTS, LANES))),
        scratch_shapes=[pltpu.VMEM((N_EXPERTS, 1), F32)],
        compiler_params=_cparams(("arbitrary",)),
        name="route",
    )(logits_t, bias_col)


def _dest_body(pstart_ref, eidx_ref, rank_ref, o_ref):
    e = eidx_ref[...]
    d = rank_ref[...]
    for j in range(N_EXPERTS):
        d = d + jnp.where(e == j, pstart_ref[j], 0)
    o_ref[...] = d


def _dest(pstart, eidx, rank, tr):
    t = eidx.shape[1]
    spec = pl.BlockSpec((SLOTS, tr), lambda i, ps: (0, i))
    return pl.pallas_call(
        _dest_body,
        out_shape=jax.ShapeDtypeStruct((SLOTS, t), I32),
        grid_spec=pltpu.PrefetchScalarGridSpec(
            num_scalar_prefetch=1, grid=(t // tr,), in_specs=[spec, spec], out_specs=spec),
        compiler_params=_cparams(("parallel",)),
        name="dest",
    )(pstart, eidx, rank)


def _sc_mesh():
    return plsc.VectorSubcoreMesh(core_axis_name="c", subcore_axis_name="s")


def _sc_worker_id():
    return lax.axis_index("s") * SC_CORES + lax.axis_index("c")


def _dispatch(hpk, dest3, n_rows):
    _, w = hpk.shape
    n_chunks, _, c = dest3.shape
    per_worker = n_chunks // SC_WORKERS

    def body(h_hbm, d_hbm, xs_hbm, idx_v, rows_v, sem):
        wid = _sc_worker_id()

        @pl.loop(0, per_worker)
        def _(i):
            ch = wid * per_worker + i
            pltpu.sync_copy(d_hbm.at[ch], idx_v)
            pltpu.sync_copy(h_hbm.at[pl.ds(ch * c, c)], rows_v)
            copies = [pltpu.async_copy(rows_v, xs_hbm.at[idx_v.at[k]], sem) for k in range(TOP_K)]
            for cp in copies:
                cp.wait()

    return pl.kernel(
        body, mesh=_sc_mesh(),
        out_type=jax.ShapeDtypeStruct((n_rows, w), U32),
        scratch_types=[pltpu.VMEM((SLOTS, c), I32), pltpu.VMEM((c, w), U32), pltpu.SemaphoreType.DMA],
        name="dispatch",
    )(hpk, dest3)


def _experts_body(n_sub, be_ref, nv_ref, first_ref, nu_ref, xs_ref, wg_ref, wu_ref, wd_ref, ys_ref,
                  wgb_ref, wub_ref, wdb_ref):
    del be_ref
    i = pl.program_id(0)

    @pl.when(first_ref[i] == 1)
    def _():
        wgb_ref[...] = wg_ref[0].astype(BF16)
        wub_ref[...] = wu_ref[0].astype(BF16)
        wdb_ref[...] = wd_ref[0].astype(BF16)

    @pl.when(i < nu_ref[0])
    def _():
        half = D_MODEL // 2
        sub = ROW_BLOCK // n_sub
        row = lax.broadcasted_iota(I32, (sub, xs_ref.shape[1]), 0)
        gu = []
        for s in range(n_sub):
            x = jnp.where(row + s * sub < nv_ref[i], xs_ref[s * sub:(s + 1) * sub, :], jnp.uint32(0))
            xa, xb = _unpack_halves(x)
            xa = xa.astype(BF16)
            xb = xb.astype(BF16)
            g = _dot(xa, wgb_ref[:half, :]) + _dot(xb, wgb_ref[half:, :])
            u = _dot(xa, wub_ref[:half, :]) + _dot(xb, wub_ref[half:, :])
            gu.append((g, u))
        for s, (g, u) in enumerate(gu):
            hb = (g * jax.nn.sigmoid(g) * u).astype(BF16)
            ys_ref[s * sub:(s + 1) * sub, :] = _pack_halves(_dot(hb, wdb_ref[...]))

    @pl.when(i >= nu_ref[0])
    def _():
        ys_ref[...] = jnp.zeros_like(ys_ref)


def _experts(block_e, block_valid, block_first, n_used, xs, wg, wu, wd):
    n_rows, w = xs.shape
    n_blocks = n_rows // ROW_BLOCK

    def row_map(i, be, nv, bf, nu):
        return (jnp.minimum(i, nu[0] - 1), 0)

    def w_map(i, be, nv, bf, nu):
        return (be[i], 0, 0)

    return pl.pallas_call(
        functools.partial(_experts_body, ROW_BLOCK // 256),
        out_shape=jax.ShapeDtypeStruct((n_rows, w), U32),
        grid_spec=pltpu.PrefetchScalarGridSpec(
            num_scalar_prefetch=4, grid=(n_blocks,),
            in_specs=[pl.BlockSpec((ROW_BLOCK, w), row_map),
                      pl.BlockSpec((1, D_MODEL, D_EXPERT), w_map),
                      pl.BlockSpec((1, D_MODEL, D_EXPERT), w_map),
                      pl.BlockSpec((1, D_EXPERT, D_MODEL), w_map)],
            out_specs=pl.BlockSpec((ROW_BLOCK, w), lambda i, be, nv, bf, nu: (i, 0)),
            scratch_shapes=[pltpu.VMEM((D_MODEL, D_EXPERT), BF16), pltpu.VMEM((D_MODEL, D_EXPERT), BF16),
                            pltpu.VMEM((D_EXPERT, D_MODEL), BF16)]),
        compiler_params=_cparams(("arbitrary",)),
        name="experts",
    )(block_e, block_valid, block_first, n_used, xs, wg, wu, wd)


def _gather_back(ys, dest3):
    _, w = ys.shape
    n_chunks, _, c = dest3.shape
    per_worker = n_chunks // SC_WORKERS

    def body(ys_hbm, d_hbm, yk_hbm, idx_v, rows_v, gsem, osem):
        wid = _sc_worker_id()

        @pl.loop(0, per_worker)
        def _(i):
            ch = wid * per_worker + i
            pltpu.sync_copy(d_hbm.at[ch], idx_v)
            gathers = [pltpu.async_copy(ys_hbm.at[idx_v.at[k]], rows_v.at[k], gsem) for k in range(TOP_K)]
            for g in gathers:
                g.wait()
            outs = [pltpu.async_copy(rows_v.at[k], yk_hbm.at[k, pl.ds(ch * c, c)], osem) for k in range(TOP_K)]
            for o in outs:
                o.wait()

    return pl.kernel(
        body, mesh=_sc_mesh(),
        out_type=jax.ShapeDtypeStruct((TOP_K, n_chunks * c, w), U32),
        scratch_types=[pltpu.VMEM((SLOTS, c), I32), pltpu.VMEM((TOP_K, c, w), U32),
                       pltpu.SemaphoreType.DMA, pltpu.SemaphoreType.DMA],
        name="gather_back",
    )(ys, dest3)


def _combine_body(yk_ref, gate_ref, hp_ref, x1_ref, mod_ref, wsg_ref, wsu_ref, wsd_ref, gfin_ref, o_ref):
    ha, hb = _unpack_halves(hp_ref[...])
    ha = ha.astype(BF16)
    hb = hb.astype(BF16)
    half = D_MODEL // 2
    g = _dot(ha, wsg_ref[:half, :]) + _dot(hb, wsg_ref[half:, :])
    u = _dot(ha, wsu_ref[:half, :]) + _dot(hb, wsu_ref[half:, :])
    shared = _dot((g * jax.nn.sigmoid(g) * u).astype(BF16), wsd_ref[...])

    gate = gate_ref[...]
    ra = jnp.zeros(ha.shape, F32)
    rb = jnp.zeros(ha.shape, F32)
    for k in range(TOP_K):
        ya, yb = _unpack_halves(yk_ref[k])
        gk = gate[:, k:k + 1]
        ra = ra + gk * ya
        rb = rb + gk * yb
    moe = shared + jnp.concatenate([ra, rb], axis=1)
    mod = mod_ref[0]
    x2 = x1_ref[...] + mod[5:6, :] * moe
    o_ref[...] = _rms(x2) * gfin_ref[...]


def _combine(yk, gate_t, hpk, x1, mod8, w, g_final, seq, tq):
    t, wd = hpk.shape
    tps = seq // tq
    return pl.pallas_call(
        _combine_body,
        out_shape=jax.ShapeDtypeStruct((t, D_MODEL), F32),
        grid=(t // tq,),
        in_specs=[pl.BlockSpec((TOP_K, tq, wd), lambda i: (0, i, 0)),
                  pl.BlockSpec((tq, SLOTS), lambda i: (i, 0)),
                  pl.BlockSpec((tq, wd), lambda i: (i, 0)),
                  pl.BlockSpec((tq, D_MODEL), lambda i: (i, 0)),
                  pl.BlockSpec((1, 8, D_MODEL), lambda i: (i // tps, 0, 0)),
                  _const_spec(w["wsg"].shape), _const_spec(w["wsu"].shape), _const_spec(w["wsd"].shape),
                  _const_spec((1, D_MODEL))],
        out_specs=pl.BlockSpec((tq, D_MODEL), lambda i: (i, 0)),
        compiler_params=_cparams(("parallel",)),
        name="combine",
    )(yk, gate_t, hpk, x1, mod8, w["wsg"], w["wsu"], w["wsd"], g_final)


def _prep_weights(w_in, b_forget, g_q_lat, w_q_up, g_kv_lat, w_kv_up, w_o_mla, w_o_fox, w_out,
                  w_router, w_sh_gate, w_sh_up, w_sh_down):
    o_q, o_kv, o_kr = 0, MLA_Q_LORA, MLA_Q_LORA + MLA_KV_LORA
    o_fq = o_kr + MLA_ROPE
    o_fk, o_fv = o_fq + FOX_WIDTH, o_fq + 2 * FOX_WIDTH
    o_fl = o_fq + 3 * FOX_WIDTH
    o_ga = o_fl + FOX_HEADS
    o_gb = o_ga + D_MODEL
    w = {}
    w["wlat"] = w_in[:, o_q:o_kr].astype(BF16)
    half = MLA_ROPE // 2

    def rope_pair(cols):
        x1, x2 = cols[..., :half], cols[..., half:]
        z = jnp.zeros(cols.shape[:-1] + (MLA_NOPE,), cols.dtype)
        zt = jnp.zeros(cols.shape[:-1] + (HEAD_PAD - MLA_QK,), cols.dtype)
        plain = jnp.concatenate([z, x1, x2, zt], axis=-1)
        rot = jnp.concatenate([z, -x2, x1, zt], axis=-1)
        return plain, rot

    kr_plain, kr_rot = rope_pair(w_in[:, o_kr:o_fq])
    w["wkr"] = jnp.concatenate([kr_plain, kr_rot], axis=1).astype(BF16)
    w["gq"] = g_q_lat.reshape(1, -1)
    w["gkv"] = g_kv_lat.reshape(1, -1)

    scale_a = LOG2E / math.sqrt(MLA_QK)
    wq = (w_q_up * scale_a).reshape(MLA_Q_LORA, MLA_HEADS, MLA_QK)
    q_plain, q_rot = rope_pair(wq[..., MLA_NOPE:])
    nope = jnp.concatenate([wq[..., :MLA_NOPE], jnp.zeros((MLA_Q_LORA, MLA_HEADS, HEAD_PAD - MLA_NOPE), F32)], -1)
    w["wqa"] = (nope + q_plain).reshape(MLA_Q_LORA, -1).astype(BF16)
    w["wqb"] = q_rot.reshape(MLA_Q_LORA, -1).astype(BF16)

    wkv = w_kv_up.reshape(MLA_KV_LORA, MLA_HEADS, MLA_NOPE + MLA_V)
    zpad = jnp.zeros((MLA_KV_LORA, MLA_HEADS, HEAD_PAD - MLA_NOPE), F32)
    w["wka"] = jnp.concatenate([wkv[..., :MLA_NOPE], zpad], -1).reshape(MLA_KV_LORA, -1).astype(BF16)

    w["wva"] = wkv[..., MLA_NOPE:].reshape(MLA_KV_LORA, -1).astype(BF16)

    def fox_heads(cols, scale):
        v = (cols * scale).reshape(D_MODEL, FOX_HEADS, FOX_HEAD_DIM)
        v = jnp.concatenate([v, jnp.zeros((D_MODEL, FOX_HEADS, HEAD_PAD - FOX_HEAD_DIM), F32)], -1)
        return v.reshape(D_MODEL, FOX_HEADS // 2, 2 * HEAD_PAD).transpose(1, 0, 2).astype(BF16)

    w["wfq"] = fox_heads(w_in[:, o_fq:o_fk], LOG2E / math.sqrt(FOX_HEAD_DIM))
    w["wfk"] = fox_heads(w_in[:, o_fk:o_fv], 1.0)
    w["wfv"] = w_in[:, o_fv:o_fl].astype(BF16)

    wfl = jnp.concatenate([w_in[:, o_fl:o_ga]] * 3 + [jnp.zeros((D_MODEL, LANES - 3 * FOX_HEADS), F32)], 1)
    w["wfl_hi"] = wfl.astype(BF16)
    w["wfl_lo"] = (wfl - w["wfl_hi"].astype(F32)).astype(BF16)
    w["bfl"] = jnp.concatenate([b_forget] * 3 + [jnp.zeros((LANES - 3 * FOX_HEADS,), F32)]).reshape(1, LANES)

    pall = np.zeros((LANES, 2 * FOX_HEADS * HEAD_PAD), np.float32)
    cq = np.zeros((1, 2 * HEAD_PAD), np.float32)
    ck = np.zeros((1, 2 * HEAD_PAD), np.float32)
    for hd in range(FOX_HEADS):
        for term in range(3):
            pall[term * FOX_HEADS + hd, hd * HEAD_PAD + FQ_COL + term] = 1.0
            pall[term * FOX_HEADS + hd, FOX_HEADS * HEAD_PAD + hd * HEAD_PAD + FK_COL + term] = -1.0
    for j in range(2):
        cq[0, j * HEAD_PAD + FK_COL:j * HEAD_PAD + FK_COL + 3] = 1.0
        ck[0, j * HEAD_PAD + FQ_COL:j * HEAD_PAD + FQ_COL + 3] = 1.0
    w["pall"] = jnp.asarray(pall, BF16)
    w["cq"] = jnp.asarray(cq)
    w["ck"] = jnp.asarray(ck)

    w["wg"] = w_in[:, o_ga:o_gb + D_MODEL].astype(BF16)
    w["womla"] = w_o_mla.astype(BF16)
    w["wofox"] = w_o_fox.astype(BF16)
    w["wout"] = w_out.astype(BF16)
    wr = jnp.concatenate([w_router, jnp.zeros((D_MODEL, LANES - N_EXPERTS), F32)], 1)
    w["wrhi"] = wr.astype(BF16)
    w["wrlo"] = (wr - w["wrhi"].astype(F32)).astype(BF16)
    w["wsg"] = w_sh_gate.astype(BF16)
    w["wsu"] = w_sh_up.astype(BF16)
    w["wsd"] = w_sh_down.astype(BF16)
    return w


def _rope_freq_row():
    half = MLA_ROPE // 2
    inv = np.power(ROPE_THETA, -np.arange(half, dtype=np.float32) / half).astype(np.float32)
    row = np.zeros((1, LANES), np.float32)
    row[0, MLA_NOPE:MLA_NOPE + half] = inv
    row[0, MLA_NOPE + half:MLA_NOPE + 2 * half] = inv
    return jnp.asarray(row)


def kernel(x, c, positions, w_mod, b_mod, g_mix_norm, w_in, b_forget, g_q_lat, w_q_up, g_kv_lat, w_kv_up,
           w_o_mla, w_o_fox, w_out, g_ffn_norm, w_router, b_router, w_exp_gate, w_exp_up, w_exp_down,
           w_sh_gate, w_sh_up, w_sh_down, g_final):
    batch, seq, d = x.shape
    assert d == D_MODEL and w_mod.shape[0] == 1
    t = batch * seq
    tm = min(512, seq)
    bq = min(2048, seq)
    tr = min(512, t)
    to = min(256, seq)
    tc = min(256, seq)
    assert seq % tm == 0 and seq % bq == 0 and t % tr == 0 and seq % tc == 0 and seq % to == 0 and batch <= 8
    assert t % (SC_CHUNK * SC_WORKERS) == 0

    w = _prep_weights(w_in[0], b_forget[0], g_q_lat[0], w_q_up[0], g_kv_lat[0], w_kv_up[0], w_o_mla[0],
                      w_o_fox[0], w_out[0], w_router[0], w_sh_gate[0], w_sh_up[0], w_sh_down[0])

    c8 = jnp.zeros((8, D_MODEL), F32).at[:batch].set(c)
    mod = _mod(c8, w_mod[0], b_mod)
    mod8 = jnp.zeros((batch, 8, D_MODEL), F32).at[:, :N_MOD].set(mod[:batch].reshape(batch, N_MOD, D_MODEL))

    x2 = x.reshape(t, D_MODEL)
    fdec = _fox_decay(x2, mod8, g_mix_norm, w["wfl_hi"], w["wfl_lo"], w["bfl"], seq, tm)
    q_all, k_all, v_all, sg = _in_proj(x2, mod8, g_mix_norm, fdec, positions.reshape(t, 1),
                                       _rope_freq_row(), w, batch, seq, tm)
    o = _attention(q_all, k_all, v_all, bq)
    x1, hpk, logits = _out_proj(o.reshape(t, D_MODEL), sg, x2, mod8, g_ffn_norm, w, seq, to)

    eidx, rank, gate, counts = _route(logits[:, :N_EXPERTS].T, b_router.reshape(N_EXPERTS, 1), tr)

    cnt = counts[:, 0]
    padded = (cnt + ROW_BLOCK - 1) // ROW_BLOCK * ROW_BLOCK
    pend = jnp.cumsum(padded)
    pstart = pend - padded
    n_blocks = t * TOP_K // ROW_BLOCK + N_EXPERTS
    n_rows = n_blocks * ROW_BLOCK
    block_row = jnp.arange(n_blocks, dtype=I32) * ROW_BLOCK
    block_e = jnp.minimum(jnp.sum(pend[None, :] <= block_row[:, None], axis=1), N_EXPERTS - 1).astype(I32)
    n_used = (pend[-1:] // ROW_BLOCK).astype(I32)
    block_valid = jnp.clip((pstart + cnt)[block_e] - block_row, 0, ROW_BLOCK).astype(I32)
    block_first = jnp.concatenate([jnp.ones((1,), I32), (block_e[1:] != block_e[:-1]).astype(I32)])

    dest = _dest(pstart.astype(I32), eidx, rank, tr)
    dest3 = dest.reshape(SLOTS, t // SC_CHUNK, SC_CHUNK).transpose(1, 0, 2)
    xs = _dispatch(hpk, dest3, n_rows)
    ys = _experts(block_e, block_valid, block_first, n_used, xs, w_exp_gate[0], w_exp_up[0], w_exp_down[0])
    yk = _gather_back(ys, dest3)
    out = _combine(yk, gate.T, hpk, x1, mod8, w, g_final.reshape(1, D_MODEL), seq, tc)
    return out.reshape(batch, seq, D_MODEL)
```

```python
import functools
import math

import numpy as np
import jax
import jax.numpy as jnp
from jax import lax
from jax.experimental import pallas as pl
from jax.experimental.pallas import tpu as pltpu
from jax.experimental.pallas import tpu_sc as plsc

F32 = jnp.float32
BF16 = jnp.bfloat16
I32 = jnp.int32
U32 = jnp.uint32

D_MODEL = 1024
MLA_HEADS = 8
MLA_Q_LORA = 256
MLA_KV_LORA = 128
MLA_NOPE = 64
MLA_ROPE = 32
MLA_V = 64
MLA_QK = MLA_NOPE + MLA_ROPE
ROPE_THETA = 10000.0
FOX_HEADS = 8
FOX_HEAD_DIM = 64
FOX_WIDTH = FOX_HEADS * FOX_HEAD_DIM
N_HEADS = MLA_HEADS + FOX_HEADS
N_EXPERTS = 64
N_GROUPS = 8
GROUP_SIZE = N_EXPERTS // N_GROUPS
TOPK_GROUPS = 4
TOP_K = 6
D_EXPERT = 256
ROUTED_SCALE = 2.5
N_MOD = 6
NORM_EPS = 1e-6
NEG_INF = -1e30
LOG2E = math.log2(math.e)

LANES = 128
HEAD_PAD = 128
V_ROWS = MLA_V + 16
ROW_BLOCK = 512
SLOTS = 8
SC_CORES = 2
SC_SUBCORES = 16
SC_WORKERS = SC_CORES * SC_SUBCORES
SC_CHUNK = 32
VMEM_LIMIT = 56 * 1024 * 1024

FQ_COL = 0
FK_COL = 3


def _cparams(sem, vmem=VMEM_LIMIT):
    return pltpu.CompilerParams(dimension_semantics=sem, vmem_limit_bytes=vmem)


def _const_spec(shape):
    nd = len(shape)
    return pl.BlockSpec(shape, lambda *_: (0,) * nd)


def _rms(x):
    return x * lax.rsqrt(jnp.mean(x * x, axis=-1, keepdims=True) + NORM_EPS)


def _split3(x):
    hi = x.astype(BF16)
    r = x - hi.astype(F32)
    mid = r.astype(BF16)
    lo = (r - mid.astype(F32)).astype(BF16)
    return hi, mid, lo


def _dot(a, b):
    return jnp.dot(a, b, preferred_element_type=F32)


def _modulated_norm(x, gain, mod, shift_row, scale_row):
    shift = mod[shift_row:shift_row + 1, :]
    scale = mod[scale_row:scale_row + 1, :]
    return _rms(x) * gain * (1.0 + scale) + shift


def _mod_body(c_ref, w_ref, b_ref, o_ref):
    c = c_ref[...]
    cond = c * jax.nn.sigmoid(c)
    o_ref[...] = _dot(cond.astype(BF16), w_ref[...].astype(BF16)) + b_ref[...]


def _mod(c8, w_mod, b_mod):
    n = w_mod.shape[1]
    tn = D_MODEL
    return pl.pallas_call(
        _mod_body,
        out_shape=jax.ShapeDtypeStruct((8, n), F32),
        grid=(n // tn,),
        in_specs=[_const_spec((8, D_MODEL)),
                  pl.BlockSpec((D_MODEL, tn), lambda j: (0, j)),
                  pl.BlockSpec((1, tn), lambda j: (0, j))],
        out_specs=pl.BlockSpec((8, tn), lambda j: (0, j)),
        compiler_params=_cparams(("parallel",)),
        name="mod",
    )(c8, w_mod, b_mod)


def _decay_body(tiles_per_seq, x_ref, mod_ref, g_ref, whi_ref, wlo_ref, b_ref, o_ref, carry_ref):
    i = pl.program_id(0)

    @pl.when(i % tiles_per_seq == 0)
    def _():
        carry_ref[...] = jnp.zeros_like(carry_ref)

    h = _modulated_norm(x_ref[...], g_ref[...], mod_ref[0], 0, 1)
    hhi = h.astype(BF16)
    hlo = (h - hhi.astype(F32)).astype(BF16)
    z = _dot(hhi, whi_ref[...]) + _dot(hhi, wlo_ref[...]) + _dot(hlo, whi_ref[...]) + b_ref[...]
    logf = jnp.minimum(z, 0.0) - jnp.log1p(jnp.exp(-jnp.abs(z)))
    tm = logf.shape[0]
    row = lax.broadcasted_iota(I32, (tm, tm), 0)
    col = lax.broadcasted_iota(I32, (tm, tm), 1)
    tri = jnp.where(col <= row, 1.0, 0.0).astype(BF16)
    hi, mid, lo = _split3(logf)
    cum = _dot(tri, hi) + _dot(tri, mid) + _dot(tri, lo) + carry_ref[...]
    o_ref[...] = cum
    carry_ref[...] = cum[tm - 1:tm, :]


def _fox_decay(x2, mod8, g_mix, wfl_hi, wfl_lo, bfl, seq, tm):
    t = x2.shape[0]
    tps = seq // tm
    return pl.pallas_call(
        functools.partial(_decay_body, tps),
        out_shape=jax.ShapeDtypeStruct((t, LANES), F32),
        grid=(t // tm,),
        in_specs=[pl.BlockSpec((tm, D_MODEL), lambda i: (i, 0)),
                  pl.BlockSpec((1, 8, D_MODEL), lambda i: (i // tps, 0, 0)),
                  _const_spec((1, D_MODEL)),
                  _const_spec((D_MODEL, LANES)),
                  _const_spec((D_MODEL, LANES)),
                  _const_spec((1, LANES))],
        out_specs=pl.BlockSpec((tm, LANES), lambda i: (i, 0)),
        scratch_shapes=[pltpu.VMEM((1, LANES), F32)],
        compiler_params=_cparams(("arbitrary",)),
        name="fox_decay",
    )(x2, mod8, g_mix, wfl_hi, wfl_lo, bfl)


def _in_proj_body(x_ref, mod_ref, g_ref, f_ref, pos_ref, freq_ref,
                  wlat_ref, wkr_ref, gq_ref, gkv_ref, wqa_ref, wqb_ref, wka_ref, wva_ref,
                  wfq_ref, wfk_ref, wfv_ref, pall_ref, wg_ref,
                  q_ref, k_ref, v_ref, sg_ref):
    h = _modulated_norm(x_ref[...], g_ref[...], mod_ref[0], 0, 1)
    hb = h.astype(BF16)

    lat = _dot(hb, wlat_ref[...])
    qn = (_rms(lat[:, :MLA_Q_LORA]) * gq_ref[...]).astype(BF16)
    kvn = (_rms(lat[:, MLA_Q_LORA:]) * gkv_ref[...]).astype(BF16)
    ang = pos_ref[...].astype(F32) * freq_ref[...]
    cs = jnp.cos(ang)
    sn = jnp.sin(ang)
    kr = _dot(hb, wkr_ref[...])
    kpe = kr[:, :HEAD_PAD] * cs + kr[:, HEAD_PAD:] * sn
    qa = _dot(qn, wqa_ref[...])
    qb = _dot(qn, wqb_ref[...])
    ka = _dot(kvn, wka_ref[...])
    va = _dot(kvn, wva_ref[...])
    for hd in range(MLA_HEADS):
        sl = slice(hd * HEAD_PAD, (hd + 1) * HEAD_PAD)
        q_ref[0, hd] = (qa[:, sl] * cs + qb[:, sl] * sn).astype(BF16)
        k_ref[0, hd] = (ka[:, sl] + kpe).astype(BF16)
    ones_rows = jnp.ones((V_ROWS - MLA_V, va.shape[0]), BF16)

    def put_values(first_head, vals):
        for hp in range(vals.shape[1] // (2 * MLA_V)):
            vt = vals[:, hp * 2 * MLA_V:(hp + 1) * 2 * MLA_V].T.astype(BF16)
            for j in range(2):
                v_ref[0, first_head + 2 * hp + j, :MLA_V, :] = vt[j * MLA_V:(j + 1) * MLA_V, :]
                v_ref[0, first_head + 2 * hp + j, MLA_V:, :] = ones_rows

    put_values(0, va)

    hi, mid, lo = _split3(f_ref[...] * LOG2E)
    lane = lax.broadcasted_iota(I32, hi.shape, 1)
    f3 = jnp.where(lane < FOX_HEADS, hi.astype(F32), jnp.where(lane < 2 * FOX_HEADS, mid.astype(F32),
                   jnp.where(lane < 3 * FOX_HEADS, lo.astype(F32), jnp.where(lane == 3 * FOX_HEADS, 1.0, 0.0))))
    aug = _dot(f3.astype(BF16), pall_ref[...]).astype(BF16)
    fq = _dot(hb, wfq_ref[...]).astype(BF16)
    fk = _dot(hb, wfk_ref[...]).astype(BF16)
    hdim = FOX_HEAD_DIM
    for hd in range(FOX_HEADS):
        own = slice((hd % 2) * hdim, (hd % 2 + 1) * hdim)
        spare = slice((1 - hd % 2) * hdim, (2 - hd % 2) * hdim)
        src = slice(hd * hdim, (hd + 1) * hdim)
        asrc = (hd ^ 1) * hdim
        q_ref[0, MLA_HEADS + hd, :, own] = fq[:, src]
        k_ref[0, MLA_HEADS + hd, :, own] = fk[:, src]
        q_ref[0, MLA_HEADS + hd, :, spare] = aug[:, asrc:asrc + hdim]
        k_ref[0, MLA_HEADS + hd, :, spare] = aug[:, FOX_WIDTH + asrc:FOX_WIDTH + asrc + hdim]
    put_values(MLA_HEADS, _dot(hb, wfv_ref[...]))

    sg_ref[...] = jax.nn.sigmoid(_dot(hb, wg_ref[...])).astype(BF16)


def _in_proj(x2, mod8, g_mix, fdec, pos, freq, w, batch, seq, tm):
    t = x2.shape[0]
    tps = seq // tm
    consts = [w["wlat"], w["wkr"], w["gq"], w["gkv"], w["wqa"], w["wqb"], w["wka"], w["wva"],
              w["wfq"], w["wfk"], w["wfv"], w["pall"], w["wg"]]
    head_shape = jax.ShapeDtypeStruct((batch, N_HEADS, seq, HEAD_PAD), BF16)
    head_spec = pl.BlockSpec((1, N_HEADS, tm, HEAD_PAD), lambda i: (i // tps, 0, i % tps, 0))
    pair_shape = jax.ShapeDtypeStruct((batch, N_HEADS, V_ROWS, seq), BF16)
    pair_spec = pl.BlockSpec((1, N_HEADS, V_ROWS, tm), lambda i: (i // tps, 0, 0, i % tps))
    return pl.pallas_call(
        _in_proj_body,
        out_shape=(head_shape, head_shape, pair_shape,
                   jax.ShapeDtypeStruct((t, 2 * D_MODEL), BF16)),
        grid=(t // tm,),
        in_specs=[pl.BlockSpec((tm, D_MODEL), lambda i: (i, 0)),
                  pl.BlockSpec((1, 8, D_MODEL), lambda i: (i // tps, 0, 0)),
                  _const_spec((1, D_MODEL)),
                  pl.BlockSpec((tm, LANES), lambda i: (i, 0)),
                  pl.BlockSpec((tm, 1), lambda i: (i, 0)),
                  _const_spec((1, LANES))] + [_const_spec(a.shape) for a in consts],
        out_specs=(head_spec, head_spec, pair_spec,
                   pl.BlockSpec((tm, 2 * D_MODEL), lambda i: (i, 0))),
        compiler_params=_cparams(("parallel",)),
        name="in_proj",
    )(x2, mod8, g_mix, fdec, pos, freq, *consts)


def _attn_body(bq, sq, q_ref, k_ref, v_ref, o_ref, s_scr, p_scr, acc_scr):
    qi = pl.program_id(2)
    bk = sq
    n_heads = q_ref.shape[1]
    n_sub = bq // sq
    assert n_sub % 2 == 0
    chains = [(hh, u) for hh in range(n_heads) for u in range(n_sub)]
    n_chains = len(chains)
    n_main = qi * n_sub

    def chunk_start(j):
        return pl.multiple_of(jnp.maximum(j, 0) * bk, bk)

    def scores(c, j, par):
        hh, u = chains[c]
        k = k_ref[0, hh, pl.ds(chunk_start(j), bk), :]
        q = q_ref[0, hh, u * sq:(u + 1) * sq, :]
        s = lax.dot_general(k, q, (((1,), (1,)), ((), ())), preferred_element_type=F32)
        s_scr[par, c] = s
        return jnp.max(s, axis=0, keepdims=True)

    def values(c, j, par, alpha):
        vt = v_ref[0, chains[c][0], :, pl.ds(chunk_start(j), bk)]
        acc_scr[c] = alpha * acc_scr[c] + _dot(vt, p_scr[par, c])

    def softmax(c, par, m, smax, masked):
        s = s_scr[par, c]
        if masked:
            key = lax.broadcasted_iota(I32, (bk, sq), 0)
            qry = lax.broadcasted_iota(I32, (bk, sq), 1)
            s = jnp.where(key <= qry, s, NEG_INF)
            smax = jnp.max(s, axis=0, keepdims=True)
        m_new = jnp.maximum(m, smax)
        p_scr[par, c] = jnp.exp2((s - m_new).astype(BF16))
        return m_new, jnp.exp2(m - m_new)

    def stage(j, par, state, active, has_next, masked_of):
        nxt = {c: scores(c, j + 1, 1 - par) if has_next(c) else state[c][1] for c in active}
        for c in active:
            values(c, j - 1, 1 - par, state[c][2])
        out = list(state)
        for c in active:
            m, smax, _ = state[c]
            m_new, alpha = softmax(c, par, m, smax, masked_of(c))
            out[c] = (m_new, nxt[c], alpha)
        return out

    p_scr[1] = jnp.zeros(p_scr.shape[1:], BF16)
    acc_scr[...] = jnp.zeros_like(acc_scr)
    state = [(jnp.full((1, sq), NEG_INF, F32), scores(c, 0, 0), jnp.ones((1, sq), F32)) for c in range(n_chains)]

    def step(jj, st):
        for par in range(2):
            st = stage(2 * jj + par, par, list(st), range(n_chains), lambda c: True, lambda c: False)
        return tuple(st)

    state = list(lax.fori_loop(0, n_main // 2, step, tuple(state)))

    for t in range(n_sub):
        active = [c for c, (_, u) in enumerate(chains) if u >= t]
        state = stage(n_main + t, t % 2, state, active, lambda c, t=t: chains[c][1] > t,
                      lambda c, t=t: chains[c][1] == t)

    heads = []
    for hh in range(n_heads):
        parts = []
        for u in range(n_sub):
            c = hh * n_sub + u
            values(c, n_main + u, u % 2, state[c][2])
            parts.append(acc_scr[c, :MLA_V, :] / acc_scr[c, MLA_V:MLA_V + 1, :])
        heads.append(jnp.concatenate(parts, axis=1))
    o_ref[0] = jnp.concatenate(heads, axis=0).T.astype(BF16)


def _attention(q_all, k_all, v_all, bq):
    batch, _, seq, _ = q_all.shape
    sq = min(256, bq)
    n_chains = 2 * (bq // sq)
    return pl.pallas_call(
        functools.partial(_attn_body, bq, sq),
        out_shape=jax.ShapeDtypeStruct((batch, seq, N_HEADS * MLA_V), BF16),
        grid=(batch, N_HEADS // 2, seq // bq),
        in_specs=[pl.BlockSpec((1, 2, bq, HEAD_PAD), lambda b, hp, qi: (b, hp, qi, 0)),
                  pl.BlockSpec((1, 2, seq, HEAD_PAD), lambda b, hp, qi: (b, hp, 0, 0)),
                  pl.BlockSpec((1, 2, V_ROWS, seq), lambda b, hp, qi: (b, hp, 0, 0))],
        out_specs=pl.BlockSpec((1, bq, 2 * MLA_V), lambda b, hp, qi: (b, qi, hp)),
        scratch_shapes=[pltpu.VMEM((2, n_chains, sq, sq), F32),
                        pltpu.VMEM((2, n_chains, sq, sq), BF16),
                        pltpu.VMEM((n_chains, V_ROWS, sq), F32)],
        compiler_params=_cparams(("parallel", "parallel", "arbitrary")),
        name="attention",
    )(q_all, k_all, v_all)


def _pack_halves(x):
    w = x.shape[1] // 2
    a = lax.bitcast_convert_type(x[:, :w].astype(BF16).astype(F32), U32)
    b = lax.bitcast_convert_type(x[:, w:].astype(BF16).astype(F32), U32)
    return a | (b >> 16)


def _unpack_halves(p):
    a = lax.bitcast_convert_type(p & jnp.uint32(0xFFFF0000), F32)
    b = lax.bitcast_convert_type(p << 16, F32)
    return a, b


def _out_proj_body(o_ref, sg_ref, x_ref, mod_ref, gffn_ref, womla_ref, wofox_ref, wout_ref,
                   wrhi_ref, wrlo_ref, x1_ref, hp_ref, lg_ref):
    o = o_ref[...]
    half = o.shape[1] // 2
    mo = _dot(o[:, :half], womla_ref[...])
    fo = _dot(o[:, half:], wofox_ref[...])
    sg = sg_ref[...]
    merged = sg[:, :D_MODEL].astype(F32) * mo + sg[:, D_MODEL:].astype(F32) * fo
    mix = _dot(merged.astype(BF16), wout_ref[...])
    mod = mod_ref[0]
    x1 = x_ref[...] + mod[2:3, :] * mix
    x1_ref[...] = x1
    h2 = _modulated_norm(x1, gffn_ref[...], mod, 3, 4)
    hhi = h2.astype(BF16)
    hlo = (h2 - hhi.astype(F32)).astype(BF16)
    lg_ref[...] = _dot(hhi, wrhi_ref[...]) + _dot(hhi, wrlo_ref[...]) + _dot(hlo, wrhi_ref[...])
    hp_ref[...] = _pack_halves(h2)


def _out_proj(o2, sg, x2, mod8, g_ffn, w, seq, tm):
    t = x2.shape[0]
    tps = seq // tm
    consts = [w["womla"], w["wofox"], w["wout"], w["wrhi"], w["wrlo"]]
    return pl.pallas_call(
        _out_proj_body,
        out_shape=(jax.ShapeDtypeStruct((t, D_MODEL), F32),
                   jax.ShapeDtypeStruct((t, D_MODEL // 2), U32),
                   jax.ShapeDtypeStruct((t, LANES), F32)),
        grid=(t // tm,),
        in_specs=[pl.BlockSpec((tm, D_MODEL), lambda i: (i, 0)),
                  pl.BlockSpec((tm, 2 * D_MODEL), lambda i: (i, 0)),
                  pl.BlockSpec((tm, D_MODEL), lambda i: (i, 0)),
                  pl.BlockSpec((1, 8, D_MODEL), lambda i: (i // tps, 0, 0)),
                  _const_spec((1, D_MODEL))] + [_const_spec(a.shape) for a in consts],
        out_specs=(pl.BlockSpec((tm, D_MODEL), lambda i: (i, 0)),
                   pl.BlockSpec((tm, D_MODEL // 2), lambda i: (i, 0)),
                   pl.BlockSpec((tm, LANES), lambda i: (i, 0))),
        compiler_params=_cparams(("parallel",)),
        name="out_proj",
    )(o2, sg, x2, mod8, g_ffn, *consts)


def _route_body(lt_ref, b_ref, eidx_ref, rank_ref, gate_ref, cnt_ref, carry_ref):
    i = pl.program_id(0)

    @pl.when(i == 0)
    def _():
        carry_ref[...] = jnp.zeros_like(carry_ref)

    s = jax.nn.sigmoid(lt_ref[...])
    c = s + b_ref[...]
    tr = s.shape[1]
    sub = lax.broadcasted_iota(I32, (GROUP_SIZE, tr), 0).astype(F32)

    gs = []
    for g in range(N_GROUPS):
        cg = c[g * GROUP_SIZE:(g + 1) * GROUP_SIZE, :]
        m1 = jnp.max(cg, axis=0, keepdims=True)
        i1 = jnp.min(jnp.where(cg == m1, sub, float(GROUP_SIZE)), axis=0, keepdims=True)
        m2 = jnp.max(jnp.where(sub == i1, NEG_INF, cg), axis=0, keepdims=True)
        gs.append(m1 + m2)

    masked = []
    for g in range(N_GROUPS):
        beats = jnp.zeros_like(gs[g])
        for o in range(N_GROUPS):
            if o == g:
                continue
            better = (gs[o] >= gs[g]) if o < g else (gs[o] > gs[g])
            beats = beats + jnp.where(better, 1.0, 0.0)
        keep = beats < float(TOPK_GROUPS)
        cg = c[g * GROUP_SIZE:(g + 1) * GROUP_SIZE, :]
        masked.append(jnp.where(keep, cg, NEG_INF))
    mc = jnp.concatenate(masked, axis=0)

    eio = lax.broadcasted_iota(I32, (N_EXPERTS, tr), 0).astype(F32)
    picks = []
    selected = jnp.zeros((N_EXPERTS, tr), F32)
    for _ in range(TOP_K):
        m = jnp.max(mc, axis=0, keepdims=True)
        idx = jnp.min(jnp.where(mc == m, eio, float(N_EXPERTS)), axis=0, keepdims=True)
        hit = eio == idx
        picks.append(idx)
        selected = jnp.where(hit, 1.0, selected)
        mc = jnp.where(hit, -3.0e38, mc)

    ssum = jnp.sum(selected * s, axis=0, keepdims=True)
    gate_full = selected * s / ssum * ROUTED_SCALE

    r_io = lax.broadcasted_iota(I32, (tr, tr), 0)
    c_io = lax.broadcasted_iota(I32, (tr, tr), 1)
    upper = jnp.where(r_io < c_io, 1.0, 0.0).astype(BF16)
    before = _dot(selected.astype(BF16), upper) + carry_ref[...]
    carry_new = carry_ref[...] + jnp.sum(selected, axis=1, keepdims=True)
    carry_ref[...] = carry_new
    cnt_ref[...] = jnp.broadcast_to(carry_new, cnt_ref.shape).astype(I32)

    for r in range(SLOTS):
        if r < TOP_K:
            hit = eio == picks[r]
            eidx_ref[r:r + 1, :] = picks[r].astype(I32)
            rank_ref[r:r + 1, :] = jnp.sum(jnp.where(hit, before, 0.0), axis=0, keepdims=True).astype(I32)
            gate_ref[r:r + 1, :] = jnp.sum(jnp.where(hit, gate_full, 0.0), axis=0, keepdims=True)
        else:
            eidx_ref[r:r + 1, :] = jnp.zeros((1, tr), I32)
            rank_ref[r:r + 1, :] = jnp.zeros((1, tr), I32)
            gate_ref[r:r + 1, :] = jnp.zeros((1, tr), F32)


def _route(logits_t, bias_col, tr):
    t = logits_t.shape[1]
    slot_i = jax.ShapeDtypeStruct((SLOTS, t), I32)
    slot_spec = pl.BlockSpec((SLOTS, tr), lambda i: (0, i))
    return pl.pallas_call(
        _route_body,
        out_shape=(slot_i, slot_i, jax.ShapeDtypeStruct((SLOTS, t), F32),
                   jax.ShapeDtypeStruct((N_EXPERTS, LANES), I32)),
        grid=(t // tr,),
        in_specs=[pl.BlockSpec((N_EXPERTS, tr), lambda i: (0, i)),
                  _const_spec((N_EXPERTS, 1))],
        out_specs=(slot_spec, slot_spec, slot_spec, _const_spec((N_EXPERTS, LANES))),
        scratch_shapes=[pltpu.VMEM((N_EXPERTS, 1), F32)],
        compiler_params=_cparams(("arbitrary",)),
        name="route",
    )(logits_t, bias_col)


def _dest_body(pstart_ref, eidx_ref, rank_ref, o_ref):
    e = eidx_ref[...]
    d = rank_ref[...]
    for j in range(N_EXPERTS):
        d = d + jnp.where(e == j, pstart_ref[j], 0)
    o_ref[...] = d


def _dest(pstart, eidx, rank, tr):
    t = eidx.shape[1]
    spec = pl.BlockSpec((SLOTS, tr), lambda i, ps: (0, i))
    return pl.pallas_call(
        _dest_body,
        out_shape=jax.ShapeDtypeStruct((SLOTS, t), I32),
        grid_spec=pltpu.PrefetchScalarGridSpec(
            num_scalar_prefetch=1, grid=(t // tr,), in_specs=[spec, spec], out_specs=spec),
        compiler_params=_cparams(("parallel",)),
        name="dest",
    )(pstart, eidx, rank)


def _sc_mesh():
    return plsc.VectorSubcoreMesh(core_axis_name="c", subcore_axis_name="s")


def _sc_worker_id():
    return lax.axis_index("s") * SC_CORES + lax.axis_index("c")


def _dispatch(hpk, dest3, n_rows):
    _, w = hpk.shape
    n_chunks, _, c = dest3.shape
    per_worker = n_chunks // SC_WORKERS

    def body(h_hbm, d_hbm, xs_hbm, idx_v, rows_v, sem):
        wid = _sc_worker_id()

        @pl.loop(0, per_worker)
        def _(i):
            ch = wid * per_worker + i
            pltpu.sync_copy(d_hbm.at[ch], idx_v)
            pltpu.sync_copy(h_hbm.at[pl.ds(ch * c, c)], rows_v)
            copies = [pltpu.async_copy(rows_v, xs_hbm.at[idx_v.at[k]], sem) for k in range(TOP_K)]
            for cp in copies:
                cp.wait()

    return pl.kernel(
        body, mesh=_sc_mesh(),
        out_type=jax.ShapeDtypeStruct((n_rows, w), U32),
        scratch_types=[pltpu.VMEM((SLOTS, c), I32), pltpu.VMEM((c, w), U32), pltpu.SemaphoreType.DMA],
        name="dispatch",
    )(hpk, dest3)


def _experts_body(n_sub, be_ref, nv_ref, first_ref, nu_ref, xs_ref, wg_ref, wu_ref, wd_ref, ys_ref,
                  wgb_ref, wub_ref, wdb_ref):
    del be_ref
    i = pl.program_id(0)

    @pl.when(first_ref[i] == 1)
    def _():
        wgb_ref[...] = wg_ref[0].astype(BF16)
        wub_ref[...] = wu_ref[0].astype(BF16)
        wdb_ref[...] = wd_ref[0].astype(BF16)

    @pl.when(i < nu_ref[0])
    def _():
        half = D_MODEL // 2
        sub = ROW_BLOCK // n_sub
        row = lax.broadcasted_iota(I32, (sub, xs_ref.shape[1]), 0)
        gu = []
        for s in range(n_sub):
            x = jnp.where(row + s * sub < nv_ref[i], xs_ref[s * sub:(s + 1) * sub, :], jnp.uint32(0))
            xa, xb = _unpack_halves(x)
            xa = xa.astype(BF16)
            xb = xb.astype(BF16)
            g = _dot(xa, wgb_ref[:half, :]) + _dot(xb, wgb_ref[half:, :])
            u = _dot(xa, wub_ref[:half, :]) + _dot(xb, wub_ref[half:, :])
            gu.append((g, u))
        for s, (g, u) in enumerate(gu):
            hb = (g * jax.nn.sigmoid(g) * u).astype(BF16)
            ys_ref[s * sub:(s + 1) * sub, :] = _pack_halves(_dot(hb, wdb_ref[...]))

    @pl.when(i >= nu_ref[0])
    def _():
        ys_ref[...] = jnp.zeros_like(ys_ref)


def _experts(block_e, block_valid, block_first, n_used, xs, wg, wu, wd):
    n_rows, w = xs.shape
    n_blocks = n_rows // ROW_BLOCK

    def row_map(i, be, nv, bf, nu):
        return (jnp.minimum(i, nu[0] - 1), 0)

    def w_map(i, be, nv, bf, nu):
        return (be[i], 0, 0)

    return pl.pallas_call(
        functools.partial(_experts_body, ROW_BLOCK // 256),
        out_shape=jax.ShapeDtypeStruct((n_rows, w), U32),
        grid_spec=pltpu.PrefetchScalarGridSpec(
            num_scalar_prefetch=4, grid=(n_blocks,),
            in_specs=[pl.BlockSpec((ROW_BLOCK, w), row_map),
                      pl.BlockSpec((1, D_MODEL, D_EXPERT), w_map),
                      pl.BlockSpec((1, D_MODEL, D_EXPERT), w_map),
                      pl.BlockSpec((1, D_EXPERT, D_MODEL), w_map)],
            out_specs=pl.BlockSpec((ROW_BLOCK, w), lambda i, be, nv, bf, nu: (i, 0)),
            scratch_shapes=[pltpu.VMEM((D_MODEL, D_EXPERT), BF16), pltpu.VMEM((D_MODEL, D_EXPERT), BF16),
                            pltpu.VMEM((D_EXPERT, D_MODEL), BF16)]),
        compiler_params=_cparams(("arbitrary",)),
        name="experts",
    )(block_e, block_valid, block_first, n_used, xs, wg, wu, wd)


def _gather_back(ys, dest3):
    _, w = ys.shape
    n_chunks, _, c = dest3.shape
    per_worker = n_chunks // SC_WORKERS

    def body(ys_hbm, d_hbm, yk_hbm, idx_v, rows_v, gsem, osem):
        wid = _sc_worker_id()

        @pl.loop(0, per_worker)
        def _(i):
            ch = wid * per_worker + i
            pltpu.sync_copy(d_hbm.at[ch], idx_v)
            gathers = [pltpu.async_copy(ys_hbm.at[idx_v.at[k]], rows_v.at[k], gsem) for k in range(TOP_K)]
            for g in gathers:
                g.wait()
            outs = [pltpu.async_copy(rows_v.at[k], yk_hbm.at[k, pl.ds(ch * c, c)], osem) for k in range(TOP_K)]
            for o in outs:
                o.wait()

    return pl.kernel(
        body, mesh=_sc_mesh(),
        out_type=jax.ShapeDtypeStruct((TOP_K, n_chunks * c, w), U32),
        scratch_types=[pltpu.VMEM((SLOTS, c), I32), pltpu.VMEM((TOP_K, c, w), U32),
                       pltpu.SemaphoreType.DMA, pltpu.SemaphoreType.DMA],
        name="gather_back",
    )(ys, dest3)


def _combine_body(yk_ref, gate_ref, hp_ref, x1_ref, mod_ref, wsg_ref, wsu_ref, wsd_ref, gfin_ref, o_ref):
    ha, hb = _unpack_halves(hp_ref[...])
    ha = ha.astype(BF16)
    hb = hb.astype(BF16)
    half = D_MODEL // 2
    g = _dot(ha, wsg_ref[:half, :]) + _dot(hb, wsg_ref[half:, :])
    u = _dot(ha, wsu_ref[:half, :]) + _dot(hb, wsu_ref[half:, :])
    shared = _dot((g * jax.nn.sigmoid(g) * u).astype(BF16), wsd_ref[...])

    gate = gate_ref[...]
    ra = jnp.zeros(ha.shape, F32)
    rb = jnp.zeros(ha.shape, F32)
    for k in range(TOP_K):
        ya, yb = _unpack_halves(yk_ref[k])
        gk = gate[:, k:k + 1]
        ra = ra + gk * ya
        rb = rb + gk * yb
    moe = shared + jnp.concatenate([ra, rb], axis=1)
    mod = mod_ref[0]
    x2 = x1_ref[...] + mod[5:6, :] * moe
    o_ref[...] = _rms(x2) * gfin_ref[...]


def _combine(yk, gate_t, hpk, x1, mod8, w, g_final, seq, tq):
    t, wd = hpk.shape
    tps = seq // tq
    return pl.pallas_call(
        _combine_body,
        out_shape=jax.ShapeDtypeStruct((t, D_MODEL), F32),
        grid=(t // tq,),
        in_specs=[pl.BlockSpec((TOP_K, tq, wd), lambda i: (0, i, 0)),
                  pl.BlockSpec((tq, SLOTS), lambda i: (i, 0)),
                  pl.BlockSpec((tq, wd), lambda i: (i, 0)),
                  pl.BlockSpec((tq, D_MODEL), lambda i: (i, 0)),
                  pl.BlockSpec((1, 8, D_MODEL), lambda i: (i // tps, 0, 0)),
                  _const_spec(w["wsg"].shape), _const_spec(w["wsu"].shape), _const_spec(w["wsd"].shape),
                  _const_spec((1, D_MODEL))],
        out_specs=pl.BlockSpec((tq, D_MODEL), lambda i: (i, 0)),
        compiler_params=_cparams(("parallel",)),
        name="combine",
    )(yk, gate_t, hpk, x1, mod8, w["wsg"], w["wsu"], w["wsd"], g_final)


def _prep_weights(w_in, b_forget, g_q_lat, w_q_up, g_kv_lat, w_kv_up, w_o_mla, w_o_fox, w_out,
                  w_router, w_sh_gate, w_sh_up, w_sh_down):
    o_q, o_kv, o_kr = 0, MLA_Q_LORA, MLA_Q_LORA + MLA_KV_LORA
    o_fq = o_kr + MLA_ROPE
    o_fk, o_fv = o_fq + FOX_WIDTH, o_fq + 2 * FOX_WIDTH
    o_fl = o_fq + 3 * FOX_WIDTH
    o_ga = o_fl + FOX_HEADS
    o_gb = o_ga + D_MODEL
    w = {}
    w["wlat"] = w_in[:, o_q:o_kr].astype(BF16)
    half = MLA_ROPE // 2

    def rope_pair(cols):
        x1, x2 = cols[..., :half], cols[..., half:]
        z = jnp.zeros(cols.shape[:-1] + (MLA_NOPE,), cols.dtype)
        zt = jnp.zeros(cols.shape[:-1] + (HEAD_PAD - MLA_QK,), cols.dtype)
        plain = jnp.concatenate([z, x1, x2, zt], axis=-1)
        rot = jnp.concatenate([z, -x2, x1, zt], axis=-1)
        return plain, rot

    kr_plain, kr_rot = rope_pair(w_in[:, o_kr:o_fq])
    w["wkr"] = jnp.concatenate([kr_plain, kr_rot], axis=1).astype(BF16)
    w["gq"] = g_q_lat.reshape(1, -1)
    w["gkv"] = g_kv_lat.reshape(1, -1)

    scale_a = LOG2E / math.sqrt(MLA_QK)
    wq = (w_q_up * scale_a).reshape(MLA_Q_LORA, MLA_HEADS, MLA_QK)
    q_plain, q_rot = rope_pair(wq[..., MLA_NOPE:])
    nope = jnp.concatenate([wq[..., :MLA_NOPE], jnp.zeros((MLA_Q_LORA, MLA_HEADS, HEAD_PAD - MLA_NOPE), F32)], -1)
    w["wqa"] = (nope + q_plain).reshape(MLA_Q_LORA, -1).astype(BF16)
    w["wqb"] = q_rot.reshape(MLA_Q_LORA, -1).astype(BF16)

    wkv = w_kv_up.reshape(MLA_KV_LORA, MLA_HEADS, MLA_NOPE + MLA_V)
    zpad = jnp.zeros((MLA_KV_LORA, MLA_HEADS, HEAD_PAD - MLA_NOPE), F32)
    w["wka"] = jnp.concatenate([wkv[..., :MLA_NOPE], zpad], -1).reshape(MLA_KV_LORA, -1).astype(BF16)

    w["wva"] = wkv[..., MLA_NOPE:].reshape(MLA_KV_LORA, -1).astype(BF16)

    w["wfq"] = (w_in[:, o_fq:o_fk] * (LOG2E / math.sqrt(FOX_HEAD_DIM))).astype(BF16)
    w["wfk"] = w_in[:, o_fk:o_fv].astype(BF16)
    w["wfv"] = w_in[:, o_fv:o_fl].astype(BF16)

    wfl = jnp.concatenate([w_in[:, o_fl:o_ga]] * 3 + [jnp.zeros((D_MODEL, LANES - 3 * FOX_HEADS), F32)], 1)
    w["wfl_hi"] = wfl.astype(BF16)
    w["wfl_lo"] = (wfl - w["wfl_hi"].astype(F32)).astype(BF16)
    w["bfl"] = jnp.concatenate([b_forget] * 3 + [jnp.zeros((LANES - 3 * FOX_HEADS,), F32)]).reshape(1, LANES)

    pall = np.zeros((LANES, 2 * FOX_WIDTH), np.float32)
    ones_row = 3 * FOX_HEADS
    for hd in range(FOX_HEADS):
        gq = (hd ^ 1) * FOX_HEAD_DIM
        gk = FOX_WIDTH + (hd ^ 1) * FOX_HEAD_DIM
        for term in range(3):
            pall[term * FOX_HEADS + hd, gq + FQ_COL + term] = 1.0
            pall[ones_row, gq + FK_COL + term] = 1.0
            pall[ones_row, gk + FQ_COL + term] = 1.0
            pall[term * FOX_HEADS + hd, gk + FK_COL + term] = -1.0
    w["pall"] = jnp.asarray(pall, BF16)

    w["wg"] = w_in[:, o_ga:o_gb + D_MODEL].astype(BF16)
    w["womla"] = w_o_mla.astype(BF16)
    w["wofox"] = w_o_fox.astype(BF16)
    w["wout"] = w_out.astype(BF16)
    wr = jnp.concatenate([w_router, jnp.zeros((D_MODEL, LANES - N_EXPERTS), F32)], 1)
    w["wrhi"] = wr.astype(BF16)
    w["wrlo"] = (wr - w["wrhi"].astype(F32)).astype(BF16)
    w["wsg"] = w_sh_gate.astype(BF16)
    w["wsu"] = w_sh_up.astype(BF16)
    w["wsd"] = w_sh_down.astype(BF16)
    return w


def _rope_freq_row():
    half = MLA_ROPE // 2
    inv = np.power(ROPE_THETA, -np.arange(half, dtype=np.float32) / half).astype(np.float32)
    row = np.zeros((1, LANES), np.float32)
    row[0, MLA_NOPE:MLA_NOPE + half] = inv
    row[0, MLA_NOPE + half:MLA_NOPE + 2 * half] = inv
    return jnp.asarray(row)


def kernel(x, c, positions, w_mod, b_mod, g_mix_norm, w_in, b_forget, g_q_lat, w_q_up, g_kv_lat, w_kv_up,
           w_o_mla, w_o_fox, w_out, g_ffn_norm, w_router, b_router, w_exp_gate, w_exp_up, w_exp_down,
           w_sh_gate, w_sh_up, w_sh_down, g_final):
    batch, seq, d = x.shape
    assert d == D_MODEL and w_mod.shape[0] == 1
    t = batch * seq
    tm = min(512, seq)
    bq = min(2048, seq)
    tr = min(512, t)
    to = min(256, seq)
    tc = min(256, seq)
    assert seq % tm == 0 and seq % bq == 0 and t % tr == 0 and seq % tc == 0 and seq % to == 0 and batch <= 8
    assert t % (SC_CHUNK * SC_WORKERS) == 0

    w = _prep_weights(w_in[0], b_forget[0], g_q_lat[0], w_q_up[0], g_kv_lat[0], w_kv_up[0], w_o_mla[0],
                      w_o_fox[0], w_out[0], w_router[0], w_sh_gate[0], w_sh_up[0], w_sh_down[0])

    c8 = jnp.zeros((8, D_MODEL), F32).at[:batch].set(c)
    mod = _mod(c8, w_mod[0], b_mod)
    mod8 = jnp.zeros((batch, 8, D_MODEL), F32).at[:, :N_MOD].set(mod[:batch].reshape(batch, N_MOD, D_MODEL))

    x2 = x.reshape(t, D_MODEL)
    fdec = _fox_decay(x2, mod8, g_mix_norm, w["wfl_hi"], w["wfl_lo"], w["bfl"], seq, tm)
    q_all, k_all, v_all, sg = _in_proj(x2, mod8, g_mix_norm, fdec, positions.reshape(t, 1),
                                       _rope_freq_row(), w, batch, seq, tm)
    o = _attention(q_all, k_all, v_all, bq)
    x1, hpk, logits = _out_proj(o.reshape(t, D_MODEL), sg, x2, mod8, g_ffn_norm, w, seq, to)

    eidx, rank, gate, counts = _route(logits[:, :N_EXPERTS].T, b_router.reshape(N_EXPERTS, 1), tr)

    cnt = counts[:, 0]
    padded = (cnt + ROW_BLOCK - 1) // ROW_BLOCK * ROW_BLOCK
    pend = jnp.cumsum(padded)
    pstart = pend - padded
    n_blocks = t * TOP_K // ROW_BLOCK + N_EXPERTS
    n_rows = n_blocks * ROW_BLOCK
    block_row = jnp.arange(n_blocks, dtype=I32) * ROW_BLOCK
    block_e = jnp.minimum(jnp.sum(pend[None, :] <= block_row[:, None], axis=1), N_EXPERTS - 1).astype(I32)
    n_used = (pend[-1:] // ROW_BLOCK).astype(I32)
    block_valid = jnp.clip((pstart + cnt)[block_e] - block_row, 0, ROW_BLOCK).astype(I32)
    block_first = jnp.concatenate([jnp.ones((1,), I32), (block_e[1:] != block_e[:-1]).astype(I32)])

    dest = _dest(pstart.astype(I32), eidx, rank, tr)
    dest3 = dest.reshape(SLOTS, t // SC_CHUNK, SC_CHUNK).transpose(1, 0, 2)
    xs = _dispatch(hpk, dest3, n_rows)
    ys = _experts(block_e, block_valid, block_first, n_used, xs, w_exp_gate[0], w_exp_up[0], w_exp_down[0])
    yk = _gather_back(ys, dest3)
    out = _combine(yk, gate.T, hpk, x1, mod8, w, g_final.reshape(1, D_MODEL), seq, tc)
    return out.reshape(batch, seq, D_MODEL)
```

```python
import functools
import math

import numpy as np
import jax
import jax.numpy as jnp
from jax import lax
from jax.experimental import pallas as pl
from jax.experimental.pallas import tpu as pltpu
from jax.experimental.pallas import tpu_sc as plsc

F32 = jnp.float32
BF16 = jnp.bfloat16
I32 = jnp.int32
U32 = jnp.uint32

D_MODEL = 1024
MLA_HEADS = 8
MLA_Q_LORA = 256
MLA_KV_LORA = 128
MLA_NOPE = 64
MLA_ROPE = 32
MLA_V = 64
MLA_QK = MLA_NOPE + MLA_ROPE
ROPE_THETA = 10000.0
FOX_HEADS = 8
FOX_HEAD_DIM = 64
FOX_WIDTH = FOX_HEADS * FOX_HEAD_DIM
N_HEADS = MLA_HEADS + FOX_HEADS
N_EXPERTS = 64
N_GROUPS = 8
GROUP_SIZE = N_EXPERTS // N_GROUPS
TOPK_GROUPS = 4
TOP_K = 6
D_EXPERT = 256
ROUTED_SCALE = 2.5
N_MOD = 6
NORM_EPS = 1e-6
NEG_INF = -1e30
LOG2E = math.log2(math.e)

LANES = 128
HEAD_PAD = 128
V_ROWS = MLA_V + 16
ROW_BLOCK = 512
SLOTS = 8
SC_CORES = 2
SC_SUBCORES = 16
SC_WORKERS = SC_CORES * SC_SUBCORES
SC_CHUNK = 32
VMEM_LIMIT = 56 * 1024 * 1024

FQ_COL = 0
FK_COL = 3


def _cparams(sem, vmem=VMEM_LIMIT):
    return pltpu.CompilerParams(dimension_semantics=sem, vmem_limit_bytes=vmem)


def _const_spec(shape):
    nd = len(shape)
    return pl.BlockSpec(shape, lambda *_: (0,) * nd)


def _rms(x):
    return x * lax.rsqrt(jnp.mean(x * x, axis=-1, keepdims=True) + NORM_EPS)


def _split3(x):
    hi = x.astype(BF16)
    r = x - hi.astype(F32)
    mid = r.astype(BF16)
    lo = (r - mid.astype(F32)).astype(BF16)
    return hi, mid, lo


def _dot(a, b):
    return jnp.dot(a, b, preferred_element_type=F32)


def _modulated_norm(x, gain, mod, shift_row, scale_row):
    shift = mod[shift_row:shift_row + 1, :]
    scale = mod[scale_row:scale_row + 1, :]
    return _rms(x) * gain * (1.0 + scale) + shift


def _mod_body(c_ref, w_ref, b_ref, o_ref):
    c = c_ref[...]
    cond = c * jax.nn.sigmoid(c)
    o_ref[...] = _dot(cond.astype(BF16), w_ref[...].astype(BF16)) + b_ref[...]


def _mod(c8, w_mod, b_mod):
    n = w_mod.shape[1]
    tn = D_MODEL
    return pl.pallas_call(
        _mod_body,
        out_shape=jax.ShapeDtypeStruct((8, n), F32),
        grid=(n // tn,),
        in_specs=[_const_spec((8, D_MODEL)),
                  pl.BlockSpec((D_MODEL, tn), lambda j: (0, j)),
                  pl.BlockSpec((1, tn), lambda j: (0, j))],
        out_specs=pl.BlockSpec((8, tn), lambda j: (0, j)),
        compiler_params=_cparams(("parallel",)),
        name="mod",
    )(c8, w_mod, b_mod)


def _decay_body(tiles_per_seq, x_ref, mod_ref, g_ref, w2_ref, b_ref, tri_ref, o_ref, carry_ref):
    i = pl.program_id(0)

    @pl.when(i % tiles_per_seq == 0)
    def _():
        carry_ref[...] = jnp.zeros_like(carry_ref)

    h = _modulated_norm(x_ref[...], g_ref[...], mod_ref[0], 0, 1)
    hhi = h.astype(BF16)
    hlo = (h - hhi.astype(F32)).astype(BF16)
    z2 = _dot(hhi, w2_ref[...])
    z = z2[:, :LANES] + z2[:, LANES:] + _dot(hlo, w2_ref[:, :LANES]) + b_ref[...]
    logf = jnp.minimum(z, 0.0) - jnp.log1p(jnp.exp(-jnp.abs(z)))
    tm = logf.shape[0]
    c3 = _dot(tri_ref[...], jnp.concatenate(_split3(logf), axis=1))
    cum = c3[:, :LANES] + c3[:, LANES:2 * LANES] + c3[:, 2 * LANES:] + carry_ref[...]
    o_ref[...] = cum
    carry_ref[...] = cum[tm - 1:tm, :]


def _fox_decay(x2, mod8, g_mix, wfl2, bfl, seq, tm):
    t = x2.shape[0]
    tps = seq // tm
    tri = jnp.asarray(np.tril(np.ones((tm, tm), np.float32)), BF16)
    return pl.pallas_call(
        functools.partial(_decay_body, tps),
        out_shape=jax.ShapeDtypeStruct((t, LANES), F32),
        grid=(t // tm,),
        in_specs=[pl.BlockSpec((tm, D_MODEL), lambda i: (i, 0)),
                  pl.BlockSpec((1, 8, D_MODEL), lambda i: (i // tps, 0, 0)),
                  _const_spec((1, D_MODEL)),
                  _const_spec((D_MODEL, 2 * LANES)),
                  _const_spec((1, LANES)),
                  _const_spec((tm, tm))],
        out_specs=pl.BlockSpec((tm, LANES), lambda i: (i, 0)),
        scratch_shapes=[pltpu.VMEM((1, LANES), F32)],
        compiler_params=_cparams(("arbitrary",)),
        name="fox_decay",
    )(x2, mod8, g_mix, wfl2, bfl, tri)


def _in_proj_body(x_ref, mod_ref, g_ref, f_ref, pos_ref, freq_ref,
                  wlat_ref, wkr_ref, gq_ref, gkv_ref, wqa_ref, wqb_ref, wka_ref, wva_ref,
                  wfq_ref, wfk_ref, wfv_ref, pall_ref, wg_ref,
                  q_ref, k_ref, v_ref, sg_ref):
    h = _modulated_norm(x_ref[...], g_ref[...], mod_ref[0], 0, 1)
    hb = h.astype(BF16)

    lat = _dot(hb, wlat_ref[...])
    qn = (_rms(lat[:, :MLA_Q_LORA]) * gq_ref[...]).astype(BF16)
    kvn = (_rms(lat[:, MLA_Q_LORA:]) * gkv_ref[...]).astype(BF16)
    ang = pos_ref[...].astype(F32) * freq_ref[...]
    cs = jnp.cos(ang)
    sn = jnp.sin(ang)
    kr = _dot(hb, wkr_ref[...])
    kpe = kr[:, :HEAD_PAD] * cs + kr[:, HEAD_PAD:] * sn
    qa = _dot(qn, wqa_ref[...])
    qb = _dot(qn, wqb_ref[...])
    ka = _dot(kvn, wka_ref[...])
    va = _dot(kvn, wva_ref[...])
    for hd in range(MLA_HEADS):
        sl = slice(hd * HEAD_PAD, (hd + 1) * HEAD_PAD)
        q_ref[0, hd] = (qa[:, sl] * cs + qb[:, sl] * sn).astype(BF16)
        k_ref[0, hd] = (ka[:, sl] + kpe).astype(BF16)
    ones_rows = jnp.ones((V_ROWS - MLA_V, va.shape[0]), BF16)

    def put_values(first_head, vals):
        for hp in range(vals.shape[1] // (2 * MLA_V)):
            vt = vals[:, hp * 2 * MLA_V:(hp + 1) * 2 * MLA_V].T.astype(BF16)
            for j in range(2):
                v_ref[0, first_head + 2 * hp + j, :MLA_V, :] = vt[j * MLA_V:(j + 1) * MLA_V, :]
                v_ref[0, first_head + 2 * hp + j, MLA_V:, :] = ones_rows

    put_values(0, va)

    hi, mid, lo = _split3(f_ref[...] * LOG2E)
    lane = lax.broadcasted_iota(I32, hi.shape, 1)
    f3 = jnp.where(lane < FOX_HEADS, hi.astype(F32), jnp.where(lane < 2 * FOX_HEADS, mid.astype(F32),
                   jnp.where(lane < 3 * FOX_HEADS, lo.astype(F32), jnp.where(lane == 3 * FOX_HEADS, 1.0, 0.0))))
    aug = _dot(f3.astype(BF16), pall_ref[...]).astype(BF16)
    fq = _dot(hb, wfq_ref[...]).astype(BF16)
    fk = _dot(hb, wfk_ref[...]).astype(BF16)
    hdim = FOX_HEAD_DIM
    for hd in range(FOX_HEADS):
        own = slice((hd % 2) * hdim, (hd % 2 + 1) * hdim)
        spare = slice((1 - hd % 2) * hdim, (2 - hd % 2) * hdim)
        src = slice(hd * hdim, (hd + 1) * hdim)
        asrc = (hd ^ 1) * hdim
        q_ref[0, MLA_HEADS + hd, :, own] = fq[:, src]
        k_ref[0, MLA_HEADS + hd, :, own] = fk[:, src]
        q_ref[0, MLA_HEADS + hd, :, spare] = aug[:, asrc:asrc + hdim]
        k_ref[0, MLA_HEADS + hd, :, spare] = aug[:, FOX_WIDTH + asrc:FOX_WIDTH + asrc + hdim]
    put_values(MLA_HEADS, _dot(hb, wfv_ref[...]))

    sg_ref[...] = jax.nn.sigmoid(_dot(hb, wg_ref[...])).astype(BF16)


def _in_proj(x2, mod8, g_mix, fdec, pos, freq, w, batch, seq, tm):
    t = x2.shape[0]
    tps = seq // tm
    consts = [w["wlat"], w["wkr"], w["gq"], w["gkv"], w["wqa"], w["wqb"], w["wka"], w["wva"],
              w["wfq"], w["wfk"], w["wfv"], w["pall"], w["wg"]]
    head_shape = jax.ShapeDtypeStruct((batch, N_HEADS, seq, HEAD_PAD), BF16)
    head_spec = pl.BlockSpec((1, N_HEADS, tm, HEAD_PAD), lambda i: (i // tps, 0, i % tps, 0))
    pair_shape = jax.ShapeDtypeStruct((batch, N_HEADS, V_ROWS, seq), BF16)
    pair_spec = pl.BlockSpec((1, N_HEADS, V_ROWS, tm), lambda i: (i // tps, 0, 0, i % tps))
    return pl.pallas_call(
        _in_proj_body,
        out_shape=(head_shape, head_shape, pair_shape,
                   jax.ShapeDtypeStruct((t, 2 * D_MODEL), BF16)),
        grid=(t // tm,),
        in_specs=[pl.BlockSpec((tm, D_MODEL), lambda i: (i, 0)),
                  pl.BlockSpec((1, 8, D_MODEL), lambda i: (i // tps, 0, 0)),
                  _const_spec((1, D_MODEL)),
                  pl.BlockSpec((tm, LANES), lambda i: (i, 0)),
                  pl.BlockSpec((tm, 1), lambda i: (i, 0)),
                  _const_spec((1, LANES))] + [_const_spec(a.shape) for a in consts],
        out_specs=(head_spec, head_spec, pair_spec,
                   pl.BlockSpec((tm, 2 * D_MODEL), lambda i: (i, 0))),
        compiler_params=_cparams(("parallel",)),
        name="in_proj",
    )(x2, mod8, g_mix, fdec, pos, freq, *consts)


def _attn_body(bq, sq, q_ref, k_ref, v_ref, o_ref, s_scr, p_scr, acc_scr):
    qi = pl.program_id(2)
    bk = sq
    n_heads = q_ref.shape[1]
    n_sub = bq // sq
    assert n_sub % 2 == 0
    chains = [(hh, u) for hh in range(n_heads) for u in range(n_sub)]
    n_chains = len(chains)
    n_main = qi * n_sub

    def chunk_start(j):
        return pl.multiple_of(jnp.maximum(j, 0) * bk, bk)

    def scores(c, j, par):
        hh, u = chains[c]
        k = k_ref[0, hh, pl.ds(chunk_start(j), bk), :]
        q = q_ref[0, hh, u * sq:(u + 1) * sq, :]
        s = lax.dot_general(k, q, (((1,), (1,)), ((), ())), preferred_element_type=F32)
        s_scr[par, c] = s
        return jnp.max(s, axis=0, keepdims=True)

    def values(c, j, par, alpha):
        vt = v_ref[0, chains[c][0], :, pl.ds(chunk_start(j), bk)]
        acc_scr[c] = alpha * acc_scr[c] + _dot(vt, p_scr[par, c])

    def softmax(c, par, m, smax, masked):
        s = s_scr[par, c]
        if masked:
            key = lax.broadcasted_iota(I32, (bk, sq), 0)
            qry = lax.broadcasted_iota(I32, (bk, sq), 1)
            s = jnp.where(key <= qry, s, NEG_INF)
            smax = jnp.max(s, axis=0, keepdims=True)
        m_new = jnp.maximum(m, smax)
        p_scr[par, c] = jnp.exp2((s - m_new).astype(BF16))
        return m_new, jnp.exp2(m - m_new)

    def stage(j, par, state, active, has_next, masked_of):
        nxt = {c: scores(c, j + 1, 1 - par) if has_next(c) else state[c][1] for c in active}
        for c in active:
            values(c, j - 1, 1 - par, state[c][2])
        out = list(state)
        for c in active:
            m, smax, _ = state[c]
            m_new, alpha = softmax(c, par, m, smax, masked_of(c))
            out[c] = (m_new, nxt[c], alpha)
        return out

    p_scr[1] = jnp.zeros(p_scr.shape[1:], BF16)
    acc_scr[...] = jnp.zeros_like(acc_scr)
    state = [(jnp.full((1, sq), NEG_INF, F32), scores(c, 0, 0), jnp.ones((1, sq), F32)) for c in range(n_chains)]

    def step(jj, st):
        for par in range(2):
            st = stage(2 * jj + par, par, list(st), range(n_chains), lambda c: True, lambda c: False)
        return tuple(st)

    state = list(lax.fori_loop(0, n_main // 2, step, tuple(state)))

    for t in range(n_sub):
        active = [c for c, (_, u) in enumerate(chains) if u >= t]
        state = stage(n_main + t, t % 2, state, active, lambda c, t=t: chains[c][1] > t,
                      lambda c, t=t: chains[c][1] == t)

    heads = []
    for hh in range(n_heads):
        parts = []
        for u in range(n_sub):
            c = hh * n_sub + u
            values(c, n_main + u, u % 2, state[c][2])
            parts.append(acc_scr[c, :MLA_V, :] / acc_scr[c, MLA_V:MLA_V + 1, :])
        heads.append(jnp.concatenate(parts, axis=1))
    o_ref[0] = jnp.concatenate(heads, axis=0).T.astype(BF16)


def _attention(q_all, k_all, v_all, bq):
    batch, _, seq, _ = q_all.shape
    sq = min(256, bq)
    n_chains = 2 * (bq // sq)
    return pl.pallas_call(
        functools.partial(_attn_body, bq, sq),
        out_shape=jax.ShapeDtypeStruct((batch, seq, N_HEADS * MLA_V), BF16),
        grid=(batch, N_HEADS // 2, seq // bq),
        in_specs=[pl.BlockSpec((1, 2, bq, HEAD_PAD), lambda b, hp, qi: (b, hp, qi, 0)),
                  pl.BlockSpec((1, 2, seq, HEAD_PAD), lambda b, hp, qi: (b, hp, 0, 0)),
                  pl.BlockSpec((1, 2, V_ROWS, seq), lambda b, hp, qi: (b, hp, 0, 0))],
        out_specs=pl.BlockSpec((1, bq, 2 * MLA_V), lambda b, hp, qi: (b, qi, hp)),
        scratch_shapes=[pltpu.VMEM((2, n_chains, sq, sq), F32),
                        pltpu.VMEM((2, n_chains, sq, sq), BF16),
                        pltpu.VMEM((n_chains, V_ROWS, sq), F32)],
        compiler_params=_cparams(("parallel", "parallel", "arbitrary")),
        name="attention",
    )(q_all, k_all, v_all)


def _pack_halves(x):
    w = x.shape[1] // 2
    a = lax.bitcast_convert_type(x[:, :w].astype(BF16).astype(F32), U32)
    b = lax.bitcast_convert_type(x[:, w:].astype(BF16).astype(F32), U32)
    return a | (b >> 16)


def _unpack_halves(p):
    a = lax.bitcast_convert_type(p & jnp.uint32(0xFFFF0000), F32)
    b = lax.bitcast_convert_type(p << 16, F32)
    return a, b


def _out_proj_body(o_ref, sg_ref, x_ref, mod_ref, gffn_ref, womla_ref, wofox_ref, wout_ref,
                   wr2_ref, x1_ref, hp_ref, lg_ref):
    o = o_ref[...]
    half = o.shape[1] // 2
    mo = _dot(o[:, :half], womla_ref[...])
    fo = _dot(o[:, half:], wofox_ref[...])
    sg = sg_ref[...]
    merged = sg[:, :D_MODEL].astype(F32) * mo + sg[:, D_MODEL:].astype(F32) * fo
    mix = _dot(merged.astype(BF16), wout_ref[...])
    mod = mod_ref[0]
    x1 = x_ref[...] + mod[2:3, :] * mix
    x1_ref[...] = x1
    h2 = _modulated_norm(x1, gffn_ref[...], mod, 3, 4)
    hhi = h2.astype(BF16)
    hlo = (h2 - hhi.astype(F32)).astype(BF16)
    l2 = _dot(hhi, wr2_ref[...])
    lg_ref[...] = (l2[:, :LANES] + l2[:, LANES:] + _dot(hlo, wr2_ref[:, :LANES])).T
    hp_ref[...] = _pack_halves(h2)


def _out_proj(o2, sg, x2, mod8, g_ffn, w, seq, tm):
    t = x2.shape[0]
    tps = seq // tm
    consts = [w["womla"], w["wofox"], w["wout"], w["wr2"]]
    return pl.pallas_call(
        _out_proj_body,
        out_shape=(jax.ShapeDtypeStruct((t, D_MODEL), F32),
                   jax.ShapeDtypeStruct((t, D_MODEL // 2), U32),
                   jax.ShapeDtypeStruct((LANES, t), F32)),
        grid=(t // tm,),
        in_specs=[pl.BlockSpec((tm, D_MODEL), lambda i: (i, 0)),
                  pl.BlockSpec((tm, 2 * D_MODEL), lambda i: (i, 0)),
                  pl.BlockSpec((tm, D_MODEL), lambda i: (i, 0)),
                  pl.BlockSpec((1, 8, D_MODEL), lambda i: (i // tps, 0, 0)),
                  _const_spec((1, D_MODEL))] + [_const_spec(a.shape) for a in consts],
        out_specs=(pl.BlockSpec((tm, D_MODEL), lambda i: (i, 0)),
                   pl.BlockSpec((tm, D_MODEL // 2), lambda i: (i, 0)),
                   pl.BlockSpec((LANES, tm), lambda i: (0, i))),
        compiler_params=_cparams(("parallel",)),
        name="out_proj",
    )(o2, sg, x2, mod8, g_ffn, *consts)


def _route_body(lt_ref, b_ref, eidx_ref, rank_ref, gate_ref, cnt_ref, carry_ref):
    i = pl.program_id(0)

    @pl.when(i == 0)
    def _():
        carry_ref[...] = jnp.zeros_like(carry_ref)

    s = jax.nn.sigmoid(lt_ref[...])
    c = s + b_ref[...]
    tr = s.shape[1]
    sub = lax.broadcasted_iota(I32, (GROUP_SIZE, tr), 0).astype(F32)

    gs = []
    for g in range(N_GROUPS):
        cg = c[g * GROUP_SIZE:(g + 1) * GROUP_SIZE, :]
        m1 = jnp.max(cg, axis=0, keepdims=True)
        i1 = jnp.min(jnp.where(cg == m1, sub, float(GROUP_SIZE)), axis=0, keepdims=True)
        m2 = jnp.max(jnp.where(sub == i1, NEG_INF, cg), axis=0, keepdims=True)
        gs.append(m1 + m2)

    masked = []
    for g in range(N_GROUPS):
        beats = jnp.zeros_like(gs[g])
        for o in range(N_GROUPS):
            if o == g:
                continue
            better = (gs[o] >= gs[g]) if o < g else (gs[o] > gs[g])
            beats = beats + jnp.where(better, 1.0, 0.0)
        keep = beats < float(TOPK_GROUPS)
        cg = c[g * GROUP_SIZE:(g + 1) * GROUP_SIZE, :]
        masked.append(jnp.where(keep, cg, NEG_INF))
    mc = jnp.concatenate(masked, axis=0)

    eio = lax.broadcasted_iota(I32, (N_EXPERTS, tr), 0).astype(F32)
    picks = []
    selected = jnp.zeros((N_EXPERTS, tr), F32)
    for _ in range(TOP_K):
        m = jnp.max(mc, axis=0, keepdims=True)
        idx = jnp.min(jnp.where(mc == m, eio, float(N_EXPERTS)), axis=0, keepdims=True)
        hit = eio == idx
        picks.append(idx)
        selected = jnp.where(hit, 1.0, selected)
        mc = jnp.where(hit, -3.0e38, mc)

    ssum = jnp.sum(selected * s, axis=0, keepdims=True)
    gate_full = selected * s / ssum * ROUTED_SCALE

    r_io = lax.broadcasted_iota(I32, (tr, tr), 0)
    c_io = lax.broadcasted_iota(I32, (tr, tr), 1)
    upper = jnp.where(r_io < c_io, 1.0, 0.0).astype(BF16)
    before = _dot(selected.astype(BF16), upper) + carry_ref[...]
    carry_new = carry_ref[...] + jnp.sum(selected, axis=1, keepdims=True)
    carry_ref[...] = carry_new
    cnt_ref[...] = jnp.broadcast_to(carry_new, cnt_ref.shape).astype(I32)

    for r in range(SLOTS):
        if r < TOP_K:
            hit = eio == picks[r]
            eidx_ref[r:r + 1, :] = picks[r].astype(I32)
            rank_ref[r:r + 1, :] = jnp.sum(jnp.where(hit, before, 0.0), axis=0, keepdims=True).astype(I32)
            gate_ref[r:r + 1, :] = jnp.sum(jnp.where(hit, gate_full, 0.0), axis=0, keepdims=True)
        else:
            eidx_ref[r:r + 1, :] = jnp.zeros((1, tr), I32)
            rank_ref[r:r + 1, :] = jnp.zeros((1, tr), I32)
            gate_ref[r:r + 1, :] = jnp.zeros((1, tr), F32)


def _route(logits_t, bias_col, tr):
    t = logits_t.shape[1]
    slot_i = jax.ShapeDtypeStruct((SLOTS, t), I32)
    slot_spec = pl.BlockSpec((SLOTS, tr), lambda i: (0, i))
    return pl.pallas_call(
        _route_body,
        out_shape=(slot_i, slot_i, jax.ShapeDtypeStruct((SLOTS, t), F32),
                   jax.ShapeDtypeStruct((N_EXPERTS, LANES), I32)),
        grid=(t // tr,),
        in_specs=[pl.BlockSpec((N_EXPERTS, tr), lambda i: (0, i)),
                  _const_spec((N_EXPERTS, 1))],
        out_specs=(slot_spec, slot_spec, slot_spec, _const_spec((N_EXPERTS, LANES))),
        scratch_shapes=[pltpu.VMEM((N_EXPERTS, 1), F32)],
        compiler_params=_cparams(("arbitrary",)),
        name="route",
    )(logits_t, bias_col)


def _dest_body(pstart_ref, eidx_ref, rank_ref, o_ref):
    e = eidx_ref[...]
    d = rank_ref[...]
    for j in range(N_EXPERTS):
        d = d + jnp.where(e == j, pstart_ref[j], 0)
    o_ref[...] = d


def _dest(pstart, eidx, rank, tr):
    t = eidx.shape[1]
    spec = pl.BlockSpec((SLOTS, tr), lambda i, ps: (0, i))
    return pl.pallas_call(
        _dest_body,
        out_shape=jax.ShapeDtypeStruct((SLOTS, t), I32),
        grid_spec=pltpu.PrefetchScalarGridSpec(
            num_scalar_prefetch=1, grid=(t // tr,), in_specs=[spec, spec], out_specs=spec),
        compiler_params=_cparams(("parallel",)),
        name="dest",
    )(pstart, eidx, rank)


def _sc_mesh():
    return plsc.VectorSubcoreMesh(core_axis_name="c", subcore_axis_name="s")


def _sc_worker_id():
    return lax.axis_index("s") * SC_CORES + lax.axis_index("c")


def _dispatch(hpk, dest3, n_rows):
    _, w = hpk.shape
    n_chunks, _, c = dest3.shape
    per_worker = n_chunks // SC_WORKERS

    def body(h_hbm, d_hbm, xs_hbm, idx_v, rows_v, sem):
        wid = _sc_worker_id()

        @pl.loop(0, per_worker)
        def _(i):
            ch = wid * per_worker + i
            pltpu.sync_copy(d_hbm.at[ch], idx_v)
            pltpu.sync_copy(h_hbm.at[pl.ds(ch * c, c)], rows_v)
            copies = [pltpu.async_copy(rows_v, xs_hbm.at[idx_v.at[k]], sem) for k in range(TOP_K)]
            for cp in copies:
                cp.wait()

    return pl.kernel(
        body, mesh=_sc_mesh(),
        out_type=jax.ShapeDtypeStruct((n_rows, w), U32),
        scratch_types=[pltpu.VMEM((SLOTS, c), I32), pltpu.VMEM((c, w), U32), pltpu.SemaphoreType.DMA],
        name="dispatch",
    )(hpk, dest3)


def _experts_body(n_sub, be_ref, nv_ref, first_ref, nu_ref, xs_ref, wg_ref, wu_ref, wd_ref, ys_ref,
                  wgb_ref, wub_ref, wdb_ref):
    del be_ref
    i = pl.program_id(0)

    @pl.when(first_ref[i] == 1)
    def _():
        wgb_ref[...] = wg_ref[0].astype(BF16)
        wub_ref[...] = wu_ref[0].astype(BF16)
        wdb_ref[...] = wd_ref[0].astype(BF16)

    @pl.when(i < nu_ref[0])
    def _():
        half = D_MODEL // 2
        sub = ROW_BLOCK // n_sub
        row = lax.broadcasted_iota(I32, (sub, xs_ref.shape[1]), 0)
        gu = []
        for s in range(n_sub):
            x = jnp.where(row + s * sub < nv_ref[i], xs_ref[s * sub:(s + 1) * sub, :], jnp.uint32(0))
            xa, xb = _unpack_halves(x)
            xa = xa.astype(BF16)
            xb = xb.astype(BF16)
            g = _dot(xa, wgb_ref[:half, :]) + _dot(xb, wgb_ref[half:, :])
            u = _dot(xa, wub_ref[:half, :]) + _dot(xb, wub_ref[half:, :])
            gu.append((g, u))
        for s, (g, u) in enumerate(gu):
            hb = (g * jax.nn.sigmoid(g) * u).astype(BF16)
            ys_ref[s * sub:(s + 1) * sub, :] = _pack_halves(_dot(hb, wdb_ref[...]))

    @pl.when(i >= nu_ref[0])
    def _():
        ys_ref[...] = jnp.zeros_like(ys_ref)


def _experts(block_e, block_valid, block_first, n_used, xs, wg, wu, wd):
    n_rows, w = xs.shape
    n_blocks = n_rows // ROW_BLOCK

    def row_map(i, be, nv, bf, nu):
        return (jnp.minimum(i, nu[0] - 1), 0)

    def w_map(i, be, nv, bf, nu):
        return (be[i], 0, 0)

    return pl.pallas_call(
        functools.partial(_experts_body, ROW_BLOCK // 256),
        out_shape=jax.ShapeDtypeStruct((n_rows, w), U32),
        grid_spec=pltpu.PrefetchScalarGridSpec(
            num_scalar_prefetch=4, grid=(n_blocks,),
            in_specs=[pl.BlockSpec((ROW_BLOCK, w), row_map),
                      pl.BlockSpec((1, D_MODEL, D_EXPERT), w_map),
                      pl.BlockSpec((1, D_MODEL, D_EXPERT), w_map),
                      pl.BlockSpec((1, D_EXPERT, D_MODEL), w_map)],
            out_specs=pl.BlockSpec((ROW_BLOCK, w), lambda i, be, nv, bf, nu: (i, 0)),
            scratch_shapes=[pltpu.VMEM((D_MODEL, D_EXPERT), BF16), pltpu.VMEM((D_MODEL, D_EXPERT), BF16),
                            pltpu.VMEM((D_EXPERT, D_MODEL), BF16)]),
        compiler_params=_cparams(("arbitrary",)),
        name="experts",
    )(block_e, block_valid, block_first, n_used, xs, wg, wu, wd)


def _gather_back(ys, dest3):
    _, w = ys.shape
    n_chunks, _, c = dest3.shape
    per_worker = n_chunks // SC_WORKERS

    def body(ys_hbm, d_hbm, yk_hbm, idx_v, rows_v, gsem, osem):
        wid = _sc_worker_id()

        @pl.loop(0, per_worker)
        def _(i):
            ch = wid * per_worker + i
            pltpu.sync_copy(d_hbm.at[ch], idx_v)
            gathers = [pltpu.async_copy(ys_hbm.at[idx_v.at[k]], rows_v.at[k], gsem) for k in range(TOP_K)]
            for g in gathers:
                g.wait()
            outs = [pltpu.async_copy(rows_v.at[k], yk_hbm.at[k, pl.ds(ch * c, c)], osem) for k in range(TOP_K)]
            for o in outs:
                o.wait()

    return pl.kernel(
        body, mesh=_sc_mesh(),
        out_type=jax.ShapeDtypeStruct((TOP_K, n_chunks * c, w), U32),
        scratch_types=[pltpu.VMEM((SLOTS, c), I32), pltpu.VMEM((TOP_K, c, w), U32),
                       pltpu.SemaphoreType.DMA, pltpu.SemaphoreType.DMA],
        name="gather_back",
    )(ys, dest3)


def _combine_body(yk_ref, gate_ref, hp_ref, x1_ref, mod_ref, wsg_ref, wsu_ref, wsd_ref, gfin_ref, o_ref):
    ha, hb = _unpack_halves(hp_ref[...])
    ha = ha.astype(BF16)
    hb = hb.astype(BF16)
    half = D_MODEL // 2
    g = _dot(ha, wsg_ref[:half, :]) + _dot(hb, wsg_ref[half:, :])
    u = _dot(ha, wsu_ref[:half, :]) + _dot(hb, wsu_ref[half:, :])
    shared = _dot((g * jax.nn.sigmoid(g) * u).astype(BF16), wsd_ref[...])

    gate = gate_ref[...]
    ra = jnp.zeros(ha.shape, F32)
    rb = jnp.zeros(ha.shape, F32)
    for k in range(TOP_K):
        ya, yb = _unpack_halves(yk_ref[k])
        gk = gate[:, k:k + 1]
        ra = ra + gk * ya
        rb = rb + gk * yb
    moe = shared + jnp.concatenate([ra, rb], axis=1)
    mod = mod_ref[0]
    x2 = x1_ref[...] + mod[5:6, :] * moe
    o_ref[...] = _rms(x2) * gfin_ref[...]


def _combine(yk, gate_t, hpk, x1, mod8, w, g_final, seq, tq):
    t, wd = hpk.shape
    tps = seq // tq
    return pl.pallas_call(
        _combine_body,
        out_shape=jax.ShapeDtypeStruct((t, D_MODEL), F32),
        grid=(t // tq,),
        in_specs=[pl.BlockSpec((TOP_K, tq, wd), lambda i: (0, i, 0)),
                  pl.BlockSpec((tq, SLOTS), lambda i: (i, 0)),
                  pl.BlockSpec((tq, wd), lambda i: (i, 0)),
                  pl.BlockSpec((tq, D_MODEL), lambda i: (i, 0)),
                  pl.BlockSpec((1, 8, D_MODEL), lambda i: (i // tps, 0, 0)),
                  _const_spec(w["wsg"].shape), _const_spec(w["wsu"].shape), _const_spec(w["wsd"].shape),
                  _const_spec((1, D_MODEL))],
        out_specs=pl.BlockSpec((tq, D_MODEL), lambda i: (i, 0)),
        compiler_params=_cparams(("parallel",)),
        name="combine",
    )(yk, gate_t, hpk, x1, mod8, w["wsg"], w["wsu"], w["wsd"], g_final)


def _prep_weights(w_in, b_forget, g_q_lat, w_q_up, g_kv_lat, w_kv_up, w_o_mla, w_o_fox, w_out,
                  w_router, w_sh_gate, w_sh_up, w_sh_down):
    o_q, o_kv, o_kr = 0, MLA_Q_LORA, MLA_Q_LORA + MLA_KV_LORA
    o_fq = o_kr + MLA_ROPE
    o_fk, o_fv = o_fq + FOX_WIDTH, o_fq + 2 * FOX_WIDTH
    o_fl = o_fq + 3 * FOX_WIDTH
    o_ga = o_fl + FOX_HEADS
    o_gb = o_ga + D_MODEL
    w = {}
    w["wlat"] = w_in[:, o_q:o_kr].astype(BF16)
    half = MLA_ROPE // 2

    def rope_pair(cols):
        x1, x2 = cols[..., :half], cols[..., half:]
        z = jnp.zeros(cols.shape[:-1] + (MLA_NOPE,), cols.dtype)
        zt = jnp.zeros(cols.shape[:-1] + (HEAD_PAD - MLA_QK,), cols.dtype)
        plain = jnp.concatenate([z, x1, x2, zt], axis=-1)
        rot = jnp.concatenate([z, -x2, x1, zt], axis=-1)
        return plain, rot

    kr_plain, kr_rot = rope_pair(w_in[:, o_kr:o_fq])
    w["wkr"] = jnp.concatenate([kr_plain, kr_rot], axis=1).astype(BF16)
    w["gq"] = g_q_lat.reshape(1, -1)
    w["gkv"] = g_kv_lat.reshape(1, -1)

    scale_a = LOG2E / math.sqrt(MLA_QK)
    wq = (w_q_up * scale_a).reshape(MLA_Q_LORA, MLA_HEADS, MLA_QK)
    q_plain, q_rot = rope_pair(wq[..., MLA_NOPE:])
    nope = jnp.concatenate([wq[..., :MLA_NOPE], jnp.zeros((MLA_Q_LORA, MLA_HEADS, HEAD_PAD - MLA_NOPE), F32)], -1)
    w["wqa"] = (nope + q_plain).reshape(MLA_Q_LORA, -1).astype(BF16)
    w["wqb"] = q_rot.reshape(MLA_Q_LORA, -1).astype(BF16)

    wkv = w_kv_up.reshape(MLA_KV_LORA, MLA_HEADS, MLA_NOPE + MLA_V)
    zpad = jnp.zeros((MLA_KV_LORA, MLA_HEADS, HEAD_PAD - MLA_NOPE), F32)
    w["wka"] = jnp.concatenate([wkv[..., :MLA_NOPE], zpad], -1).reshape(MLA_KV_LORA, -1).astype(BF16)

    w["wva"] = wkv[..., MLA_NOPE:].reshape(MLA_KV_LORA, -1).astype(BF16)

    w["wfq"] = (w_in[:, o_fq:o_fk] * (LOG2E / math.sqrt(FOX_HEAD_DIM))).astype(BF16)
    w["wfk"] = w_in[:, o_fk:o_fv].astype(BF16)
    w["wfv"] = w_in[:, o_fv:o_fl].astype(BF16)

    wfl = jnp.concatenate([w_in[:, o_fl:o_ga]] * 3 + [jnp.zeros((D_MODEL, LANES - 3 * FOX_HEADS), F32)], 1)
    wfl_hi = wfl.astype(BF16)
    w["wfl2"] = jnp.concatenate([wfl_hi, (wfl - wfl_hi.astype(F32)).astype(BF16)], axis=1)
    w["bfl"] = jnp.concatenate([b_forget] * 3 + [jnp.zeros((LANES - 3 * FOX_HEADS,), F32)]).reshape(1, LANES)

    pall = np.zeros((LANES, 2 * FOX_WIDTH), np.float32)
    ones_row = 3 * FOX_HEADS
    for hd in range(FOX_HEADS):
        gq = (hd ^ 1) * FOX_HEAD_DIM
        gk = FOX_WIDTH + (hd ^ 1) * FOX_HEAD_DIM
        for term in range(3):
            pall[term * FOX_HEADS + hd, gq + FQ_COL + term] = 1.0
            pall[ones_row, gq + FK_COL + term] = 1.0
            pall[ones_row, gk + FQ_COL + term] = 1.0
            pall[term * FOX_HEADS + hd, gk + FK_COL + term] = -1.0
    w["pall"] = jnp.asarray(pall, BF16)

    w["wg"] = w_in[:, o_ga:o_gb + D_MODEL].astype(BF16)
    w["womla"] = w_o_mla.astype(BF16)
    w["wofox"] = w_o_fox.astype(BF16)
    w["wout"] = w_out.astype(BF16)
    wr = jnp.concatenate([w_router, jnp.zeros((D_MODEL, LANES - N_EXPERTS), F32)], 1)
    wr_hi = wr.astype(BF16)
    w["wr2"] = jnp.concatenate([wr_hi, (wr - wr_hi.astype(F32)).astype(BF16)], axis=1)
    w["wsg"] = w_sh_gate.astype(BF16)
    w["wsu"] = w_sh_up.astype(BF16)
    w["wsd"] = w_sh_down.astype(BF16)
    return w


def _rope_freq_row():
    half = MLA_ROPE // 2
    inv = np.power(ROPE_THETA, -np.arange(half, dtype=np.float32) / half).astype(np.float32)
    row = np.zeros((1, LANES), np.float32)
    row[0, MLA_NOPE:MLA_NOPE + half] = inv
    row[0, MLA_NOPE + half:MLA_NOPE + 2 * half] = inv
    return jnp.asarray(row)


def kernel(x, c, positions, w_mod, b_mod, g_mix_norm, w_in, b_forget, g_q_lat, w_q_up, g_kv_lat, w_kv_up,
           w_o_mla, w_o_fox, w_out, g_ffn_norm, w_router, b_router, w_exp_gate, w_exp_up, w_exp_down,
           w_sh_gate, w_sh_up, w_sh_down, g_final):
    batch, seq, d = x.shape
    assert d == D_MODEL and w_mod.shape[0] == 1
    t = batch * seq
    tm = min(512, seq)
    bq = min(2048, seq)
    tr = min(512, t)
    to = min(256, seq)
    tc = min(256, seq)
    assert seq % tm == 0 and seq % bq == 0 and t % tr == 0 and seq % tc == 0 and seq % to == 0 and batch <= 8
    assert t % (SC_CHUNK * SC_WORKERS) == 0

    w = _prep_weights(w_in[0], b_forget[0], g_q_lat[0], w_q_up[0], g_kv_lat[0], w_kv_up[0], w_o_mla[0],
                      w_o_fox[0], w_out[0], w_router[0], w_sh_gate[0], w_sh_up[0], w_sh_down[0])

    c8 = jnp.zeros((8, D_MODEL), F32).at[:batch].set(c)
    mod = _mod(c8, w_mod[0], b_mod)
    mod8 = jnp.zeros((batch, 8, D_MODEL), F32).at[:, :N_MOD].set(mod[:batch].reshape(batch, N_MOD, D_MODEL))

    x2 = x.reshape(t, D_MODEL)
    fdec = _fox_decay(x2, mod8, g_mix_norm, w["wfl2"], w["bfl"], seq, tm)
    q_all, k_all, v_all, sg = _in_proj(x2, mod8, g_mix_norm, fdec, positions.reshape(t, 1),
                                       _rope_freq_row(), w, batch, seq, tm)
    o = _attention(q_all, k_all, v_all, bq)
    x1, hpk, logits_t = _out_proj(o.reshape(t, D_MODEL), sg, x2, mod8, g_ffn_norm, w, seq, to)

    eidx, rank, gate, counts = _route(logits_t, b_router.reshape(N_EXPERTS, 1), tr)

    cnt = counts[:, 0]
    padded = (cnt + ROW_BLOCK - 1) // ROW_BLOCK * ROW_BLOCK
    pend = jnp.cumsum(padded)
    pstart = pend - padded
    n_blocks = t * TOP_K // ROW_BLOCK + N_EXPERTS
    n_rows = n_blocks * ROW_BLOCK
    block_row = jnp.arange(n_blocks, dtype=I32) * ROW_BLOCK
    block_e = jnp.minimum(jnp.sum(pend[None, :] <= block_row[:, None], axis=1), N_EXPERTS - 1).astype(I32)
    n_used = (pend[-1:] // ROW_BLOCK).astype(I32)
    block_valid = jnp.clip((pstart + cnt)[block_e] - block_row, 0, ROW_BLOCK).astype(I32)
    block_first = jnp.concatenate([jnp.ones((1,), I32), (block_e[1:] != block_e[:-1]).astype(I32)])

    dest = _dest(pstart.astype(I32), eidx, rank, min(2048, t))
    dest3 = dest.reshape(SLOTS, t // SC_CHUNK, SC_CHUNK).transpose(1, 0, 2)
    xs = _dispatch(hpk, dest3, n_rows)
    ys = _experts(block_e, block_valid, block_first, n_used, xs, w_exp_gate[0], w_exp_up[0], w_exp_down[0])
    yk = _gather_back(ys, dest3)
    out = _combine(yk, gate.T, hpk, x1, mod8, w, g_final.reshape(1, D_MODEL), seq, tc)
    return out.reshape(batch, seq, D_MODEL)
```

```python
import functools
import math

import numpy as np
import jax
import jax.numpy as jnp
from jax import lax
from jax.experimental import pallas as pl
from jax.experimental.pallas import tpu as pltpu
from jax.experimental.pallas import tpu_sc as plsc

F32 = jnp.float32
BF16 = jnp.bfloat16
I32 = jnp.int32
U32 = jnp.uint32

D_MODEL = 1024
MLA_HEADS = 8
MLA_Q_LORA = 256
MLA_KV_LORA = 128
MLA_NOPE = 64
MLA_ROPE = 32
MLA_V = 64
MLA_QK = MLA_NOPE + MLA_ROPE
ROPE_THETA = 10000.0
FOX_HEADS = 8
FOX_HEAD_DIM = 64
FOX_WIDTH = FOX_HEADS * FOX_HEAD_DIM
N_HEADS = MLA_HEADS + FOX_HEADS
N_EXPERTS = 64
N_GROUPS = 8
GROUP_SIZE = N_EXPERTS // N_GROUPS
TOPK_GROUPS = 4
TOP_K = 6
D_EXPERT = 256
ROUTED_SCALE = 2.5
N_MOD = 6
NORM_EPS = 1e-6
NEG_INF = -1e30
LOG2E = math.log2(math.e)

LANES = 128
HEAD_PAD = 128
V_ROWS = MLA_V + 16
ROW_BLOCK = 512
SLOTS = 8
SC_CORES = 2
SC_SUBCORES = 16
SC_WORKERS = SC_CORES * SC_SUBCORES
SC_CHUNK = 32
VMEM_LIMIT = 56 * 1024 * 1024

FQ_COL = 0
FK_COL = 3


def _cparams(sem, vmem=VMEM_LIMIT):
    return pltpu.CompilerParams(dimension_semantics=sem, vmem_limit_bytes=vmem)


def _const_spec(shape):
    nd = len(shape)
    return pl.BlockSpec(shape, lambda *_: (0,) * nd)


def _rms(x):
    return x * lax.rsqrt(jnp.mean(x * x, axis=-1, keepdims=True) + NORM_EPS)


def _split3(x):
    hi = x.astype(BF16)
    r = x - hi.astype(F32)
    mid = r.astype(BF16)
    lo = (r - mid.astype(F32)).astype(BF16)
    return hi, mid, lo


def _dot(a, b):
    return jnp.dot(a, b, preferred_element_type=F32)


def _modulated_norm(x, gain, mod, shift_row, scale_row):
    shift = mod[shift_row:shift_row + 1, :]
    scale = mod[scale_row:scale_row + 1, :]
    return _rms(x) * gain * (1.0 + scale) + shift


def _mod_body(c_ref, w_ref, b_ref, o_ref):
    c = c_ref[...]
    cond = c * jax.nn.sigmoid(c)
    o_ref[...] = _dot(cond.astype(BF16), w_ref[...].astype(BF16)) + b_ref[...]


def _mod(c8, w_mod, b_mod):
    n = w_mod.shape[1]
    tn = D_MODEL
    return pl.pallas_call(
        _mod_body,
        out_shape=jax.ShapeDtypeStruct((8, n), F32),
        grid=(n // tn,),
        in_specs=[_const_spec((8, D_MODEL)),
                  pl.BlockSpec((D_MODEL, tn), lambda j: (0, j)),
                  pl.BlockSpec((1, tn), lambda j: (0, j))],
        out_specs=pl.BlockSpec((8, tn), lambda j: (0, j)),
        compiler_params=_cparams(("parallel",)),
        name="mod",
    )(c8, w_mod, b_mod)


def _decay_body(tiles_per_seq, x_ref, mod_ref, g_ref, w2_ref, b_ref, tri_ref, o_ref, carry_ref):
    i = pl.program_id(0)

    @pl.when(i % tiles_per_seq == 0)
    def _():
        carry_ref[...] = jnp.zeros_like(carry_ref)

    h = _modulated_norm(x_ref[...], g_ref[...], mod_ref[0], 0, 1)
    hhi = h.astype(BF16)
    hlo = (h - hhi.astype(F32)).astype(BF16)
    z2 = _dot(hhi, w2_ref[...])
    z = z2[:, :LANES] + z2[:, LANES:] + _dot(hlo, w2_ref[:, :LANES]) + b_ref[...]
    logf = jnp.minimum(z, 0.0) - jnp.log1p(jnp.exp(-jnp.abs(z)))
    tm = logf.shape[0]
    c3 = _dot(tri_ref[...], jnp.concatenate(_split3(logf), axis=1))
    cum = c3[:, :LANES] + c3[:, LANES:2 * LANES] + c3[:, 2 * LANES:] + carry_ref[...]
    o_ref[...] = cum
    carry_ref[...] = cum[tm - 1:tm, :]


def _fox_decay(x2, mod8, g_mix, wfl2, bfl, seq, tm):
    t = x2.shape[0]
    tps = seq // tm
    tri = jnp.asarray(np.tril(np.ones((tm, tm), np.float32)), BF16)
    return pl.pallas_call(
        functools.partial(_decay_body, tps),
        out_shape=jax.ShapeDtypeStruct((t, LANES), F32),
        grid=(t // tm,),
        in_specs=[pl.BlockSpec((tm, D_MODEL), lambda i: (i, 0)),
                  pl.BlockSpec((1, 8, D_MODEL), lambda i: (i // tps, 0, 0)),
                  _const_spec((1, D_MODEL)),
                  _const_spec((D_MODEL, 2 * LANES)),
                  _const_spec((1, LANES)),
                  _const_spec((tm, tm))],
        out_specs=pl.BlockSpec((tm, LANES), lambda i: (i, 0)),
        scratch_shapes=[pltpu.VMEM((1, LANES), F32)],
        compiler_params=_cparams(("arbitrary",)),
        name="fox_decay",
    )(x2, mod8, g_mix, wfl2, bfl, tri)


def _in_proj_body(x_ref, mod_ref, g_ref, f_ref, pos_ref, freq_ref,
                  wlat_ref, wkr_ref, gq_ref, gkv_ref, wqa_ref, wqb_ref, wka_ref, wva_ref,
                  wfq_ref, wfk_ref, wfv_ref, pall_ref, wg_ref,
                  q_ref, k_ref, v_ref, sg_ref):
    h = _modulated_norm(x_ref[...], g_ref[...], mod_ref[0], 0, 1)
    hb = h.astype(BF16)

    lat = _dot(hb, wlat_ref[...])
    qn = (_rms(lat[:, :MLA_Q_LORA]) * gq_ref[...]).astype(BF16)
    kvn = (_rms(lat[:, MLA_Q_LORA:]) * gkv_ref[...]).astype(BF16)
    ang = pos_ref[...].astype(F32) * freq_ref[...]
    cs = jnp.cos(ang)
    sn = jnp.sin(ang)
    kr = _dot(hb, wkr_ref[...])
    kpe = kr[:, :HEAD_PAD] * cs + kr[:, HEAD_PAD:] * sn
    qa = _dot(qn, wqa_ref[...])
    qb = _dot(qn, wqb_ref[...])
    ka = _dot(kvn, wka_ref[...])
    va = _dot(kvn, wva_ref[...])
    for hd in range(MLA_HEADS):
        sl = slice(hd * HEAD_PAD, (hd + 1) * HEAD_PAD)
        q_ref[0, hd] = (qa[:, sl] * cs + qb[:, sl] * sn).astype(BF16)
        k_ref[0, hd] = (ka[:, sl] + kpe).astype(BF16)
    ones_rows = jnp.ones((V_ROWS - MLA_V, va.shape[0]), BF16)

    def put_values(first_head, vals):
        for hp in range(vals.shape[1] // (2 * MLA_V)):
            vt = vals[:, hp * 2 * MLA_V:(hp + 1) * 2 * MLA_V].T.astype(BF16)
            for j in range(2):
                v_ref[0, first_head + 2 * hp + j, :MLA_V, :] = vt[j * MLA_V:(j + 1) * MLA_V, :]
                v_ref[0, first_head + 2 * hp + j, MLA_V:, :] = ones_rows

    put_values(0, va)

    hi, mid, lo = _split3(f_ref[...] * LOG2E)
    lane = lax.broadcasted_iota(I32, hi.shape, 1)
    f3 = jnp.where(lane < FOX_HEADS, hi.astype(F32), jnp.where(lane < 2 * FOX_HEADS, mid.astype(F32),
                   jnp.where(lane < 3 * FOX_HEADS, lo.astype(F32), jnp.where(lane == 3 * FOX_HEADS, 1.0, 0.0))))
    aug = _dot(f3.astype(BF16), pall_ref[...]).astype(BF16)
    fq = _dot(hb, wfq_ref[...]).astype(BF16)
    fk = _dot(hb, wfk_ref[...]).astype(BF16)
    hdim = FOX_HEAD_DIM
    for hd in range(FOX_HEADS):
        own = slice((hd % 2) * hdim, (hd % 2 + 1) * hdim)
        spare = slice((1 - hd % 2) * hdim, (2 - hd % 2) * hdim)
        src = slice(hd * hdim, (hd + 1) * hdim)
        asrc = (hd ^ 1) * hdim
        q_ref[0, MLA_HEADS + hd, :, own] = fq[:, src]
        k_ref[0, MLA_HEADS + hd, :, own] = fk[:, src]
        q_ref[0, MLA_HEADS + hd, :, spare] = aug[:, asrc:asrc + hdim]
        k_ref[0, MLA_HEADS + hd, :, spare] = aug[:, FOX_WIDTH + asrc:FOX_WIDTH + asrc + hdim]
    put_values(MLA_HEADS, _dot(hb, wfv_ref[...]))

    sg_ref[...] = jax.nn.sigmoid(_dot(hb, wg_ref[...])).astype(BF16)


def _in_proj(x2, mod8, g_mix, fdec, pos, freq, w, batch, seq, tm):
    t = x2.shape[0]
    tps = seq // tm
    consts = [w["wlat"], w["wkr"], w["gq"], w["gkv"], w["wqa"], w["wqb"], w["wka"], w["wva"],
              w["wfq"], w["wfk"], w["wfv"], w["pall"], w["wg"]]
    head_shape = jax.ShapeDtypeStruct((batch, N_HEADS, seq, HEAD_PAD), BF16)
    head_spec = pl.BlockSpec((1, N_HEADS, tm, HEAD_PAD), lambda i: (i // tps, 0, i % tps, 0))
    pair_shape = jax.ShapeDtypeStruct((batch, N_HEADS, V_ROWS, seq), BF16)
    pair_spec = pl.BlockSpec((1, N_HEADS, V_ROWS, tm), lambda i: (i // tps, 0, 0, i % tps))
    return pl.pallas_call(
        _in_proj_body,
        out_shape=(head_shape, head_shape, pair_shape,
                   jax.ShapeDtypeStruct((t, 2 * D_MODEL), BF16)),
        grid=(t // tm,),
        in_specs=[pl.BlockSpec((tm, D_MODEL), lambda i: (i, 0)),
                  pl.BlockSpec((1, 8, D_MODEL), lambda i: (i // tps, 0, 0)),
                  _const_spec((1, D_MODEL)),
                  pl.BlockSpec((tm, LANES), lambda i: (i, 0)),
                  pl.BlockSpec((tm, 1), lambda i: (i, 0)),
                  _const_spec((1, LANES))] + [_const_spec(a.shape) for a in consts],
        out_specs=(head_spec, head_spec, pair_spec,
                   pl.BlockSpec((tm, 2 * D_MODEL), lambda i: (i, 0))),
        compiler_params=_cparams(("parallel",)),
        name="in_proj",
    )(x2, mod8, g_mix, fdec, pos, freq, *consts)


def _attn_body(bq, sq, q_ref, k_ref, v_ref, o_ref, s_scr, p_scr, acc_scr):
    qi = pl.program_id(2)
    bk = sq
    n_heads = q_ref.shape[1]
    n_sub = bq // sq
    assert n_sub % 2 == 0
    chains = [(hh, u) for hh in range(n_heads) for u in range(n_sub)]
    n_chains = len(chains)
    n_main = qi * n_sub

    def chunk_start(j):
        return pl.multiple_of(jnp.maximum(j, 0) * bk, bk)

    def scores(c, j, par):
        hh, u = chains[c]
        k = k_ref[0, hh, pl.ds(chunk_start(j), bk), :]
        q = q_ref[0, hh, u * sq:(u + 1) * sq, :]
        s = lax.dot_general(k, q, (((1,), (1,)), ((), ())), preferred_element_type=F32)
        s_scr[par, c] = s
        return jnp.max(s, axis=0, keepdims=True)

    def values(c, j, par, alpha):
        vt = v_ref[0, chains[c][0], :, pl.ds(chunk_start(j), bk)]
        acc_scr[c] = alpha * acc_scr[c] + _dot(vt, p_scr[par, c])

    def softmax(c, par, m, smax, masked):
        s = s_scr[par, c]
        if masked:
            key = lax.broadcasted_iota(I32, (bk, sq), 0)
            qry = lax.broadcasted_iota(I32, (bk, sq), 1)
            s = jnp.where(key <= qry, s, NEG_INF)
            smax = jnp.max(s, axis=0, keepdims=True)
        m_new = jnp.maximum(m, smax)
        p_scr[par, c] = jnp.exp2((s - m_new).astype(BF16))
        return m_new, jnp.exp2(m - m_new)

    def stage(j, par, state, active, has_next, masked_of):
        nxt = {c: scores(c, j + 1, 1 - par) if has_next(c) else state[c][1] for c in active}
        for c in active:
            values(c, j - 1, 1 - par, state[c][2])
        out = list(state)
        for c in active:
            m, smax, _ = state[c]
            m_new, alpha = softmax(c, par, m, smax, masked_of(c))
            out[c] = (m_new, nxt[c], alpha)
        return out

    p_scr[1] = jnp.zeros(p_scr.shape[1:], BF16)
    acc_scr[...] = jnp.zeros_like(acc_scr)
    state = [(jnp.full((1, sq), NEG_INF, F32), scores(c, 0, 0), jnp.ones((1, sq), F32)) for c in range(n_chains)]

    def step(jj, st):
        for par in range(2):
            st = stage(2 * jj + par, par, list(st), range(n_chains), lambda c: True, lambda c: False)
        return tuple(st)

    state = list(lax.fori_loop(0, n_main // 2, step, tuple(state)))

    for t in range(n_sub):
        active = [c for c, (_, u) in enumerate(chains) if u >= t]
        state = stage(n_main + t, t % 2, state, active, lambda c, t=t: chains[c][1] > t,
                      lambda c, t=t: chains[c][1] == t)

    heads = []
    for hh in range(n_heads):
        parts = []
        for u in range(n_sub):
            c = hh * n_sub + u
            values(c, n_main + u, u % 2, state[c][2])
            parts.append(acc_scr[c, :MLA_V, :] / acc_scr[c, MLA_V:MLA_V + 1, :])
        heads.append(jnp.concatenate(parts, axis=1))
    o_ref[0] = jnp.concatenate(heads, axis=0).T.astype(BF16)


def _attention(q_all, k_all, v_all, bq):
    batch, _, seq, _ = q_all.shape
    sq = min(256, bq)
    n_chains = 2 * (bq // sq)
    return pl.pallas_call(
        functools.partial(_attn_body, bq, sq),
        out_shape=jax.ShapeDtypeStruct((batch, seq, N_HEADS * MLA_V), BF16),
        grid=(batch, N_HEADS // 2, seq // bq),
        in_specs=[pl.BlockSpec((1, 2, bq, HEAD_PAD), lambda b, hp, qi: (b, hp, qi, 0)),
                  pl.BlockSpec((1, 2, seq, HEAD_PAD), lambda b, hp, qi: (b, hp, 0, 0)),
                  pl.BlockSpec((1, 2, V_ROWS, seq), lambda b, hp, qi: (b, hp, 0, 0))],
        out_specs=pl.BlockSpec((1, bq, 2 * MLA_V), lambda b, hp, qi: (b, qi, hp)),
        scratch_shapes=[pltpu.VMEM((2, n_chains, sq, sq), F32),
                        pltpu.VMEM((2, n_chains, sq, sq), BF16),
                        pltpu.VMEM((n_chains, V_ROWS, sq), F32)],
        compiler_params=_cparams(("parallel", "parallel", "arbitrary")),
        name="attention",
    )(q_all, k_all, v_all)


def _pack_halves(x):
    w = x.shape[1] // 2
    a = lax.bitcast_convert_type(x[:, :w].astype(BF16).astype(F32), U32)
    b = lax.bitcast_convert_type(x[:, w:].astype(BF16).astype(F32), U32)
    return a | (b >> 16)


def _unpack_halves(p):
    a = lax.bitcast_convert_type(p & jnp.uint32(0xFFFF0000), F32)
    b = lax.bitcast_convert_type(p << 16, F32)
    return a, b


def _out_proj_body(o_ref, sg_ref, x_ref, mod_ref, gffn_ref, womla_ref, wofox_ref, wout_ref,
                   wr2_ref, x1_ref, hp_ref, lg_ref):
    o = o_ref[...]
    half = o.shape[1] // 2
    mo = _dot(o[:, :half], womla_ref[...])
    fo = _dot(o[:, half:], wofox_ref[...])
    sg = sg_ref[...]
    merged = sg[:, :D_MODEL].astype(F32) * mo + sg[:, D_MODEL:].astype(F32) * fo
    mix = _dot(merged.astype(BF16), wout_ref[...])
    mod = mod_ref[0]
    x1 = x_ref[...] + mod[2:3, :] * mix
    x1_ref[...] = x1
    h2 = _modulated_norm(x1, gffn_ref[...], mod, 3, 4)
    hhi = h2.astype(BF16)
    hlo = (h2 - hhi.astype(F32)).astype(BF16)
    l2 = _dot(hhi, wr2_ref[...])
    lg_ref[...] = (l2[:, :LANES] + l2[:, LANES:] + _dot(hlo, wr2_ref[:, :LANES])).T
    hp_ref[...] = _pack_halves(h2)


def _out_proj(o2, sg, x2, mod8, g_ffn, w, seq, tm):
    t = x2.shape[0]
    tps = seq // tm
    consts = [w["womla"], w["wofox"], w["wout"], w["wr2"]]
    return pl.pallas_call(
        _out_proj_body,
        out_shape=(jax.ShapeDtypeStruct((t, D_MODEL), F32),
                   jax.ShapeDtypeStruct((t, D_MODEL // 2), U32),
                   jax.ShapeDtypeStruct((LANES, t), F32)),
        grid=(t // tm,),
        in_specs=[pl.BlockSpec((tm, D_MODEL), lambda i: (i, 0)),
                  pl.BlockSpec((tm, 2 * D_MODEL), lambda i: (i, 0)),
                  pl.BlockSpec((tm, D_MODEL), lambda i: (i, 0)),
                  pl.BlockSpec((1, 8, D_MODEL), lambda i: (i // tps, 0, 0)),
                  _const_spec((1, D_MODEL))] + [_const_spec(a.shape) for a in consts],
        out_specs=(pl.BlockSpec((tm, D_MODEL), lambda i: (i, 0)),
                   pl.BlockSpec((tm, D_MODEL // 2), lambda i: (i, 0)),
                   pl.BlockSpec((LANES, tm), lambda i: (0, i))),
        compiler_params=_cparams(("parallel",)),
        name="out_proj",
    )(o2, sg, x2, mod8, g_ffn, *consts)


def _route_body(lt_ref, b_ref, eidx_ref, rank_ref, gate_ref, cnt_ref, carry_ref):
    i = pl.program_id(0)

    @pl.when(i == 0)
    def _():
        carry_ref[...] = jnp.zeros_like(carry_ref)

    s = jax.nn.sigmoid(lt_ref[...])
    c = s + b_ref[...]
    tr = s.shape[1]
    sub = lax.broadcasted_iota(I32, (GROUP_SIZE, tr), 0).astype(F32)

    gs = []
    for g in range(N_GROUPS):
        cg = c[g * GROUP_SIZE:(g + 1) * GROUP_SIZE, :]
        m1 = jnp.max(cg, axis=0, keepdims=True)
        i1 = jnp.min(jnp.where(cg == m1, sub, float(GROUP_SIZE)), axis=0, keepdims=True)
        m2 = jnp.max(jnp.where(sub == i1, NEG_INF, cg), axis=0, keepdims=True)
        gs.append(m1 + m2)

    masked = []
    for g in range(N_GROUPS):
        beats = jnp.zeros_like(gs[g])
        for o in range(N_GROUPS):
            if o == g:
                continue
            better = (gs[o] >= gs[g]) if o < g else (gs[o] > gs[g])
            beats = beats + jnp.where(better, 1.0, 0.0)
        keep = beats < float(TOPK_GROUPS)
        cg = c[g * GROUP_SIZE:(g + 1) * GROUP_SIZE, :]
        masked.append(jnp.where(keep, cg, NEG_INF))
    mc = jnp.concatenate(masked, axis=0)

    eio = lax.broadcasted_iota(I32, (N_EXPERTS, tr), 0).astype(F32)
    picks = []
    selected = jnp.zeros((N_EXPERTS, tr), F32)
    for _ in range(TOP_K):
        m = jnp.max(mc, axis=0, keepdims=True)
        idx = jnp.min(jnp.where(mc == m, eio, float(N_EXPERTS)), axis=0, keepdims=True)
        hit = eio == idx
        picks.append(idx)
        selected = jnp.where(hit, 1.0, selected)
        mc = jnp.where(hit, -3.0e38, mc)

    ssum = jnp.sum(selected * s, axis=0, keepdims=True)
    gate_full = selected * s / ssum * ROUTED_SCALE

    r_io = lax.broadcasted_iota(I32, (tr, tr), 0)
    c_io = lax.broadcasted_iota(I32, (tr, tr), 1)
    upper = jnp.where(r_io < c_io, 1.0, 0.0).astype(BF16)
    before = _dot(selected.astype(BF16), upper) + carry_ref[...]
    carry_new = carry_ref[...] + jnp.sum(selected, axis=1, keepdims=True)
    carry_ref[...] = carry_new
    cnt_ref[...] = jnp.broadcast_to(carry_new, cnt_ref.shape).astype(I32)

    for r in range(SLOTS):
        if r < TOP_K:
            hit = eio == picks[r]
            eidx_ref[r:r + 1, :] = picks[r].astype(I32)
            rank_ref[r:r + 1, :] = jnp.sum(jnp.where(hit, before, 0.0), axis=0, keepdims=True).astype(I32)
            gate_ref[r:r + 1, :] = jnp.sum(jnp.where(hit, gate_full, 0.0), axis=0, keepdims=True)
        else:
            eidx_ref[r:r + 1, :] = jnp.zeros((1, tr), I32)
            rank_ref[r:r + 1, :] = jnp.zeros((1, tr), I32)
            gate_ref[r:r + 1, :] = jnp.zeros((1, tr), F32)


def _route(logits_t, bias_col, tr):
    t = logits_t.shape[1]
    slot_i = jax.ShapeDtypeStruct((SLOTS, t), I32)
    slot_spec = pl.BlockSpec((SLOTS, tr), lambda i: (0, i))
    return pl.pallas_call(
        _route_body,
        out_shape=(slot_i, slot_i, jax.ShapeDtypeStruct((SLOTS, t), F32),
                   jax.ShapeDtypeStruct((N_EXPERTS, LANES), I32)),
        grid=(t // tr,),
        in_specs=[pl.BlockSpec((N_EXPERTS, tr), lambda i: (0, i)),
                  _const_spec((N_EXPERTS, 1))],
        out_specs=(slot_spec, slot_spec, slot_spec, _const_spec((N_EXPERTS, LANES))),
        scratch_shapes=[pltpu.VMEM((N_EXPERTS, 1), F32)],
        compiler_params=_cparams(("arbitrary",)),
        name="route",
    )(logits_t, bias_col)


def _dest_body(pstart_ref, eidx_ref, rank_ref, o_ref):
    e = eidx_ref[...]
    d = rank_ref[...]
    for j in range(N_EXPERTS):
        d = d + jnp.where(e == j, pstart_ref[j], 0)
    o_ref[...] = d


def _dest(pstart, eidx, rank, tr):
    t = eidx.shape[1]
    spec = pl.BlockSpec((SLOTS, tr), lambda i, ps: (0, i))
    return pl.pallas_call(
        _dest_body,
        out_shape=jax.ShapeDtypeStruct((SLOTS, t), I32),
        grid_spec=pltpu.PrefetchScalarGridSpec(
            num_scalar_prefetch=1, grid=(t // tr,), in_specs=[spec, spec], out_specs=spec),
        compiler_params=_cparams(("parallel",)),
        name="dest",
    )(pstart, eidx, rank)


def _sc_mesh():
    return plsc.VectorSubcoreMesh(core_axis_name="c", subcore_axis_name="s")


def _sc_worker_id():
    return lax.axis_index("s") * SC_CORES + lax.axis_index("c")


def _dispatch(hpk, dest3, n_rows):
    _, w = hpk.shape
    n_chunks, _, c = dest3.shape
    per_worker = n_chunks // SC_WORKERS

    def body(h_hbm, d_hbm, xs_hbm, idx_v, rows_v, sem):
        wid = _sc_worker_id()

        @pl.loop(0, per_worker)
        def _(i):
            ch = wid * per_worker + i
            pltpu.sync_copy(d_hbm.at[ch], idx_v)
            pltpu.sync_copy(h_hbm.at[pl.ds(ch * c, c)], rows_v)
            copies = [pltpu.async_copy(rows_v, xs_hbm.at[idx_v.at[k]], sem) for k in range(TOP_K)]
            for cp in copies:
                cp.wait()

    return pl.kernel(
        body, mesh=_sc_mesh(),
        out_type=jax.ShapeDtypeStruct((n_rows, w), U32),
        scratch_types=[pltpu.VMEM((SLOTS, c), I32), pltpu.VMEM((c, w), U32), pltpu.SemaphoreType.DMA],
        name="dispatch",
    )(hpk, dest3)


def _experts_body(n_sub, be_ref, nv_ref, first_ref, nu_ref, xs_ref, wg_ref, wu_ref, wd_ref, ys_ref,
                  wgb_ref, wub_ref, wdb_ref):
    del be_ref
    i = pl.program_id(0)

    @pl.when(first_ref[i] == 1)
    def _():
        wgb_ref[...] = wg_ref[0].astype(BF16)
        wub_ref[...] = wu_ref[0].astype(BF16)
        wdb_ref[...] = wd_ref[0].astype(BF16)

    @pl.when(i < nu_ref[0])
    def _():
        half = D_MODEL // 2
        sub = ROW_BLOCK // n_sub
        row = lax.broadcasted_iota(I32, (sub, xs_ref.shape[1]), 0)
        gu = []
        for s in range(n_sub):
            x = jnp.where(row + s * sub < nv_ref[i], xs_ref[s * sub:(s + 1) * sub, :], jnp.uint32(0))
            xa, xb = _unpack_halves(x)
            xa = xa.astype(BF16)
            xb = xb.astype(BF16)
            g = _dot(xa, wgb_ref[:half, :]) + _dot(xb, wgb_ref[half:, :])
            u = _dot(xa, wub_ref[:half, :]) + _dot(xb, wub_ref[half:, :])
            gu.append((g, u))
        for s, (g, u) in enumerate(gu):
            hb = (g * jax.nn.sigmoid(g) * u).astype(BF16)
            ys_ref[s * sub:(s + 1) * sub, :] = _pack_halves(_dot(hb, wdb_ref[...]))

    @pl.when(i >= nu_ref[0])
    def _():
        ys_ref[...] = jnp.zeros_like(ys_ref)


def _experts(block_e, block_valid, block_first, n_used, xs, wg, wu, wd):
    n_rows, w = xs.shape
    n_blocks = n_rows // ROW_BLOCK

    def row_map(i, be, nv, bf, nu):
        return (jnp.minimum(i, nu[0] - 1), 0)

    def w_map(i, be, nv, bf, nu):
        return (be[i], 0, 0)

    return pl.pallas_call(
        functools.partial(_experts_body, ROW_BLOCK // 256),
        out_shape=jax.ShapeDtypeStruct((n_rows, w), U32),
        grid_spec=pltpu.PrefetchScalarGridSpec(
            num_scalar_prefetch=4, grid=(n_blocks,),
            in_specs=[pl.BlockSpec((ROW_BLOCK, w), row_map),
                      pl.BlockSpec((1, D_MODEL, D_EXPERT), w_map),
                      pl.BlockSpec((1, D_MODEL, D_EXPERT), w_map),
                      pl.BlockSpec((1, D_EXPERT, D_MODEL), w_map)],
            out_specs=pl.BlockSpec((ROW_BLOCK, w), lambda i, be, nv, bf, nu: (i, 0)),
            scratch_shapes=[pltpu.VMEM((D_MODEL, D_EXPERT), BF16), pltpu.VMEM((D_MODEL, D_EXPERT), BF16),
                            pltpu.VMEM((D_EXPERT, D_MODEL), BF16)]),
        compiler_params=_cparams(("arbitrary",)),
        name="experts",
    )(block_e, block_valid, block_first, n_used, xs, wg, wu, wd)


def _gather_back(ys, dest3):
    _, w = ys.shape
    n_chunks, _, c = dest3.shape
    per_worker = n_chunks // SC_WORKERS

    def body(ys_hbm, d_hbm, yk_hbm, idx_v, rows_v, gsem, osem):
        wid = _sc_worker_id()

        @pl.loop(0, per_worker)
        def _(i):
            ch = wid * per_worker + i
            pltpu.sync_copy(d_hbm.at[ch], idx_v)
            gathers = [pltpu.async_copy(ys_hbm.at[idx_v.at[k]], rows_v.at[k], gsem) for k in range(TOP_K)]
            for g in gathers:
                g.wait()
            outs = [pltpu.async_copy(rows_v.at[k], yk_hbm.at[k, pl.ds(ch * c, c)], osem) for k in range(TOP_K)]
            for o in outs:
                o.wait()

    return pl.kernel(
        body, mesh=_sc_mesh(),
        out_type=jax.ShapeDtypeStruct((TOP_K, n_chunks * c, w), U32),
        scratch_types=[pltpu.VMEM((SLOTS, c), I32), pltpu.VMEM((TOP_K, c, w), U32),
                       pltpu.SemaphoreType.DMA, pltpu.SemaphoreType.DMA],
        name="gather_back",
    )(ys, dest3)


def _combine_body(yk_ref, gate_ref, hp_ref, x1_ref, mod_ref, wsg_ref, wsu_ref, wsd_ref, gfin_ref, o_ref):
    ha, hb = _unpack_halves(hp_ref[...])
    ha = ha.astype(BF16)
    hb = hb.astype(BF16)
    half = D_MODEL // 2
    g = _dot(ha, wsg_ref[:half, :]) + _dot(hb, wsg_ref[half:, :])
    u = _dot(ha, wsu_ref[:half, :]) + _dot(hb, wsu_ref[half:, :])
    shared = _dot((g * jax.nn.sigmoid(g) * u).astype(BF16), wsd_ref[...])

    gate = gate_ref[...]
    ra = jnp.zeros(ha.shape, F32)
    rb = jnp.zeros(ha.shape, F32)
    for k in range(TOP_K):
        ya, yb = _unpack_halves(yk_ref[k])
        gk = gate[:, k:k + 1]
        ra = ra + gk * ya
        rb = rb + gk * yb
    moe = shared + jnp.concatenate([ra, rb], axis=1)
    mod = mod_ref[0]
    x2 = x1_ref[...] + mod[5:6, :] * moe
    o_ref[...] = _rms(x2) * gfin_ref[...]


def _combine(yk, gate_t, hpk, x1, mod8, w, g_final, seq, tq):
    t, wd = hpk.shape
    tps = seq // tq
    return pl.pallas_call(
        _combine_body,
        out_shape=jax.ShapeDtypeStruct((t, D_MODEL), F32),
        grid=(t // tq,),
        in_specs=[pl.BlockSpec((TOP_K, tq, wd), lambda i: (0, i, 0)),
                  pl.BlockSpec((tq, SLOTS), lambda i: (i, 0)),
                  pl.BlockSpec((tq, wd), lambda i: (i, 0)),
                  pl.BlockSpec((tq, D_MODEL), lambda i: (i, 0)),
                  pl.BlockSpec((1, 8, D_MODEL), lambda i: (i // tps, 0, 0)),
                  _const_spec(w["wsg"].shape), _const_spec(w["wsu"].shape), _const_spec(w["wsd"].shape),
                  _const_spec((1, D_MODEL))],
        out_specs=pl.BlockSpec((tq, D_MODEL), lambda i: (i, 0)),
        compiler_params=_cparams(("parallel",)),
        name="combine",
    )(yk, gate_t, hpk, x1, mod8, w["wsg"], w["wsu"], w["wsd"], g_final)


def _prep_weights(w_in, b_forget, g_q_lat, w_q_up, g_kv_lat, w_kv_up, w_o_mla, w_o_fox, w_out,
                  w_router, w_sh_gate, w_sh_up, w_sh_down):
    o_q, o_kv, o_kr = 0, MLA_Q_LORA, MLA_Q_LORA + MLA_KV_LORA
    o_fq = o_kr + MLA_ROPE
    o_fk, o_fv = o_fq + FOX_WIDTH, o_fq + 2 * FOX_WIDTH
    o_fl = o_fq + 3 * FOX_WIDTH
    o_ga = o_fl + FOX_HEADS
    o_gb = o_ga + D_MODEL
    w = {}
    w["wlat"] = w_in[:, o_q:o_kr].astype(BF16)
    half = MLA_ROPE // 2

    def rope_pair(cols):
        x1, x2 = cols[..., :half], cols[..., half:]
        z = jnp.zeros(cols.shape[:-1] + (MLA_NOPE,), cols.dtype)
        zt = jnp.zeros(cols.shape[:-1] + (HEAD_PAD - MLA_QK,), cols.dtype)
        plain = jnp.concatenate([z, x1, x2, zt], axis=-1)
        rot = jnp.concatenate([z, -x2, x1, zt], axis=-1)
        return plain, rot

    kr_plain, kr_rot = rope_pair(w_in[:, o_kr:o_fq])
    w["wkr"] = jnp.concatenate([kr_plain, kr_rot], axis=1).astype(BF16)
    w["gq"] = g_q_lat.reshape(1, -1)
    w["gkv"] = g_kv_lat.reshape(1, -1)

    scale_a = LOG2E / math.sqrt(MLA_QK)
    wq = (w_q_up * scale_a).reshape(MLA_Q_LORA, MLA_HEADS, MLA_QK)
    q_plain, q_rot = rope_pair(wq[..., MLA_NOPE:])
    nope = jnp.concatenate([wq[..., :MLA_NOPE], jnp.zeros((MLA_Q_LORA, MLA_HEADS, HEAD_PAD - MLA_NOPE), F32)], -1)
    w["wqa"] = (nope + q_plain).reshape(MLA_Q_LORA, -1).astype(BF16)
    w["wqb"] = q_rot.reshape(MLA_Q_LORA, -1).astype(BF16)

    wkv = w_kv_up.reshape(MLA_KV_LORA, MLA_HEADS, MLA_NOPE + MLA_V)
    zpad = jnp.zeros((MLA_KV_LORA, MLA_HEADS, HEAD_PAD - MLA_NOPE), F32)
    w["wka"] = jnp.concatenate([wkv[..., :MLA_NOPE], zpad], -1).reshape(MLA_KV_LORA, -1).astype(BF16)

    w["wva"] = wkv[..., MLA_NOPE:].reshape(MLA_KV_LORA, -1).astype(BF16)

    w["wfq"] = (w_in[:, o_fq:o_fk] * (LOG2E / math.sqrt(FOX_HEAD_DIM))).astype(BF16)
    w["wfk"] = w_in[:, o_fk:o_fv].astype(BF16)
    w["wfv"] = w_in[:, o_fv:o_fl].astype(BF16)

    wfl = jnp.concatenate([w_in[:, o_fl:o_ga]] * 3 + [jnp.zeros((D_MODEL, LANES - 3 * FOX_HEADS), F32)], 1)
    wfl_hi = wfl.astype(BF16)
    w["wfl2"] = jnp.concatenate([wfl_hi, (wfl - wfl_hi.astype(F32)).astype(BF16)], axis=1)
    w["bfl"] = jnp.concatenate([b_forget] * 3 + [jnp.zeros((LANES - 3 * FOX_HEADS,), F32)]).reshape(1, LANES)

    pall = np.zeros((LANES, 2 * FOX_WIDTH), np.float32)
    ones_row = 3 * FOX_HEADS
    for hd in range(FOX_HEADS):
        gq = (hd ^ 1) * FOX_HEAD_DIM
        gk = FOX_WIDTH + (hd ^ 1) * FOX_HEAD_DIM
        for term in range(3):
            pall[term * FOX_HEADS + hd, gq + FQ_COL + term] = 1.0
            pall[ones_row, gq + FK_COL + term] = 1.0
            pall[ones_row, gk + FQ_COL + term] = 1.0
            pall[term * FOX_HEADS + hd, gk + FK_COL + term] = -1.0
    w["pall"] = jnp.asarray(pall, BF16)

    w["wg"] = w_in[:, o_ga:o_gb + D_MODEL].astype(BF16)
    w["womla"] = w_o_mla.astype(BF16)
    w["wofox"] = w_o_fox.astype(BF16)
    w["wout"] = w_out.astype(BF16)
    wr = jnp.concatenate([w_router, jnp.zeros((D_MODEL, LANES - N_EXPERTS), F32)], 1)
    wr_hi = wr.astype(BF16)
    w["wr2"] = jnp.concatenate([wr_hi, (wr - wr_hi.astype(F32)).astype(BF16)], axis=1)
    w["wsg"] = w_sh_gate.astype(BF16)
    w["wsu"] = w_sh_up.astype(BF16)
    w["wsd"] = w_sh_down.astype(BF16)
    return w


def _rope_freq_row():
    half = MLA_ROPE // 2
    inv = np.power(ROPE_THETA, -np.arange(half, dtype=np.float32) / half).astype(np.float32)
    row = np.zeros((1, LANES), np.float32)
    row[0, MLA_NOPE:MLA_NOPE + half] = inv
    row[0, MLA_NOPE + half:MLA_NOPE + 2 * half] = inv
    return jnp.asarray(row)


def kernel(x, c, positions, w_mod, b_mod, g_mix_norm, w_in, b_forget, g_q_lat, w_q_up, g_kv_lat, w_kv_up,
           w_o_mla, w_o_fox, w_out, g_ffn_norm, w_router, b_router, w_exp_gate, w_exp_up, w_exp_down,
           w_sh_gate, w_sh_up, w_sh_down, g_final):
    batch, seq, d = x.shape
    assert d == D_MODEL and w_mod.shape[0] == 1
    t = batch * seq
    tm = min(512, seq)
    bq = min(2048, seq)
    tr = min(512, t)
    to = min(256, seq)
    tc = min(256, seq)
    assert seq % tm == 0 and seq % bq == 0 and t % tr == 0 and seq % tc == 0 and seq % to == 0 and batch <= 8
    assert t % (SC_CHUNK * SC_WORKERS) == 0

    def layer0(a):
        return a.reshape(a.shape[1:])

    w = _prep_weights(*(layer0(a) for a in (w_in, b_forget, g_q_lat, w_q_up, g_kv_lat, w_kv_up, w_o_mla, w_o_fox,
                                            w_out, w_router, w_sh_gate, w_sh_up, w_sh_down)))

    c8 = jnp.zeros((8, D_MODEL), F32).at[:batch].set(c)
    mod = _mod(c8, layer0(w_mod), b_mod)
    mod8 = jnp.zeros((batch, 8, D_MODEL), F32).at[:, :N_MOD].set(mod[:batch].reshape(batch, N_MOD, D_MODEL))

    x2 = x.reshape(t, D_MODEL)
    fdec = _fox_decay(x2, mod8, g_mix_norm, w["wfl2"], w["bfl"], seq, tm)
    q_all, k_all, v_all, sg = _in_proj(x2, mod8, g_mix_norm, fdec, positions.reshape(t, 1),
                                       _rope_freq_row(), w, batch, seq, tm)
    o = _attention(q_all, k_all, v_all, bq)
    x1, hpk, logits_t = _out_proj(o.reshape(t, D_MODEL), sg, x2, mod8, g_ffn_norm, w, seq, to)

    eidx, rank, gate, counts = _route(logits_t, b_router.reshape(N_EXPERTS, 1), tr)

    cnt = counts[:, 0]
    padded = (cnt + ROW_BLOCK - 1) // ROW_BLOCK * ROW_BLOCK
    pend = jnp.cumsum(padded)
    pstart = pend - padded
    n_blocks = t * TOP_K // ROW_BLOCK + N_EXPERTS
    n_rows = n_blocks * ROW_BLOCK
    block_row = jnp.arange(n_blocks, dtype=I32) * ROW_BLOCK
    block_e = jnp.minimum(jnp.sum(pend[None, :] <= block_row[:, None], axis=1), N_EXPERTS - 1).astype(I32)
    n_used = (pend[-1:] // ROW_BLOCK).astype(I32)
    block_valid = jnp.clip((pstart + cnt)[block_e] - block_row, 0, ROW_BLOCK).astype(I32)
    block_first = jnp.concatenate([jnp.ones((1,), I32), (block_e[1:] != block_e[:-1]).astype(I32)])

    dest = _dest(pstart.astype(I32), eidx, rank, min(2048, t))
    dest3 = dest.reshape(SLOTS, t // SC_CHUNK, SC_CHUNK).transpose(1, 0, 2)
    xs = _dispatch(hpk, dest3, n_rows)
    ys = _experts(block_e, block_valid, block_first, n_used, xs,
                  layer0(w_exp_gate), layer0(w_exp_up), layer0(w_exp_down))
    yk = _gather_back(ys, dest3)
    out = _combine(yk, gate.T, hpk, x1, mod8, w, g_final.reshape(1, D_MODEL), seq, tc)
    return out.reshape(batch, seq, D_MODEL)
```

```python
import functools
import math

import numpy as np
import jax
import jax.numpy as jnp
from jax import lax
from jax.experimental import pallas as pl
from jax.experimental.pallas import tpu as pltpu
from jax.experimental.pallas import tpu_sc as plsc

F32 = jnp.float32
BF16 = jnp.bfloat16
I32 = jnp.int32
U32 = jnp.uint32

D_MODEL = 1024
MLA_HEADS = 8
MLA_Q_LORA = 256
MLA_KV_LORA = 128
MLA_NOPE = 64
MLA_ROPE = 32
MLA_V = 64
MLA_QK = MLA_NOPE + MLA_ROPE
ROPE_THETA = 10000.0
FOX_HEADS = 8
FOX_HEAD_DIM = 64
FOX_WIDTH = FOX_HEADS * FOX_HEAD_DIM
N_HEADS = MLA_HEADS + FOX_HEADS
N_EXPERTS = 64
N_GROUPS = 8
GROUP_SIZE = N_EXPERTS // N_GROUPS
TOPK_GROUPS = 4
TOP_K = 6
D_EXPERT = 256
ROUTED_SCALE = 2.5
N_MOD = 6
NORM_EPS = 1e-6
NEG_INF = -1e30
LOG2E = math.log2(math.e)

LANES = 128
HEAD_PAD = 128
V_ROWS = MLA_V + 16
ROW_BLOCK = 512
SLOTS = 8
SC_CORES = 2
SC_SUBCORES = 16
SC_WORKERS = SC_CORES * SC_SUBCORES
SC_CHUNK = 32
VMEM_LIMIT = 56 * 1024 * 1024

FQ_COL = 0
FK_COL = 3


def _cparams(sem, vmem=VMEM_LIMIT):
    return pltpu.CompilerParams(dimension_semantics=sem, vmem_limit_bytes=vmem)


def _const_spec(shape):
    nd = len(shape)
    return pl.BlockSpec(shape, lambda *_: (0,) * nd)


def _rms(x):
    return x * lax.rsqrt(jnp.mean(x * x, axis=-1, keepdims=True) + NORM_EPS)


def _split3(x):
    hi = x.astype(BF16)
    r = x - hi.astype(F32)
    mid = r.astype(BF16)
    lo = (r - mid.astype(F32)).astype(BF16)
    return hi, mid, lo


def _dot(a, b):
    return jnp.dot(a, b, preferred_element_type=F32)


def _modulated_norm(x, gain, mod, shift_row, scale_row):
    shift = mod[shift_row:shift_row + 1, :]
    scale = mod[scale_row:scale_row + 1, :]
    return _rms(x) * gain * (1.0 + scale) + shift


def _mod_body(c_ref, w_ref, b_ref, o_ref):
    c = c_ref[...]
    cond = c * jax.nn.sigmoid(c)
    o_ref[...] = _dot(cond.astype(BF16), w_ref[...].astype(BF16)) + b_ref[...]


def _mod(c8, w_mod, b_mod):
    n = w_mod.shape[1]
    tn = D_MODEL
    return pl.pallas_call(
        _mod_body,
        out_shape=jax.ShapeDtypeStruct((8, n), F32),
        grid=(n // tn,),
        in_specs=[_const_spec((8, D_MODEL)),
                  pl.BlockSpec((D_MODEL, tn), lambda j: (0, j)),
                  pl.BlockSpec((1, tn), lambda j: (0, j))],
        out_specs=pl.BlockSpec((8, tn), lambda j: (0, j)),
        compiler_params=_cparams(("parallel",)),
        name="mod",
    )(c8, w_mod, b_mod)


def _decay_body(tiles_per_seq, x_ref, mod_ref, g_ref, w2_ref, b_ref, tri_ref, o_ref, carry_ref):
    i = pl.program_id(0)

    @pl.when(i % tiles_per_seq == 0)
    def _():
        carry_ref[...] = jnp.zeros_like(carry_ref)

    h = _modulated_norm(x_ref[...], g_ref[...], mod_ref[0], 0, 1)
    hhi = h.astype(BF16)
    hlo = (h - hhi.astype(F32)).astype(BF16)
    z2 = _dot(hhi, w2_ref[...])
    z = z2[:, :LANES] + z2[:, LANES:] + _dot(hlo, w2_ref[:, :LANES]) + b_ref[...]
    logf = jnp.minimum(z, 0.0) - jnp.log1p(jnp.exp(-jnp.abs(z)))
    tm = logf.shape[0]
    c3 = _dot(tri_ref[...], jnp.concatenate(_split3(logf), axis=1))
    cum = c3[:, :LANES] + c3[:, LANES:2 * LANES] + c3[:, 2 * LANES:] + carry_ref[...]
    o_ref[...] = cum
    carry_ref[...] = cum[tm - 1:tm, :]


def _fox_decay(x2, mod8, g_mix, wfl2, bfl, seq, tm):
    t = x2.shape[0]
    tps = seq // tm
    tri = jnp.asarray(np.tril(np.ones((tm, tm), np.float32)), BF16)
    return pl.pallas_call(
        functools.partial(_decay_body, tps),
        out_shape=jax.ShapeDtypeStruct((t, LANES), F32),
        grid=(t // tm,),
        in_specs=[pl.BlockSpec((tm, D_MODEL), lambda i: (i, 0)),
                  pl.BlockSpec((1, 8, D_MODEL), lambda i: (i // tps, 0, 0)),
                  _const_spec((1, D_MODEL)),
                  _const_spec((D_MODEL, 2 * LANES)),
                  _const_spec((1, LANES)),
                  _const_spec((tm, tm))],
        out_specs=pl.BlockSpec((tm, LANES), lambda i: (i, 0)),
        scratch_shapes=[pltpu.VMEM((1, LANES), F32)],
        compiler_params=_cparams(("arbitrary",)),
        name="fox_decay",
    )(x2, mod8, g_mix, wfl2, bfl, tri)


def _in_proj_body(x_ref, mod_ref, g_ref, f_ref, pos_ref, freq_ref,
                  wlat_ref, wkr_ref, gq_ref, gkv_ref, wqa_ref, wqb_ref, wka_ref, wva_ref,
                  wfq_ref, wfk_ref, wfv_ref, pall_ref, wg_ref,
                  q_ref, k_ref, v_ref, sg_ref):
    h = _modulated_norm(x_ref[...], g_ref[...], mod_ref[0], 0, 1)
    hb = h.astype(BF16)

    lat = _dot(hb, wlat_ref[...])
    qn = (_rms(lat[:, :MLA_Q_LORA]) * gq_ref[...]).astype(BF16)
    kvn = (_rms(lat[:, MLA_Q_LORA:]) * gkv_ref[...]).astype(BF16)
    ang = pos_ref[...].astype(F32) * freq_ref[...]
    cs = jnp.cos(ang)
    sn = jnp.sin(ang)
    kr = _dot(hb, wkr_ref[...])
    kpe = kr[:, :HEAD_PAD] * cs + kr[:, HEAD_PAD:] * sn
    qa = _dot(qn, wqa_ref[...])
    qb = _dot(qn, wqb_ref[...])
    ka = _dot(kvn, wka_ref[...])
    va = _dot(kvn, wva_ref[...])
    for hd in range(MLA_HEADS):
        sl = slice(hd * HEAD_PAD, (hd + 1) * HEAD_PAD)
        q_ref[0, hd] = (qa[:, sl] * cs + qb[:, sl] * sn).astype(BF16)
        k_ref[0, hd] = (ka[:, sl] + kpe).astype(BF16)
    ones_rows = jnp.ones((V_ROWS - MLA_V, va.shape[0]), BF16)

    def put_values(first_head, vals):
        for hp in range(vals.shape[1] // (2 * MLA_V)):
            vt = vals[:, hp * 2 * MLA_V:(hp + 1) * 2 * MLA_V].T.astype(BF16)
            for j in range(2):
                v_ref[0, first_head + 2 * hp + j, :MLA_V, :] = vt[j * MLA_V:(j + 1) * MLA_V, :]
                v_ref[0, first_head + 2 * hp + j, MLA_V:, :] = ones_rows

    put_values(0, va)

    hi, mid, lo = _split3(f_ref[...] * LOG2E)
    lane = lax.broadcasted_iota(I32, hi.shape, 1)
    f3 = jnp.where(lane < FOX_HEADS, hi.astype(F32), jnp.where(lane < 2 * FOX_HEADS, mid.astype(F32),
                   jnp.where(lane < 3 * FOX_HEADS, lo.astype(F32), jnp.where(lane == 3 * FOX_HEADS, 1.0, 0.0))))
    aug = _dot(f3.astype(BF16), pall_ref[...]).astype(BF16)
    fq = _dot(hb, wfq_ref[...]).astype(BF16)
    fk = _dot(hb, wfk_ref[...]).astype(BF16)
    hdim = FOX_HEAD_DIM
    for hd in range(FOX_HEADS):
        own = slice((hd % 2) * hdim, (hd % 2 + 1) * hdim)
        spare = slice((1 - hd % 2) * hdim, (2 - hd % 2) * hdim)
        src = slice(hd * hdim, (hd + 1) * hdim)
        asrc = (hd ^ 1) * hdim
        q_ref[0, MLA_HEADS + hd, :, own] = fq[:, src]
        k_ref[0, MLA_HEADS + hd, :, own] = fk[:, src]
        q_ref[0, MLA_HEADS + hd, :, spare] = aug[:, asrc:asrc + hdim]
        k_ref[0, MLA_HEADS + hd, :, spare] = aug[:, FOX_WIDTH + asrc:FOX_WIDTH + asrc + hdim]
    put_values(MLA_HEADS, _dot(hb, wfv_ref[...]))

    sg_ref[...] = jax.nn.sigmoid(_dot(hb, wg_ref[...])).astype(BF16)


def _in_proj(x2, mod8, g_mix, fdec, pos, freq, w, batch, seq, tm):
    t = x2.shape[0]
    tps = seq // tm
    consts = [w["wlat"], w["wkr"], w["gq"], w["gkv"], w["wqa"], w["wqb"], w["wka"], w["wva"],
              w["wfq"], w["wfk"], w["wfv"], w["pall"], w["wg"]]
    head_shape = jax.ShapeDtypeStruct((batch, N_HEADS, seq, HEAD_PAD), BF16)
    head_spec = pl.BlockSpec((1, N_HEADS, tm, HEAD_PAD), lambda i: (i // tps, 0, i % tps, 0))
    pair_shape = jax.ShapeDtypeStruct((batch, N_HEADS, V_ROWS, seq), BF16)
    pair_spec = pl.BlockSpec((1, N_HEADS, V_ROWS, tm), lambda i: (i // tps, 0, 0, i % tps))
    return pl.pallas_call(
        _in_proj_body,
        out_shape=(head_shape, head_shape, pair_shape,
                   jax.ShapeDtypeStruct((t, 2 * D_MODEL), BF16)),
        grid=(t // tm,),
        in_specs=[pl.BlockSpec((tm, D_MODEL), lambda i: (i, 0)),
                  pl.BlockSpec((1, 8, D_MODEL), lambda i: (i // tps, 0, 0)),
                  _const_spec((1, D_MODEL)),
                  pl.BlockSpec((tm, LANES), lambda i: (i, 0)),
                  pl.BlockSpec((tm, 1), lambda i: (i, 0)),
                  _const_spec((1, LANES))] + [_const_spec(a.shape) for a in consts],
        out_specs=(head_spec, head_spec, pair_spec,
                   pl.BlockSpec((tm, 2 * D_MODEL), lambda i: (i, 0))),
        compiler_params=_cparams(("parallel",)),
        name="in_proj",
    )(x2, mod8, g_mix, fdec, pos, freq, *consts)


def _attn_body(bq, sq, q_ref, k_ref, v_ref, o_ref, s_scr, p_scr, acc_scr):
    qi = pl.program_id(2)
    bk = sq
    n_heads = q_ref.shape[1]
    n_sub = bq // sq
    assert n_sub % 2 == 0
    chains = [(hh, u) for hh in range(n_heads) for u in range(n_sub)]
    n_chains = len(chains)
    n_main = qi * n_sub

    def chunk_start(j):
        return pl.multiple_of(jnp.maximum(j, 0) * bk, bk)

    def scores(c, j, par):
        hh, u = chains[c]
        k = k_ref[0, hh, pl.ds(chunk_start(j), bk), :]
        q = q_ref[0, hh, u * sq:(u + 1) * sq, :]
        s = lax.dot_general(k, q, (((1,), (1,)), ((), ())), preferred_element_type=F32)
        s_scr[par, c] = s
        return jnp.max(s, axis=0, keepdims=True)

    def values(c, j, par, alpha):
        vt = v_ref[0, chains[c][0], :, pl.ds(chunk_start(j), bk)]
        acc_scr[c] = alpha * acc_scr[c] + _dot(vt, p_scr[par, c])

    def softmax(c, par, m, smax, masked):
        s = s_scr[par, c]
        if masked:
            key = lax.broadcasted_iota(I32, (bk, sq), 0)
            qry = lax.broadcasted_iota(I32, (bk, sq), 1)
            s = jnp.where(key <= qry, s, NEG_INF)
            smax = jnp.max(s, axis=0, keepdims=True)
        m_new = jnp.maximum(m, smax)
        p_scr[par, c] = jnp.exp2((s - m_new).astype(BF16))
        return m_new, jnp.exp2(m - m_new)

    def stage(j, par, state, active, has_next, masked_of):
        nxt = {c: scores(c, j + 1, 1 - par) if has_next(c) else state[c][1] for c in active}
        for c in active:
            values(c, j - 1, 1 - par, state[c][2])
        out = list(state)
        for c in active:
            m, smax, _ = state[c]
            m_new, alpha = softmax(c, par, m, smax, masked_of(c))
            out[c] = (m_new, nxt[c], alpha)
        return out

    p_scr[1] = jnp.zeros(p_scr.shape[1:], BF16)
    acc_scr[...] = jnp.zeros_like(acc_scr)
    state = [(jnp.full((1, sq), NEG_INF, F32), scores(c, 0, 0), jnp.ones((1, sq), F32)) for c in range(n_chains)]

    def step(jj, st):
        for par in range(2):
            st = stage(2 * jj + par, par, list(st), range(n_chains), lambda c: True, lambda c: False)
        return tuple(st)

    state = list(lax.fori_loop(0, n_main // 2, step, tuple(state)))

    for t in range(n_sub):
        active = [c for c, (_, u) in enumerate(chains) if u >= t]
        state = stage(n_main + t, t % 2, state, active, lambda c, t=t: chains[c][1] > t,
                      lambda c, t=t: chains[c][1] == t)

    heads = []
    for hh in range(n_heads):
        parts = []
        for u in range(n_sub):
            c = hh * n_sub + u
            values(c, n_main + u, u % 2, state[c][2])
            parts.append(acc_scr[c, :MLA_V, :] / acc_scr[c, MLA_V:MLA_V + 1, :])
        heads.append(jnp.concatenate(parts, axis=1))
    o_ref[0] = jnp.concatenate(heads, axis=0).T.astype(BF16)


def _attention(q_all, k_all, v_all, bq):
    batch, _, seq, _ = q_all.shape
    sq = min(256, bq)
    n_chains = 2 * (bq // sq)
    return pl.pallas_call(
        functools.partial(_attn_body, bq, sq),
        out_shape=jax.ShapeDtypeStruct((batch, seq, N_HEADS * MLA_V), BF16),
        grid=(batch, N_HEADS // 2, seq // bq),
        in_specs=[pl.BlockSpec((1, 2, bq, HEAD_PAD), lambda b, hp, qi: (b, hp, qi, 0)),
                  pl.BlockSpec((1, 2, seq, HEAD_PAD), lambda b, hp, qi: (b, hp, 0, 0)),
                  pl.BlockSpec((1, 2, V_ROWS, seq), lambda b, hp, qi: (b, hp, 0, 0))],
        out_specs=pl.BlockSpec((1, bq, 2 * MLA_V), lambda b, hp, qi: (b, qi, hp)),
        scratch_shapes=[pltpu.VMEM((2, n_chains, sq, sq), F32),
                        pltpu.VMEM((2, n_chains, sq, sq), BF16),
                        pltpu.VMEM((n_chains, V_ROWS, sq), F32)],
        compiler_params=_cparams(("parallel", "parallel", "arbitrary")),
        name="attention",
    )(q_all, k_all, v_all)


def _pack_halves(x):
    w = x.shape[1] // 2
    a = lax.bitcast_convert_type(x[:, :w].astype(BF16).astype(F32), U32)
    b = lax.bitcast_convert_type(x[:, w:].astype(BF16).astype(F32), U32)
    return a | (b >> 16)


def _unpack_halves(p):
    a = lax.bitcast_convert_type(p & jnp.uint32(0xFFFF0000), F32)
    b = lax.bitcast_convert_type(p << 16, F32)
    return a, b


def _out_proj_body(o_ref, sg_ref, x_ref, mod_ref, gffn_ref, womla_ref, wofox_ref, wout_ref,
                   wr2_ref, x1_ref, hp_ref, lg_ref):
    o = o_ref[...]
    half = o.shape[1] // 2
    mo = _dot(o[:, :half], womla_ref[...])
    fo = _dot(o[:, half:], wofox_ref[...])
    sg = sg_ref[...]
    merged = sg[:, :D_MODEL].astype(F32) * mo + sg[:, D_MODEL:].astype(F32) * fo
    mix = _dot(merged.astype(BF16), wout_ref[...])
    mod = mod_ref[0]
    x1 = x_ref[...] + mod[2:3, :] * mix
    x1_ref[...] = x1
    h2 = _modulated_norm(x1, gffn_ref[...], mod, 3, 4)
    hhi = h2.astype(BF16)
    hlo = (h2 - hhi.astype(F32)).astype(BF16)
    l2 = _dot(hhi, wr2_ref[...])
    lg_ref[...] = (l2[:, :LANES] + l2[:, LANES:] + _dot(hlo, wr2_ref[:, :LANES])).T
    hp_ref[...] = _pack_halves(h2)


def _out_proj(o2, sg, x2, mod8, g_ffn, w, seq, tm):
    t = x2.shape[0]
    tps = seq // tm
    consts = [w["womla"], w["wofox"], w["wout"], w["wr2"]]
    return pl.pallas_call(
        _out_proj_body,
        out_shape=(jax.ShapeDtypeStruct((t, D_MODEL), F32),
                   jax.ShapeDtypeStruct((t, D_MODEL // 2), U32),
                   jax.ShapeDtypeStruct((LANES, t), F32)),
        grid=(t // tm,),
        in_specs=[pl.BlockSpec((tm, D_MODEL), lambda i: (i, 0)),
                  pl.BlockSpec((tm, 2 * D_MODEL), lambda i: (i, 0)),
                  pl.BlockSpec((tm, D_MODEL), lambda i: (i, 0)),
                  pl.BlockSpec((1, 8, D_MODEL), lambda i: (i // tps, 0, 0)),
                  _const_spec((1, D_MODEL))] + [_const_spec(a.shape) for a in consts],
        out_specs=(pl.BlockSpec((tm, D_MODEL), lambda i: (i, 0)),
                   pl.BlockSpec((tm, D_MODEL // 2), lambda i: (i, 0)),
                   pl.BlockSpec((LANES, tm), lambda i: (0, i))),
        compiler_params=_cparams(("parallel",)),
        name="out_proj",
    )(o2, sg, x2, mod8, g_ffn, *consts)


def _route_body(lt_ref, b_ref, eidx_ref, rank_ref, gate_ref, cnt_ref, carry_ref):
    i = pl.program_id(0)

    @pl.when(i == 0)
    def _():
        carry_ref[...] = jnp.zeros_like(carry_ref)

    s = jax.nn.sigmoid(lt_ref[...])
    c = s + b_ref[...]
    tr = s.shape[1]
    sub = lax.broadcasted_iota(I32, (GROUP_SIZE, tr), 0).astype(F32)

    gs = []
    for g in range(N_GROUPS):
        cg = c[g * GROUP_SIZE:(g + 1) * GROUP_SIZE, :]
        m1 = jnp.max(cg, axis=0, keepdims=True)
        i1 = jnp.min(jnp.where(cg == m1, sub, float(GROUP_SIZE)), axis=0, keepdims=True)
        m2 = jnp.max(jnp.where(sub == i1, NEG_INF, cg), axis=0, keepdims=True)
        gs.append(m1 + m2)

    masked = []
    for g in range(N_GROUPS):
        beats = jnp.zeros_like(gs[g])
        for o in range(N_GROUPS):
            if o == g:
                continue
            better = (gs[o] >= gs[g]) if o < g else (gs[o] > gs[g])
            beats = beats + jnp.where(better, 1.0, 0.0)
        keep = beats < float(TOPK_GROUPS)
        cg = c[g * GROUP_SIZE:(g + 1) * GROUP_SIZE, :]
        masked.append(jnp.where(keep, cg, NEG_INF))
    mc = jnp.concatenate(masked, axis=0)

    eio = lax.broadcasted_iota(I32, (N_EXPERTS, tr), 0).astype(F32)
    picks = []
    selected = jnp.zeros((N_EXPERTS, tr), F32)
    for _ in range(TOP_K):
        m = jnp.max(mc, axis=0, keepdims=True)
        idx = jnp.min(jnp.where(mc == m, eio, float(N_EXPERTS)), axis=0, keepdims=True)
        hit = eio == idx
        picks.append(idx)
        selected = jnp.where(hit, 1.0, selected)
        mc = jnp.where(hit, -3.0e38, mc)

    ssum = jnp.sum(selected * s, axis=0, keepdims=True)
    gate_full = selected * s / ssum * ROUTED_SCALE

    r_io = lax.broadcasted_iota(I32, (tr, tr), 0)
    c_io = lax.broadcasted_iota(I32, (tr, tr), 1)
    upper = jnp.where(r_io < c_io, 1.0, 0.0).astype(BF16)
    before = _dot(selected.astype(BF16), upper) + carry_ref[...]
    carry_new = carry_ref[...] + jnp.sum(selected, axis=1, keepdims=True)
    carry_ref[...] = carry_new
    cnt_ref[...] = jnp.broadcast_to(carry_new, cnt_ref.shape).astype(I32)

    for r in range(SLOTS):
        if r < TOP_K:
            hit = eio == picks[r]
            eidx_ref[r:r + 1, :] = picks[r].astype(I32)
            rank_ref[r:r + 1, :] = jnp.sum(jnp.where(hit, before, 0.0), axis=0, keepdims=True).astype(I32)
            gate_ref[r:r + 1, :] = jnp.sum(jnp.where(hit, gate_full, 0.0), axis=0, keepdims=True)
        else:
            eidx_ref[r:r + 1, :] = jnp.zeros((1, tr), I32)
            rank_ref[r:r + 1, :] = jnp.zeros((1, tr), I32)
            gate_ref[r:r + 1, :] = jnp.zeros((1, tr), F32)


def _route(logits_t, bias_col, tr):
    t = logits_t.shape[1]
    slot_i = jax.ShapeDtypeStruct((SLOTS, t), I32)
    slot_spec = pl.BlockSpec((SLOTS, tr), lambda i: (0, i))
    return pl.pallas_call(
        _route_body,
        out_shape=(slot_i, slot_i, jax.ShapeDtypeStruct((SLOTS, t), F32),
                   jax.ShapeDtypeStruct((N_EXPERTS, LANES), I32)),
        grid=(t // tr,),
        in_specs=[pl.BlockSpec((N_EXPERTS, tr), lambda i: (0, i)),
                  _const_spec((N_EXPERTS, 1))],
        out_specs=(slot_spec, slot_spec, slot_spec, _const_spec((N_EXPERTS, LANES))),
        scratch_shapes=[pltpu.VMEM((N_EXPERTS, 1), F32)],
        compiler_params=_cparams(("arbitrary",)),
        name="route",
    )(logits_t, bias_col)


def _dest_body(pstart_ref, eidx_ref, rank_ref, o_ref):
    e = eidx_ref[...]
    d = rank_ref[...]
    for j in range(N_EXPERTS):
        d = d + jnp.where(e == j, pstart_ref[j], 0)
    o_ref[...] = d


def _dest(pstart, eidx, rank, tr):
    t = eidx.shape[1]
    spec = pl.BlockSpec((SLOTS, tr), lambda i, ps: (0, i))
    return pl.pallas_call(
        _dest_body,
        out_shape=jax.ShapeDtypeStruct((SLOTS, t), I32),
        grid_spec=pltpu.PrefetchScalarGridSpec(
            num_scalar_prefetch=1, grid=(t // tr,), in_specs=[spec, spec], out_specs=spec),
        compiler_params=_cparams(("parallel",)),
        name="dest",
    )(pstart, eidx, rank)


def _sc_mesh():
    return plsc.VectorSubcoreMesh(core_axis_name="c", subcore_axis_name="s")


def _sc_worker_id():
    return lax.axis_index("s") * SC_CORES + lax.axis_index("c")


def _dispatch(hpk, dest3, n_rows):
    _, w = hpk.shape
    n_chunks, _, c = dest3.shape
    per_worker = n_chunks // SC_WORKERS

    def body(h_hbm, d_hbm, xs_hbm, idx_v, rows_v, sem):
        wid = _sc_worker_id()

        @pl.loop(0, per_worker)
        def _(i):
            ch = wid * per_worker + i
            pltpu.sync_copy(d_hbm.at[ch], idx_v)
            pltpu.sync_copy(h_hbm.at[pl.ds(ch * c, c)], rows_v)
            copies = [pltpu.async_copy(rows_v, xs_hbm.at[idx_v.at[k]], sem) for k in range(TOP_K)]
            for cp in copies:
                cp.wait()

    return pl.kernel(
        body, mesh=_sc_mesh(),
        out_type=jax.ShapeDtypeStruct((n_rows, w), U32),
        scratch_types=[pltpu.VMEM((SLOTS, c), I32), pltpu.VMEM((c, w), U32), pltpu.SemaphoreType.DMA],
        name="dispatch",
    )(hpk, dest3)


def _experts_body(n_sub, be_ref, nv_ref, first_ref, nu_ref, xs_ref, wg_ref, wu_ref, wd_ref, ys_ref,
                  wgb_ref, wub_ref, wdb_ref):
    del be_ref
    i = pl.program_id(0)

    @pl.when(first_ref[i] == 1)
    def _():
        wgb_ref[...] = wg_ref[0].astype(BF16)
        wub_ref[...] = wu_ref[0].astype(BF16)
        wdb_ref[...] = wd_ref[0].astype(BF16)

    half = D_MODEL // 2
    sub = ROW_BLOCK // n_sub

    def run(n_active):
        row = lax.broadcasted_iota(I32, (sub, xs_ref.shape[1]), 0)
        gu = []
        for s in range(n_active):
            x = jnp.where(row + s * sub < nv_ref[i], xs_ref[s * sub:(s + 1) * sub, :], jnp.uint32(0))
            xa, xb = _unpack_halves(x)
            xa = xa.astype(BF16)
            xb = xb.astype(BF16)
            g = _dot(xa, wgb_ref[:half, :]) + _dot(xb, wgb_ref[half:, :])
            u = _dot(xa, wub_ref[:half, :]) + _dot(xb, wub_ref[half:, :])
            gu.append((g, u))
        for s, (g, u) in enumerate(gu):
            hb = (g * jax.nn.sigmoid(g) * u).astype(BF16)
            ys_ref[s * sub:(s + 1) * sub, :] = _pack_halves(_dot(hb, wdb_ref[...]))
        if n_active < n_sub:
            ys_ref[n_active * sub:, :] = jnp.zeros((ROW_BLOCK - n_active * sub, ys_ref.shape[1]), U32)

    used = i < nu_ref[0]
    for n_active in range(1, n_sub + 1):
        lo = (n_active - 1) * sub
        in_range = (nv_ref[i] > lo) if n_active == n_sub else ((nv_ref[i] > lo) & (nv_ref[i] <= lo + sub))
        pl.when(used & in_range)(functools.partial(run, n_active))

    @pl.when(jnp.logical_not(used))
    def _():
        ys_ref[...] = jnp.zeros_like(ys_ref)


def _experts(block_e, block_valid, block_first, n_used, xs, wg, wu, wd):
    n_rows, w = xs.shape
    n_blocks = n_rows // ROW_BLOCK

    def row_map(i, be, nv, bf, nu):
        return (jnp.minimum(i, nu[0] - 1), 0)

    def w_map(i, be, nv, bf, nu):
        return (be[i], 0, 0)

    return pl.pallas_call(
        functools.partial(_experts_body, ROW_BLOCK // 256),
        out_shape=jax.ShapeDtypeStruct((n_rows, w), U32),
        grid_spec=pltpu.PrefetchScalarGridSpec(
            num_scalar_prefetch=4, grid=(n_blocks,),
            in_specs=[pl.BlockSpec((ROW_BLOCK, w), row_map),
                      pl.BlockSpec((1, D_MODEL, D_EXPERT), w_map),
                      pl.BlockSpec((1, D_MODEL, D_EXPERT), w_map),
                      pl.BlockSpec((1, D_EXPERT, D_MODEL), w_map)],
            out_specs=pl.BlockSpec((ROW_BLOCK, w), lambda i, be, nv, bf, nu: (i, 0)),
            scratch_shapes=[pltpu.VMEM((D_MODEL, D_EXPERT), BF16), pltpu.VMEM((D_MODEL, D_EXPERT), BF16),
                            pltpu.VMEM((D_EXPERT, D_MODEL), BF16)]),
        compiler_params=_cparams(("arbitrary",)),
        name="experts",
    )(block_e, block_valid, block_first, n_used, xs, wg, wu, wd)


def _gather_back(ys, dest3):
    _, w = ys.shape
    n_chunks, _, c = dest3.shape
    per_worker = n_chunks // SC_WORKERS

    def body(ys_hbm, d_hbm, yk_hbm, idx_v, rows_v, gsem, osem):
        wid = _sc_worker_id()

        @pl.loop(0, per_worker)
        def _(i):
            ch = wid * per_worker + i
            pltpu.sync_copy(d_hbm.at[ch], idx_v)
            gathers = [pltpu.async_copy(ys_hbm.at[idx_v.at[k]], rows_v.at[k], gsem) for k in range(TOP_K)]
            for g in gathers:
                g.wait()
            outs = [pltpu.async_copy(rows_v.at[k], yk_hbm.at[k, pl.ds(ch * c, c)], osem) for k in range(TOP_K)]
            for o in outs:
                o.wait()

    return pl.kernel(
        body, mesh=_sc_mesh(),
        out_type=jax.ShapeDtypeStruct((TOP_K, n_chunks * c, w), U32),
        scratch_types=[pltpu.VMEM((SLOTS, c), I32), pltpu.VMEM((TOP_K, c, w), U32),
                       pltpu.SemaphoreType.DMA, pltpu.SemaphoreType.DMA],
        name="gather_back",
    )(ys, dest3)


def _combine_body(yk_ref, gate_ref, hp_ref, x1_ref, mod_ref, wsg_ref, wsu_ref, wsd_ref, gfin_ref, o_ref):
    ha, hb = _unpack_halves(hp_ref[...])
    ha = ha.astype(BF16)
    hb = hb.astype(BF16)
    half = D_MODEL // 2
    g = _dot(ha, wsg_ref[:half, :]) + _dot(hb, wsg_ref[half:, :])
    u = _dot(ha, wsu_ref[:half, :]) + _dot(hb, wsu_ref[half:, :])
    shared = _dot((g * jax.nn.sigmoid(g) * u).astype(BF16), wsd_ref[...])

    gate = gate_ref[...]
    ra = jnp.zeros(ha.shape, F32)
    rb = jnp.zeros(ha.shape, F32)
    for k in range(TOP_K):
        ya, yb = _unpack_halves(yk_ref[k])
        gk = gate[:, k:k + 1]
        ra = ra + gk * ya
        rb = rb + gk * yb
    moe = shared + jnp.concatenate([ra, rb], axis=1)
    mod = mod_ref[0]
    x2 = x1_ref[...] + mod[5:6, :] * moe
    o_ref[...] = _rms(x2) * gfin_ref[...]


def _combine(yk, gate_t, hpk, x1, mod8, w, g_final, seq, tq):
    t, wd = hpk.shape
    tps = seq // tq
    return pl.pallas_call(
        _combine_body,
        out_shape=jax.ShapeDtypeStruct((t, D_MODEL), F32),
        grid=(t // tq,),
        in_specs=[pl.BlockSpec((TOP_K, tq, wd), lambda i: (0, i, 0)),
                  pl.BlockSpec((tq, SLOTS), lambda i: (i, 0)),
                  pl.BlockSpec((tq, wd), lambda i: (i, 0)),
                  pl.BlockSpec((tq, D_MODEL), lambda i: (i, 0)),
                  pl.BlockSpec((1, 8, D_MODEL), lambda i: (i // tps, 0, 0)),
                  _const_spec(w["wsg"].shape), _const_spec(w["wsu"].shape), _const_spec(w["wsd"].shape),
                  _const_spec((1, D_MODEL))],
        out_specs=pl.BlockSpec((tq, D_MODEL), lambda i: (i, 0)),
        compiler_params=_cparams(("parallel",)),
        name="combine",
    )(yk, gate_t, hpk, x1, mod8, w["wsg"], w["wsu"], w["wsd"], g_final)


def _prep_weights(w_in, b_forget, g_q_lat, w_q_up, g_kv_lat, w_kv_up, w_o_mla, w_o_fox, w_out,
                  w_router, w_sh_gate, w_sh_up, w_sh_down):
    o_q, o_kv, o_kr = 0, MLA_Q_LORA, MLA_Q_LORA + MLA_KV_LORA
    o_fq = o_kr + MLA_ROPE
    o_fk, o_fv = o_fq + FOX_WIDTH, o_fq + 2 * FOX_WIDTH
    o_fl = o_fq + 3 * FOX_WIDTH
    o_ga = o_fl + FOX_HEADS
    o_gb = o_ga + D_MODEL
    w = {}
    w["wlat"] = w_in[:, o_q:o_kr].astype(BF16)
    half = MLA_ROPE // 2

    def rope_pair(cols):
        x1, x2 = cols[..., :half], cols[..., half:]
        z = jnp.zeros(cols.shape[:-1] + (MLA_NOPE,), cols.dtype)
        zt = jnp.zeros(cols.shape[:-1] + (HEAD_PAD - MLA_QK,), cols.dtype)
        plain = jnp.concatenate([z, x1, x2, zt], axis=-1)
        rot = jnp.concatenate([z, -x2, x1, zt], axis=-1)
        return plain, rot

    kr_plain, kr_rot = rope_pair(w_in[:, o_kr:o_fq])
    w["wkr"] = jnp.concatenate([kr_plain, kr_rot], axis=1).astype(BF16)
    w["gq"] = g_q_lat.reshape(1, -1)
    w["gkv"] = g_kv_lat.reshape(1, -1)

    scale_a = LOG2E / math.sqrt(MLA_QK)
    wq = (w_q_up * scale_a).reshape(MLA_Q_LORA, MLA_HEADS, MLA_QK)
    q_plain, q_rot = rope_pair(wq[..., MLA_NOPE:])
    nope = jnp.concatenate([wq[..., :MLA_NOPE], jnp.zeros((MLA_Q_LORA, MLA_HEADS, HEAD_PAD - MLA_NOPE), F32)], -1)
    w["wqa"] = (nope + q_plain).reshape(MLA_Q_LORA, -1).astype(BF16)
    w["wqb"] = q_rot.reshape(MLA_Q_LORA, -1).astype(BF16)

    wkv = w_kv_up.reshape(MLA_KV_LORA, MLA_HEADS, MLA_NOPE + MLA_V)
    zpad = jnp.zeros((MLA_KV_LORA, MLA_HEADS, HEAD_PAD - MLA_NOPE), F32)
    w["wka"] = jnp.concatenate([wkv[..., :MLA_NOPE], zpad], -1).reshape(MLA_KV_LORA, -1).astype(BF16)

    w["wva"] = wkv[..., MLA_NOPE:].reshape(MLA_KV_LORA, -1).astype(BF16)

    w["wfq"] = (w_in[:, o_fq:o_fk] * (LOG2E / math.sqrt(FOX_HEAD_DIM))).astype(BF16)
    w["wfk"] = w_in[:, o_fk:o_fv].astype(BF16)
    w["wfv"] = w_in[:, o_fv:o_fl].astype(BF16)

    wfl = jnp.concatenate([w_in[:, o_fl:o_ga]] * 3 + [jnp.zeros((D_MODEL, LANES - 3 * FOX_HEADS), F32)], 1)
    wfl_hi = wfl.astype(BF16)
    w["wfl2"] = jnp.concatenate([wfl_hi, (wfl - wfl_hi.astype(F32)).astype(BF16)], axis=1)
    w["bfl"] = jnp.concatenate([b_forget] * 3 + [jnp.zeros((LANES - 3 * FOX_HEADS,), F32)]).reshape(1, LANES)

    pall = np.zeros((LANES, 2 * FOX_WIDTH), np.float32)
    ones_row = 3 * FOX_HEADS
    for hd in range(FOX_HEADS):
        gq = (hd ^ 1) * FOX_HEAD_DIM
        gk = FOX_WIDTH + (hd ^ 1) * FOX_HEAD_DIM
        for term in range(3):
            pall[term * FOX_HEADS + hd, gq + FQ_COL + term] = 1.0
            pall[ones_row, gq + FK_COL + term] = 1.0
            pall[ones_row, gk + FQ_COL + term] = 1.0
            pall[term * FOX_HEADS + hd, gk + FK_COL + term] = -1.0
    w["pall"] = jnp.asarray(pall, BF16)

    w["wg"] = w_in[:, o_ga:o_gb + D_MODEL].astype(BF16)
    w["womla"] = w_o_mla.astype(BF16)
    w["wofox"] = w_o_fox.astype(BF16)
    w["wout"] = w_out.astype(BF16)
    wr = jnp.concatenate([w_router, jnp.zeros((D_MODEL, LANES - N_EXPERTS), F32)], 1)
    wr_hi = wr.astype(BF16)
    w["wr2"] = jnp.concatenate([wr_hi, (wr - wr_hi.astype(F32)).astype(BF16)], axis=1)
    w["wsg"] = w_sh_gate.astype(BF16)
    w["wsu"] = w_sh_up.astype(BF16)
    w["wsd"] = w_sh_down.astype(BF16)
    return w


def _rope_freq_row():
    half = MLA_ROPE // 2
    inv = np.power(ROPE_THETA, -np.arange(half, dtype=np.float32) / half).astype(np.float32)
    row = np.zeros((1, LANES), np.float32)
    row[0, MLA_NOPE:MLA_NOPE + half] = inv
    row[0, MLA_NOPE + half:MLA_NOPE + 2 * half] = inv
    return jnp.asarray(row)


def kernel(x, c, positions, w_mod, b_mod, g_mix_norm, w_in, b_forget, g_q_lat, w_q_up, g_kv_lat, w_kv_up,
           w_o_mla, w_o_fox, w_out, g_ffn_norm, w_router, b_router, w_exp_gate, w_exp_up, w_exp_down,
           w_sh_gate, w_sh_up, w_sh_down, g_final):
    batch, seq, d = x.shape
    assert d == D_MODEL and w_mod.shape[0] == 1
    t = batch * seq
    tm = min(512, seq)
    bq = min(2048, seq)
    tr = min(512, t)
    to = min(256, seq)
    tc = min(512, seq)
    assert seq % tm == 0 and seq % bq == 0 and t % tr == 0 and seq % tc == 0 and seq % to == 0 and batch <= 8
    assert t % (SC_CHUNK * SC_WORKERS) == 0

    def layer0(a):
        return a.reshape(a.shape[1:])

    w = _prep_weights(*(layer0(a) for a in (w_in, b_forget, g_q_lat, w_q_up, g_kv_lat, w_kv_up, w_o_mla, w_o_fox,
                                            w_out, w_router, w_sh_gate, w_sh_up, w_sh_down)))

    c8 = jnp.zeros((8, D_MODEL), F32).at[:batch].set(c)
    mod = _mod(c8, layer0(w_mod), b_mod)
    mod8 = jnp.zeros((batch, 8, D_MODEL), F32).at[:, :N_MOD].set(mod[:batch].reshape(batch, N_MOD, D_MODEL))

    x2 = x.reshape(t, D_MODEL)
    fdec = _fox_decay(x2, mod8, g_mix_norm, w["wfl2"], w["bfl"], seq, tm)
    q_all, k_all, v_all, sg = _in_proj(x2, mod8, g_mix_norm, fdec, positions.reshape(t, 1),
                                       _rope_freq_row(), w, batch, seq, tm)
    o = _attention(q_all, k_all, v_all, bq)
    x1, hpk, logits_t = _out_proj(o.reshape(t, D_MODEL), sg, x2, mod8, g_ffn_norm, w, seq, to)

    eidx, rank, gate, counts = _route(logits_t, b_router.reshape(N_EXPERTS, 1), tr)

    cnt = counts[:, 0]
    padded = (cnt + ROW_BLOCK - 1) // ROW_BLOCK * ROW_BLOCK
    pend = jnp.cumsum(padded)
    pstart = pend - padded
    n_blocks = t * TOP_K // ROW_BLOCK + N_EXPERTS
    n_rows = n_blocks * ROW_BLOCK
    block_row = jnp.arange(n_blocks, dtype=I32) * ROW_BLOCK
    block_e = jnp.minimum(jnp.sum(pend[None, :] <= block_row[:, None], axis=1), N_EXPERTS - 1).astype(I32)
    n_used = (pend[-1:] // ROW_BLOCK).astype(I32)
    block_valid = jnp.clip((pstart + cnt)[block_e] - block_row, 0, ROW_BLOCK).astype(I32)
    block_first = jnp.concatenate([jnp.ones((1,), I32), (block_e[1:] != block_e[:-1]).astype(I32)])

    dest = _dest(pstart.astype(I32), eidx, rank, min(2048, t))
    dest3 = dest.reshape(SLOTS, t // SC_CHUNK, SC_CHUNK).transpose(1, 0, 2)
    xs = _dispatch(hpk, dest3, n_rows)
    ys = _experts(block_e, block_valid, block_first, n_used, xs,
                  layer0(w_exp_gate), layer0(w_exp_up), layer0(w_exp_down))
    yk = _gather_back(ys, dest3)
    out = _combine(yk, gate.T, hpk, x1, mod8, w, g_final.reshape(1, D_MODEL), seq, tc)
    return out.reshape(batch, seq, D_MODEL)
```

```python
import functools
import math

import numpy as np
import jax
import jax.numpy as jnp
from jax import lax
from jax.experimental import pallas as pl
from jax.experimental.pallas import tpu as pltpu
from jax.experimental.pallas import tpu_sc as plsc

F32 = jnp.float32
BF16 = jnp.bfloat16
I32 = jnp.int32
U32 = jnp.uint32

D_MODEL = 1024
MLA_HEADS = 8
MLA_Q_LORA = 256
MLA_KV_LORA = 128
MLA_NOPE = 64
MLA_ROPE = 32
MLA_V = 64
MLA_QK = MLA_NOPE + MLA_ROPE
ROPE_THETA = 10000.0
FOX_HEADS = 8
FOX_HEAD_DIM = 64
FOX_WIDTH = FOX_HEADS * FOX_HEAD_DIM
N_HEADS = MLA_HEADS + FOX_HEADS
N_EXPERTS = 64
N_GROUPS = 8
GROUP_SIZE = N_EXPERTS // N_GROUPS
TOPK_GROUPS = 4
TOP_K = 6
D_EXPERT = 256
ROUTED_SCALE = 2.5
N_MOD = 6
NORM_EPS = 1e-6
NEG_INF = -1e30
LOG2E = math.log2(math.e)

LANES = 128
HEAD_PAD = 128
V_ROWS = MLA_V + 16
ROW_BLOCK = 512
SLOTS = 8
SC_CORES = 2
SC_SUBCORES = 16
SC_WORKERS = SC_CORES * SC_SUBCORES
SC_CHUNK = 32
VMEM_LIMIT = 56 * 1024 * 1024

FQ_COL = 0
FK_COL = 3


def _cparams(sem, vmem=VMEM_LIMIT):
    return pltpu.CompilerParams(dimension_semantics=sem, vmem_limit_bytes=vmem)


def _const_spec(shape):
    nd = len(shape)
    return pl.BlockSpec(shape, lambda *_: (0,) * nd)


def _rms(x):
    return x * lax.rsqrt(jnp.mean(x * x, axis=-1, keepdims=True) + NORM_EPS)


def _split3(x):
    hi = x.astype(BF16)
    r = x - hi.astype(F32)
    mid = r.astype(BF16)
    lo = (r - mid.astype(F32)).astype(BF16)
    return hi, mid, lo


def _dot(a, b):
    return jnp.dot(a, b, preferred_element_type=F32)


def _modulated_norm(x, gain, mod, shift_row, scale_row):
    shift = mod[shift_row:shift_row + 1, :]
    scale = mod[scale_row:scale_row + 1, :]
    return _rms(x) * gain * (1.0 + scale) + shift


def _mod_body(c_ref, w_ref, b_ref, o_ref):
    c = c_ref[...]
    cond = c * jax.nn.sigmoid(c)
    o_ref[...] = _dot(cond.astype(BF16), w_ref[...].astype(BF16)) + b_ref[...]


def _mod(c8, w_mod, b_mod):
    n = w_mod.shape[1]
    tn = D_MODEL
    return pl.pallas_call(
        _mod_body,
        out_shape=jax.ShapeDtypeStruct((8, n), F32),
        grid=(n // tn,),
        in_specs=[_const_spec((8, D_MODEL)),
                  pl.BlockSpec((D_MODEL, tn), lambda j: (0, j)),
                  pl.BlockSpec((1, tn), lambda j: (0, j))],
        out_specs=pl.BlockSpec((8, tn), lambda j: (0, j)),
        compiler_params=_cparams(("parallel",)),
        name="mod",
    )(c8, w_mod, b_mod)


def _decay_body(tiles_per_seq, x_ref, mod_ref, g_ref, w2_ref, b_ref, tri_ref, o_ref, carry_ref):
    i = pl.program_id(0)

    @pl.when(i % tiles_per_seq == 0)
    def _():
        carry_ref[...] = jnp.zeros_like(carry_ref)

    h = _modulated_norm(x_ref[...], g_ref[...], mod_ref[0], 0, 1)
    hhi = h.astype(BF16)
    hlo = (h - hhi.astype(F32)).astype(BF16)
    z2 = _dot(hhi, w2_ref[...])
    z = z2[:, :LANES] + z2[:, LANES:] + _dot(hlo, w2_ref[:, :LANES]) + b_ref[...]
    logf = jnp.minimum(z, 0.0) - jnp.log1p(jnp.exp(-jnp.abs(z)))
    tm = logf.shape[0]
    c3 = _dot(tri_ref[...], jnp.concatenate(_split3(logf), axis=1))
    cum = c3[:, :LANES] + c3[:, LANES:2 * LANES] + c3[:, 2 * LANES:] + carry_ref[...]
    o_ref[...] = cum
    carry_ref[...] = cum[tm - 1:tm, :]


def _fox_decay(x2, mod8, g_mix, wfl2, bfl, seq, tm):
    t = x2.shape[0]
    tps = seq // tm
    tri = jnp.asarray(np.tril(np.ones((tm, tm), np.float32)), BF16)
    return pl.pallas_call(
        functools.partial(_decay_body, tps),
        out_shape=jax.ShapeDtypeStruct((t, LANES), F32),
        grid=(t // tm,),
        in_specs=[pl.BlockSpec((tm, D_MODEL), lambda i: (i, 0)),
                  pl.BlockSpec((1, 8, D_MODEL), lambda i: (i // tps, 0, 0)),
                  _const_spec((1, D_MODEL)),
                  _const_spec((D_MODEL, 2 * LANES)),
                  _const_spec((1, LANES)),
                  _const_spec((tm, tm))],
        out_specs=pl.BlockSpec((tm, LANES), lambda i: (i, 0)),
        scratch_shapes=[pltpu.VMEM((1, LANES), F32)],
        compiler_params=_cparams(("arbitrary",)),
        name="fox_decay",
    )(x2, mod8, g_mix, wfl2, bfl, tri)


def _in_proj_body(x_ref, mod_ref, g_ref, f_ref, pos_ref, freq_ref,
                  wlat_ref, wkr_ref, gq_ref, gkv_ref, wqa_ref, wqb_ref, wka_ref, wva_ref,
                  wfq_ref, wfk_ref, wfv_ref, pall_ref, wg_ref,
                  q_ref, k_ref, v_ref, sg_ref):
    h = _modulated_norm(x_ref[...], g_ref[...], mod_ref[0], 0, 1)
    hb = h.astype(BF16)

    lat = _dot(hb, wlat_ref[...])
    qn = (_rms(lat[:, :MLA_Q_LORA]) * gq_ref[...]).astype(BF16)
    kvn = (_rms(lat[:, MLA_Q_LORA:]) * gkv_ref[...]).astype(BF16)
    ang = pos_ref[...].astype(F32) * freq_ref[...]
    cs = jnp.cos(ang)
    sn = jnp.sin(ang)
    kr = _dot(hb, wkr_ref[...])
    kpe = kr[:, :HEAD_PAD] * cs + kr[:, HEAD_PAD:] * sn
    qa = _dot(qn, wqa_ref[...])
    qb = _dot(qn, wqb_ref[...])
    ka = _dot(kvn, wka_ref[...])
    va = _dot(kvn, wva_ref[...])
    for hd in range(MLA_HEADS):
        sl = slice(hd * HEAD_PAD, (hd + 1) * HEAD_PAD)
        q_ref[0, hd] = (qa[:, sl] * cs + qb[:, sl] * sn).astype(BF16)
        k_ref[0, hd] = (ka[:, sl] + kpe).astype(BF16)
    ones_rows = jnp.ones((V_ROWS - MLA_V, va.shape[0]), BF16)

    def put_values(first_head, vals):
        for hp in range(vals.shape[1] // (2 * MLA_V)):
            vt = vals[:, hp * 2 * MLA_V:(hp + 1) * 2 * MLA_V].T.astype(BF16)
            for j in range(2):
                v_ref[0, first_head + 2 * hp + j, :MLA_V, :] = vt[j * MLA_V:(j + 1) * MLA_V, :]
                v_ref[0, first_head + 2 * hp + j, MLA_V:, :] = ones_rows

    put_values(0, va)

    hi, mid, lo = _split3(f_ref[...] * LOG2E)
    lane = lax.broadcasted_iota(I32, hi.shape, 1)
    f3 = jnp.where(lane < FOX_HEADS, hi.astype(F32), jnp.where(lane < 2 * FOX_HEADS, mid.astype(F32),
                   jnp.where(lane < 3 * FOX_HEADS, lo.astype(F32), jnp.where(lane == 3 * FOX_HEADS, 1.0, 0.0))))
    aug = _dot(f3.astype(BF16), pall_ref[...]).astype(BF16)
    fq = _dot(hb, wfq_ref[...]).astype(BF16)
    fk = _dot(hb, wfk_ref[...]).astype(BF16)
    hdim = FOX_HEAD_DIM
    for hd in range(FOX_HEADS):
        own = slice((hd % 2) * hdim, (hd % 2 + 1) * hdim)
        spare = slice((1 - hd % 2) * hdim, (2 - hd % 2) * hdim)
        src = slice(hd * hdim, (hd + 1) * hdim)
        asrc = (hd ^ 1) * hdim
        q_ref[0, MLA_HEADS + hd, :, own] = fq[:, src]
        k_ref[0, MLA_HEADS + hd, :, own] = fk[:, src]
        q_ref[0, MLA_HEADS + hd, :, spare] = aug[:, asrc:asrc + hdim]
        k_ref[0, MLA_HEADS + hd, :, spare] = aug[:, FOX_WIDTH + asrc:FOX_WIDTH + asrc + hdim]
    put_values(MLA_HEADS, _dot(hb, wfv_ref[...]))

    sg_ref[...] = jax.nn.sigmoid(_dot(hb, wg_ref[...])).astype(BF16)


def _in_proj(x2, mod8, g_mix, fdec, pos, freq, w, batch, seq, tm):
    t = x2.shape[0]
    tps = seq // tm
    consts = [w["wlat"], w["wkr"], w["gq"], w["gkv"], w["wqa"], w["wqb"], w["wka"], w["wva"],
              w["wfq"], w["wfk"], w["wfv"], w["pall"], w["wg"]]
    head_shape = jax.ShapeDtypeStruct((batch, N_HEADS, seq, HEAD_PAD), BF16)
    head_spec = pl.BlockSpec((1, N_HEADS, tm, HEAD_PAD), lambda i: (i // tps, 0, i % tps, 0))
    pair_shape = jax.ShapeDtypeStruct((batch, N_HEADS, V_ROWS, seq), BF16)
    pair_spec = pl.BlockSpec((1, N_HEADS, V_ROWS, tm), lambda i: (i // tps, 0, 0, i % tps))
    return pl.pallas_call(
        _in_proj_body,
        out_shape=(head_shape, head_shape, pair_shape,
                   jax.ShapeDtypeStruct((t, 2 * D_MODEL), BF16)),
        grid=(t // tm,),
        in_specs=[pl.BlockSpec((tm, D_MODEL), lambda i: (i, 0)),
                  pl.BlockSpec((1, 8, D_MODEL), lambda i: (i // tps, 0, 0)),
                  _const_spec((1, D_MODEL)),
                  pl.BlockSpec((tm, LANES), lambda i: (i, 0)),
                  pl.BlockSpec((tm, 1), lambda i: (i, 0)),
                  _const_spec((1, LANES))] + [_const_spec(a.shape) for a in consts],
        out_specs=(head_spec, head_spec, pair_spec,
                   pl.BlockSpec((tm, 2 * D_MODEL), lambda i: (i, 0))),
        compiler_params=_cparams(("parallel",)),
        name="in_proj",
    )(x2, mod8, g_mix, fdec, pos, freq, *consts)


def _attn_body(bq, sq, q_ref, k_ref, v_ref, o_ref, s_scr, p_scr, acc_scr):
    qi = pl.program_id(2)
    bk = sq
    n_heads = q_ref.shape[1]
    n_sub = bq // sq
    assert n_sub % 2 == 0
    chains = [(hh, u) for hh in range(n_heads) for u in range(n_sub)]
    n_chains = len(chains)
    n_main = qi * n_sub

    def chunk_start(j):
        return pl.multiple_of(jnp.maximum(j, 0) * bk, bk)

    def scores(c, j, par):
        hh, u = chains[c]
        k = k_ref[0, hh, pl.ds(chunk_start(j), bk), :]
        q = q_ref[0, hh, u * sq:(u + 1) * sq, :]
        s = lax.dot_general(k, q, (((1,), (1,)), ((), ())), preferred_element_type=F32)
        s_scr[par, c] = s
        return jnp.max(s, axis=0, keepdims=True)

    def values(c, j, par, alpha):
        vt = v_ref[0, chains[c][0], :, pl.ds(chunk_start(j), bk)]
        acc_scr[c] = alpha * acc_scr[c] + _dot(vt, p_scr[par, c])

    def softmax(c, par, m, smax, masked):
        s = s_scr[par, c]
        if masked:
            key = lax.broadcasted_iota(I32, (bk, sq), 0)
            qry = lax.broadcasted_iota(I32, (bk, sq), 1)
            s = jnp.where(key <= qry, s, NEG_INF)
            smax = jnp.max(s, axis=0, keepdims=True)
        m_new = jnp.maximum(m, smax)
        p_scr[par, c] = jnp.exp2((s - m_new).astype(BF16))
        return m_new, jnp.exp2(m - m_new)

    def stage(j, par, state, active, has_next, masked_of):
        nxt = {c: scores(c, j + 1, 1 - par) if has_next(c) else state[c][1] for c in active}
        for c in active:
            values(c, j - 1, 1 - par, state[c][2])
        out = list(state)
        for c in active:
            m, smax, _ = state[c]
            m_new, alpha = softmax(c, par, m, smax, masked_of(c))
            out[c] = (m_new, nxt[c], alpha)
        return out

    p_scr[1] = jnp.zeros(p_scr.shape[1:], BF16)
    acc_scr[...] = jnp.zeros_like(acc_scr)
    state = [(jnp.full((1, sq), NEG_INF, F32), scores(c, 0, 0), jnp.ones((1, sq), F32)) for c in range(n_chains)]

    def step(jj, st):
        for par in range(2):
            st = stage(2 * jj + par, par, list(st), range(n_chains), lambda c: True, lambda c: False)
        return tuple(st)

    state = list(lax.fori_loop(0, n_main // 2, step, tuple(state)))

    for t in range(n_sub):
        active = [c for c, (_, u) in enumerate(chains) if u >= t]
        state = stage(n_main + t, t % 2, state, active, lambda c, t=t: chains[c][1] > t,
                      lambda c, t=t: chains[c][1] == t)

    heads = []
    for hh in range(n_heads):
        parts = []
        for u in range(n_sub):
            c = hh * n_sub + u
            values(c, n_main + u, u % 2, state[c][2])
            parts.append(acc_scr[c, :MLA_V, :] / acc_scr[c, MLA_V:MLA_V + 1, :])
        heads.append(jnp.concatenate(parts, axis=1))
    o_ref[0] = jnp.concatenate(heads, axis=0).T.astype(BF16)


def _attention(q_all, k_all, v_all, bq):
    batch, _, seq, _ = q_all.shape
    sq = min(256, bq)
    n_chains = 2 * (bq // sq)
    return pl.pallas_call(
        functools.partial(_attn_body, bq, sq),
        out_shape=jax.ShapeDtypeStruct((batch, seq, N_HEADS * MLA_V), BF16),
        grid=(batch, N_HEADS // 2, seq // bq),
        in_specs=[pl.BlockSpec((1, 2, bq, HEAD_PAD), lambda b, hp, qi: (b, hp, qi, 0)),
                  pl.BlockSpec((1, 2, seq, HEAD_PAD), lambda b, hp, qi: (b, hp, 0, 0)),
                  pl.BlockSpec((1, 2, V_ROWS, seq), lambda b, hp, qi: (b, hp, 0, 0))],
        out_specs=pl.BlockSpec((1, bq, 2 * MLA_V), lambda b, hp, qi: (b, qi, hp)),
        scratch_shapes=[pltpu.VMEM((2, n_chains, sq, sq), F32),
                        pltpu.VMEM((2, n_chains, sq, sq), BF16),
                        pltpu.VMEM((n_chains, V_ROWS, sq), F32)],
        compiler_params=_cparams(("parallel", "parallel", "arbitrary")),
        name="attention",
    )(q_all, k_all, v_all)


def _pack_halves(x):
    w = x.shape[1] // 2
    a = lax.bitcast_convert_type(x[:, :w].astype(BF16).astype(F32), U32)
    b = lax.bitcast_convert_type(x[:, w:].astype(BF16).astype(F32), U32)
    return a | (b >> 16)


def _unpack_halves(p):
    a = lax.bitcast_convert_type(p & jnp.uint32(0xFFFF0000), F32)
    b = lax.bitcast_convert_type(p << 16, F32)
    return a, b


def _out_proj_body(o_ref, sg_ref, x_ref, mod_ref, gffn_ref, womla_ref, wofox_ref, wout_ref,
                   wr2_ref, x1_ref, hp_ref, lg_ref):
    o = o_ref[...]
    half = o.shape[1] // 2
    mo = _dot(o[:, :half], womla_ref[...])
    fo = _dot(o[:, half:], wofox_ref[...])
    sg = sg_ref[...]
    merged = sg[:, :D_MODEL].astype(F32) * mo + sg[:, D_MODEL:].astype(F32) * fo
    mix = _dot(merged.astype(BF16), wout_ref[...])
    mod = mod_ref[0]
    x1 = x_ref[...] + mod[2:3, :] * mix
    x1_ref[...] = x1
    h2 = _modulated_norm(x1, gffn_ref[...], mod, 3, 4)
    hhi = h2.astype(BF16)
    hlo = (h2 - hhi.astype(F32)).astype(BF16)
    l2 = _dot(hhi, wr2_ref[...])
    lg_ref[...] = (l2[:, :LANES] + l2[:, LANES:] + _dot(hlo, wr2_ref[:, :LANES])).T
    hp_ref[...] = _pack_halves(h2)


def _out_proj(o2, sg, x2, mod8, g_ffn, w, seq, tm):
    t = x2.shape[0]
    tps = seq // tm
    consts = [w["womla"], w["wofox"], w["wout"], w["wr2"]]
    return pl.pallas_call(
        _out_proj_body,
        out_shape=(jax.ShapeDtypeStruct((t, D_MODEL), F32),
                   jax.ShapeDtypeStruct((t, D_MODEL // 2), U32),
                   jax.ShapeDtypeStruct((LANES, t), F32)),
        grid=(t // tm,),
        in_specs=[pl.BlockSpec((tm, D_MODEL), lambda i: (i, 0)),
                  pl.BlockSpec((tm, 2 * D_MODEL), lambda i: (i, 0)),
                  pl.BlockSpec((tm, D_MODEL), lambda i: (i, 0)),
                  pl.BlockSpec((1, 8, D_MODEL), lambda i: (i // tps, 0, 0)),
                  _const_spec((1, D_MODEL))] + [_const_spec(a.shape) for a in consts],
        out_specs=(pl.BlockSpec((tm, D_MODEL), lambda i: (i, 0)),
                   pl.BlockSpec((tm, D_MODEL // 2), lambda i: (i, 0)),
                   pl.BlockSpec((LANES, tm), lambda i: (0, i))),
        compiler_params=_cparams(("parallel",)),
        name="out_proj",
    )(o2, sg, x2, mod8, g_ffn, *consts)


def _route_body(lt_ref, b_ref, eidx_ref, rank_ref, gate_ref, cnt_ref, carry_ref):
    i = pl.program_id(0)

    @pl.when(i == 0)
    def _():
        carry_ref[...] = jnp.zeros_like(carry_ref)

    s = jax.nn.sigmoid(lt_ref[...])
    c = s + b_ref[...]
    tr = s.shape[1]
    sub = lax.broadcasted_iota(I32, (GROUP_SIZE, tr), 0).astype(F32)

    gs = []
    for g in range(N_GROUPS):
        cg = c[g * GROUP_SIZE:(g + 1) * GROUP_SIZE, :]
        m1 = jnp.max(cg, axis=0, keepdims=True)
        i1 = jnp.min(jnp.where(cg == m1, sub, float(GROUP_SIZE)), axis=0, keepdims=True)
        m2 = jnp.max(jnp.where(sub == i1, NEG_INF, cg), axis=0, keepdims=True)
        gs.append(m1 + m2)

    masked = []
    for g in range(N_GROUPS):
        beats = jnp.zeros_like(gs[g])
        for o in range(N_GROUPS):
            if o == g:
                continue
            better = (gs[o] >= gs[g]) if o < g else (gs[o] > gs[g])
            beats = beats + jnp.where(better, 1.0, 0.0)
        keep = beats < float(TOPK_GROUPS)
        cg = c[g * GROUP_SIZE:(g + 1) * GROUP_SIZE, :]
        masked.append(jnp.where(keep, cg, NEG_INF))
    mc = jnp.concatenate(masked, axis=0)

    eio = lax.broadcasted_iota(I32, (N_EXPERTS, tr), 0).astype(F32)
    picks = []
    selected = jnp.zeros((N_EXPERTS, tr), F32)
    for _ in range(TOP_K):
        m = jnp.max(mc, axis=0, keepdims=True)
        idx = jnp.min(jnp.where(mc == m, eio, float(N_EXPERTS)), axis=0, keepdims=True)
        hit = eio == idx
        picks.append(idx)
        selected = jnp.where(hit, 1.0, selected)
        mc = jnp.where(hit, -3.0e38, mc)

    ssum = jnp.sum(selected * s, axis=0, keepdims=True)
    gate_full = selected * s / ssum * ROUTED_SCALE

    r_io = lax.broadcasted_iota(I32, (tr, tr), 0)
    c_io = lax.broadcasted_iota(I32, (tr, tr), 1)
    upper = jnp.where(r_io < c_io, 1.0, 0.0).astype(BF16)
    before = _dot(selected.astype(BF16), upper) + carry_ref[...]
    carry_new = carry_ref[...] + jnp.sum(selected, axis=1, keepdims=True)
    carry_ref[...] = carry_new
    cnt_ref[...] = jnp.broadcast_to(carry_new, cnt_ref.shape).astype(I32)

    for r in range(SLOTS):
        if r < TOP_K:
            hit = eio == picks[r]
            eidx_ref[r:r + 1, :] = picks[r].astype(I32)
            rank_ref[r:r + 1, :] = jnp.sum(jnp.where(hit, before, 0.0), axis=0, keepdims=True).astype(I32)
            gate_ref[r:r + 1, :] = jnp.sum(jnp.where(hit, gate_full, 0.0), axis=0, keepdims=True)
        else:
            eidx_ref[r:r + 1, :] = jnp.zeros((1, tr), I32)
            rank_ref[r:r + 1, :] = jnp.zeros((1, tr), I32)
            gate_ref[r:r + 1, :] = jnp.zeros((1, tr), F32)


def _route(logits_t, bias_col, tr):
    t = logits_t.shape[1]
    slot_i = jax.ShapeDtypeStruct((SLOTS, t), I32)
    slot_spec = pl.BlockSpec((SLOTS, tr), lambda i: (0, i))
    return pl.pallas_call(
        _route_body,
        out_shape=(slot_i, slot_i, jax.ShapeDtypeStruct((SLOTS, t), F32),
                   jax.ShapeDtypeStruct((N_EXPERTS, LANES), I32)),
        grid=(t // tr,),
        in_specs=[pl.BlockSpec((N_EXPERTS, tr), lambda i: (0, i)),
                  _const_spec((N_EXPERTS, 1))],
        out_specs=(slot_spec, slot_spec, slot_spec, _const_spec((N_EXPERTS, LANES))),
        scratch_shapes=[pltpu.VMEM((N_EXPERTS, 1), F32)],
        compiler_params=_cparams(("arbitrary",)),
        name="route",
    )(logits_t, bias_col)


def _dest_body(pstart_ref, eidx_ref, rank_ref, o_ref):
    e = eidx_ref[...]
    d = rank_ref[...]
    for j in range(N_EXPERTS):
        d = d + jnp.where(e == j, pstart_ref[j], 0)
    o_ref[...] = d


def _dest(pstart, eidx, rank, tr):
    t = eidx.shape[1]
    spec = pl.BlockSpec((SLOTS, tr), lambda i, ps: (0, i))
    return pl.pallas_call(
        _dest_body,
        out_shape=jax.ShapeDtypeStruct((SLOTS, t), I32),
        grid_spec=pltpu.PrefetchScalarGridSpec(
            num_scalar_prefetch=1, grid=(t // tr,), in_specs=[spec, spec], out_specs=spec),
        compiler_params=_cparams(("parallel",)),
        name="dest",
    )(pstart, eidx, rank)


def _sc_mesh():
    return plsc.VectorSubcoreMesh(core_axis_name="c", subcore_axis_name="s")


def _sc_worker_id():
    return lax.axis_index("s") * SC_CORES + lax.axis_index("c")


def _dispatch(hpk, dest3, n_rows):
    _, w = hpk.shape
    n_chunks, _, c = dest3.shape
    per_worker = n_chunks // SC_WORKERS

    def body(h_hbm, d_hbm, xs_hbm, idx_v, rows_v, sem):
        wid = _sc_worker_id()

        @pl.loop(0, per_worker)
        def _(i):
            ch = wid * per_worker + i
            pltpu.sync_copy(d_hbm.at[ch], idx_v)
            pltpu.sync_copy(h_hbm.at[pl.ds(ch * c, c)], rows_v)
            copies = [pltpu.async_copy(rows_v, xs_hbm.at[idx_v.at[k]], sem) for k in range(TOP_K)]
            for cp in copies:
                cp.wait()

    return pl.kernel(
        body, mesh=_sc_mesh(),
        out_type=jax.ShapeDtypeStruct((n_rows, w), U32),
        scratch_types=[pltpu.VMEM((SLOTS, c), I32), pltpu.VMEM((c, w), U32), pltpu.SemaphoreType.DMA],
        name="dispatch",
    )(hpk, dest3)


def _experts_body(n_sub, first_ref, nblk_ref, cnt_ref, nu_ref, xs_hbm, wg_ref, wu_ref, wd_ref, ys_hbm,
                  wgb_ref, wub_ref, wdb_ref, xbuf, ybuf, xsem, ysem):
    e = pl.program_id(0)
    nb = nblk_ref[e]
    b0 = first_ref[e]
    cnt = cnt_ref[e]
    n_used = nu_ref[0]
    w = xbuf.shape[2]

    def x_copy(g, slot):
        start = pl.multiple_of(g * ROW_BLOCK, ROW_BLOCK)
        return pltpu.make_async_copy(xs_hbm.at[pl.ds(start, ROW_BLOCK), :], xbuf.at[slot], xsem.at[slot])

    def y_copy(g, slot):
        start = pl.multiple_of(g * ROW_BLOCK, ROW_BLOCK)
        return pltpu.make_async_copy(ybuf.at[slot], ys_hbm.at[pl.ds(start, ROW_BLOCK), :], ysem.at[slot])

    @pl.when((e == 0) & (n_used > 0))
    def _():
        x_copy(0, 0).start()

    wgb_ref[...] = wg_ref[0].astype(BF16)
    wub_ref[...] = wu_ref[0].astype(BF16)
    wdb_ref[...] = wd_ref[0].astype(BF16)

    half = D_MODEL // 2
    sub = ROW_BLOCK // n_sub
    row = lax.broadcasted_iota(I32, (sub, w), 0)

    def block(g, j, slot):
        x_copy(g, slot).wait()

        @pl.when(g + 1 < n_used)
        def _():
            x_copy(g + 1, 1 - slot).start()

        @pl.when(g >= 2)
        def _():
            y_copy(g - 2, slot).wait()

        valid = cnt - j * ROW_BLOCK
        gu = []
        for s in range(n_sub):
            x = jnp.where(row + s * sub < valid, xbuf[slot, s * sub:(s + 1) * sub, :], jnp.uint32(0))
            xa, xb = _unpack_halves(x)
            xa = xa.astype(BF16)
            xb = xb.astype(BF16)
            g1 = _dot(xa, wgb_ref[:half, :]) + _dot(xb, wgb_ref[half:, :])
            u1 = _dot(xa, wub_ref[:half, :]) + _dot(xb, wub_ref[half:, :])
            gu.append((g1, u1))
        for s, (g1, u1) in enumerate(gu):
            hb = (g1 * jax.nn.sigmoid(g1) * u1).astype(BF16)
            ybuf[slot, s * sub:(s + 1) * sub, :] = _pack_halves(_dot(hb, wdb_ref[...]))
        y_copy(g, slot).start()

    def body(j, _):
        g = b0 + j
        for slot in range(2):
            @pl.when(g % 2 == slot)
            def _():
                block(g, j, slot)
        return 0

    lax.fori_loop(0, nb, body, 0)

    @pl.when(e == pl.num_programs(0) - 1)
    def _():
        for slot in range(2):
            @pl.when(n_used > slot)
            def _():
                y_copy(0, slot).wait()

        n_blocks = ys_hbm.shape[0] // ROW_BLOCK
        ybuf[0] = jnp.zeros(ybuf.shape[1:], U32)
        lax.fori_loop(n_used, n_blocks, lambda g, c: (y_copy(g, 0).start(), c)[1], 0)
        lax.fori_loop(n_used, n_blocks, lambda g, c: (y_copy(g, 0).wait(), c)[1], 0)


def _experts(first_blk, n_blk, cnt, n_used, xs, wg, wu, wd):
    n_rows, w = xs.shape

    def w_map(e, fb, nb, ct, nu):
        return (e, 0, 0)

    return pl.pallas_call(
        functools.partial(_experts_body, ROW_BLOCK // 256),
        out_shape=jax.ShapeDtypeStruct((n_rows, w), U32),
        grid_spec=pltpu.PrefetchScalarGridSpec(
            num_scalar_prefetch=4, grid=(N_EXPERTS,),
            in_specs=[pl.BlockSpec(memory_space=pl.ANY),
                      pl.BlockSpec((1, D_MODEL, D_EXPERT), w_map),
                      pl.BlockSpec((1, D_MODEL, D_EXPERT), w_map),
                      pl.BlockSpec((1, D_EXPERT, D_MODEL), w_map)],
            out_specs=pl.BlockSpec(memory_space=pl.ANY),
            scratch_shapes=[pltpu.VMEM((D_MODEL, D_EXPERT), BF16), pltpu.VMEM((D_MODEL, D_EXPERT), BF16),
                            pltpu.VMEM((D_EXPERT, D_MODEL), BF16),
                            pltpu.VMEM((2, ROW_BLOCK, w), U32), pltpu.VMEM((2, ROW_BLOCK, w), U32),
                            pltpu.SemaphoreType.DMA((2,)), pltpu.SemaphoreType.DMA((2,))]),
        compiler_params=_cparams(("arbitrary",)),
        name="experts",
    )(first_blk, n_blk, cnt, n_used, xs, wg, wu, wd)


def _gather_back(ys, dest3):
    _, w = ys.shape
    n_chunks, _, c = dest3.shape
    per_worker = n_chunks // SC_WORKERS

    def body(ys_hbm, d_hbm, yk_hbm, idx_v, rows_v, gsem, osem):
        wid = _sc_worker_id()

        @pl.loop(0, per_worker)
        def _(i):
            ch = wid * per_worker + i
            pltpu.sync_copy(d_hbm.at[ch], idx_v)
            gathers = [pltpu.async_copy(ys_hbm.at[idx_v.at[k]], rows_v.at[k], gsem) for k in range(TOP_K)]
            for g in gathers:
                g.wait()
            outs = [pltpu.async_copy(rows_v.at[k], yk_hbm.at[k, pl.ds(ch * c, c)], osem) for k in range(TOP_K)]
            for o in outs:
                o.wait()

    return pl.kernel(
        body, mesh=_sc_mesh(),
        out_type=jax.ShapeDtypeStruct((TOP_K, n_chunks * c, w), U32),
        scratch_types=[pltpu.VMEM((SLOTS, c), I32), pltpu.VMEM((TOP_K, c, w), U32),
                       pltpu.SemaphoreType.DMA, pltpu.SemaphoreType.DMA],
        name="gather_back",
    )(ys, dest3)


def _combine_body(yk_ref, gate_ref, hp_ref, x1_ref, mod_ref, wsg_ref, wsu_ref, wsd_ref, gfin_ref, o_ref):
    ha, hb = _unpack_halves(hp_ref[...])
    ha = ha.astype(BF16)
    hb = hb.astype(BF16)
    half = D_MODEL // 2
    g = _dot(ha, wsg_ref[:half, :]) + _dot(hb, wsg_ref[half:, :])
    u = _dot(ha, wsu_ref[:half, :]) + _dot(hb, wsu_ref[half:, :])
    shared = _dot((g * jax.nn.sigmoid(g) * u).astype(BF16), wsd_ref[...])

    gate = gate_ref[...]
    ra = jnp.zeros(ha.shape, F32)
    rb = jnp.zeros(ha.shape, F32)
    for k in range(TOP_K):
        ya, yb = _unpack_halves(yk_ref[k])
        gk = gate[:, k:k + 1]
        ra = ra + gk * ya
        rb = rb + gk * yb
    moe = shared + jnp.concatenate([ra, rb], axis=1)
    mod = mod_ref[0]
    x2 = x1_ref[...] + mod[5:6, :] * moe
    o_ref[...] = _rms(x2) * gfin_ref[...]


def _combine(yk, gate_t, hpk, x1, mod8, w, g_final, seq, tq):
    t, wd = hpk.shape
    tps = seq // tq
    return pl.pallas_call(
        _combine_body,
        out_shape=jax.ShapeDtypeStruct((t, D_MODEL), F32),
        grid=(t // tq,),
        in_specs=[pl.BlockSpec((TOP_K, tq, wd), lambda i: (0, i, 0)),
                  pl.BlockSpec((tq, SLOTS), lambda i: (i, 0)),
                  pl.BlockSpec((tq, wd), lambda i: (i, 0)),
                  pl.BlockSpec((tq, D_MODEL), lambda i: (i, 0)),
                  pl.BlockSpec((1, 8, D_MODEL), lambda i: (i // tps, 0, 0)),
                  _const_spec(w["wsg"].shape), _const_spec(w["wsu"].shape), _const_spec(w["wsd"].shape),
                  _const_spec((1, D_MODEL))],
        out_specs=pl.BlockSpec((tq, D_MODEL), lambda i: (i, 0)),
        compiler_params=_cparams(("parallel",)),
        name="combine",
    )(yk, gate_t, hpk, x1, mod8, w["wsg"], w["wsu"], w["wsd"], g_final)


def _prep_weights(w_in, b_forget, g_q_lat, w_q_up, g_kv_lat, w_kv_up, w_o_mla, w_o_fox, w_out,
                  w_router, w_sh_gate, w_sh_up, w_sh_down):
    o_q, o_kv, o_kr = 0, MLA_Q_LORA, MLA_Q_LORA + MLA_KV_LORA
    o_fq = o_kr + MLA_ROPE
    o_fk, o_fv = o_fq + FOX_WIDTH, o_fq + 2 * FOX_WIDTH
    o_fl = o_fq + 3 * FOX_WIDTH
    o_ga = o_fl + FOX_HEADS
    o_gb = o_ga + D_MODEL
    w = {}
    w["wlat"] = w_in[:, o_q:o_kr].astype(BF16)
    half = MLA_ROPE // 2

    def rope_pair(cols):
        x1, x2 = cols[..., :half], cols[..., half:]
        z = jnp.zeros(cols.shape[:-1] + (MLA_NOPE,), cols.dtype)
        zt = jnp.zeros(cols.shape[:-1] + (HEAD_PAD - MLA_QK,), cols.dtype)
        plain = jnp.concatenate([z, x1, x2, zt], axis=-1)
        rot = jnp.concatenate([z, -x2, x1, zt], axis=-1)
        return plain, rot

    kr_plain, kr_rot = rope_pair(w_in[:, o_kr:o_fq])
    w["wkr"] = jnp.concatenate([kr_plain, kr_rot], axis=1).astype(BF16)
    w["gq"] = g_q_lat.reshape(1, -1)
    w["gkv"] = g_kv_lat.reshape(1, -1)

    scale_a = LOG2E / math.sqrt(MLA_QK)
    wq = (w_q_up * scale_a).reshape(MLA_Q_LORA, MLA_HEADS, MLA_QK)
    q_plain, q_rot = rope_pair(wq[..., MLA_NOPE:])
    nope = jnp.concatenate([wq[..., :MLA_NOPE], jnp.zeros((MLA_Q_LORA, MLA_HEADS, HEAD_PAD - MLA_NOPE), F32)], -1)
    w["wqa"] = (nope + q_plain).reshape(MLA_Q_LORA, -1).astype(BF16)
    w["wqb"] = q_rot.reshape(MLA_Q_LORA, -1).astype(BF16)

    wkv = w_kv_up.reshape(MLA_KV_LORA, MLA_HEADS, MLA_NOPE + MLA_V)
    zpad = jnp.zeros((MLA_KV_LORA, MLA_HEADS, HEAD_PAD - MLA_NOPE), F32)
    w["wka"] = jnp.concatenate([wkv[..., :MLA_NOPE], zpad], -1).reshape(MLA_KV_LORA, -1).astype(BF16)

    w["wva"] = wkv[..., MLA_NOPE:].reshape(MLA_KV_LORA, -1).astype(BF16)

    w["wfq"] = (w_in[:, o_fq:o_fk] * (LOG2E / math.sqrt(FOX_HEAD_DIM))).astype(BF16)
    w["wfk"] = w_in[:, o_fk:o_fv].astype(BF16)
    w["wfv"] = w_in[:, o_fv:o_fl].astype(BF16)

    wfl = jnp.concatenate([w_in[:, o_fl:o_ga]] * 3 + [jnp.zeros((D_MODEL, LANES - 3 * FOX_HEADS), F32)], 1)
    wfl_hi = wfl.astype(BF16)
    w["wfl2"] = jnp.concatenate([wfl_hi, (wfl - wfl_hi.astype(F32)).astype(BF16)], axis=1)
    w["bfl"] = jnp.concatenate([b_forget] * 3 + [jnp.zeros((LANES - 3 * FOX_HEADS,), F32)]).reshape(1, LANES)

    pall = np.zeros((LANES, 2 * FOX_WIDTH), np.float32)
    ones_row = 3 * FOX_HEADS
    for hd in range(FOX_HEADS):
        gq = (hd ^ 1) * FOX_HEAD_DIM
        gk = FOX_WIDTH + (hd ^ 1) * FOX_HEAD_DIM
        for term in range(3):
            pall[term * FOX_HEADS + hd, gq + FQ_COL + term] = 1.0
            pall[ones_row, gq + FK_COL + term] = 1.0
            pall[ones_row, gk + FQ_COL + term] = 1.0
            pall[term * FOX_HEADS + hd, gk + FK_COL + term] = -1.0
    w["pall"] = jnp.asarray(pall, BF16)

    w["wg"] = w_in[:, o_ga:o_gb + D_MODEL].astype(BF16)
    w["womla"] = w_o_mla.astype(BF16)
    w["wofox"] = w_o_fox.astype(BF16)
    w["wout"] = w_out.astype(BF16)
    wr = jnp.concatenate([w_router, jnp.zeros((D_MODEL, LANES - N_EXPERTS), F32)], 1)
    wr_hi = wr.astype(BF16)
    w["wr2"] = jnp.concatenate([wr_hi, (wr - wr_hi.astype(F32)).astype(BF16)], axis=1)
    w["wsg"] = w_sh_gate.astype(BF16)
    w["wsu"] = w_sh_up.astype(BF16)
    w["wsd"] = w_sh_down.astype(BF16)
    return w


def _rope_freq_row():
    half = MLA_ROPE // 2
    inv = np.power(ROPE_THETA, -np.arange(half, dtype=np.float32) / half).astype(np.float32)
    row = np.zeros((1, LANES), np.float32)
    row[0, MLA_NOPE:MLA_NOPE + half] = inv
    row[0, MLA_NOPE + half:MLA_NOPE + 2 * half] = inv
    return jnp.asarray(row)


def kernel(x, c, positions, w_mod, b_mod, g_mix_norm, w_in, b_forget, g_q_lat, w_q_up, g_kv_lat, w_kv_up,
           w_o_mla, w_o_fox, w_out, g_ffn_norm, w_router, b_router, w_exp_gate, w_exp_up, w_exp_down,
           w_sh_gate, w_sh_up, w_sh_down, g_final):
    batch, seq, d = x.shape
    assert d == D_MODEL and w_mod.shape[0] == 1
    t = batch * seq
    tm = min(512, seq)
    bq = min(2048, seq)
    tr = min(512, t)
    to = min(256, seq)
    tc = min(512, seq)
    assert seq % tm == 0 and seq % bq == 0 and t % tr == 0 and seq % tc == 0 and seq % to == 0 and batch <= 8
    assert t % (SC_CHUNK * SC_WORKERS) == 0

    def layer0(a):
        return a.reshape(a.shape[1:])

    w = _prep_weights(*(layer0(a) for a in (w_in, b_forget, g_q_lat, w_q_up, g_kv_lat, w_kv_up, w_o_mla, w_o_fox,
                                            w_out, w_router, w_sh_gate, w_sh_up, w_sh_down)))

    c8 = jnp.zeros((8, D_MODEL), F32).at[:batch].set(c)
    mod = _mod(c8, layer0(w_mod), b_mod)
    mod8 = jnp.zeros((batch, 8, D_MODEL), F32).at[:, :N_MOD].set(mod[:batch].reshape(batch, N_MOD, D_MODEL))

    x2 = x.reshape(t, D_MODEL)
    fdec = _fox_decay(x2, mod8, g_mix_norm, w["wfl2"], w["bfl"], seq, tm)
    q_all, k_all, v_all, sg = _in_proj(x2, mod8, g_mix_norm, fdec, positions.reshape(t, 1),
                                       _rope_freq_row(), w, batch, seq, tm)
    o = _attention(q_all, k_all, v_all, bq)
    x1, hpk, logits_t = _out_proj(o.reshape(t, D_MODEL), sg, x2, mod8, g_ffn_norm, w, seq, to)

    eidx, rank, gate, counts = _route(logits_t, b_router.reshape(N_EXPERTS, 1), tr)

    cnt = counts[:, 0]
    padded = (cnt + ROW_BLOCK - 1) // ROW_BLOCK * ROW_BLOCK
    pend = jnp.cumsum(padded)
    pstart = pend - padded
    n_blocks = t * TOP_K // ROW_BLOCK + N_EXPERTS
    n_rows = n_blocks * ROW_BLOCK
    n_used = (pend[-1:] // ROW_BLOCK).astype(I32)

    dest = _dest(pstart.astype(I32), eidx, rank, min(2048, t))
    dest3 = dest.reshape(SLOTS, t // SC_CHUNK, SC_CHUNK).transpose(1, 0, 2)
    xs = _dispatch(hpk, dest3, n_rows)
    ys = _experts((pstart // ROW_BLOCK).astype(I32), (padded // ROW_BLOCK).astype(I32), cnt.astype(I32), n_used, xs,
                  layer0(w_exp_gate), layer0(w_exp_up), layer0(w_exp_down))
    yk = _gather_back(ys, dest3)
    out = _combine(yk, gate.T, hpk, x1, mod8, w, g_final.reshape(1, D_MODEL), seq, tc)
    return out.reshape(batch, seq, D_MODEL)
```

```python
import functools
import math

import numpy as np
import jax
import jax.numpy as jnp
from jax import lax
from jax.experimental import pallas as pl
from jax.experimental.pallas import tpu as pltpu
from jax.experimental.pallas import tpu_sc as plsc

F32 = jnp.float32
BF16 = jnp.bfloat16
I32 = jnp.int32
U32 = jnp.uint32

D_MODEL = 1024
MLA_HEADS = 8
MLA_Q_LORA = 256
MLA_KV_LORA = 128
MLA_NOPE = 64
MLA_ROPE = 32
MLA_V = 64
MLA_QK = MLA_NOPE + MLA_ROPE
ROPE_THETA = 10000.0
FOX_HEADS = 8
FOX_HEAD_DIM = 64
FOX_WIDTH = FOX_HEADS * FOX_HEAD_DIM
N_HEADS = MLA_HEADS + FOX_HEADS
N_EXPERTS = 64
N_GROUPS = 8
GROUP_SIZE = N_EXPERTS // N_GROUPS
TOPK_GROUPS = 4
TOP_K = 6
D_EXPERT = 256
ROUTED_SCALE = 2.5
N_MOD = 6
NORM_EPS = 1e-6
NEG_INF = -1e30
LOG2E = math.log2(math.e)

LANES = 128
HEAD_PAD = 128
V_ROWS = MLA_V + 16
ROW_BLOCK = 512
EXPERT_SUB_BLOCKS = 1
SLOTS = 8
SC_CORES = 2
SC_SUBCORES = 16
SC_WORKERS = SC_CORES * SC_SUBCORES
SC_CHUNK = 32
VMEM_LIMIT = 56 * 1024 * 1024

FQ_COL = 0
FK_COL = 3


def _cparams(sem, vmem=VMEM_LIMIT):
    return pltpu.CompilerParams(dimension_semantics=sem, vmem_limit_bytes=vmem)


def _const_spec(shape):
    nd = len(shape)
    return pl.BlockSpec(shape, lambda *_: (0,) * nd)


def _rms(x):
    return x * lax.rsqrt(jnp.mean(x * x, axis=-1, keepdims=True) + NORM_EPS)


def _split3(x):
    hi = x.astype(BF16)
    r = x - hi.astype(F32)
    mid = r.astype(BF16)
    lo = (r - mid.astype(F32)).astype(BF16)
    return hi, mid, lo


def _dot(a, b):
    return jnp.dot(a, b, preferred_element_type=F32)


def _modulated_norm(x, gain, mod, shift_row, scale_row):
    shift = mod[shift_row:shift_row + 1, :]
    scale = mod[scale_row:scale_row + 1, :]
    return _rms(x) * gain * (1.0 + scale) + shift


def _mod_body(c_ref, w_ref, b_ref, o_ref):
    c = c_ref[...]
    cond = c * jax.nn.sigmoid(c)
    o_ref[...] = _dot(cond.astype(BF16), w_ref[...].astype(BF16)) + b_ref[...]


def _mod(c8, w_mod, b_mod):
    n = w_mod.shape[1]
    tn = D_MODEL
    return pl.pallas_call(
        _mod_body,
        out_shape=jax.ShapeDtypeStruct((8, n), F32),
        grid=(n // tn,),
        in_specs=[_const_spec((8, D_MODEL)),
                  pl.BlockSpec((D_MODEL, tn), lambda j: (0, j)),
                  pl.BlockSpec((1, tn), lambda j: (0, j))],
        out_specs=pl.BlockSpec((8, tn), lambda j: (0, j)),
        compiler_params=_cparams(("parallel",)),
        name="mod",
    )(c8, w_mod, b_mod)


def _decay_body(tiles_per_seq, x_ref, mod_ref, g_ref, w2_ref, b_ref, tri_ref, o_ref, carry_ref):
    i = pl.program_id(0)

    @pl.when(i % tiles_per_seq == 0)
    def _():
        carry_ref[...] = jnp.zeros_like(carry_ref)

    h = _modulated_norm(x_ref[...], g_ref[...], mod_ref[0], 0, 1)
    hhi = h.astype(BF16)
    hlo = (h - hhi.astype(F32)).astype(BF16)
    z2 = _dot(hhi, w2_ref[...])
    z = z2[:, :LANES] + z2[:, LANES:] + _dot(hlo, w2_ref[:, :LANES]) + b_ref[...]
    logf = jnp.minimum(z, 0.0) - jnp.log1p(jnp.exp(-jnp.abs(z)))
    tm = logf.shape[0]
    c3 = _dot(tri_ref[...], jnp.concatenate(_split3(logf), axis=1))
    cum = c3[:, :LANES] + c3[:, LANES:2 * LANES] + c3[:, 2 * LANES:] + carry_ref[...]
    o_ref[...] = cum
    carry_ref[...] = cum[tm - 1:tm, :]


def _fox_decay(x2, mod8, g_mix, wfl2, bfl, seq, tm):
    t = x2.shape[0]
    tps = seq // tm
    tri = jnp.asarray(np.tril(np.ones((tm, tm), np.float32)), BF16)
    return pl.pallas_call(
        functools.partial(_decay_body, tps),
        out_shape=jax.ShapeDtypeStruct((t, LANES), F32),
        grid=(t // tm,),
        in_specs=[pl.BlockSpec((tm, D_MODEL), lambda i: (i, 0)),
                  pl.BlockSpec((1, 8, D_MODEL), lambda i: (i // tps, 0, 0)),
                  _const_spec((1, D_MODEL)),
                  _const_spec((D_MODEL, 2 * LANES)),
                  _const_spec((1, LANES)),
                  _const_spec((tm, tm))],
        out_specs=pl.BlockSpec((tm, LANES), lambda i: (i, 0)),
        scratch_shapes=[pltpu.VMEM((1, LANES), F32)],
        compiler_params=_cparams(("arbitrary",)),
        name="fox_decay",
    )(x2, mod8, g_mix, wfl2, bfl, tri)


def _in_proj_body(x_ref, mod_ref, g_ref, f_ref, pos_ref, freq_ref,
                  wlat_ref, wkr_ref, gq_ref, gkv_ref, wqa_ref, wqb_ref, wka_ref, wva_ref,
                  wfq_ref, wfk_ref, wfv_ref, pall_ref, wg_ref,
                  q_ref, k_ref, v_ref, sg_ref):
    h = _modulated_norm(x_ref[...], g_ref[...], mod_ref[0], 0, 1)
    hb = h.astype(BF16)

    lat = _dot(hb, wlat_ref[...])
    qn = (_rms(lat[:, :MLA_Q_LORA]) * gq_ref[...]).astype(BF16)
    kvn = (_rms(lat[:, MLA_Q_LORA:]) * gkv_ref[...]).astype(BF16)
    ang = pos_ref[...].astype(F32) * freq_ref[...]
    cs = jnp.cos(ang)
    sn = jnp.sin(ang)
    kr = _dot(hb, wkr_ref[...])
    kpe = kr[:, :HEAD_PAD] * cs + kr[:, HEAD_PAD:] * sn
    qa = _dot(qn, wqa_ref[...])
    qb = _dot(qn, wqb_ref[...])
    ka = _dot(kvn, wka_ref[...])
    va = _dot(kvn, wva_ref[...])
    for hd in range(MLA_HEADS):
        sl = slice(hd * HEAD_PAD, (hd + 1) * HEAD_PAD)
        q_ref[0, hd] = (qa[:, sl] * cs + qb[:, sl] * sn).astype(BF16)
        k_ref[0, hd] = (ka[:, sl] + kpe).astype(BF16)
    ones_rows = jnp.ones((V_ROWS - MLA_V, va.shape[0]), BF16)

    def put_values(first_head, vals):
        for hp in range(vals.shape[1] // (2 * MLA_V)):
            vt = vals[:, hp * 2 * MLA_V:(hp + 1) * 2 * MLA_V].T.astype(BF16)
            for j in range(2):
                v_ref[0, first_head + 2 * hp + j, :MLA_V, :] = vt[j * MLA_V:(j + 1) * MLA_V, :]
                v_ref[0, first_head + 2 * hp + j, MLA_V:, :] = ones_rows

    put_values(0, va)

    hi, mid, lo = _split3(f_ref[...] * LOG2E)
    lane = lax.broadcasted_iota(I32, hi.shape, 1)
    f3 = jnp.where(lane < FOX_HEADS, hi.astype(F32), jnp.where(lane < 2 * FOX_HEADS, mid.astype(F32),
                   jnp.where(lane < 3 * FOX_HEADS, lo.astype(F32), jnp.where(lane == 3 * FOX_HEADS, 1.0, 0.0))))
    aug = _dot(f3.astype(BF16), pall_ref[...]).astype(BF16)
    fq = _dot(hb, wfq_ref[...]).astype(BF16)
    fk = _dot(hb, wfk_ref[...]).astype(BF16)
    hdim = FOX_HEAD_DIM
    for hd in range(FOX_HEADS):
        own = slice((hd % 2) * hdim, (hd % 2 + 1) * hdim)
        spare = slice((1 - hd % 2) * hdim, (2 - hd % 2) * hdim)
        src = slice(hd * hdim, (hd + 1) * hdim)
        asrc = (hd ^ 1) * hdim
        q_ref[0, MLA_HEADS + hd, :, own] = fq[:, src]
        k_ref[0, MLA_HEADS + hd, :, own] = fk[:, src]
        q_ref[0, MLA_HEADS + hd, :, spare] = aug[:, asrc:asrc + hdim]
        k_ref[0, MLA_HEADS + hd, :, spare] = aug[:, FOX_WIDTH + asrc:FOX_WIDTH + asrc + hdim]
    put_values(MLA_HEADS, _dot(hb, wfv_ref[...]))

    sg_ref[...] = jax.nn.sigmoid(_dot(hb, wg_ref[...])).astype(BF16)


def _in_proj(x2, mod8, g_mix, fdec, pos, freq, w, batch, seq, tm):
    t = x2.shape[0]
    tps = seq // tm
    consts = [w["wlat"], w["wkr"], w["gq"], w["gkv"], w["wqa"], w["wqb"], w["wka"], w["wva"],
              w["wfq"], w["wfk"], w["wfv"], w["pall"], w["wg"]]
    head_shape = jax.ShapeDtypeStruct((batch, N_HEADS, seq, HEAD_PAD), BF16)
    head_spec = pl.BlockSpec((1, N_HEADS, tm, HEAD_PAD), lambda i: (i // tps, 0, i % tps, 0))
    pair_shape = jax.ShapeDtypeStruct((batch, N_HEADS, V_ROWS, seq), BF16)
    pair_spec = pl.BlockSpec((1, N_HEADS, V_ROWS, tm), lambda i: (i // tps, 0, 0, i % tps))
    return pl.pallas_call(
        _in_proj_body,
        out_shape=(head_shape, head_shape, pair_shape,
                   jax.ShapeDtypeStruct((t, 2 * D_MODEL), BF16)),
        grid=(t // tm,),
        in_specs=[pl.BlockSpec((tm, D_MODEL), lambda i: (i, 0)),
                  pl.BlockSpec((1, 8, D_MODEL), lambda i: (i // tps, 0, 0)),
                  _const_spec((1, D_MODEL)),
                  pl.BlockSpec((tm, LANES), lambda i: (i, 0)),
                  pl.BlockSpec((tm, 1), lambda i: (i, 0)),
                  _const_spec((1, LANES))] + [_const_spec(a.shape) for a in consts],
        out_specs=(head_spec, head_spec, pair_spec,
                   pl.BlockSpec((tm, 2 * D_MODEL), lambda i: (i, 0))),
        compiler_params=_cparams(("parallel",)),
        name="in_proj",
    )(x2, mod8, g_mix, fdec, pos, freq, *consts)


def _attn_body(bq, sq, q_ref, k_ref, v_ref, o_ref, s_scr, p_scr, acc_scr):
    qi = pl.program_id(2)
    bk = sq
    n_heads = q_ref.shape[1]
    n_sub = bq // sq
    assert n_sub % 2 == 0
    chains = [(hh, u) for hh in range(n_heads) for u in range(n_sub)]
    n_chains = len(chains)
    n_main = qi * n_sub

    def chunk_start(j):
        return pl.multiple_of(jnp.maximum(j, 0) * bk, bk)

    def scores(c, j, par):
        hh, u = chains[c]
        k = k_ref[0, hh, pl.ds(chunk_start(j), bk), :]
        q = q_ref[0, hh, u * sq:(u + 1) * sq, :]
        s = lax.dot_general(k, q, (((1,), (1,)), ((), ())), preferred_element_type=F32)
        s_scr[par, c] = s
        return jnp.max(s, axis=0, keepdims=True)

    def values(c, j, par, alpha):
        vt = v_ref[0, chains[c][0], :, pl.ds(chunk_start(j), bk)]
        acc_scr[c] = alpha * acc_scr[c] + _dot(vt, p_scr[par, c])

    def softmax(c, par, m, smax, masked):
        s = s_scr[par, c]
        if masked:
            key = lax.broadcasted_iota(I32, (bk, sq), 0)
            qry = lax.broadcasted_iota(I32, (bk, sq), 1)
            s = jnp.where(key <= qry, s, NEG_INF)
            smax = jnp.max(s, axis=0, keepdims=True)
        m_new = jnp.maximum(m, smax)
        p_scr[par, c] = jnp.exp2((s - m_new).astype(BF16))
        return m_new, jnp.exp2(m - m_new)

    def stage(j, par, state, active, has_next, masked_of):
        nxt = {c: scores(c, j + 1, 1 - par) if has_next(c) else state[c][1] for c in active}
        for c in active:
            values(c, j - 1, 1 - par, state[c][2])
        out = list(state)
        for c in active:
            m, smax, _ = state[c]
            m_new, alpha = softmax(c, par, m, smax, masked_of(c))
            out[c] = (m_new, nxt[c], alpha)
        return out

    p_scr[1] = jnp.zeros(p_scr.shape[1:], BF16)
    acc_scr[...] = jnp.zeros_like(acc_scr)
    state = [(jnp.full((1, sq), NEG_INF, F32), scores(c, 0, 0), jnp.ones((1, sq), F32)) for c in range(n_chains)]

    def step(jj, st):
        for par in range(2):
            st = stage(2 * jj + par, par, list(st), range(n_chains), lambda c: True, lambda c: False)
        return tuple(st)

    state = list(lax.fori_loop(0, n_main // 2, step, tuple(state)))

    for t in range(n_sub):
        active = [c for c, (_, u) in enumerate(chains) if u >= t]
        state = stage(n_main + t, t % 2, state, active, lambda c, t=t: chains[c][1] > t,
                      lambda c, t=t: chains[c][1] == t)

    heads = []
    for hh in range(n_heads):
        parts = []
        for u in range(n_sub):
            c = hh * n_sub + u
            values(c, n_main + u, u % 2, state[c][2])
            parts.append(acc_scr[c, :MLA_V, :] / acc_scr[c, MLA_V:MLA_V + 1, :])
        heads.append(jnp.concatenate(parts, axis=1))
    o_ref[0] = jnp.concatenate(heads, axis=0).T.astype(BF16)


def _attention(q_all, k_all, v_all, bq):
    batch, _, seq, _ = q_all.shape
    sq = min(256, bq)
    n_chains = 2 * (bq // sq)
    return pl.pallas_call(
        functools.partial(_attn_body, bq, sq),
        out_shape=jax.ShapeDtypeStruct((batch, seq, N_HEADS * MLA_V), BF16),
        grid=(batch, N_HEADS // 2, seq // bq),
        in_specs=[pl.BlockSpec((1, 2, bq, HEAD_PAD), lambda b, hp, qi: (b, hp, qi, 0)),
                  pl.BlockSpec((1, 2, seq, HEAD_PAD), lambda b, hp, qi: (b, hp, 0, 0)),
                  pl.BlockSpec((1, 2, V_ROWS, seq), lambda b, hp, qi: (b, hp, 0, 0))],
        out_specs=pl.BlockSpec((1, bq, 2 * MLA_V), lambda b, hp, qi: (b, qi, hp)),
        scratch_shapes=[pltpu.VMEM((2, n_chains, sq, sq), F32),
                        pltpu.VMEM((2, n_chains, sq, sq), BF16),
                        pltpu.VMEM((n_chains, V_ROWS, sq), F32)],
        compiler_params=_cparams(("parallel", "parallel", "arbitrary")),
        name="attention",
    )(q_all, k_all, v_all)


def _pack_halves(x):
    w = x.shape[1] // 2
    a = lax.bitcast_convert_type(x[:, :w].astype(BF16).astype(F32), U32)
    b = lax.bitcast_convert_type(x[:, w:].astype(BF16).astype(F32), U32)
    return a | (b >> 16)


def _unpack_halves(p):
    a = lax.bitcast_convert_type(p & jnp.uint32(0xFFFF0000), F32)
    b = lax.bitcast_convert_type(p << 16, F32)
    return a, b


def _out_proj_body(o_ref, sg_ref, x_ref, mod_ref, gffn_ref, womla_ref, wofox_ref, wout_ref,
                   wr2_ref, x1_ref, hp_ref, lg_ref):
    o = o_ref[...]
    half = o.shape[1] // 2
    mo = _dot(o[:, :half], womla_ref[...])
    fo = _dot(o[:, half:], wofox_ref[...])
    sg = sg_ref[...]
    merged = sg[:, :D_MODEL].astype(F32) * mo + sg[:, D_MODEL:].astype(F32) * fo
    mix = _dot(merged.astype(BF16), wout_ref[...])
    mod = mod_ref[0]
    x1 = x_ref[...] + mod[2:3, :] * mix
    x1_ref[...] = x1
    h2 = _modulated_norm(x1, gffn_ref[...], mod, 3, 4)
    hhi = h2.astype(BF16)
    hlo = (h2 - hhi.astype(F32)).astype(BF16)
    l2 = _dot(hhi, wr2_ref[...])
    lg_ref[...] = (l2[:, :LANES] + l2[:, LANES:] + _dot(hlo, wr2_ref[:, :LANES])).T
    hp_ref[...] = _pack_halves(h2)


def _out_proj(o2, sg, x2, mod8, g_ffn, w, seq, tm):
    t = x2.shape[0]
    tps = seq // tm
    consts = [w["womla"], w["wofox"], w["wout"], w["wr2"]]
    return pl.pallas_call(
        _out_proj_body,
        out_shape=(jax.ShapeDtypeStruct((t, D_MODEL), F32),
                   jax.ShapeDtypeStruct((t, D_MODEL // 2), U32),
                   jax.ShapeDtypeStruct((LANES, t), F32)),
        grid=(t // tm,),
        in_specs=[pl.BlockSpec((tm, D_MODEL), lambda i: (i, 0)),
                  pl.BlockSpec((tm, 2 * D_MODEL), lambda i: (i, 0)),
                  pl.BlockSpec((tm, D_MODEL), lambda i: (i, 0)),
                  pl.BlockSpec((1, 8, D_MODEL), lambda i: (i // tps, 0, 0)),
                  _const_spec((1, D_MODEL))] + [_const_spec(a.shape) for a in consts],
        out_specs=(pl.BlockSpec((tm, D_MODEL), lambda i: (i, 0)),
                   pl.BlockSpec((tm, D_MODEL // 2), lambda i: (i, 0)),
                   pl.BlockSpec((LANES, tm), lambda i: (0, i))),
        compiler_params=_cparams(("parallel",)),
        name="out_proj",
    )(o2, sg, x2, mod8, g_ffn, *consts)


def _route_body(lt_ref, b_ref, eidx_ref, rank_ref, gate_ref, cnt_ref, carry_ref):
    i = pl.program_id(0)

    @pl.when(i == 0)
    def _():
        carry_ref[...] = jnp.zeros_like(carry_ref)

    s = jax.nn.sigmoid(lt_ref[...])
    c = s + b_ref[...]
    tr = s.shape[1]
    sub = lax.broadcasted_iota(I32, (GROUP_SIZE, tr), 0).astype(F32)

    gs = []
    for g in range(N_GROUPS):
        cg = c[g * GROUP_SIZE:(g + 1) * GROUP_SIZE, :]
        m1 = jnp.max(cg, axis=0, keepdims=True)
        i1 = jnp.min(jnp.where(cg == m1, sub, float(GROUP_SIZE)), axis=0, keepdims=True)
        m2 = jnp.max(jnp.where(sub == i1, NEG_INF, cg), axis=0, keepdims=True)
        gs.append(m1 + m2)

    masked = []
    for g in range(N_GROUPS):
        beats = jnp.zeros_like(gs[g])
        for o in range(N_GROUPS):
            if o == g:
                continue
            better = (gs[o] >= gs[g]) if o < g else (gs[o] > gs[g])
            beats = beats + jnp.where(better, 1.0, 0.0)
        keep = beats < float(TOPK_GROUPS)
        cg = c[g * GROUP_SIZE:(g + 1) * GROUP_SIZE, :]
        masked.append(jnp.where(keep, cg, NEG_INF))
    mc = jnp.concatenate(masked, axis=0)

    eio = lax.broadcasted_iota(I32, (N_EXPERTS, tr), 0).astype(F32)
    picks = []
    selected = jnp.zeros((N_EXPERTS, tr), F32)
    for _ in range(TOP_K):
        m = jnp.max(mc, axis=0, keepdims=True)
        idx = jnp.min(jnp.where(mc == m, eio, float(N_EXPERTS)), axis=0, keepdims=True)
        hit = eio == idx
        picks.append(idx)
        selected = jnp.where(hit, 1.0, selected)
        mc = jnp.where(hit, -3.0e38, mc)

    ssum = jnp.sum(selected * s, axis=0, keepdims=True)
    gate_full = selected * s / ssum * ROUTED_SCALE

    r_io = lax.broadcasted_iota(I32, (tr, tr), 0)
    c_io = lax.broadcasted_iota(I32, (tr, tr), 1)
    upper = jnp.where(r_io < c_io, 1.0, 0.0).astype(BF16)
    before = _dot(selected.astype(BF16), upper) + carry_ref[...]
    carry_new = carry_ref[...] + jnp.sum(selected, axis=1, keepdims=True)
    carry_ref[...] = carry_new
    cnt_ref[...] = jnp.broadcast_to(carry_new, cnt_ref.shape).astype(I32)

    for r in range(SLOTS):
        if r < TOP_K:
            hit = eio == picks[r]
            eidx_ref[r:r + 1, :] = picks[r].astype(I32)
            rank_ref[r:r + 1, :] = jnp.sum(jnp.where(hit, before, 0.0), axis=0, keepdims=True).astype(I32)
            gate_ref[r:r + 1, :] = jnp.sum(jnp.where(hit, gate_full, 0.0), axis=0, keepdims=True)
        else:
            eidx_ref[r:r + 1, :] = jnp.zeros((1, tr), I32)
            rank_ref[r:r + 1, :] = jnp.zeros((1, tr), I32)
            gate_ref[r:r + 1, :] = jnp.zeros((1, tr), F32)


def _route(logits_t, bias_col, tr):
    t = logits_t.shape[1]
    slot_i = jax.ShapeDtypeStruct((SLOTS, t), I32)
    slot_spec = pl.BlockSpec((SLOTS, tr), lambda i: (0, i))
    return pl.pallas_call(
        _route_body,
        out_shape=(slot_i, slot_i, jax.ShapeDtypeStruct((SLOTS, t), F32),
                   jax.ShapeDtypeStruct((N_EXPERTS, LANES), I32)),
        grid=(t // tr,),
        in_specs=[pl.BlockSpec((N_EXPERTS, tr), lambda i: (0, i)),
                  _const_spec((N_EXPERTS, 1))],
        out_specs=(slot_spec, slot_spec, slot_spec, _const_spec((N_EXPERTS, LANES))),
        scratch_shapes=[pltpu.VMEM((N_EXPERTS, 1), F32)],
        compiler_params=_cparams(("arbitrary",)),
        name="route",
    )(logits_t, bias_col)


def _dest_body(pstart_ref, eidx_ref, rank_ref, o_ref):
    e = eidx_ref[...]
    d = rank_ref[...]
    for j in range(N_EXPERTS):
        d = d + jnp.where(e == j, pstart_ref[j], 0)
    o_ref[...] = d


def _dest(pstart, eidx, rank, tr):
    t = eidx.shape[1]
    spec = pl.BlockSpec((SLOTS, tr), lambda i, ps: (0, i))
    return pl.pallas_call(
        _dest_body,
        out_shape=jax.ShapeDtypeStruct((SLOTS, t), I32),
        grid_spec=pltpu.PrefetchScalarGridSpec(
            num_scalar_prefetch=1, grid=(t // tr,), in_specs=[spec, spec], out_specs=spec),
        compiler_params=_cparams(("parallel",)),
        name="dest",
    )(pstart, eidx, rank)


def _sc_mesh():
    return plsc.VectorSubcoreMesh(core_axis_name="c", subcore_axis_name="s")


def _sc_worker_id():
    return lax.axis_index("s") * SC_CORES + lax.axis_index("c")


def _dispatch(hpk, dest3, n_rows):
    _, w = hpk.shape
    n_chunks, _, c = dest3.shape
    per_worker = n_chunks // SC_WORKERS

    def body(h_hbm, d_hbm, xs_hbm, idx_v, rows_v, sem):
        wid = _sc_worker_id()

        @pl.loop(0, per_worker)
        def _(i):
            ch = wid * per_worker + i
            pltpu.sync_copy(d_hbm.at[ch], idx_v)
            pltpu.sync_copy(h_hbm.at[pl.ds(ch * c, c)], rows_v)
            copies = [pltpu.async_copy(rows_v, xs_hbm.at[idx_v.at[k]], sem) for k in range(TOP_K)]
            for cp in copies:
                cp.wait()

    return pl.kernel(
        body, mesh=_sc_mesh(),
        out_type=jax.ShapeDtypeStruct((n_rows, w), U32),
        scratch_types=[pltpu.VMEM((SLOTS, c), I32), pltpu.VMEM((c, w), U32), pltpu.SemaphoreType.DMA],
        name="dispatch",
    )(hpk, dest3)


def _experts_body(n_sub, first_ref, nblk_ref, cnt_ref, nu_ref, xs_hbm, wg_ref, wu_ref, wd_ref, ys_hbm,
                  wgb_ref, wub_ref, wdb_ref, xbuf, ybuf, xsem, ysem):
    e = pl.program_id(0)
    nb = nblk_ref[e]
    b0 = first_ref[e]
    cnt = cnt_ref[e]
    n_used = nu_ref[0]
    w = xbuf.shape[2]

    def x_copy(g, slot):
        start = pl.multiple_of(g * ROW_BLOCK, ROW_BLOCK)
        return pltpu.make_async_copy(xs_hbm.at[pl.ds(start, ROW_BLOCK), :], xbuf.at[slot], xsem.at[slot])

    def y_copy(g, slot):
        start = pl.multiple_of(g * ROW_BLOCK, ROW_BLOCK)
        return pltpu.make_async_copy(ybuf.at[slot], ys_hbm.at[pl.ds(start, ROW_BLOCK), :], ysem.at[slot])

    @pl.when((e == 0) & (n_used > 0))
    def _():
        x_copy(0, 0).start()

    wgb_ref[...] = wg_ref[0].astype(BF16)
    wub_ref[...] = wu_ref[0].astype(BF16)
    wdb_ref[...] = wd_ref[0].astype(BF16)

    half = D_MODEL // 2
    sub = ROW_BLOCK // n_sub
    row = lax.broadcasted_iota(I32, (sub, w), 0)

    def block(g, j, slot):
        x_copy(g, slot).wait()

        @pl.when(g + 1 < n_used)
        def _():
            x_copy(g + 1, 1 - slot).start()

        @pl.when(g >= 2)
        def _():
            y_copy(g - 2, slot).wait()

        valid = cnt - j * ROW_BLOCK
        gu = []
        for s in range(n_sub):
            x = jnp.where(row + s * sub < valid, xbuf[slot, s * sub:(s + 1) * sub, :], jnp.uint32(0))
            xa, xb = _unpack_halves(x)
            xa = xa.astype(BF16)
            xb = xb.astype(BF16)
            g1 = _dot(xa, wgb_ref[:half, :]) + _dot(xb, wgb_ref[half:, :])
            u1 = _dot(xa, wub_ref[:half, :]) + _dot(xb, wub_ref[half:, :])
            gu.append((g1, u1))
        for s, (g1, u1) in enumerate(gu):
            hb = (g1 * jax.nn.sigmoid(g1) * u1).astype(BF16)
            ybuf[slot, s * sub:(s + 1) * sub, :] = _pack_halves(_dot(hb, wdb_ref[...]))
        y_copy(g, slot).start()

    def body(j, _):
        g = b0 + j
        for slot in range(2):
            @pl.when(g % 2 == slot)
            def _():
                block(g, j, slot)
        return 0

    lax.fori_loop(0, nb, body, 0)

    @pl.when(e == pl.num_programs(0) - 1)
    def _():
        for slot in range(2):
            @pl.when(n_used > slot)
            def _():
                y_copy(0, slot).wait()

        n_blocks = ys_hbm.shape[0] // ROW_BLOCK
        ybuf[0] = jnp.zeros(ybuf.shape[1:], U32)
        lax.fori_loop(n_used, n_blocks, lambda g, c: (y_copy(g, 0).start(), c)[1], 0)
        lax.fori_loop(n_used, n_blocks, lambda g, c: (y_copy(g, 0).wait(), c)[1], 0)


def _experts(first_blk, n_blk, cnt, n_used, xs, wg, wu, wd):
    n_rows, w = xs.shape

    def w_map(e, fb, nb, ct, nu):
        return (e, 0, 0)

    return pl.pallas_call(
        functools.partial(_experts_body, EXPERT_SUB_BLOCKS),
        out_shape=jax.ShapeDtypeStruct((n_rows, w), U32),
        grid_spec=pltpu.PrefetchScalarGridSpec(
            num_scalar_prefetch=4, grid=(N_EXPERTS,),
            in_specs=[pl.BlockSpec(memory_space=pl.ANY),
                      pl.BlockSpec((1, D_MODEL, D_EXPERT), w_map),
                      pl.BlockSpec((1, D_MODEL, D_EXPERT), w_map),
                      pl.BlockSpec((1, D_EXPERT, D_MODEL), w_map)],
            out_specs=pl.BlockSpec(memory_space=pl.ANY),
            scratch_shapes=[pltpu.VMEM((D_MODEL, D_EXPERT), BF16), pltpu.VMEM((D_MODEL, D_EXPERT), BF16),
                            pltpu.VMEM((D_EXPERT, D_MODEL), BF16),
                            pltpu.VMEM((2, ROW_BLOCK, w), U32), pltpu.VMEM((2, ROW_BLOCK, w), U32),
                            pltpu.SemaphoreType.DMA((2,)), pltpu.SemaphoreType.DMA((2,))]),
        compiler_params=_cparams(("arbitrary",)),
        name="experts",
    )(first_blk, n_blk, cnt, n_used, xs, wg, wu, wd)


def _gather_back(ys, dest3):
    _, w = ys.shape
    n_chunks, _, c = dest3.shape
    per_worker = n_chunks // SC_WORKERS

    def body(ys_hbm, d_hbm, yk_hbm, idx_v, rows_v, gsem, osem):
        wid = _sc_worker_id()

        @pl.loop(0, per_worker)
        def _(i):
            ch = wid * per_worker + i
            pltpu.sync_copy(d_hbm.at[ch], idx_v)
            gathers = [pltpu.async_copy(ys_hbm.at[idx_v.at[k]], rows_v.at[k], gsem) for k in range(TOP_K)]
            for g in gathers:
                g.wait()
            outs = [pltpu.async_copy(rows_v.at[k], yk_hbm.at[k, pl.ds(ch * c, c)], osem) for k in range(TOP_K)]
            for o in outs:
                o.wait()

    return pl.kernel(
        body, mesh=_sc_mesh(),
        out_type=jax.ShapeDtypeStruct((TOP_K, n_chunks * c, w), U32),
        scratch_types=[pltpu.VMEM((SLOTS, c), I32), pltpu.VMEM((TOP_K, c, w), U32),
                       pltpu.SemaphoreType.DMA, pltpu.SemaphoreType.DMA],
        name="gather_back",
    )(ys, dest3)


def _combine_body(yk_ref, gate_ref, hp_ref, x1_ref, mod_ref, wsg_ref, wsu_ref, wsd_ref, gfin_ref, o_ref):
    ha, hb = _unpack_halves(hp_ref[...])
    ha = ha.astype(BF16)
    hb = hb.astype(BF16)
    half = D_MODEL // 2
    g = _dot(ha, wsg_ref[:half, :]) + _dot(hb, wsg_ref[half:, :])
    u = _dot(ha, wsu_ref[:half, :]) + _dot(hb, wsu_ref[half:, :])
    shared = _dot((g * jax.nn.sigmoid(g) * u).astype(BF16), wsd_ref[...])

    gate = gate_ref[...]
    ra = jnp.zeros(ha.shape, F32)
    rb = jnp.zeros(ha.shape, F32)
    for k in range(TOP_K):
        ya, yb = _unpack_halves(yk_ref[k])
        gk = gate[:, k:k + 1]
        ra = ra + gk * ya
        rb = rb + gk * yb
    moe = shared + jnp.concatenate([ra, rb], axis=1)
    mod = mod_ref[0]
    x2 = x1_ref[...] + mod[5:6, :] * moe
    o_ref[...] = _rms(x2) * gfin_ref[...]


def _combine(yk, gate_t, hpk, x1, mod8, w, g_final, seq, tq):
    t, wd = hpk.shape
    tps = seq // tq
    return pl.pallas_call(
        _combine_body,
        out_shape=jax.ShapeDtypeStruct((t, D_MODEL), F32),
        grid=(t // tq,),
        in_specs=[pl.BlockSpec((TOP_K, tq, wd), lambda i: (0, i, 0)),
                  pl.BlockSpec((tq, SLOTS), lambda i: (i, 0)),
                  pl.BlockSpec((tq, wd), lambda i: (i, 0)),
                  pl.BlockSpec((tq, D_MODEL), lambda i: (i, 0)),
                  pl.BlockSpec((1, 8, D_MODEL), lambda i: (i // tps, 0, 0)),
                  _const_spec(w["wsg"].shape), _const_spec(w["wsu"].shape), _const_spec(w["wsd"].shape),
                  _const_spec((1, D_MODEL))],
        out_specs=pl.BlockSpec((tq, D_MODEL), lambda i: (i, 0)),
        compiler_params=_cparams(("parallel",)),
        name="combine",
    )(yk, gate_t, hpk, x1, mod8, w["wsg"], w["wsu"], w["wsd"], g_final)


def _prep_weights(w_in, b_forget, g_q_lat, w_q_up, g_kv_lat, w_kv_up, w_o_mla, w_o_fox, w_out,
                  w_router, w_sh_gate, w_sh_up, w_sh_down):
    o_q, o_kv, o_kr = 0, MLA_Q_LORA, MLA_Q_LORA + MLA_KV_LORA
    o_fq = o_kr + MLA_ROPE
    o_fk, o_fv = o_fq + FOX_WIDTH, o_fq + 2 * FOX_WIDTH
    o_fl = o_fq + 3 * FOX_WIDTH
    o_ga = o_fl + FOX_HEADS
    o_gb = o_ga + D_MODEL
    w = {}
    w["wlat"] = w_in[:, o_q:o_kr].astype(BF16)
    half = MLA_ROPE // 2

    def rope_pair(cols):
        x1, x2 = cols[..., :half], cols[..., half:]
        z = jnp.zeros(cols.shape[:-1] + (MLA_NOPE,), cols.dtype)
        zt = jnp.zeros(cols.shape[:-1] + (HEAD_PAD - MLA_QK,), cols.dtype)
        plain = jnp.concatenate([z, x1, x2, zt], axis=-1)
        rot = jnp.concatenate([z, -x2, x1, zt], axis=-1)
        return plain, rot

    kr_plain, kr_rot = rope_pair(w_in[:, o_kr:o_fq])
    w["wkr"] = jnp.concatenate([kr_plain, kr_rot], axis=1).astype(BF16)
    w["gq"] = g_q_lat.reshape(1, -1)
    w["gkv"] = g_kv_lat.reshape(1, -1)

    scale_a = LOG2E / math.sqrt(MLA_QK)
    wq = (w_q_up * scale_a).reshape(MLA_Q_LORA, MLA_HEADS, MLA_QK)
    q_plain, q_rot = rope_pair(wq[..., MLA_NOPE:])
    nope = jnp.concatenate([wq[..., :MLA_NOPE], jnp.zeros((MLA_Q_LORA, MLA_HEADS, HEAD_PAD - MLA_NOPE), F32)], -1)
    w["wqa"] = (nope + q_plain).reshape(MLA_Q_LORA, -1).astype(BF16)
    w["wqb"] = q_rot.reshape(MLA_Q_LORA, -1).astype(BF16)

    wkv = w_kv_up.reshape(MLA_KV_LORA, MLA_HEADS, MLA_NOPE + MLA_V)
    zpad = jnp.zeros((MLA_KV_LORA, MLA_HEADS, HEAD_PAD - MLA_NOPE), F32)
    w["wka"] = jnp.concatenate([wkv[..., :MLA_NOPE], zpad], -1).reshape(MLA_KV_LORA, -1).astype(BF16)

    w["wva"] = wkv[..., MLA_NOPE:].reshape(MLA_KV_LORA, -1).astype(BF16)

    w["wfq"] = (w_in[:, o_fq:o_fk] * (LOG2E / math.sqrt(FOX_HEAD_DIM))).astype(BF16)
    w["wfk"] = w_in[:, o_fk:o_fv].astype(BF16)
    w["wfv"] = w_in[:, o_fv:o_fl].astype(BF16)

    wfl = jnp.concatenate([w_in[:, o_fl:o_ga]] * 3 + [jnp.zeros((D_MODEL, LANES - 3 * FOX_HEADS), F32)], 1)
    wfl_hi = wfl.astype(BF16)
    w["wfl2"] = jnp.concatenate([wfl_hi, (wfl - wfl_hi.astype(F32)).astype(BF16)], axis=1)
    w["bfl"] = jnp.concatenate([b_forget] * 3 + [jnp.zeros((LANES - 3 * FOX_HEADS,), F32)]).reshape(1, LANES)

    pall = np.zeros((LANES, 2 * FOX_WIDTH), np.float32)
    ones_row = 3 * FOX_HEADS
    for hd in range(FOX_HEADS):
        gq = (hd ^ 1) * FOX_HEAD_DIM
        gk = FOX_WIDTH + (hd ^ 1) * FOX_HEAD_DIM
        for term in range(3):
            pall[term * FOX_HEADS + hd, gq + FQ_COL + term] = 1.0
            pall[ones_row, gq + FK_COL + term] = 1.0
            pall[ones_row, gk + FQ_COL + term] = 1.0
            pall[term * FOX_HEADS + hd, gk + FK_COL + term] = -1.0
    w["pall"] = jnp.asarray(pall, BF16)

    w["wg"] = w_in[:, o_ga:o_gb + D_MODEL].astype(BF16)
    w["womla"] = w_o_mla.astype(BF16)
    w["wofox"] = w_o_fox.astype(BF16)
    w["wout"] = w_out.astype(BF16)
    wr = jnp.concatenate([w_router, jnp.zeros((D_MODEL, LANES - N_EXPERTS), F32)], 1)
    wr_hi = wr.astype(BF16)
    w["wr2"] = jnp.concatenate([wr_hi, (wr - wr_hi.astype(F32)).astype(BF16)], axis=1)
    w["wsg"] = w_sh_gate.astype(BF16)
    w["wsu"] = w_sh_up.astype(BF16)
    w["wsd"] = w_sh_down.astype(BF16)
    return w


def _rope_freq_row():
    half = MLA_ROPE // 2
    inv = np.power(ROPE_THETA, -np.arange(half, dtype=np.float32) / half).astype(np.float32)
    row = np.zeros((1, LANES), np.float32)
    row[0, MLA_NOPE:MLA_NOPE + half] = inv
    row[0, MLA_NOPE + half:MLA_NOPE + 2 * half] = inv
    return jnp.asarray(row)


def kernel(x, c, positions, w_mod, b_mod, g_mix_norm, w_in, b_forget, g_q_lat, w_q_up, g_kv_lat, w_kv_up,
           w_o_mla, w_o_fox, w_out, g_ffn_norm, w_router, b_router, w_exp_gate, w_exp_up, w_exp_down,
           w_sh_gate, w_sh_up, w_sh_down, g_final):
    batch, seq, d = x.shape
    assert d == D_MODEL and w_mod.shape[0] == 1
    t = batch * seq
    tm = min(512, seq)
    bq = min(2048, seq)
    tr = min(512, t)
    to = min(256, seq)
    tc = min(512, seq)
    assert seq % tm == 0 and seq % bq == 0 and t % tr == 0 and seq % tc == 0 and seq % to == 0 and batch <= 8
    assert t % (SC_CHUNK * SC_WORKERS) == 0

    def layer0(a):
        return a.reshape(a.shape[1:])

    w = _prep_weights(*(layer0(a) for a in (w_in, b_forget, g_q_lat, w_q_up, g_kv_lat, w_kv_up, w_o_mla, w_o_fox,
                                            w_out, w_router, w_sh_gate, w_sh_up, w_sh_down)))

    c8 = jnp.zeros((8, D_MODEL), F32).at[:batch].set(c)
    mod = _mod(c8, layer0(w_mod), b_mod)
    mod8 = jnp.zeros((batch, 8, D_MODEL), F32).at[:, :N_MOD].set(mod[:batch].reshape(batch, N_MOD, D_MODEL))

    x2 = x.reshape(t, D_MODEL)
    fdec = _fox_decay(x2, mod8, g_mix_norm, w["wfl2"], w["bfl"], seq, tm)
    q_all, k_all, v_all, sg = _in_proj(x2, mod8, g_mix_norm, fdec, positions.reshape(t, 1),
                                       _rope_freq_row(), w, batch, seq, tm)
    o = _attention(q_all, k_all, v_all, bq)
    x1, hpk, logits_t = _out_proj(o.reshape(t, D_MODEL), sg, x2, mod8, g_ffn_norm, w, seq, to)

    eidx, rank, gate, counts = _route(logits_t, b_router.reshape(N_EXPERTS, 1), tr)

    cnt = counts[:, 0]
    padded = (cnt + ROW_BLOCK - 1) // ROW_BLOCK * ROW_BLOCK
    pend = jnp.cumsum(padded)
    pstart = pend - padded
    n_blocks = t * TOP_K // ROW_BLOCK + N_EXPERTS
    n_rows = n_blocks * ROW_BLOCK
    n_used = (pend[-1:] // ROW_BLOCK).astype(I32)

    dest = _dest(pstart.astype(I32), eidx, rank, min(2048, t))
    dest3 = dest.reshape(SLOTS, t // SC_CHUNK, SC_CHUNK).transpose(1, 0, 2)
    xs = _dispatch(hpk, dest3, n_rows)
    ys = _experts((pstart // ROW_BLOCK).astype(I32), (padded // ROW_BLOCK).astype(I32), cnt.astype(I32), n_used, xs,
                  layer0(w_exp_gate), layer0(w_exp_up), layer0(w_exp_down))
    yk = _gather_back(ys, dest3)
    out = _combine(yk, gate.T, hpk, x1, mod8, w, g_final.reshape(1, D_MODEL), seq, tc)
    return out.reshape(batch, seq, D_MODEL)
```

```python
import functools
import math

import numpy as np
import jax
import jax.numpy as jnp
from jax import lax
from jax.experimental import pallas as pl
from jax.experimental.pallas import tpu as pltpu
from jax.experimental.pallas import tpu_sc as plsc

F32 = jnp.float32
BF16 = jnp.bfloat16
I32 = jnp.int32
U32 = jnp.uint32

D_MODEL = 1024
MLA_HEADS = 8
MLA_Q_LORA = 256
MLA_KV_LORA = 128
MLA_NOPE = 64
MLA_ROPE = 32
MLA_V = 64
MLA_QK = MLA_NOPE + MLA_ROPE
ROPE_THETA = 10000.0
FOX_HEADS = 8
FOX_HEAD_DIM = 64
FOX_WIDTH = FOX_HEADS * FOX_HEAD_DIM
N_HEADS = MLA_HEADS + FOX_HEADS
N_EXPERTS = 64
N_GROUPS = 8
GROUP_SIZE = N_EXPERTS // N_GROUPS
TOPK_GROUPS = 4
TOP_K = 6
D_EXPERT = 256
ROUTED_SCALE = 2.5
N_MOD = 6
NORM_EPS = 1e-6
NEG_INF = -1e30
LOG2E = math.log2(math.e)

LANES = 128
HEAD_PAD = 128
V_ROWS = MLA_V + 16
ROW_BLOCK = 512
MXU_ROWS = 256
SLOTS = 8
SC_CORES = 2
SC_SUBCORES = 16
SC_WORKERS = SC_CORES * SC_SUBCORES
SC_CHUNK = 32
VMEM_LIMIT = 56 * 1024 * 1024

FQ_COL = 0
FK_COL = 3


def _cparams(sem, vmem=VMEM_LIMIT):
    return pltpu.CompilerParams(dimension_semantics=sem, vmem_limit_bytes=vmem)


def _const_spec(shape):
    nd = len(shape)
    return pl.BlockSpec(shape, lambda *_: (0,) * nd)


def _rms(x):
    return x * lax.rsqrt(jnp.mean(x * x, axis=-1, keepdims=True) + NORM_EPS)


def _split3(x):
    hi = x.astype(BF16)
    r = x - hi.astype(F32)
    mid = r.astype(BF16)
    lo = (r - mid.astype(F32)).astype(BF16)
    return hi, mid, lo


def _dot(a, b):
    return jnp.dot(a, b, preferred_element_type=F32)


def _modulated_norm(x, gain, mod, shift_row, scale_row):
    shift = mod[shift_row:shift_row + 1, :]
    scale = mod[scale_row:scale_row + 1, :]
    return _rms(x) * gain * (1.0 + scale) + shift


def _mod_body(c_ref, w_ref, b_ref, o_ref):
    c = c_ref[...]
    cond = c * jax.nn.sigmoid(c)
    o_ref[...] = _dot(cond.astype(BF16), w_ref[...].astype(BF16)) + b_ref[...]


def _mod(c8, w_mod, b_mod):
    n = w_mod.shape[1]
    tn = D_MODEL
    return pl.pallas_call(
        _mod_body,
        out_shape=jax.ShapeDtypeStruct((8, n), F32),
        grid=(n // tn,),
        in_specs=[_const_spec((8, D_MODEL)),
                  pl.BlockSpec((D_MODEL, tn), lambda j: (0, j)),
                  pl.BlockSpec((1, tn), lambda j: (0, j))],
        out_specs=pl.BlockSpec((8, tn), lambda j: (0, j)),
        compiler_params=_cparams(("parallel",)),
        name="mod",
    )(c8, w_mod, b_mod)


def _decay_body(tiles_per_seq, x_ref, mod_ref, g_ref, w2_ref, b_ref, tri_ref, o_ref, carry_ref):
    i = pl.program_id(0)

    @pl.when(i % tiles_per_seq == 0)
    def _():
        carry_ref[...] = jnp.zeros_like(carry_ref)

    h = _modulated_norm(x_ref[...], g_ref[...], mod_ref[0], 0, 1)
    hhi = h.astype(BF16)
    hlo = (h - hhi.astype(F32)).astype(BF16)
    z2 = _dot(hhi, w2_ref[...])
    z = z2[:, :LANES] + z2[:, LANES:] + _dot(hlo, w2_ref[:, :LANES]) + b_ref[...]
    logf = jnp.minimum(z, 0.0) - jnp.log1p(jnp.exp(-jnp.abs(z)))
    tm = logf.shape[0]
    c3 = _dot(tri_ref[...], jnp.concatenate(_split3(logf), axis=1))
    cum = c3[:, :LANES] + c3[:, LANES:2 * LANES] + c3[:, 2 * LANES:] + carry_ref[...]
    o_ref[...] = cum
    carry_ref[...] = cum[tm - 1:tm, :]


def _fox_decay(x2, mod8, g_mix, wfl2, bfl, seq, tm):
    t = x2.shape[0]
    tps = seq // tm
    tri = jnp.asarray(np.tril(np.ones((tm, tm), np.float32)), BF16)
    return pl.pallas_call(
        functools.partial(_decay_body, tps),
        out_shape=jax.ShapeDtypeStruct((t, LANES), F32),
        grid=(t // tm,),
        in_specs=[pl.BlockSpec((tm, D_MODEL), lambda i: (i, 0)),
                  pl.BlockSpec((1, 8, D_MODEL), lambda i: (i // tps, 0, 0)),
                  _const_spec((1, D_MODEL)),
                  _const_spec((D_MODEL, 2 * LANES)),
                  _const_spec((1, LANES)),
                  _const_spec((tm, tm))],
        out_specs=pl.BlockSpec((tm, LANES), lambda i: (i, 0)),
        scratch_shapes=[pltpu.VMEM((1, LANES), F32)],
        compiler_params=_cparams(("arbitrary",)),
        name="fox_decay",
    )(x2, mod8, g_mix, wfl2, bfl, tri)


def _in_proj_body(x_ref, mod_ref, g_ref, f_ref, pos_ref, freq_ref,
                  wlat_ref, wkr_ref, gq_ref, gkv_ref, wqa_ref, wqb_ref, wka_ref, wva_ref,
                  wfq_ref, wfk_ref, wfv_ref, pall_ref, wg_ref,
                  q_ref, k_ref, v_ref, sg_ref):
    h = _modulated_norm(x_ref[...], g_ref[...], mod_ref[0], 0, 1)
    hb = h.astype(BF16)

    lat = _dot(hb, wlat_ref[...])
    qn = (_rms(lat[:, :MLA_Q_LORA]) * gq_ref[...]).astype(BF16)
    kvn = (_rms(lat[:, MLA_Q_LORA:]) * gkv_ref[...]).astype(BF16)
    ang = pos_ref[...].astype(F32) * freq_ref[...]
    cs = jnp.cos(ang)
    sn = jnp.sin(ang)
    kr = _dot(hb, wkr_ref[...])
    kpe = kr[:, :HEAD_PAD] * cs + kr[:, HEAD_PAD:] * sn
    qa = _dot(qn, wqa_ref[...])
    qb = _dot(qn, wqb_ref[...])
    ka = _dot(kvn, wka_ref[...])
    va = _dot(kvn, wva_ref[...])
    for hd in range(MLA_HEADS):
        sl = slice(hd * HEAD_PAD, (hd + 1) * HEAD_PAD)
        q_ref[0, hd] = (qa[:, sl] * cs + qb[:, sl] * sn).astype(BF16)
        k_ref[0, hd] = (ka[:, sl] + kpe).astype(BF16)
    ones_rows = jnp.ones((V_ROWS - MLA_V, va.shape[0]), BF16)

    def put_values(first_head, vals):
        for hp in range(vals.shape[1] // (2 * MLA_V)):
            vt = vals[:, hp * 2 * MLA_V:(hp + 1) * 2 * MLA_V].T.astype(BF16)
            for j in range(2):
                v_ref[0, first_head + 2 * hp + j, :MLA_V, :] = vt[j * MLA_V:(j + 1) * MLA_V, :]
                v_ref[0, first_head + 2 * hp + j, MLA_V:, :] = ones_rows

    put_values(0, va)

    hi, mid, lo = _split3(f_ref[...] * LOG2E)
    lane = lax.broadcasted_iota(I32, hi.shape, 1)
    f3 = jnp.where(lane < FOX_HEADS, hi.astype(F32), jnp.where(lane < 2 * FOX_HEADS, mid.astype(F32),
                   jnp.where(lane < 3 * FOX_HEADS, lo.astype(F32), jnp.where(lane == 3 * FOX_HEADS, 1.0, 0.0))))
    aug = _dot(f3.astype(BF16), pall_ref[...]).astype(BF16)
    fq = _dot(hb, wfq_ref[...]).astype(BF16)
    fk = _dot(hb, wfk_ref[...]).astype(BF16)
    hdim = FOX_HEAD_DIM
    for hd in range(FOX_HEADS):
        own = slice((hd % 2) * hdim, (hd % 2 + 1) * hdim)
        spare = slice((1 - hd % 2) * hdim, (2 - hd % 2) * hdim)
        src = slice(hd * hdim, (hd + 1) * hdim)
        asrc = (hd ^ 1) * hdim
        q_ref[0, MLA_HEADS + hd, :, own] = fq[:, src]
        k_ref[0, MLA_HEADS + hd, :, own] = fk[:, src]
        q_ref[0, MLA_HEADS + hd, :, spare] = aug[:, asrc:asrc + hdim]
        k_ref[0, MLA_HEADS + hd, :, spare] = aug[:, FOX_WIDTH + asrc:FOX_WIDTH + asrc + hdim]
    put_values(MLA_HEADS, _dot(hb, wfv_ref[...]))

    sg_ref[...] = jax.nn.sigmoid(_dot(hb, wg_ref[...])).astype(BF16)


def _in_proj(x2, mod8, g_mix, fdec, pos, freq, w, batch, seq, tm):
    t = x2.shape[0]
    tps = seq // tm
    consts = [w["wlat"], w["wkr"], w["gq"], w["gkv"], w["wqa"], w["wqb"], w["wka"], w["wva"],
              w["wfq"], w["wfk"], w["wfv"], w["pall"], w["wg"]]
    head_shape = jax.ShapeDtypeStruct((batch, N_HEADS, seq, HEAD_PAD), BF16)
    head_spec = pl.BlockSpec((1, N_HEADS, tm, HEAD_PAD), lambda i: (i // tps, 0, i % tps, 0))
    pair_shape = jax.ShapeDtypeStruct((batch, N_HEADS, V_ROWS, seq), BF16)
    pair_spec = pl.BlockSpec((1, N_HEADS, V_ROWS, tm), lambda i: (i // tps, 0, 0, i % tps))
    return pl.pallas_call(
        _in_proj_body,
        out_shape=(head_shape, head_shape, pair_shape,
                   jax.ShapeDtypeStruct((t, 2 * D_MODEL), BF16)),
        grid=(t // tm,),
        in_specs=[pl.BlockSpec((tm, D_MODEL), lambda i: (i, 0)),
                  pl.BlockSpec((1, 8, D_MODEL), lambda i: (i // tps, 0, 0)),
                  _const_spec((1, D_MODEL)),
                  pl.BlockSpec((tm, LANES), lambda i: (i, 0)),
                  pl.BlockSpec((tm, 1), lambda i: (i, 0)),
                  _const_spec((1, LANES))] + [_const_spec(a.shape) for a in consts],
        out_specs=(head_spec, head_spec, pair_spec,
                   pl.BlockSpec((tm, 2 * D_MODEL), lambda i: (i, 0))),
        compiler_params=_cparams(("parallel",)),
        name="in_proj",
    )(x2, mod8, g_mix, fdec, pos, freq, *consts)


def _attn_body(bq, sq, q_ref, k_ref, v_ref, o_ref, s_scr, p_scr, acc_scr):
    qi = pl.program_id(2)
    bk = sq
    n_heads = q_ref.shape[1]
    n_sub = bq // sq
    assert n_sub % 2 == 0
    chains = [(hh, u) for hh in range(n_heads) for u in range(n_sub)]
    n_chains = len(chains)
    n_main = qi * n_sub

    def chunk_start(j):
        return pl.multiple_of(jnp.maximum(j, 0) * bk, bk)

    def scores(c, j, par):
        hh, u = chains[c]
        k = k_ref[0, hh, pl.ds(chunk_start(j), bk), :]
        q = q_ref[0, hh, u * sq:(u + 1) * sq, :]
        s = lax.dot_general(k, q, (((1,), (1,)), ((), ())), preferred_element_type=F32)
        s_scr[par, c] = s
        return jnp.max(s, axis=0, keepdims=True)

    def values(c, j, par, alpha):
        vt = v_ref[0, chains[c][0], :, pl.ds(chunk_start(j), bk)]
        acc_scr[c] = alpha * acc_scr[c] + _dot(vt, p_scr[par, c])

    def softmax(c, par, m, smax, masked):
        s = s_scr[par, c]
        if masked:
            key = lax.broadcasted_iota(I32, (bk, sq), 0)
            qry = lax.broadcasted_iota(I32, (bk, sq), 1)
            s = jnp.where(key <= qry, s, NEG_INF)
            smax = jnp.max(s, axis=0, keepdims=True)
        m_new = jnp.maximum(m, smax)
        p_scr[par, c] = jnp.exp2((s - m_new).astype(BF16))
        return m_new, jnp.exp2(m - m_new)

    def stage(j, par, state, active, has_next, masked_of):
        nxt = {c: scores(c, j + 1, 1 - par) if has_next(c) else state[c][1] for c in active}
        for c in active:
            values(c, j - 1, 1 - par, state[c][2])
        out = list(state)
        for c in active:
            m, smax, _ = state[c]
            m_new, alpha = softmax(c, par, m, smax, masked_of(c))
            out[c] = (m_new, nxt[c], alpha)
        return out

    p_scr[1] = jnp.zeros(p_scr.shape[1:], BF16)
    acc_scr[...] = jnp.zeros_like(acc_scr)
    state = [(jnp.full((1, sq), NEG_INF, F32), scores(c, 0, 0), jnp.ones((1, sq), F32)) for c in range(n_chains)]

    def step(jj, st):
        for par in range(2):
            st = stage(2 * jj + par, par, list(st), range(n_chains), lambda c: True, lambda c: False)
        return tuple(st)

    state = list(lax.fori_loop(0, n_main // 2, step, tuple(state)))

    for t in range(n_sub):
        active = [c for c, (_, u) in enumerate(chains) if u >= t]
        state = stage(n_main + t, t % 2, state, active, lambda c, t=t: chains[c][1] > t,
                      lambda c, t=t: chains[c][1] == t)

    heads = []
    for hh in range(n_heads):
        parts = []
        for u in range(n_sub):
            c = hh * n_sub + u
            values(c, n_main + u, u % 2, state[c][2])
            parts.append(acc_scr[c, :MLA_V, :] / acc_scr[c, MLA_V:MLA_V + 1, :])
        heads.append(jnp.concatenate(parts, axis=1))
    o_ref[0] = jnp.concatenate(heads, axis=0).T.astype(BF16)


def _attention(q_all, k_all, v_all, bq):
    batch, _, seq, _ = q_all.shape
    sq = min(256, bq)
    n_chains = 2 * (bq // sq)
    return pl.pallas_call(
        functools.partial(_attn_body, bq, sq),
        out_shape=jax.ShapeDtypeStruct((batch, seq, N_HEADS * MLA_V), BF16),
        grid=(batch, N_HEADS // 2, seq // bq),
        in_specs=[pl.BlockSpec((1, 2, bq, HEAD_PAD), lambda b, hp, qi: (b, hp, qi, 0)),
                  pl.BlockSpec((1, 2, seq, HEAD_PAD), lambda b, hp, qi: (b, hp, 0, 0)),
                  pl.BlockSpec((1, 2, V_ROWS, seq), lambda b, hp, qi: (b, hp, 0, 0))],
        out_specs=pl.BlockSpec((1, bq, 2 * MLA_V), lambda b, hp, qi: (b, qi, hp)),
        scratch_shapes=[pltpu.VMEM((2, n_chains, sq, sq), F32),
                        pltpu.VMEM((2, n_chains, sq, sq), BF16),
                        pltpu.VMEM((n_chains, V_ROWS, sq), F32)],
        compiler_params=_cparams(("parallel", "parallel", "arbitrary")),
        name="attention",
    )(q_all, k_all, v_all)


def _pack_halves(x):
    w = x.shape[1] // 2
    a = lax.bitcast_convert_type(x[:, :w].astype(BF16).astype(F32), U32)
    b = lax.bitcast_convert_type(x[:, w:].astype(BF16).astype(F32), U32)
    return a | (b >> 16)


def _unpack_halves(p):
    a = lax.bitcast_convert_type(p & jnp.uint32(0xFFFF0000), F32)
    b = lax.bitcast_convert_type(p << 16, F32)
    return a, b


def _out_proj_body(o_ref, sg_ref, x_ref, mod_ref, gffn_ref, womla_ref, wofox_ref, wout_ref,
                   wr2_ref, x1_ref, hp_ref, lg_ref):
    o = o_ref[...]
    half = o.shape[1] // 2
    mo = _dot(o[:, :half], womla_ref[...])
    fo = _dot(o[:, half:], wofox_ref[...])
    sg = sg_ref[...]
    merged = sg[:, :D_MODEL].astype(F32) * mo + sg[:, D_MODEL:].astype(F32) * fo
    mix = _dot(merged.astype(BF16), wout_ref[...])
    mod = mod_ref[0]
    x1 = x_ref[...] + mod[2:3, :] * mix
    x1_ref[...] = x1
    h2 = _modulated_norm(x1, gffn_ref[...], mod, 3, 4)
    hhi = h2.astype(BF16)
    hlo = (h2 - hhi.astype(F32)).astype(BF16)
    l2 = _dot(hhi, wr2_ref[...])
    lg_ref[...] = (l2[:, :LANES] + l2[:, LANES:] + _dot(hlo, wr2_ref[:, :LANES])).T
    hp_ref[...] = _pack_halves(h2)


def _out_proj(o2, sg, x2, mod8, g_ffn, w, seq, tm):
    t = x2.shape[0]
    tps = seq // tm
    consts = [w["womla"], w["wofox"], w["wout"], w["wr2"]]
    return pl.pallas_call(
        _out_proj_body,
        out_shape=(jax.ShapeDtypeStruct((t, D_MODEL), F32),
                   jax.ShapeDtypeStruct((t, D_MODEL // 2), U32),
                   jax.ShapeDtypeStruct((LANES, t), F32)),
        grid=(t // tm,),
        in_specs=[pl.BlockSpec((tm, D_MODEL), lambda i: (i, 0)),
                  pl.BlockSpec((tm, 2 * D_MODEL), lambda i: (i, 0)),
                  pl.BlockSpec((tm, D_MODEL), lambda i: (i, 0)),
                  pl.BlockSpec((1, 8, D_MODEL), lambda i: (i // tps, 0, 0)),
                  _const_spec((1, D_MODEL))] + [_const_spec(a.shape) for a in consts],
        out_specs=(pl.BlockSpec((tm, D_MODEL), lambda i: (i, 0)),
                   pl.BlockSpec((tm, D_MODEL // 2), lambda i: (i, 0)),
                   pl.BlockSpec((LANES, tm), lambda i: (0, i))),
        compiler_params=_cparams(("parallel",)),
        name="out_proj",
    )(o2, sg, x2, mod8, g_ffn, *consts)


def _route_body(lt_ref, b_ref, eidx_ref, rank_ref, gate_ref, cnt_ref, carry_ref):
    i = pl.program_id(0)

    @pl.when(i == 0)
    def _():
        carry_ref[...] = jnp.zeros_like(carry_ref)

    s = jax.nn.sigmoid(lt_ref[...])
    c = s + b_ref[...]
    tr = s.shape[1]
    sub = lax.broadcasted_iota(I32, (GROUP_SIZE, tr), 0).astype(F32)

    gs = []
    for g in range(N_GROUPS):
        cg = c[g * GROUP_SIZE:(g + 1) * GROUP_SIZE, :]
        m1 = jnp.max(cg, axis=0, keepdims=True)
        i1 = jnp.min(jnp.where(cg == m1, sub, float(GROUP_SIZE)), axis=0, keepdims=True)
        m2 = jnp.max(jnp.where(sub == i1, NEG_INF, cg), axis=0, keepdims=True)
        gs.append(m1 + m2)

    masked = []
    for g in range(N_GROUPS):
        beats = jnp.zeros_like(gs[g])
        for o in range(N_GROUPS):
            if o == g:
                continue
            better = (gs[o] >= gs[g]) if o < g else (gs[o] > gs[g])
            beats = beats + jnp.where(better, 1.0, 0.0)
        keep = beats < float(TOPK_GROUPS)
        cg = c[g * GROUP_SIZE:(g + 1) * GROUP_SIZE, :]
        masked.append(jnp.where(keep, cg, NEG_INF))
    mc = jnp.concatenate(masked, axis=0)

    eio = lax.broadcasted_iota(I32, (N_EXPERTS, tr), 0).astype(F32)
    picks = []
    selected = jnp.zeros((N_EXPERTS, tr), F32)
    for _ in range(TOP_K):
        m = jnp.max(mc, axis=0, keepdims=True)
        idx = jnp.min(jnp.where(mc == m, eio, float(N_EXPERTS)), axis=0, keepdims=True)
        hit = eio == idx
        picks.append(idx)
        selected = jnp.where(hit, 1.0, selected)
        mc = jnp.where(hit, -3.0e38, mc)

    ssum = jnp.sum(selected * s, axis=0, keepdims=True)
    gate_full = selected * s / ssum * ROUTED_SCALE

    r_io = lax.broadcasted_iota(I32, (tr, tr), 0)
    c_io = lax.broadcasted_iota(I32, (tr, tr), 1)
    upper = jnp.where(r_io < c_io, 1.0, 0.0).astype(BF16)
    before = _dot(selected.astype(BF16), upper) + carry_ref[...]
    carry_new = carry_ref[...] + jnp.sum(selected, axis=1, keepdims=True)
    carry_ref[...] = carry_new
    cnt_ref[...] = jnp.broadcast_to(carry_new, cnt_ref.shape).astype(I32)

    for r in range(SLOTS):
        if r < TOP_K:
            hit = eio == picks[r]
            eidx_ref[r:r + 1, :] = picks[r].astype(I32)
            rank_ref[r:r + 1, :] = jnp.sum(jnp.where(hit, before, 0.0), axis=0, keepdims=True).astype(I32)
            gate_ref[r:r + 1, :] = jnp.sum(jnp.where(hit, gate_full, 0.0), axis=0, keepdims=True)
        else:
            eidx_ref[r:r + 1, :] = jnp.zeros((1, tr), I32)
            rank_ref[r:r + 1, :] = jnp.zeros((1, tr), I32)
            gate_ref[r:r + 1, :] = jnp.zeros((1, tr), F32)


def _route(logits_t, bias_col, tr):
    t = logits_t.shape[1]
    slot_i = jax.ShapeDtypeStruct((SLOTS, t), I32)
    slot_spec = pl.BlockSpec((SLOTS, tr), lambda i: (0, i))
    return pl.pallas_call(
        _route_body,
        out_shape=(slot_i, slot_i, jax.ShapeDtypeStruct((SLOTS, t), F32),
                   jax.ShapeDtypeStruct((N_EXPERTS, LANES), I32)),
        grid=(t // tr,),
        in_specs=[pl.BlockSpec((N_EXPERTS, tr), lambda i: (0, i)),
                  _const_spec((N_EXPERTS, 1))],
        out_specs=(slot_spec, slot_spec, slot_spec, _const_spec((N_EXPERTS, LANES))),
        scratch_shapes=[pltpu.VMEM((N_EXPERTS, 1), F32)],
        compiler_params=_cparams(("arbitrary",)),
        name="route",
    )(logits_t, bias_col)


def _dest_body(pstart_ref, eidx_ref, rank_ref, o_ref):
    e = eidx_ref[...]
    d = rank_ref[...]
    for j in range(N_EXPERTS):
        d = d + jnp.where(e == j, pstart_ref[j], 0)
    o_ref[...] = d


def _dest(pstart, eidx, rank, tr):
    t = eidx.shape[1]
    spec = pl.BlockSpec((SLOTS, tr), lambda i, ps: (0, i))
    return pl.pallas_call(
        _dest_body,
        out_shape=jax.ShapeDtypeStruct((SLOTS, t), I32),
        grid_spec=pltpu.PrefetchScalarGridSpec(
            num_scalar_prefetch=1, grid=(t // tr,), in_specs=[spec, spec], out_specs=spec),
        compiler_params=_cparams(("parallel",)),
        name="dest",
    )(pstart, eidx, rank)


def _sc_mesh():
    return plsc.VectorSubcoreMesh(core_axis_name="c", subcore_axis_name="s")


def _sc_worker_id():
    return lax.axis_index("s") * SC_CORES + lax.axis_index("c")


def _dispatch(hpk, dest3, n_rows):
    _, w = hpk.shape
    n_chunks, _, c = dest3.shape
    per_worker = n_chunks // SC_WORKERS

    def body(h_hbm, d_hbm, xs_hbm, idx_v, rows_v, sem):
        wid = _sc_worker_id()

        @pl.loop(0, per_worker)
        def _(i):
            ch = wid * per_worker + i
            pltpu.sync_copy(d_hbm.at[ch], idx_v)
            pltpu.sync_copy(h_hbm.at[pl.ds(ch * c, c)], rows_v)
            copies = [pltpu.async_copy(rows_v, xs_hbm.at[idx_v.at[k]], sem) for k in range(TOP_K)]
            for cp in copies:
                cp.wait()

    return pl.kernel(
        body, mesh=_sc_mesh(),
        out_type=jax.ShapeDtypeStruct((n_rows, w), U32),
        scratch_types=[pltpu.VMEM((SLOTS, c), I32), pltpu.VMEM((c, w), U32), pltpu.SemaphoreType.DMA],
        name="dispatch",
    )(hpk, dest3)


def _experts_body(n_sub, be_ref, nv_ref, first_ref, nu_ref, xs_ref, wg_ref, wu_ref, wd_ref, ys_ref,
                  wgb_ref, wub_ref, wdb_ref):
    del be_ref
    i = pl.program_id(0)

    @pl.when(first_ref[i] == 1)
    def _():
        wgb_ref[...] = wg_ref[0].astype(BF16)
        wub_ref[...] = wu_ref[0].astype(BF16)
        wdb_ref[...] = wd_ref[0].astype(BF16)

    half = D_MODEL // 2
    sub = ROW_BLOCK // n_sub

    def run(n_active):
        row = lax.broadcasted_iota(I32, (sub, xs_ref.shape[1]), 0)
        gu = []
        for s in range(n_active):
            x = jnp.where(row + s * sub < nv_ref[i], xs_ref[s * sub:(s + 1) * sub, :], jnp.uint32(0))
            xa, xb = _unpack_halves(x)
            xa = xa.astype(BF16)
            xb = xb.astype(BF16)
            g = _dot(xa, wgb_ref[:half, :]) + _dot(xb, wgb_ref[half:, :])
            u = _dot(xa, wub_ref[:half, :]) + _dot(xb, wub_ref[half:, :])
            gu.append((g, u))
        for s, (g, u) in enumerate(gu):
            hb = (g * jax.nn.sigmoid(g) * u).astype(BF16)
            ys_ref[s * sub:(s + 1) * sub, :] = _pack_halves(_dot(hb, wdb_ref[...]))
        if n_active < n_sub:
            ys_ref[n_active * sub:, :] = jnp.zeros((ROW_BLOCK - n_active * sub, ys_ref.shape[1]), U32)

    used = i < nu_ref[0]
    for n_active in range(1, n_sub + 1):
        lo = (n_active - 1) * sub
        in_range = (nv_ref[i] > lo) if n_active == n_sub else ((nv_ref[i] > lo) & (nv_ref[i] <= lo + sub))
        pl.when(used & in_range)(functools.partial(run, n_active))

    @pl.when(jnp.logical_not(used))
    def _():
        ys_ref[...] = jnp.zeros_like(ys_ref)


def _experts(block_e, block_valid, block_first, n_used, xs, wg, wu, wd):
    n_rows, w = xs.shape
    n_blocks = n_rows // ROW_BLOCK

    def row_map(i, be, nv, bf, nu):
        return (jnp.minimum(i, nu[0] - 1), 0)

    def w_map(i, be, nv, bf, nu):
        return (be[i], 0, 0)

    return pl.pallas_call(
        functools.partial(_experts_body, ROW_BLOCK // MXU_ROWS),
        out_shape=jax.ShapeDtypeStruct((n_rows, w), U32),
        grid_spec=pltpu.PrefetchScalarGridSpec(
            num_scalar_prefetch=4, grid=(n_blocks,),
            in_specs=[pl.BlockSpec((ROW_BLOCK, w), row_map),
                      pl.BlockSpec((1, D_MODEL, D_EXPERT), w_map),
                      pl.BlockSpec((1, D_MODEL, D_EXPERT), w_map),
                      pl.BlockSpec((1, D_EXPERT, D_MODEL), w_map)],
            out_specs=pl.BlockSpec((ROW_BLOCK, w), lambda i, be, nv, bf, nu: (i, 0)),
            scratch_shapes=[pltpu.VMEM((D_MODEL, D_EXPERT), BF16), pltpu.VMEM((D_MODEL, D_EXPERT), BF16),
                            pltpu.VMEM((D_EXPERT, D_MODEL), BF16)]),
        compiler_params=_cparams(("arbitrary",)),
        name="experts",
    )(block_e, block_valid, block_first, n_used, xs, wg, wu, wd)


def _gather_back(ys, dest3):
    _, w = ys.shape
    n_chunks, _, c = dest3.shape
    per_worker = n_chunks // SC_WORKERS

    def body(ys_hbm, d_hbm, yk_hbm, idx_v, rows_v, gsem, osem):
        wid = _sc_worker_id()

        @pl.loop(0, per_worker)
        def _(i):
            ch = wid * per_worker + i
            pltpu.sync_copy(d_hbm.at[ch], idx_v)
            gathers = [pltpu.async_copy(ys_hbm.at[idx_v.at[k]], rows_v.at[k], gsem) for k in range(TOP_K)]
            for g in gathers:
                g.wait()
            outs = [pltpu.async_copy(rows_v.at[k], yk_hbm.at[k, pl.ds(ch * c, c)], osem) for k in range(TOP_K)]
            for o in outs:
                o.wait()

    return pl.kernel(
        body, mesh=_sc_mesh(),
        out_type=jax.ShapeDtypeStruct((TOP_K, n_chunks * c, w), U32),
        scratch_types=[pltpu.VMEM((SLOTS, c), I32), pltpu.VMEM((TOP_K, c, w), U32),
                       pltpu.SemaphoreType.DMA, pltpu.SemaphoreType.DMA],
        name="gather_back",
    )(ys, dest3)


def _combine_body(yk_ref, gate_ref, hp_ref, x1_ref, mod_ref, wsg_ref, wsu_ref, wsd_ref, gfin_ref, o_ref):
    ha, hb = _unpack_halves(hp_ref[...])
    ha = ha.astype(BF16)
    hb = hb.astype(BF16)
    half = D_MODEL // 2
    g = _dot(ha, wsg_ref[:half, :]) + _dot(hb, wsg_ref[half:, :])
    u = _dot(ha, wsu_ref[:half, :]) + _dot(hb, wsu_ref[half:, :])
    shared = _dot((g * jax.nn.sigmoid(g) * u).astype(BF16), wsd_ref[...])

    gate = gate_ref[...]
    ra = jnp.zeros(ha.shape, F32)
    rb = jnp.zeros(ha.shape, F32)
    for k in range(TOP_K):
        ya, yb = _unpack_halves(yk_ref[k])
        gk = gate[:, k:k + 1]
        ra = ra + gk * ya
        rb = rb + gk * yb
    moe = shared + jnp.concatenate([ra, rb], axis=1)
    mod = mod_ref[0]
    x2 = x1_ref[...] + mod[5:6, :] * moe
    o_ref[...] = _rms(x2) * gfin_ref[...]


def _combine(yk, gate_t, hpk, x1, mod8, w, g_final, seq, tq):
    t, wd = hpk.shape
    tps = seq // tq
    return pl.pallas_call(
        _combine_body,
        out_shape=jax.ShapeDtypeStruct((t, D_MODEL), F32),
        grid=(t // tq,),
        in_specs=[pl.BlockSpec((TOP_K, tq, wd), lambda i: (0, i, 0)),
                  pl.BlockSpec((tq, SLOTS), lambda i: (i, 0)),
                  pl.BlockSpec((tq, wd), lambda i: (i, 0)),
                  pl.BlockSpec((tq, D_MODEL), lambda i: (i, 0)),
                  pl.BlockSpec((1, 8, D_MODEL), lambda i: (i // tps, 0, 0)),
                  _const_spec(w["wsg"].shape), _const_spec(w["wsu"].shape), _const_spec(w["wsd"].shape),
                  _const_spec((1, D_MODEL))],
        out_specs=pl.BlockSpec((tq, D_MODEL), lambda i: (i, 0)),
        compiler_params=_cparams(("parallel",)),
        name="combine",
    )(yk, gate_t, hpk, x1, mod8, w["wsg"], w["wsu"], w["wsd"], g_final)


def _prep_weights(w_in, b_forget, g_q_lat, w_q_up, g_kv_lat, w_kv_up, w_o_mla, w_o_fox, w_out,
                  w_router, w_sh_gate, w_sh_up, w_sh_down):
    o_q, o_kv, o_kr = 0, MLA_Q_LORA, MLA_Q_LORA + MLA_KV_LORA
    o_fq = o_kr + MLA_ROPE
    o_fk, o_fv = o_fq + FOX_WIDTH, o_fq + 2 * FOX_WIDTH
    o_fl = o_fq + 3 * FOX_WIDTH
    o_ga = o_fl + FOX_HEADS
    o_gb = o_ga + D_MODEL
    w = {}
    w["wlat"] = w_in[:, o_q:o_kr].astype(BF16)
    half = MLA_ROPE // 2

    def rope_pair(cols):
        x1, x2 = cols[..., :half], cols[..., half:]
        z = jnp.zeros(cols.shape[:-1] + (MLA_NOPE,), cols.dtype)
        zt = jnp.zeros(cols.shape[:-1] + (HEAD_PAD - MLA_QK,), cols.dtype)
        plain = jnp.concatenate([z, x1, x2, zt], axis=-1)
        rot = jnp.concatenate([z, -x2, x1, zt], axis=-1)
        return plain, rot

    kr_plain, kr_rot = rope_pair(w_in[:, o_kr:o_fq])
    w["wkr"] = jnp.concatenate([kr_plain, kr_rot], axis=1).astype(BF16)
    w["gq"] = g_q_lat.reshape(1, -1)
    w["gkv"] = g_kv_lat.reshape(1, -1)

    scale_a = LOG2E / math.sqrt(MLA_QK)
    wq = (w_q_up * scale_a).reshape(MLA_Q_LORA, MLA_HEADS, MLA_QK)
    q_plain, q_rot = rope_pair(wq[..., MLA_NOPE:])
    nope = jnp.concatenate([wq[..., :MLA_NOPE], jnp.zeros((MLA_Q_LORA, MLA_HEADS, HEAD_PAD - MLA_NOPE), F32)], -1)
    w["wqa"] = (nope + q_plain).reshape(MLA_Q_LORA, -1).astype(BF16)
    w["wqb"] = q_rot.reshape(MLA_Q_LORA, -1).astype(BF16)

    wkv = w_kv_up.reshape(MLA_KV_LORA, MLA_HEADS, MLA_NOPE + MLA_V)
    zpad = jnp.zeros((MLA_KV_LORA, MLA_HEADS, HEAD_PAD - MLA_NOPE), F32)
    w["wka"] = jnp.concatenate([wkv[..., :MLA_NOPE], zpad], -1).reshape(MLA_KV_LORA, -1).astype(BF16)

    w["wva"] = wkv[..., MLA_NOPE:].reshape(MLA_KV_LORA, -1).astype(BF16)

    w["wfq"] = (w_in[:, o_fq:o_fk] * (LOG2E / math.sqrt(FOX_HEAD_DIM))).astype(BF16)
    w["wfk"] = w_in[:, o_fk:o_fv].astype(BF16)
    w["wfv"] = w_in[:, o_fv:o_fl].astype(BF16)

    wfl = jnp.concatenate([w_in[:, o_fl:o_ga]] * 3 + [jnp.zeros((D_MODEL, LANES - 3 * FOX_HEADS), F32)], 1)
    wfl_hi = wfl.astype(BF16)
    w["wfl2"] = jnp.concatenate([wfl_hi, (wfl - wfl_hi.astype(F32)).astype(BF16)], axis=1)
    w["bfl"] = jnp.concatenate([b_forget] * 3 + [jnp.zeros((LANES - 3 * FOX_HEADS,), F32)]).reshape(1, LANES)

    pall = np.zeros((LANES, 2 * FOX_WIDTH), np.float32)
    ones_row = 3 * FOX_HEADS
    for hd in range(FOX_HEADS):
        gq = (hd ^ 1) * FOX_HEAD_DIM
        gk = FOX_WIDTH + (hd ^ 1) * FOX_HEAD_DIM
        for term in range(3):
            pall[term * FOX_HEADS + hd, gq + FQ_COL + term] = 1.0
            pall[ones_row, gq + FK_COL + term] = 1.0
            pall[ones_row, gk + FQ_COL + term] = 1.0
            pall[term * FOX_HEADS + hd, gk + FK_COL + term] = -1.0
    w["pall"] = jnp.asarray(pall, BF16)

    w["wg"] = w_in[:, o_ga:o_gb + D_MODEL].astype(BF16)
    w["womla"] = w_o_mla.astype(BF16)
    w["wofox"] = w_o_fox.astype(BF16)
    w["wout"] = w_out.astype(BF16)
    wr = jnp.concatenate([w_router, jnp.zeros((D_MODEL, LANES - N_EXPERTS), F32)], 1)
    wr_hi = wr.astype(BF16)
    w["wr2"] = jnp.concatenate([wr_hi, (wr - wr_hi.astype(F32)).astype(BF16)], axis=1)
    w["wsg"] = w_sh_gate.astype(BF16)
    w["wsu"] = w_sh_up.astype(BF16)
    w["wsd"] = w_sh_down.astype(BF16)
    return w


def _rope_freq_row():
    half = MLA_ROPE // 2
    inv = np.power(ROPE_THETA, -np.arange(half, dtype=np.float32) / half).astype(np.float32)
    row = np.zeros((1, LANES), np.float32)
    row[0, MLA_NOPE:MLA_NOPE + half] = inv
    row[0, MLA_NOPE + half:MLA_NOPE + 2 * half] = inv
    return jnp.asarray(row)


def kernel(x, c, positions, w_mod, b_mod, g_mix_norm, w_in, b_forget, g_q_lat, w_q_up, g_kv_lat, w_kv_up,
           w_o_mla, w_o_fox, w_out, g_ffn_norm, w_router, b_router, w_exp_gate, w_exp_up, w_exp_down,
           w_sh_gate, w_sh_up, w_sh_down, g_final):
    batch, seq, d = x.shape
    assert d == D_MODEL and w_mod.shape[0] == 1
    t = batch * seq
    tm = min(512, seq)
    bq = min(2048, seq)
    tr = min(512, t)
    to = min(256, seq)
    tc = min(512, seq)
    assert seq % tm == 0 and seq % bq == 0 and t % tr == 0 and seq % tc == 0 and seq % to == 0 and batch <= 8
    assert t % (SC_CHUNK * SC_WORKERS) == 0

    def layer0(a):
        return a.reshape(a.shape[1:])

    w = _prep_weights(*(layer0(a) for a in (w_in, b_forget, g_q_lat, w_q_up, g_kv_lat, w_kv_up, w_o_mla, w_o_fox,
                                            w_out, w_router, w_sh_gate, w_sh_up, w_sh_down)))

    c8 = jnp.zeros((8, D_MODEL), F32).at[:batch].set(c)
    mod = _mod(c8, layer0(w_mod), b_mod)
    mod8 = jnp.zeros((batch, 8, D_MODEL), F32).at[:, :N_MOD].set(mod[:batch].reshape(batch, N_MOD, D_MODEL))

    x2 = x.reshape(t, D_MODEL)
    fdec = _fox_decay(x2, mod8, g_mix_norm, w["wfl2"], w["bfl"], seq, tm)
    q_all, k_all, v_all, sg = _in_proj(x2, mod8, g_mix_norm, fdec, positions.reshape(t, 1),
                                       _rope_freq_row(), w, batch, seq, tm)
    o = _attention(q_all, k_all, v_all, bq)
    x1, hpk, logits_t = _out_proj(o.reshape(t, D_MODEL), sg, x2, mod8, g_ffn_norm, w, seq, to)

    eidx, rank, gate, counts = _route(logits_t, b_router.reshape(N_EXPERTS, 1), tr)

    cnt = counts[:, 0]
    padded = (cnt + ROW_BLOCK - 1) // ROW_BLOCK * ROW_BLOCK
    pend = jnp.cumsum(padded)
    pstart = pend - padded
    n_blocks = t * TOP_K // ROW_BLOCK + N_EXPERTS
    n_rows = n_blocks * ROW_BLOCK
    block_row = jnp.arange(n_blocks, dtype=I32) * ROW_BLOCK
    block_e = jnp.minimum(jnp.sum(pend[None, :] <= block_row[:, None], axis=1), N_EXPERTS - 1).astype(I32)
    n_used = (pend[-1:] // ROW_BLOCK).astype(I32)
    block_valid = jnp.clip((pstart + cnt)[block_e] - block_row, 0, ROW_BLOCK).astype(I32)
    block_first = jnp.concatenate([jnp.ones((1,), I32), (block_e[1:] != block_e[:-1]).astype(I32)])

    dest = _dest(pstart.astype(I32), eidx, rank, min(2048, t))
    dest3 = dest.reshape(SLOTS, t // SC_CHUNK, SC_CHUNK).transpose(1, 0, 2)
    xs = _dispatch(hpk, dest3, n_rows)
    ys = _experts(block_e, block_valid, block_first, n_used, xs,
                  layer0(w_exp_gate), layer0(w_exp_up), layer0(w_exp_down))
    yk = _gather_back(ys, dest3)
    out = _combine(yk, gate.T, hpk, x1, mod8, w, g_final.reshape(1, D_MODEL), seq, tc)
    return out.reshape(batch, seq, D_MODEL)
```

```python
import functools
import math

import numpy as np
import jax
import jax.numpy as jnp
from jax import lax
from jax.experimental import pallas as pl
from jax.experimental.pallas import tpu as pltpu
from jax.experimental.pallas import tpu_sc as plsc

F32 = jnp.float32
BF16 = jnp.bfloat16
I32 = jnp.int32
U32 = jnp.uint32

D_MODEL = 1024
MLA_HEADS = 8
MLA_Q_LORA = 256
MLA_KV_LORA = 128
MLA_NOPE = 64
MLA_ROPE = 32
MLA_V = 64
MLA_QK = MLA_NOPE + MLA_ROPE
ROPE_THETA = 10000.0
FOX_HEADS = 8
FOX_HEAD_DIM = 64
FOX_WIDTH = FOX_HEADS * FOX_HEAD_DIM
N_HEADS = MLA_HEADS + FOX_HEADS
N_EXPERTS = 64
N_GROUPS = 8
GROUP_SIZE = N_EXPERTS // N_GROUPS
TOPK_GROUPS = 4
TOP_K = 6
D_EXPERT = 256
ROUTED_SCALE = 2.5
N_MOD = 6
NORM_EPS = 1e-6
NEG_INF = -1e30
LOG2E = math.log2(math.e)

LANES = 128
HEAD_PAD = 128
V_ROWS = MLA_V + 16
ROW_BLOCK = 512
MXU_ROWS = 256
SLOTS = 8
SC_CORES = 2
SC_SUBCORES = 16
SC_WORKERS = SC_CORES * SC_SUBCORES
SC_CHUNK = 32
VMEM_LIMIT = 56 * 1024 * 1024

FQ_COL = 0
FK_COL = 3


def _cparams(sem, vmem=VMEM_LIMIT):
    return pltpu.CompilerParams(dimension_semantics=sem, vmem_limit_bytes=vmem)


def _const_spec(shape):
    nd = len(shape)
    return pl.BlockSpec(shape, lambda *_: (0,) * nd)


def _rms(x):
    return x * lax.rsqrt(jnp.mean(x * x, axis=-1, keepdims=True) + NORM_EPS)


def _split3(x):
    hi = x.astype(BF16)
    r = x - hi.astype(F32)
    mid = r.astype(BF16)
    lo = (r - mid.astype(F32)).astype(BF16)
    return hi, mid, lo


def _dot(a, b):
    return jnp.dot(a, b, preferred_element_type=F32)


def _modulated_norm(x, gain, mod, shift_row, scale_row):
    shift = mod[shift_row:shift_row + 1, :]
    scale = mod[scale_row:scale_row + 1, :]
    return _rms(x) * gain * (1.0 + scale) + shift


def _mod_body(c_ref, w_ref, b_ref, o_ref):
    c = c_ref[...]
    cond = c * jax.nn.sigmoid(c)
    o_ref[...] = _dot(cond.astype(BF16), w_ref[...].astype(BF16)) + b_ref[...]


def _mod(c8, w_mod, b_mod):
    n = w_mod.shape[1]
    tn = D_MODEL
    return pl.pallas_call(
        _mod_body,
        out_shape=jax.ShapeDtypeStruct((8, n), F32),
        grid=(n // tn,),
        in_specs=[_const_spec((8, D_MODEL)),
                  pl.BlockSpec((D_MODEL, tn), lambda j: (0, j)),
                  pl.BlockSpec((1, tn), lambda j: (0, j))],
        out_specs=pl.BlockSpec((8, tn), lambda j: (0, j)),
        compiler_params=_cparams(("parallel",)),
        name="mod",
    )(c8, w_mod, b_mod)


def _decay_body(tiles_per_seq, x_ref, mod_ref, g_ref, w2_ref, b_ref, tri_ref, o_ref, carry_ref):
    i = pl.program_id(0)

    @pl.when(i % tiles_per_seq == 0)
    def _():
        carry_ref[...] = jnp.zeros_like(carry_ref)

    h = _modulated_norm(x_ref[...], g_ref[...], mod_ref[0], 0, 1)
    hhi = h.astype(BF16)
    hlo = (h - hhi.astype(F32)).astype(BF16)
    z2 = _dot(hhi, w2_ref[...])
    z = z2[:, :LANES] + z2[:, LANES:] + _dot(hlo, w2_ref[:, :LANES]) + b_ref[...]
    logf = jnp.minimum(z, 0.0) - jnp.log1p(jnp.exp(-jnp.abs(z)))
    tm = logf.shape[0]
    c3 = _dot(tri_ref[...], jnp.concatenate(_split3(logf), axis=1))
    cum = c3[:, :LANES] + c3[:, LANES:2 * LANES] + c3[:, 2 * LANES:] + carry_ref[...]
    o_ref[...] = cum
    carry_ref[...] = cum[tm - 1:tm, :]


def _fox_decay(x2, mod8, g_mix, wfl2, bfl, seq, tm):
    t = x2.shape[0]
    tps = seq // tm
    tri = jnp.asarray(np.tril(np.ones((tm, tm), np.float32)), BF16)
    return pl.pallas_call(
        functools.partial(_decay_body, tps),
        out_shape=jax.ShapeDtypeStruct((t, LANES), F32),
        grid=(t // tm,),
        in_specs=[pl.BlockSpec((tm, D_MODEL), lambda i: (i, 0)),
                  pl.BlockSpec((1, 8, D_MODEL), lambda i: (i // tps, 0, 0)),
                  _const_spec((1, D_MODEL)),
                  _const_spec((D_MODEL, 2 * LANES)),
                  _const_spec((1, LANES)),
                  _const_spec((tm, tm))],
        out_specs=pl.BlockSpec((tm, LANES), lambda i: (i, 0)),
        scratch_shapes=[pltpu.VMEM((1, LANES), F32)],
        compiler_params=_cparams(("arbitrary",)),
        name="fox_decay",
    )(x2, mod8, g_mix, wfl2, bfl, tri)


def _in_proj_body(x_ref, mod_ref, g_ref, f_ref, pos_ref, freq_ref,
                  wh_ref, gq_ref, gkv_ref, wqa_ref, wqb_ref, wka_ref, wva_ref, pall_ref,
                  q_ref, k_ref, v_ref, sg_ref):
    h = _modulated_norm(x_ref[...], g_ref[...], mod_ref[0], 0, 1)
    hb = h.astype(BF16)
    edges = np.cumsum([0, MLA_Q_LORA + MLA_KV_LORA, 2 * HEAD_PAD, FOX_WIDTH, FOX_WIDTH, FOX_WIDTH, 2 * D_MODEL])
    wlat_ref, wkr_ref, wfq_ref, wfk_ref, wfv_ref, wg_ref = (
        wh_ref.at[:, int(a):int(b)] for a, b in zip(edges[:-1], edges[1:]))

    lat = _dot(hb, wlat_ref[...])
    qn = (_rms(lat[:, :MLA_Q_LORA]) * gq_ref[...]).astype(BF16)
    kvn = (_rms(lat[:, MLA_Q_LORA:]) * gkv_ref[...]).astype(BF16)
    ang = pos_ref[...].astype(F32) * freq_ref[...]
    cs = jnp.cos(ang)
    sn = jnp.sin(ang)
    kr = _dot(hb, wkr_ref[...])
    kpe = kr[:, :HEAD_PAD] * cs + kr[:, HEAD_PAD:] * sn
    qa = _dot(qn, wqa_ref[...])
    qb = _dot(qn, wqb_ref[...])
    ka = _dot(kvn, wka_ref[...])
    va = _dot(kvn, wva_ref[...])
    for hd in range(MLA_HEADS):
        sl = slice(hd * HEAD_PAD, (hd + 1) * HEAD_PAD)
        q_ref[0, hd] = (qa[:, sl] * cs + qb[:, sl] * sn).astype(BF16)
        k_ref[0, hd] = (ka[:, sl] + kpe).astype(BF16)
    ones_rows = jnp.ones((V_ROWS - MLA_V, va.shape[0]), BF16)

    def put_values(first_head, vals):
        for hp in range(vals.shape[1] // (2 * MLA_V)):
            vt = vals[:, hp * 2 * MLA_V:(hp + 1) * 2 * MLA_V].T.astype(BF16)
            for j in range(2):
                v_ref[0, first_head + 2 * hp + j, :MLA_V, :] = vt[j * MLA_V:(j + 1) * MLA_V, :]
                v_ref[0, first_head + 2 * hp + j, MLA_V:, :] = ones_rows

    put_values(0, va)

    hi, mid, lo = _split3(f_ref[...] * LOG2E)
    lane = lax.broadcasted_iota(I32, hi.shape, 1)
    f3 = jnp.where(lane < FOX_HEADS, hi.astype(F32), jnp.where(lane < 2 * FOX_HEADS, mid.astype(F32),
                   jnp.where(lane < 3 * FOX_HEADS, lo.astype(F32), jnp.where(lane == 3 * FOX_HEADS, 1.0, 0.0))))
    aug = _dot(f3.astype(BF16), pall_ref[...]).astype(BF16)
    fq = _dot(hb, wfq_ref[...]).astype(BF16)
    fk = _dot(hb, wfk_ref[...]).astype(BF16)
    hdim = FOX_HEAD_DIM
    for hd in range(FOX_HEADS):
        own = slice((hd % 2) * hdim, (hd % 2 + 1) * hdim)
        spare = slice((1 - hd % 2) * hdim, (2 - hd % 2) * hdim)
        src = slice(hd * hdim, (hd + 1) * hdim)
        asrc = (hd ^ 1) * hdim
        q_ref[0, MLA_HEADS + hd, :, own] = fq[:, src]
        k_ref[0, MLA_HEADS + hd, :, own] = fk[:, src]
        q_ref[0, MLA_HEADS + hd, :, spare] = aug[:, asrc:asrc + hdim]
        k_ref[0, MLA_HEADS + hd, :, spare] = aug[:, FOX_WIDTH + asrc:FOX_WIDTH + asrc + hdim]
    put_values(MLA_HEADS, _dot(hb, wfv_ref[...]))

    sg_ref[...] = jax.nn.sigmoid(_dot(hb, wg_ref[...])).astype(BF16)


def _in_proj(x2, mod8, g_mix, fdec, pos, freq, w, batch, seq, tm):
    t = x2.shape[0]
    tps = seq // tm
    consts = [w["wh"], w["gq"], w["gkv"], w["wqa"], w["wqb"], w["wka"], w["wva"], w["pall"]]
    head_shape = jax.ShapeDtypeStruct((batch, N_HEADS, seq, HEAD_PAD), BF16)
    head_spec = pl.BlockSpec((1, N_HEADS, tm, HEAD_PAD), lambda i: (i // tps, 0, i % tps, 0))
    pair_shape = jax.ShapeDtypeStruct((batch, N_HEADS, V_ROWS, seq), BF16)
    pair_spec = pl.BlockSpec((1, N_HEADS, V_ROWS, tm), lambda i: (i // tps, 0, 0, i % tps))
    return pl.pallas_call(
        _in_proj_body,
        out_shape=(head_shape, head_shape, pair_shape,
                   jax.ShapeDtypeStruct((t, 2 * D_MODEL), BF16)),
        grid=(t // tm,),
        in_specs=[pl.BlockSpec((tm, D_MODEL), lambda i: (i, 0)),
                  pl.BlockSpec((1, 8, D_MODEL), lambda i: (i // tps, 0, 0)),
                  _const_spec((1, D_MODEL)),
                  pl.BlockSpec((tm, LANES), lambda i: (i, 0)),
                  pl.BlockSpec((tm, 1), lambda i: (i, 0)),
                  _const_spec((1, LANES))] + [_const_spec(a.shape) for a in consts],
        out_specs=(head_spec, head_spec, pair_spec,
                   pl.BlockSpec((tm, 2 * D_MODEL), lambda i: (i, 0))),
        compiler_params=_cparams(("parallel",)),
        name="in_proj",
    )(x2, mod8, g_mix, fdec, pos, freq, *consts)


def _attn_body(bq, sq, q_ref, k_ref, v_ref, o_ref, s_scr, p_scr, acc_scr):
    qi = pl.program_id(2)
    bk = sq
    n_heads = q_ref.shape[1]
    n_sub = bq // sq
    assert n_sub % 2 == 0
    chains = [(hh, u) for hh in range(n_heads) for u in range(n_sub)]
    n_chains = len(chains)
    n_main = qi * n_sub

    def chunk_start(j):
        return pl.multiple_of(jnp.maximum(j, 0) * bk, bk)

    def scores(c, j, par):
        hh, u = chains[c]
        k = k_ref[0, hh, pl.ds(chunk_start(j), bk), :]
        q = q_ref[0, hh, u * sq:(u + 1) * sq, :]
        s = lax.dot_general(k, q, (((1,), (1,)), ((), ())), preferred_element_type=F32)
        s_scr[par, c] = s
        return jnp.max(s, axis=0, keepdims=True)

    def values(c, j, par, alpha):
        vt = v_ref[0, chains[c][0], :, pl.ds(chunk_start(j), bk)]
        acc_scr[c] = alpha * acc_scr[c] + _dot(vt, p_scr[par, c])

    def softmax(c, par, m, smax, masked):
        s = s_scr[par, c]
        if masked:
            key = lax.broadcasted_iota(I32, (bk, sq), 0)
            qry = lax.broadcasted_iota(I32, (bk, sq), 1)
            s = jnp.where(key <= qry, s, NEG_INF)
            smax = jnp.max(s, axis=0, keepdims=True)
        m_new = jnp.maximum(m, smax)
        p_scr[par, c] = jnp.exp2((s - m_new).astype(BF16))
        return m_new, jnp.exp2(m - m_new)

    def stage(j, par, state, active, has_next, masked_of):
        nxt = {c: scores(c, j + 1, 1 - par) if has_next(c) else state[c][1] for c in active}
        for c in active:
            values(c, j - 1, 1 - par, state[c][2])
        out = list(state)
        for c in active:
            m, smax, _ = state[c]
            m_new, alpha = softmax(c, par, m, smax, masked_of(c))
            out[c] = (m_new, nxt[c], alpha)
        return out

    p_scr[1] = jnp.zeros(p_scr.shape[1:], BF16)
    acc_scr[...] = jnp.zeros_like(acc_scr)
    state = [(jnp.full((1, sq), NEG_INF, F32), scores(c, 0, 0), jnp.ones((1, sq), F32)) for c in range(n_chains)]

    def step(jj, st):
        for par in range(2):
            st = stage(2 * jj + par, par, list(st), range(n_chains), lambda c: True, lambda c: False)
        return tuple(st)

    state = list(lax.fori_loop(0, n_main // 2, step, tuple(state)))

    for t in range(n_sub):
        active = [c for c, (_, u) in enumerate(chains) if u >= t]
        state = stage(n_main + t, t % 2, state, active, lambda c, t=t: chains[c][1] > t,
                      lambda c, t=t: chains[c][1] == t)

    heads = []
    for hh in range(n_heads):
        parts = []
        for u in range(n_sub):
            c = hh * n_sub + u
            values(c, n_main + u, u % 2, state[c][2])
            parts.append(acc_scr[c, :MLA_V, :] / acc_scr[c, MLA_V:MLA_V + 1, :])
        heads.append(jnp.concatenate(parts, axis=1))
    o_ref[0] = jnp.concatenate(heads, axis=0).T.astype(BF16)


def _attention(q_all, k_all, v_all, bq):
    batch, _, seq, _ = q_all.shape
    sq = min(256, bq)
    n_chains = 2 * (bq // sq)
    return pl.pallas_call(
        functools.partial(_attn_body, bq, sq),
        out_shape=jax.ShapeDtypeStruct((batch, seq, N_HEADS * MLA_V), BF16),
        grid=(batch, N_HEADS // 2, seq // bq),
        in_specs=[pl.BlockSpec((1, 2, bq, HEAD_PAD), lambda b, hp, qi: (b, hp, qi, 0)),
                  pl.BlockSpec((1, 2, seq, HEAD_PAD), lambda b, hp, qi: (b, hp, 0, 0)),
                  pl.BlockSpec((1, 2, V_ROWS, seq), lambda b, hp, qi: (b, hp, 0, 0))],
        out_specs=pl.BlockSpec((1, bq, 2 * MLA_V), lambda b, hp, qi: (b, qi, hp)),
        scratch_shapes=[pltpu.VMEM((2, n_chains, sq, sq), F32),
                        pltpu.VMEM((2, n_chains, sq, sq), BF16),
                        pltpu.VMEM((n_chains, V_ROWS, sq), F32)],
        compiler_params=_cparams(("parallel", "parallel", "arbitrary")),
        name="attention",
    )(q_all, k_all, v_all)


def _pack_halves(x):
    w = x.shape[1] // 2
    a = lax.bitcast_convert_type(x[:, :w].astype(BF16).astype(F32), U32)
    b = lax.bitcast_convert_type(x[:, w:].astype(BF16).astype(F32), U32)
    return a | (b >> 16)


def _unpack_halves(p):
    a = lax.bitcast_convert_type(p & jnp.uint32(0xFFFF0000), F32)
    b = lax.bitcast_convert_type(p << 16, F32)
    return a, b


def _out_proj_body(o_ref, sg_ref, x_ref, mod_ref, gffn_ref, wo_ref, wr2_ref, x1_ref, hp_ref, lg_ref):
    o = o_ref[...]
    half = o.shape[1] // 2
    womla_ref, wofox_ref, wout_ref = wo_ref.at[:half, :], wo_ref.at[half:2 * half, :], wo_ref.at[2 * half:, :]
    mo = _dot(o[:, :half], womla_ref[...])
    fo = _dot(o[:, half:], wofox_ref[...])
    sg = sg_ref[...]
    merged = sg[:, :D_MODEL].astype(F32) * mo + sg[:, D_MODEL:].astype(F32) * fo
    mix = _dot(merged.astype(BF16), wout_ref[...])
    mod = mod_ref[0]
    x1 = x_ref[...] + mod[2:3, :] * mix
    x1_ref[...] = x1
    h2 = _modulated_norm(x1, gffn_ref[...], mod, 3, 4)
    hhi = h2.astype(BF16)
    hlo = (h2 - hhi.astype(F32)).astype(BF16)
    l2 = _dot(hhi, wr2_ref[...])
    lg_ref[...] = (l2[:, :LANES] + l2[:, LANES:] + _dot(hlo, wr2_ref[:, :LANES])).T
    hp_ref[...] = _pack_halves(h2)


def _out_proj(o2, sg, x2, mod8, g_ffn, w, seq, tm):
    t = x2.shape[0]
    tps = seq // tm
    consts = [w["wo"], w["wr2"]]
    return pl.pallas_call(
        _out_proj_body,
        out_shape=(jax.ShapeDtypeStruct((t, D_MODEL), F32),
                   jax.ShapeDtypeStruct((t, D_MODEL // 2), U32),
                   jax.ShapeDtypeStruct((LANES, t), F32)),
        grid=(t // tm,),
        in_specs=[pl.BlockSpec((tm, D_MODEL), lambda i: (i, 0)),
                  pl.BlockSpec((tm, 2 * D_MODEL), lambda i: (i, 0)),
                  pl.BlockSpec((tm, D_MODEL), lambda i: (i, 0)),
                  pl.BlockSpec((1, 8, D_MODEL), lambda i: (i // tps, 0, 0)),
                  _const_spec((1, D_MODEL))] + [_const_spec(a.shape) for a in consts],
        out_specs=(pl.BlockSpec((tm, D_MODEL), lambda i: (i, 0)),
                   pl.BlockSpec((tm, D_MODEL // 2), lambda i: (i, 0)),
                   pl.BlockSpec((LANES, tm), lambda i: (0, i))),
        compiler_params=_cparams(("parallel",)),
        name="out_proj",
    )(o2, sg, x2, mod8, g_ffn, *consts)


def _route_body(lt_ref, b_ref, eidx_ref, rank_ref, gate_ref, cnt_ref, carry_ref):
    i = pl.program_id(0)

    @pl.when(i == 0)
    def _():
        carry_ref[...] = jnp.zeros_like(carry_ref)

    s = jax.nn.sigmoid(lt_ref[...])
    c = s + b_ref[...]
    tr = s.shape[1]
    sub = lax.broadcasted_iota(I32, (GROUP_SIZE, tr), 0).astype(F32)

    gs = []
    for g in range(N_GROUPS):
        cg = c[g * GROUP_SIZE:(g + 1) * GROUP_SIZE, :]
        m1 = jnp.max(cg, axis=0, keepdims=True)
        i1 = jnp.min(jnp.where(cg == m1, sub, float(GROUP_SIZE)), axis=0, keepdims=True)
        m2 = jnp.max(jnp.where(sub == i1, NEG_INF, cg), axis=0, keepdims=True)
        gs.append(m1 + m2)

    masked = []
    for g in range(N_GROUPS):
        beats = jnp.zeros_like(gs[g])
        for o in range(N_GROUPS):
            if o == g:
                continue
            better = (gs[o] >= gs[g]) if o < g else (gs[o] > gs[g])
            beats = beats + jnp.where(better, 1.0, 0.0)
        keep = beats < float(TOPK_GROUPS)
        cg = c[g * GROUP_SIZE:(g + 1) * GROUP_SIZE, :]
        masked.append(jnp.where(keep, cg, NEG_INF))
    mc = jnp.concatenate(masked, axis=0)

    eio = lax.broadcasted_iota(I32, (N_EXPERTS, tr), 0).astype(F32)
    picks = []
    selected = jnp.zeros((N_EXPERTS, tr), F32)
    for _ in range(TOP_K):
        m = jnp.max(mc, axis=0, keepdims=True)
        idx = jnp.min(jnp.where(mc == m, eio, float(N_EXPERTS)), axis=0, keepdims=True)
        hit = eio == idx
        picks.append(idx)
        selected = jnp.where(hit, 1.0, selected)
        mc = jnp.where(hit, -3.0e38, mc)

    ssum = jnp.sum(selected * s, axis=0, keepdims=True)
    gate_full = selected * s / ssum * ROUTED_SCALE

    r_io = lax.broadcasted_iota(I32, (tr, tr), 0)
    c_io = lax.broadcasted_iota(I32, (tr, tr), 1)
    upper = jnp.where(r_io < c_io, 1.0, 0.0).astype(BF16)
    before = _dot(selected.astype(BF16), upper) + carry_ref[...]
    carry_new = carry_ref[...] + jnp.sum(selected, axis=1, keepdims=True)
    carry_ref[...] = carry_new
    cnt_ref[...] = jnp.broadcast_to(carry_new, cnt_ref.shape).astype(I32)

    for r in range(SLOTS):
        if r < TOP_K:
            hit = eio == picks[r]
            eidx_ref[r:r + 1, :] = picks[r].astype(I32)
            rank_ref[r:r + 1, :] = jnp.sum(jnp.where(hit, before, 0.0), axis=0, keepdims=True).astype(I32)
            gate_ref[r:r + 1, :] = jnp.sum(jnp.where(hit, gate_full, 0.0), axis=0, keepdims=True)
        else:
            eidx_ref[r:r + 1, :] = jnp.zeros((1, tr), I32)
            rank_ref[r:r + 1, :] = jnp.zeros((1, tr), I32)
            gate_ref[r:r + 1, :] = jnp.zeros((1, tr), F32)


def _route(logits_t, bias_col, tr):
    t = logits_t.shape[1]
    slot_i = jax.ShapeDtypeStruct((SLOTS, t), I32)
    slot_spec = pl.BlockSpec((SLOTS, tr), lambda i: (0, i))
    return pl.pallas_call(
        _route_body,
        out_shape=(slot_i, slot_i, jax.ShapeDtypeStruct((SLOTS, t), F32),
                   jax.ShapeDtypeStruct((N_EXPERTS, LANES), I32)),
        grid=(t // tr,),
        in_specs=[pl.BlockSpec((N_EXPERTS, tr), lambda i: (0, i)),
                  _const_spec((N_EXPERTS, 1))],
        out_specs=(slot_spec, slot_spec, slot_spec, _const_spec((N_EXPERTS, LANES))),
        scratch_shapes=[pltpu.VMEM((N_EXPERTS, 1), F32)],
        compiler_params=_cparams(("arbitrary",)),
        name="route",
    )(logits_t, bias_col)


def _dest_body(pstart_ref, eidx_ref, rank_ref, o_ref):
    e = eidx_ref[...]
    d = rank_ref[...]
    for j in range(N_EXPERTS):
        d = d + jnp.where(e == j, pstart_ref[j], 0)
    o_ref[...] = d


def _dest(pstart, eidx, rank, tr):
    t = eidx.shape[1]
    spec = pl.BlockSpec((SLOTS, tr), lambda i, ps: (0, i))
    return pl.pallas_call(
        _dest_body,
        out_shape=jax.ShapeDtypeStruct((SLOTS, t), I32),
        grid_spec=pltpu.PrefetchScalarGridSpec(
            num_scalar_prefetch=1, grid=(t // tr,), in_specs=[spec, spec], out_specs=spec),
        compiler_params=_cparams(("parallel",)),
        name="dest",
    )(pstart, eidx, rank)


def _sc_mesh():
    return plsc.VectorSubcoreMesh(core_axis_name="c", subcore_axis_name="s")


def _sc_worker_id():
    return lax.axis_index("s") * SC_CORES + lax.axis_index("c")


def _dispatch(hpk, dest3, n_rows):
    _, w = hpk.shape
    n_chunks, _, c = dest3.shape
    per_worker = n_chunks // SC_WORKERS

    def body(h_hbm, d_hbm, xs_hbm, idx_v, rows_v, sem):
        wid = _sc_worker_id()

        @pl.loop(0, per_worker)
        def _(i):
            ch = wid * per_worker + i
            pltpu.sync_copy(d_hbm.at[ch], idx_v)
            pltpu.sync_copy(h_hbm.at[pl.ds(ch * c, c)], rows_v)
            copies = [pltpu.async_copy(rows_v, xs_hbm.at[idx_v.at[k]], sem) for k in range(TOP_K)]
            for cp in copies:
                cp.wait()

    return pl.kernel(
        body, mesh=_sc_mesh(),
        out_type=jax.ShapeDtypeStruct((n_rows, w), U32),
        scratch_types=[pltpu.VMEM((SLOTS, c), I32), pltpu.VMEM((c, w), U32), pltpu.SemaphoreType.DMA],
        name="dispatch",
    )(hpk, dest3)


def _experts_body(n_sub, be_ref, nv_ref, first_ref, nu_ref, xs_ref, wg_ref, wu_ref, wd_ref, ys_ref,
                  wgb_ref, wub_ref, wdb_ref):
    del be_ref
    i = pl.program_id(0)

    @pl.when(first_ref[i] == 1)
    def _():
        wgb_ref[...] = wg_ref[0].astype(BF16)
        wub_ref[...] = wu_ref[0].astype(BF16)
        wdb_ref[...] = wd_ref[0].astype(BF16)

    half = D_MODEL // 2
    sub = ROW_BLOCK // n_sub

    def run(n_active):
        row = lax.broadcasted_iota(I32, (sub, xs_ref.shape[1]), 0)
        gu = []
        for s in range(n_active):
            x = jnp.where(row + s * sub < nv_ref[i], xs_ref[s * sub:(s + 1) * sub, :], jnp.uint32(0))
            xa, xb = _unpack_halves(x)
            xa = xa.astype(BF16)
            xb = xb.astype(BF16)
            g = _dot(xa, wgb_ref[:half, :]) + _dot(xb, wgb_ref[half:, :])
            u = _dot(xa, wub_ref[:half, :]) + _dot(xb, wub_ref[half:, :])
            gu.append((g, u))
        for s, (g, u) in enumerate(gu):
            hb = (g * jax.nn.sigmoid(g) * u).astype(BF16)
            ys_ref[s * sub:(s + 1) * sub, :] = _pack_halves(_dot(hb, wdb_ref[...]))
        if n_active < n_sub:
            ys_ref[n_active * sub:, :] = jnp.zeros((ROW_BLOCK - n_active * sub, ys_ref.shape[1]), U32)

    used = i < nu_ref[0]
    for n_active in range(1, n_sub + 1):
        lo = (n_active - 1) * sub
        in_range = (nv_ref[i] > lo) if n_active == n_sub else ((nv_ref[i] > lo) & (nv_ref[i] <= lo + sub))
        pl.when(used & in_range)(functools.partial(run, n_active))

    @pl.when(jnp.logical_not(used))
    def _():
        ys_ref[...] = jnp.zeros_like(ys_ref)


def _experts(block_e, block_valid, block_first, n_used, xs, wg, wu, wd):
    n_rows, w = xs.shape
    n_blocks = n_rows // ROW_BLOCK

    def row_map(i, be, nv, bf, nu):
        return (jnp.minimum(i, nu[0] - 1), 0)

    def w_map(i, be, nv, bf, nu):
        return (be[i], 0, 0)

    return pl.pallas_call(
        functools.partial(_experts_body, ROW_BLOCK // MXU_ROWS),
        out_shape=jax.ShapeDtypeStruct((n_rows, w), U32),
        grid_spec=pltpu.PrefetchScalarGridSpec(
            num_scalar_prefetch=4, grid=(n_blocks,),
            in_specs=[pl.BlockSpec((ROW_BLOCK, w), row_map),
                      pl.BlockSpec((1, D_MODEL, D_EXPERT), w_map),
                      pl.BlockSpec((1, D_MODEL, D_EXPERT), w_map),
                      pl.BlockSpec((1, D_EXPERT, D_MODEL), w_map)],
            out_specs=pl.BlockSpec((ROW_BLOCK, w), lambda i, be, nv, bf, nu: (i, 0)),
            scratch_shapes=[pltpu.VMEM((D_MODEL, D_EXPERT), BF16), pltpu.VMEM((D_MODEL, D_EXPERT), BF16),
                            pltpu.VMEM((D_EXPERT, D_MODEL), BF16)]),
        compiler_params=_cparams(("arbitrary",)),
        name="experts",
    )(block_e, block_valid, block_first, n_used, xs, wg, wu, wd)


def _gather_back(ys, dest3):
    _, w = ys.shape
    n_chunks, _, c = dest3.shape
    per_worker = n_chunks // SC_WORKERS

    def body(ys_hbm, d_hbm, yk_hbm, idx_v, rows_v, gsem, osem):
        wid = _sc_worker_id()

        @pl.loop(0, per_worker)
        def _(i):
            ch = wid * per_worker + i
            pltpu.sync_copy(d_hbm.at[ch], idx_v)
            gathers = [pltpu.async_copy(ys_hbm.at[idx_v.at[k]], rows_v.at[k], gsem) for k in range(TOP_K)]
            for g in gathers:
                g.wait()
            outs = [pltpu.async_copy(rows_v.at[k], yk_hbm.at[k, pl.ds(ch * c, c)], osem) for k in range(TOP_K)]
            for o in outs:
                o.wait()

    return pl.kernel(
        body, mesh=_sc_mesh(),
        out_type=jax.ShapeDtypeStruct((TOP_K, n_chunks * c, w), U32),
        scratch_types=[pltpu.VMEM((SLOTS, c), I32), pltpu.VMEM((TOP_K, c, w), U32),
                       pltpu.SemaphoreType.DMA, pltpu.SemaphoreType.DMA],
        name="gather_back",
    )(ys, dest3)


def _combine_body(yk_ref, gate_ref, hp_ref, x1_ref, mod_ref, wsg_ref, wsu_ref, wsd_ref, gfin_ref, o_ref):
    ha, hb = _unpack_halves(hp_ref[...])
    ha = ha.astype(BF16)
    hb = hb.astype(BF16)
    half = D_MODEL // 2
    g = _dot(ha, wsg_ref[:half, :]) + _dot(hb, wsg_ref[half:, :])
    u = _dot(ha, wsu_ref[:half, :]) + _dot(hb, wsu_ref[half:, :])
    shared = _dot((g * jax.nn.sigmoid(g) * u).astype(BF16), wsd_ref[...])

    gate = gate_ref[...]
    ra = jnp.zeros(ha.shape, F32)
    rb = jnp.zeros(ha.shape, F32)
    for k in range(TOP_K):
        ya, yb = _unpack_halves(yk_ref[k])
        gk = gate[:, k:k + 1]
        ra = ra + gk * ya
        rb = rb + gk * yb
    moe = shared + jnp.concatenate([ra, rb], axis=1)
    mod = mod_ref[0]
    x2 = x1_ref[...] + mod[5:6, :] * moe
    o_ref[...] = _rms(x2) * gfin_ref[...]


def _combine(yk, gate_t, hpk, x1, mod8, w, g_final, seq, tq):
    t, wd = hpk.shape
    tps = seq // tq
    return pl.pallas_call(
        _combine_body,
        out_shape=jax.ShapeDtypeStruct((t, D_MODEL), F32),
        grid=(t // tq,),
        in_specs=[pl.BlockSpec((TOP_K, tq, wd), lambda i: (0, i, 0)),
                  pl.BlockSpec((tq, SLOTS), lambda i: (i, 0)),
                  pl.BlockSpec((tq, wd), lambda i: (i, 0)),
                  pl.BlockSpec((tq, D_MODEL), lambda i: (i, 0)),
                  pl.BlockSpec((1, 8, D_MODEL), lambda i: (i // tps, 0, 0)),
                  _const_spec(w["wsg"].shape), _const_spec(w["wsu"].shape), _const_spec(w["wsd"].shape),
                  _const_spec((1, D_MODEL))],
        out_specs=pl.BlockSpec((tq, D_MODEL), lambda i: (i, 0)),
        compiler_params=_cparams(("parallel",)),
        name="combine",
    )(yk, gate_t, hpk, x1, mod8, w["wsg"], w["wsu"], w["wsd"], g_final)


def _prep_weights(w_in, b_forget, g_q_lat, w_q_up, g_kv_lat, w_kv_up, w_o_mla, w_o_fox, w_out,
                  w_router, w_sh_gate, w_sh_up, w_sh_down):
    o_q, o_kv, o_kr = 0, MLA_Q_LORA, MLA_Q_LORA + MLA_KV_LORA
    o_fq = o_kr + MLA_ROPE
    o_fk, o_fv = o_fq + FOX_WIDTH, o_fq + 2 * FOX_WIDTH
    o_fl = o_fq + 3 * FOX_WIDTH
    o_ga = o_fl + FOX_HEADS
    o_gb = o_ga + D_MODEL
    w = {}
    half = MLA_ROPE // 2

    def rope_pair(cols):
        x1, x2 = cols[..., :half], cols[..., half:]
        z = jnp.zeros(cols.shape[:-1] + (MLA_NOPE,), cols.dtype)
        zt = jnp.zeros(cols.shape[:-1] + (HEAD_PAD - MLA_QK,), cols.dtype)
        plain = jnp.concatenate([z, x1, x2, zt], axis=-1)
        rot = jnp.concatenate([z, -x2, x1, zt], axis=-1)
        return plain, rot

    kr_plain, kr_rot = rope_pair(w_in[:, o_kr:o_fq])
    w["wh"] = jnp.concatenate(
        [w_in[:, o_q:o_kr], kr_plain, kr_rot, w_in[:, o_fq:o_fk] * (LOG2E / math.sqrt(FOX_HEAD_DIM)),
         w_in[:, o_fk:o_fv], w_in[:, o_fv:o_fl], w_in[:, o_ga:o_gb + D_MODEL]], axis=1).astype(BF16)
    w["gq"] = g_q_lat.reshape(1, -1)
    w["gkv"] = g_kv_lat.reshape(1, -1)

    scale_a = LOG2E / math.sqrt(MLA_QK)
    wq = (w_q_up * scale_a).reshape(MLA_Q_LORA, MLA_HEADS, MLA_QK)
    q_plain, q_rot = rope_pair(wq[..., MLA_NOPE:])
    nope = jnp.concatenate([wq[..., :MLA_NOPE], jnp.zeros((MLA_Q_LORA, MLA_HEADS, HEAD_PAD - MLA_NOPE), F32)], -1)
    w["wqa"] = (nope + q_plain).reshape(MLA_Q_LORA, -1).astype(BF16)
    w["wqb"] = q_rot.reshape(MLA_Q_LORA, -1).astype(BF16)

    wkv = w_kv_up.reshape(MLA_KV_LORA, MLA_HEADS, MLA_NOPE + MLA_V)
    zpad = jnp.zeros((MLA_KV_LORA, MLA_HEADS, HEAD_PAD - MLA_NOPE), F32)
    w["wka"] = jnp.concatenate([wkv[..., :MLA_NOPE], zpad], -1).reshape(MLA_KV_LORA, -1).astype(BF16)

    w["wva"] = wkv[..., MLA_NOPE:].reshape(MLA_KV_LORA, -1).astype(BF16)


    wfl = jnp.concatenate([w_in[:, o_fl:o_ga]] * 3 + [jnp.zeros((D_MODEL, LANES - 3 * FOX_HEADS), F32)], 1)
    wfl_hi = wfl.astype(BF16)
    w["wfl2"] = jnp.concatenate([wfl_hi, (wfl - wfl_hi.astype(F32)).astype(BF16)], axis=1)
    w["bfl"] = jnp.concatenate([b_forget] * 3 + [jnp.zeros((LANES - 3 * FOX_HEADS,), F32)]).reshape(1, LANES)

    pall = np.zeros((LANES, 2 * FOX_WIDTH), np.float32)
    ones_row = 3 * FOX_HEADS
    for hd in range(FOX_HEADS):
        gq = (hd ^ 1) * FOX_HEAD_DIM
        gk = FOX_WIDTH + (hd ^ 1) * FOX_HEAD_DIM
        for term in range(3):
            pall[term * FOX_HEADS + hd, gq + FQ_COL + term] = 1.0
            pall[ones_row, gq + FK_COL + term] = 1.0
            pall[ones_row, gk + FQ_COL + term] = 1.0
            pall[term * FOX_HEADS + hd, gk + FK_COL + term] = -1.0
    w["pall"] = jnp.asarray(pall, BF16)

    w["wo"] = jnp.concatenate([w_o_mla, w_o_fox, w_out], axis=0).astype(BF16)
    wr = jnp.concatenate([w_router, jnp.zeros((D_MODEL, LANES - N_EXPERTS), F32)], 1)
    wr_hi = wr.astype(BF16)
    w["wr2"] = jnp.concatenate([wr_hi, (wr - wr_hi.astype(F32)).astype(BF16)], axis=1)
    w["wsg"] = w_sh_gate.astype(BF16)
    w["wsu"] = w_sh_up.astype(BF16)
    w["wsd"] = w_sh_down.astype(BF16)
    return w


def _rope_freq_row():
    half = MLA_ROPE // 2
    inv = np.power(ROPE_THETA, -np.arange(half, dtype=np.float32) / half).astype(np.float32)
    row = np.zeros((1, LANES), np.float32)
    row[0, MLA_NOPE:MLA_NOPE + half] = inv
    row[0, MLA_NOPE + half:MLA_NOPE + 2 * half] = inv
    return jnp.asarray(row)


def kernel(x, c, positions, w_mod, b_mod, g_mix_norm, w_in, b_forget, g_q_lat, w_q_up, g_kv_lat, w_kv_up,
           w_o_mla, w_o_fox, w_out, g_ffn_norm, w_router, b_router, w_exp_gate, w_exp_up, w_exp_down,
           w_sh_gate, w_sh_up, w_sh_down, g_final):
    batch, seq, d = x.shape
    assert d == D_MODEL and w_mod.shape[0] == 1
    t = batch * seq
    tm = min(512, seq)
    bq = min(2048, seq)
    tr = min(512, t)
    to = min(256, seq)
    tc = min(512, seq)
    assert seq % tm == 0 and seq % bq == 0 and t % tr == 0 and seq % tc == 0 and seq % to == 0 and batch <= 8
    assert t % (SC_CHUNK * SC_WORKERS) == 0

    def layer0(a):
        return a.reshape(a.shape[1:])

    w = _prep_weights(*(layer0(a) for a in (w_in, b_forget, g_q_lat, w_q_up, g_kv_lat, w_kv_up, w_o_mla, w_o_fox,
                                            w_out, w_router, w_sh_gate, w_sh_up, w_sh_down)))

    c8 = jnp.zeros((8, D_MODEL), F32).at[:batch].set(c)
    mod = _mod(c8, layer0(w_mod), b_mod)
    mod8 = jnp.zeros((batch, 8, D_MODEL), F32).at[:, :N_MOD].set(mod[:batch].reshape(batch, N_MOD, D_MODEL))

    x2 = x.reshape(t, D_MODEL)
    fdec = _fox_decay(x2, mod8, g_mix_norm, w["wfl2"], w["bfl"], seq, tm)
    q_all, k_all, v_all, sg = _in_proj(x2, mod8, g_mix_norm, fdec, positions.reshape(t, 1),
                                       _rope_freq_row(), w, batch, seq, tm)
    o = _attention(q_all, k_all, v_all, bq)
    x1, hpk, logits_t = _out_proj(o.reshape(t, D_MODEL), sg, x2, mod8, g_ffn_norm, w, seq, to)

    eidx, rank, gate, counts = _route(logits_t, b_router.reshape(N_EXPERTS, 1), tr)

    cnt = counts[:, 0]
    padded = (cnt + ROW_BLOCK - 1) // ROW_BLOCK * ROW_BLOCK
    pend = jnp.cumsum(padded)
    pstart = pend - padded
    n_blocks = t * TOP_K // ROW_BLOCK + N_EXPERTS
    n_rows = n_blocks * ROW_BLOCK
    block_row = jnp.arange(n_blocks, dtype=I32) * ROW_BLOCK
    block_e = jnp.minimum(jnp.sum(pend[None, :] <= block_row[:, None], axis=1), N_EXPERTS - 1).astype(I32)
    n_used = (pend[-1:] // ROW_BLOCK).astype(I32)
    block_valid = jnp.clip((pstart + cnt)[block_e] - block_row, 0, ROW_BLOCK).astype(I32)
    block_first = jnp.concatenate([jnp.ones((1,), I32), (block_e[1:] != block_e[:-1]).astype(I32)])

    dest = _dest(pstart.astype(I32), eidx, rank, min(2048, t))
    dest3 = dest.reshape(SLOTS, t // SC_CHUNK, SC_CHUNK).transpose(1, 0, 2)
    xs = _dispatch(hpk, dest3, n_rows)
    ys = _experts(block_e, block_valid, block_first, n_used, xs,
                  layer0(w_exp_gate), layer0(w_exp_up), layer0(w_exp_down))
    yk = _gather_back(ys, dest3)
    out = _combine(yk, gate.T, hpk, x1, mod8, w, g_final.reshape(1, D_MODEL), seq, tc)
    return out.reshape(batch, seq, D_MODEL)
```

```python
import functools
import math

import numpy as np
import jax
import jax.numpy as jnp
from jax import lax
from jax.experimental import pallas as pl
from jax.experimental.pallas import tpu as pltpu
from jax.experimental.pallas import tpu_sc as plsc

F32 = jnp.float32
BF16 = jnp.bfloat16
I32 = jnp.int32
U32 = jnp.uint32

D_MODEL = 1024
MLA_HEADS = 8
MLA_Q_LORA = 256
MLA_KV_LORA = 128
MLA_NOPE = 64
MLA_ROPE = 32
MLA_V = 64
MLA_QK = MLA_NOPE + MLA_ROPE
ROPE_THETA = 10000.0
FOX_HEADS = 8
FOX_HEAD_DIM = 64
FOX_WIDTH = FOX_HEADS * FOX_HEAD_DIM
N_HEADS = MLA_HEADS + FOX_HEADS
N_EXPERTS = 64
N_GROUPS = 8
GROUP_SIZE = N_EXPERTS // N_GROUPS
TOPK_GROUPS = 4
TOP_K = 6
D_EXPERT = 256
ROUTED_SCALE = 2.5
N_MOD = 6
NORM_EPS = 1e-6
NEG_INF = -1e30
LOG2E = math.log2(math.e)

LANES = 128
HEAD_PAD = 128
V_ROWS = MLA_V + 16
ROW_BLOCK = 512
MXU_ROWS = 256
SLOTS = 8
SC_CORES = 2
SC_SUBCORES = 16
SC_WORKERS = SC_CORES * SC_SUBCORES
SC_CHUNK = 32
VMEM_LIMIT = 56 * 1024 * 1024

FQ_COL = 0
FK_COL = 3


def _cparams(sem, vmem=VMEM_LIMIT):
    return pltpu.CompilerParams(dimension_semantics=sem, vmem_limit_bytes=vmem)


def _const_spec(shape):
    nd = len(shape)
    return pl.BlockSpec(shape, lambda *_: (0,) * nd)


def _rms(x):
    return x * lax.rsqrt(jnp.mean(x * x, axis=-1, keepdims=True) + NORM_EPS)


def _split3(x):
    hi = x.astype(BF16)
    r = x - hi.astype(F32)
    mid = r.astype(BF16)
    lo = (r - mid.astype(F32)).astype(BF16)
    return hi, mid, lo


def _dot(a, b):
    return jnp.dot(a, b, preferred_element_type=F32)


def _modulated_norm(x, gain, mod, shift_row, scale_row):
    shift = mod[shift_row:shift_row + 1, :]
    scale = mod[scale_row:scale_row + 1, :]
    return _rms(x) * gain * (1.0 + scale) + shift


def _mod_body(c_ref, w_ref, b_ref, o_ref):
    c = c_ref[...]
    cond = c * jax.nn.sigmoid(c)
    o_ref[...] = _dot(cond.astype(BF16), w_ref[...].astype(BF16)) + b_ref[...]


def _mod(c8, w_mod, b_mod):
    n = w_mod.shape[1]
    tn = D_MODEL
    return pl.pallas_call(
        _mod_body,
        out_shape=jax.ShapeDtypeStruct((8, n), F32),
        grid=(n // tn,),
        in_specs=[_const_spec((8, D_MODEL)),
                  pl.BlockSpec((D_MODEL, tn), lambda j: (0, j)),
                  pl.BlockSpec((1, tn), lambda j: (0, j))],
        out_specs=pl.BlockSpec((8, tn), lambda j: (0, j)),
        compiler_params=_cparams(("parallel",)),
        name="mod",
    )(c8, w_mod, b_mod)


def _decay_body(tiles_per_seq, x_ref, mod_ref, g_ref, w2_ref, b_ref, tri_ref, o_ref, carry_ref):
    i = pl.program_id(0)

    @pl.when(i % tiles_per_seq == 0)
    def _():
        carry_ref[...] = jnp.zeros_like(carry_ref)

    h = _modulated_norm(x_ref[...], g_ref[...], mod_ref[0], 0, 1)
    hhi = h.astype(BF16)
    hlo = (h - hhi.astype(F32)).astype(BF16)
    z2 = _dot(hhi, w2_ref[...])
    z = z2[:, :LANES] + z2[:, LANES:] + _dot(hlo, w2_ref[:, :LANES]) + b_ref[...]
    logf = jnp.minimum(z, 0.0) - jnp.log1p(jnp.exp(-jnp.abs(z)))
    tm = logf.shape[0]
    c3 = _dot(tri_ref[...], jnp.concatenate(_split3(logf), axis=1))
    cum = c3[:, :LANES] + c3[:, LANES:2 * LANES] + c3[:, 2 * LANES:] + carry_ref[...]
    o_ref[...] = cum
    carry_ref[...] = cum[tm - 1:tm, :]


def _fox_decay(x2, mod8, g_mix, wfl2, bfl, seq, tm):
    t = x2.shape[0]
    tps = seq // tm
    tri = jnp.asarray(np.tril(np.ones((tm, tm), np.float32)), BF16)
    return pl.pallas_call(
        functools.partial(_decay_body, tps),
        out_shape=jax.ShapeDtypeStruct((t, LANES), F32),
        grid=(t // tm,),
        in_specs=[pl.BlockSpec((tm, D_MODEL), lambda i: (i, 0)),
                  pl.BlockSpec((1, 8, D_MODEL), lambda i: (i // tps, 0, 0)),
                  _const_spec((1, D_MODEL)),
                  _const_spec((D_MODEL, 2 * LANES)),
                  _const_spec((1, LANES)),
                  _const_spec((tm, tm))],
        out_specs=pl.BlockSpec((tm, LANES), lambda i: (i, 0)),
        scratch_shapes=[pltpu.VMEM((1, LANES), F32)],
        compiler_params=_cparams(("arbitrary",)),
        name="fox_decay",
    )(x2, mod8, g_mix, wfl2, bfl, tri)


def _in_proj_body(x_ref, mod_ref, g_ref, f_ref, pos_ref, freq_ref,
                  wlat_ref, wkr_ref, gq_ref, gkv_ref, wqa_ref, wqb_ref, wka_ref, wva_ref,
                  wfq_ref, wfk_ref, wfv_ref, pall_ref, wg_ref,
                  q_ref, k_ref, v_ref, sg_ref):
    h = _modulated_norm(x_ref[...], g_ref[...], mod_ref[0], 0, 1)
    hb = h.astype(BF16)

    lat = _dot(hb, wlat_ref[...])
    qn = (_rms(lat[:, :MLA_Q_LORA]) * gq_ref[...]).astype(BF16)
    kvn = (_rms(lat[:, MLA_Q_LORA:]) * gkv_ref[...]).astype(BF16)
    ang = pos_ref[...].astype(F32) * freq_ref[...]
    cs = jnp.cos(ang)
    sn = jnp.sin(ang)
    kr = _dot(hb, wkr_ref[...])
    kpe = kr[:, :HEAD_PAD] * cs + kr[:, HEAD_PAD:] * sn
    qa = _dot(qn, wqa_ref[...])
    qb = _dot(qn, wqb_ref[...])
    ka = _dot(kvn, wka_ref[...])
    va = _dot(kvn, wva_ref[...])
    for hd in range(MLA_HEADS):
        sl = slice(hd * HEAD_PAD, (hd + 1) * HEAD_PAD)
        q_ref[0, hd] = (qa[:, sl] * cs + qb[:, sl] * sn).astype(BF16)
        k_ref[0, hd] = (ka[:, sl] + kpe).astype(BF16)
    ones_rows = jnp.ones((V_ROWS - MLA_V, va.shape[0]), BF16)

    def put_values(first_head, vals):
        for hp in range(vals.shape[1] // (2 * MLA_V)):
            vt = vals[:, hp * 2 * MLA_V:(hp + 1) * 2 * MLA_V].T.astype(BF16)
            for j in range(2):
                v_ref[0, first_head + 2 * hp + j, :MLA_V, :] = vt[j * MLA_V:(j + 1) * MLA_V, :]
                v_ref[0, first_head + 2 * hp + j, MLA_V:, :] = ones_rows

    put_values(0, va)

    hi, mid, lo = _split3(f_ref[...] * LOG2E)
    lane = lax.broadcasted_iota(I32, hi.shape, 1)
    f3 = jnp.where(lane < FOX_HEADS, hi.astype(F32), jnp.where(lane < 2 * FOX_HEADS, mid.astype(F32),
                   jnp.where(lane < 3 * FOX_HEADS, lo.astype(F32), jnp.where(lane == 3 * FOX_HEADS, 1.0, 0.0))))
    aug = _dot(f3.astype(BF16), pall_ref[...]).astype(BF16)
    fq = _dot(hb, wfq_ref[...]).astype(BF16)
    fk = _dot(hb, wfk_ref[...]).astype(BF16)
    hdim = FOX_HEAD_DIM
    for hd in range(FOX_HEADS):
        own = slice((hd % 2) * hdim, (hd % 2 + 1) * hdim)
        spare = slice((1 - hd % 2) * hdim, (2 - hd % 2) * hdim)
        src = slice(hd * hdim, (hd + 1) * hdim)
        asrc = (hd ^ 1) * hdim
        q_ref[0, MLA_HEADS + hd, :, own] = fq[:, src]
        k_ref[0, MLA_HEADS + hd, :, own] = fk[:, src]
        q_ref[0, MLA_HEADS + hd, :, spare] = aug[:, asrc:asrc + hdim]
        k_ref[0, MLA_HEADS + hd, :, spare] = aug[:, FOX_WIDTH + asrc:FOX_WIDTH + asrc + hdim]
    put_values(MLA_HEADS, _dot(hb, wfv_ref[...]))

    sg_ref[...] = jax.nn.sigmoid(_dot(hb, wg_ref[...])).astype(BF16)


def _in_proj(x2, mod8, g_mix, fdec, pos, freq, w, batch, seq, tm):
    t = x2.shape[0]
    tps = seq // tm
    consts = [w["wlat"], w["wkr"], w["gq"], w["gkv"], w["wqa"], w["wqb"], w["wka"], w["wva"],
              w["wfq"], w["wfk"], w["wfv"], w["pall"], w["wg"]]
    head_shape = jax.ShapeDtypeStruct((batch, N_HEADS, seq, HEAD_PAD), BF16)
    head_spec = pl.BlockSpec((1, N_HEADS, tm, HEAD_PAD), lambda i: (i // tps, 0, i % tps, 0))
    pair_shape = jax.ShapeDtypeStruct((batch, N_HEADS, V_ROWS, seq), BF16)
    pair_spec = pl.BlockSpec((1, N_HEADS, V_ROWS, tm), lambda i: (i // tps, 0, 0, i % tps))
    return pl.pallas_call(
        _in_proj_body,
        out_shape=(head_shape, head_shape, pair_shape,
                   jax.ShapeDtypeStruct((t, 2 * D_MODEL), BF16)),
        grid=(t // tm,),
        in_specs=[pl.BlockSpec((tm, D_MODEL), lambda i: (i, 0)),
                  pl.BlockSpec((1, 8, D_MODEL), lambda i: (i // tps, 0, 0)),
                  _const_spec((1, D_MODEL)),
                  pl.BlockSpec((tm, LANES), lambda i: (i, 0)),
                  pl.BlockSpec((tm, 1), lambda i: (i, 0)),
                  _const_spec((1, LANES))] + [_const_spec(a.shape) for a in consts],
        out_specs=(head_spec, head_spec, pair_spec,
                   pl.BlockSpec((tm, 2 * D_MODEL), lambda i: (i, 0))),
        compiler_params=_cparams(("parallel",)),
        name="in_proj",
    )(x2, mod8, g_mix, fdec, pos, freq, *consts)


def _attn_body(bq, sq, q_ref, k_ref, v_ref, o_ref, s_scr, p_scr, acc_scr):
    qi = pl.program_id(2)
    bk = sq
    n_heads = q_ref.shape[1]
    n_sub = bq // sq
    assert n_sub % 2 == 0
    chains = [(hh, u) for hh in range(n_heads) for u in range(n_sub)]
    n_chains = len(chains)
    n_main = qi * n_sub

    def chunk_start(j):
        return pl.multiple_of(jnp.maximum(j, 0) * bk, bk)

    def scores(c, j, par):
        hh, u = chains[c]
        k = k_ref[0, hh, pl.ds(chunk_start(j), bk), :]
        q = q_ref[0, hh, u * sq:(u + 1) * sq, :]
        s = lax.dot_general(k, q, (((1,), (1,)), ((), ())), preferred_element_type=F32)
        s_scr[par, c] = s
        return jnp.max(s, axis=0, keepdims=True)

    def values(c, j, par, alpha):
        vt = v_ref[0, chains[c][0], :, pl.ds(chunk_start(j), bk)]
        acc_scr[c] = alpha * acc_scr[c] + _dot(vt, p_scr[par, c])

    def softmax(c, par, m, smax, masked):
        s = s_scr[par, c]
        if masked:
            key = lax.broadcasted_iota(I32, (bk, sq), 0)
            qry = lax.broadcasted_iota(I32, (bk, sq), 1)
            s = jnp.where(key <= qry, s, NEG_INF)
            smax = jnp.max(s, axis=0, keepdims=True)
        m_new = jnp.maximum(m, smax)
        p_scr[par, c] = jnp.exp2((s - m_new).astype(BF16))
        return m_new, jnp.exp2(m - m_new)

    def stage(j, par, state, active, has_next, masked_of):
        nxt = {c: scores(c, j + 1, 1 - par) if has_next(c) else state[c][1] for c in active}
        for c in active:
            values(c, j - 1, 1 - par, state[c][2])
        out = list(state)
        for c in active:
            m, smax, _ = state[c]
            m_new, alpha = softmax(c, par, m, smax, masked_of(c))
            out[c] = (m_new, nxt[c], alpha)
        return out

    p_scr[1] = jnp.zeros(p_scr.shape[1:], BF16)
    acc_scr[...] = jnp.zeros_like(acc_scr)
    state = [(jnp.full((1, sq), NEG_INF, F32), scores(c, 0, 0), jnp.ones((1, sq), F32)) for c in range(n_chains)]

    def step(jj, st):
        for par in range(2):
            st = stage(2 * jj + par, par, list(st), range(n_chains), lambda c: True, lambda c: False)
        return tuple(st)

    state = list(lax.fori_loop(0, n_main // 2, step, tuple(state)))

    for t in range(n_sub):
        active = [c for c, (_, u) in enumerate(chains) if u >= t]
        state = stage(n_main + t, t % 2, state, active, lambda c, t=t: chains[c][1] > t,
                      lambda c, t=t: chains[c][1] == t)

    heads = []
    for hh in range(n_heads):
        parts = []
        for u in range(n_sub):
            c = hh * n_sub + u
            values(c, n_main + u, u % 2, state[c][2])
            parts.append(acc_scr[c, :MLA_V, :] / acc_scr[c, MLA_V:MLA_V + 1, :])
        heads.append(jnp.concatenate(parts, axis=1))
    o_ref[0] = jnp.concatenate(heads, axis=0).T.astype(BF16)


def _attention(q_all, k_all, v_all, bq):
    batch, _, seq, _ = q_all.shape
    sq = min(256, bq)
    n_chains = 2 * (bq // sq)
    return pl.pallas_call(
        functools.partial(_attn_body, bq, sq),
        out_shape=jax.ShapeDtypeStruct((batch, seq, N_HEADS * MLA_V), BF16),
        grid=(batch, N_HEADS // 2, seq // bq),
        in_specs=[pl.BlockSpec((1, 2, bq, HEAD_PAD), lambda b, hp, qi: (b, hp, qi, 0)),
                  pl.BlockSpec((1, 2, seq, HEAD_PAD), lambda b, hp, qi: (b, hp, 0, 0)),
                  pl.BlockSpec((1, 2, V_ROWS, seq), lambda b, hp, qi: (b, hp, 0, 0))],
        out_specs=pl.BlockSpec((1, bq, 2 * MLA_V), lambda b, hp, qi: (b, qi, hp)),
        scratch_shapes=[pltpu.VMEM((2, n_chains, sq, sq), F32),
                        pltpu.VMEM((2, n_chains, sq, sq), BF16),
                        pltpu.VMEM((n_chains, V_ROWS, sq), F32)],
        compiler_params=_cparams(("parallel", "parallel", "arbitrary")),
        name="attention",
    )(q_all, k_all, v_all)


def _pack_halves(x):
    w = x.shape[1] // 2
    a = lax.bitcast_convert_type(x[:, :w].astype(BF16).astype(F32), U32)
    b = lax.bitcast_convert_type(x[:, w:].astype(BF16).astype(F32), U32)
    return a | (b >> 16)


def _unpack_halves(p):
    a = lax.bitcast_convert_type(p & jnp.uint32(0xFFFF0000), F32)
    b = lax.bitcast_convert_type(p << 16, F32)
    return a, b


def _out_proj_body(o_ref, sg_ref, x_ref, mod_ref, gffn_ref, womla_ref, wofox_ref, wout_ref,
                   wr2_ref, x1_ref, hp_ref, lg_ref):
    o = o_ref[...]
    half = o.shape[1] // 2
    mo = _dot(o[:, :half], womla_ref[...])
    fo = _dot(o[:, half:], wofox_ref[...])
    sg = sg_ref[...]
    merged = sg[:, :D_MODEL].astype(F32) * mo + sg[:, D_MODEL:].astype(F32) * fo
    mix = _dot(merged.astype(BF16), wout_ref[...])
    mod = mod_ref[0]
    x1 = x_ref[...] + mod[2:3, :] * mix
    x1_ref[...] = x1
    h2 = _modulated_norm(x1, gffn_ref[...], mod, 3, 4)
    hhi = h2.astype(BF16)
    hlo = (h2 - hhi.astype(F32)).astype(BF16)
    l2 = _dot(hhi, wr2_ref[...])
    lg_ref[...] = (l2[:, :LANES] + l2[:, LANES:] + _dot(hlo, wr2_ref[:, :LANES])).T
    hp_ref[...] = _pack_halves(h2)


def _out_proj(o2, sg, x2, mod8, g_ffn, w, seq, tm):
    t = x2.shape[0]
    tps = seq // tm
    consts = [w["womla"], w["wofox"], w["wout"], w["wr2"]]
    return pl.pallas_call(
        _out_proj_body,
        out_shape=(jax.ShapeDtypeStruct((t, D_MODEL), F32),
                   jax.ShapeDtypeStruct((t, D_MODEL // 2), U32),
                   jax.ShapeDtypeStruct((LANES, t), F32)),
        grid=(t // tm,),
        in_specs=[pl.BlockSpec((tm, D_MODEL), lambda i: (i, 0)),
                  pl.BlockSpec((tm, 2 * D_MODEL), lambda i: (i, 0)),
                  pl.BlockSpec((tm, D_MODEL), lambda i: (i, 0)),
                  pl.BlockSpec((1, 8, D_MODEL), lambda i: (i // tps, 0, 0)),
                  _const_spec((1, D_MODEL))] + [_const_spec(a.shape) for a in consts],
        out_specs=(pl.BlockSpec((tm, D_MODEL), lambda i: (i, 0)),
                   pl.BlockSpec((tm, D_MODEL // 2), lambda i: (i, 0)),
                   pl.BlockSpec((LANES, tm), lambda i: (0, i))),
        compiler_params=_cparams(("parallel",)),
        name="out_proj",
    )(o2, sg, x2, mod8, g_ffn, *consts)


def _route_body(lt_ref, b_ref, eidx_ref, rank_ref, gate_ref, cnt_ref, carry_ref):
    i = pl.program_id(0)

    @pl.when(i == 0)
    def _():
        carry_ref[...] = jnp.zeros_like(carry_ref)

    s = jax.nn.sigmoid(lt_ref[...])
    c = s + b_ref[...]
    tr = s.shape[1]
    sub = lax.broadcasted_iota(I32, (GROUP_SIZE, tr), 0).astype(F32)

    gs = []
    for g in range(N_GROUPS):
        cg = c[g * GROUP_SIZE:(g + 1) * GROUP_SIZE, :]
        m1 = jnp.max(cg, axis=0, keepdims=True)
        i1 = jnp.min(jnp.where(cg == m1, sub, float(GROUP_SIZE)), axis=0, keepdims=True)
        m2 = jnp.max(jnp.where(sub == i1, NEG_INF, cg), axis=0, keepdims=True)
        gs.append(m1 + m2)

    masked = []
    for g in range(N_GROUPS):
        beats = jnp.zeros_like(gs[g])
        for o in range(N_GROUPS):
            if o == g:
                continue
            better = (gs[o] >= gs[g]) if o < g else (gs[o] > gs[g])
            beats = beats + jnp.where(better, 1.0, 0.0)
        keep = beats < float(TOPK_GROUPS)
        cg = c[g * GROUP_SIZE:(g + 1) * GROUP_SIZE, :]
        masked.append(jnp.where(keep, cg, NEG_INF))
    mc = jnp.concatenate(masked, axis=0)

    eio = lax.broadcasted_iota(I32, (N_EXPERTS, tr), 0).astype(F32)
    picks = []
    selected = jnp.zeros((N_EXPERTS, tr), F32)
    for _ in range(TOP_K):
        m = jnp.max(mc, axis=0, keepdims=True)
        idx = jnp.min(jnp.where(mc == m, eio, float(N_EXPERTS)), axis=0, keepdims=True)
        hit = eio == idx
        picks.append(idx)
        selected = jnp.where(hit, 1.0, selected)
        mc = jnp.where(hit, -3.0e38, mc)

    ssum = jnp.sum(selected * s, axis=0, keepdims=True)
    gate_full = selected * s / ssum * ROUTED_SCALE

    r_io = lax.broadcasted_iota(I32, (tr, tr), 0)
    c_io = lax.broadcasted_iota(I32, (tr, tr), 1)
    upper = jnp.where(r_io < c_io, 1.0, 0.0).astype(BF16)
    before = _dot(selected.astype(BF16), upper) + carry_ref[...]
    carry_new = carry_ref[...] + jnp.sum(selected, axis=1, keepdims=True)
    carry_ref[...] = carry_new
    cnt_ref[...] = jnp.broadcast_to(carry_new, cnt_ref.shape).astype(I32)

    for r in range(SLOTS):
        if r < TOP_K:
            hit = eio == picks[r]
            eidx_ref[r:r + 1, :] = picks[r].astype(I32)
            rank_ref[r:r + 1, :] = jnp.sum(jnp.where(hit, before, 0.0), axis=0, keepdims=True).astype(I32)
            gate_ref[r:r + 1, :] = jnp.sum(jnp.where(hit, gate_full, 0.0), axis=0, keepdims=True)
        else:
            eidx_ref[r:r + 1, :] = jnp.zeros((1, tr), I32)
            rank_ref[r:r + 1, :] = jnp.zeros((1, tr), I32)
            gate_ref[r:r + 1, :] = jnp.zeros((1, tr), F32)


def _route(logits_t, bias_col, tr):
    t = logits_t.shape[1]
    slot_i = jax.ShapeDtypeStruct((SLOTS, t), I32)
    slot_spec = pl.BlockSpec((SLOTS, tr), lambda i: (0, i))
    return pl.pallas_call(
        _route_body,
        out_shape=(slot_i, slot_i, jax.ShapeDtypeStruct((SLOTS, t), F32),
                   jax.ShapeDtypeStruct((N_EXPERTS, LANES), I32)),
        grid=(t // tr,),
        in_specs=[pl.BlockSpec((N_EXPERTS, tr), lambda i: (0, i)),
                  _const_spec((N_EXPERTS, 1))],
        out_specs=(slot_spec, slot_spec, slot_spec, _const_spec((N_EXPERTS, LANES))),
        scratch_shapes=[pltpu.VMEM((N_EXPERTS, 1), F32)],
        compiler_params=_cparams(("arbitrary",)),
        name="route",
    )(logits_t, bias_col)


def _dest_body(pstart_ref, eidx_ref, rank_ref, o_ref):
    e = eidx_ref[...]
    d = rank_ref[...]
    for j in range(N_EXPERTS):
        d = d + jnp.where(e == j, pstart_ref[j], 0)
    o_ref[...] = d


def _dest(pstart, eidx, rank, tr):
    t = eidx.shape[1]
    spec = pl.BlockSpec((SLOTS, tr), lambda i, ps: (0, i))
    return pl.pallas_call(
        _dest_body,
        out_shape=jax.ShapeDtypeStruct((SLOTS, t), I32),
        grid_spec=pltpu.PrefetchScalarGridSpec(
            num_scalar_prefetch=1, grid=(t // tr,), in_specs=[spec, spec], out_specs=spec),
        compiler_params=_cparams(("parallel",)),
        name="dest",
    )(pstart, eidx, rank)


def _sc_mesh():
    return plsc.VectorSubcoreMesh(core_axis_name="c", subcore_axis_name="s")


def _sc_worker_id():
    return lax.axis_index("s") * SC_CORES + lax.axis_index("c")


def _dispatch(hpk, dest3, n_rows):
    _, w = hpk.shape
    n_chunks, _, c = dest3.shape
    per_worker = n_chunks // SC_WORKERS
    assert per_worker % 2 == 0

    def body(h_hbm, d_hbm, xs_hbm, idx_v, rows_v, lsem, ssem):
        base = _sc_worker_id() * per_worker

        def loads(ch, slot):
            return (pltpu.make_async_copy(d_hbm.at[ch], idx_v.at[slot], lsem.at[slot]),
                    pltpu.make_async_copy(h_hbm.at[pl.ds(ch * c, c)], rows_v.at[slot], lsem.at[slot]))

        def scatters(slot):
            return [pltpu.make_async_copy(rows_v.at[slot], xs_hbm.at[idx_v.at[slot, k]], ssem.at[slot])
                    for k in range(TOP_K)]

        for cp in loads(base, 0):
            cp.start()

        @pl.loop(0, per_worker, step=2)
        def _(i0):
            for slot in range(2):
                i = i0 + slot
                for cp in loads(base + i, slot):
                    cp.wait()
                for cp in scatters(slot):
                    cp.start()

                @pl.when(i > 0)
                def _():
                    for cp in scatters(1 - slot):
                        cp.wait()

                @pl.when(i + 1 < per_worker)
                def _():
                    for cp in loads(base + i + 1, 1 - slot):
                        cp.start()

        for cp in scatters(1):
            cp.wait()

    return pl.kernel(
        body, mesh=_sc_mesh(),
        out_type=jax.ShapeDtypeStruct((n_rows, w), U32),
        scratch_types=[pltpu.VMEM((2, SLOTS, c), I32), pltpu.VMEM((2, c, w), U32),
                       pltpu.SemaphoreType.DMA((2,)), pltpu.SemaphoreType.DMA((2,))],
        name="dispatch",
    )(hpk, dest3)


def _experts_body(n_sub, be_ref, nv_ref, first_ref, nu_ref, xs_ref, wg_ref, wu_ref, wd_ref, ys_ref,
                  wgb_ref, wub_ref, wdb_ref):
    del be_ref
    i = pl.program_id(0)

    @pl.when(first_ref[i] == 1)
    def _():
        wgb_ref[...] = wg_ref[0].astype(BF16)
        wub_ref[...] = wu_ref[0].astype(BF16)
        wdb_ref[...] = wd_ref[0].astype(BF16)

    half = D_MODEL // 2
    sub = ROW_BLOCK // n_sub

    def run(n_active):
        row = lax.broadcasted_iota(I32, (sub, xs_ref.shape[1]), 0)
        gu = []
        for s in range(n_active):
            x = jnp.where(row + s * sub < nv_ref[i], xs_ref[s * sub:(s + 1) * sub, :], jnp.uint32(0))
            xa, xb = _unpack_halves(x)
            xa = xa.astype(BF16)
            xb = xb.astype(BF16)
            g = _dot(xa, wgb_ref[:half, :]) + _dot(xb, wgb_ref[half:, :])
            u = _dot(xa, wub_ref[:half, :]) + _dot(xb, wub_ref[half:, :])
            gu.append((g, u))
        for s, (g, u) in enumerate(gu):
            hb = (g * jax.nn.sigmoid(g) * u).astype(BF16)
            ys_ref[s * sub:(s + 1) * sub, :] = _pack_halves(_dot(hb, wdb_ref[...]))
        if n_active < n_sub:
            ys_ref[n_active * sub:, :] = jnp.zeros((ROW_BLOCK - n_active * sub, ys_ref.shape[1]), U32)

    used = i < nu_ref[0]
    for n_active in range(1, n_sub + 1):
        lo = (n_active - 1) * sub
        in_range = (nv_ref[i] > lo) if n_active == n_sub else ((nv_ref[i] > lo) & (nv_ref[i] <= lo + sub))
        pl.when(used & in_range)(functools.partial(run, n_active))

    @pl.when(jnp.logical_not(used))
    def _():
        ys_ref[...] = jnp.zeros_like(ys_ref)


def _experts(block_e, block_valid, block_first, n_used, xs, wg, wu, wd):
    n_rows, w = xs.shape
    n_blocks = n_rows // ROW_BLOCK

    def row_map(i, be, nv, bf, nu):
        return (jnp.minimum(i, nu[0] - 1), 0)

    def w_map(i, be, nv, bf, nu):
        return (be[i], 0, 0)

    return pl.pallas_call(
        functools.partial(_experts_body, ROW_BLOCK // MXU_ROWS),
        out_shape=jax.ShapeDtypeStruct((n_rows, w), U32),
        grid_spec=pltpu.PrefetchScalarGridSpec(
            num_scalar_prefetch=4, grid=(n_blocks,),
            in_specs=[pl.BlockSpec((ROW_BLOCK, w), row_map),
                      pl.BlockSpec((1, D_MODEL, D_EXPERT), w_map),
                      pl.BlockSpec((1, D_MODEL, D_EXPERT), w_map),
                      pl.BlockSpec((1, D_EXPERT, D_MODEL), w_map)],
            out_specs=pl.BlockSpec((ROW_BLOCK, w), lambda i, be, nv, bf, nu: (i, 0)),
            scratch_shapes=[pltpu.VMEM((D_MODEL, D_EXPERT), BF16), pltpu.VMEM((D_MODEL, D_EXPERT), BF16),
                            pltpu.VMEM((D_EXPERT, D_MODEL), BF16)]),
        compiler_params=_cparams(("arbitrary",)),
        name="experts",
    )(block_e, block_valid, block_first, n_used, xs, wg, wu, wd)


def _gather_back(ys, dest3):
    _, w = ys.shape
    n_chunks, _, c = dest3.shape
    per_worker = n_chunks // SC_WORKERS
    assert per_worker % 2 == 0 and TOP_K % 2 == 0
    hk = TOP_K // 2

    def body(ys_hbm, d_hbm, yk_hbm, idx_v, rows_v, isem, gsem, osem):
        base = _sc_worker_id() * per_worker

        def idx_load(ch, par):
            return pltpu.make_async_copy(d_hbm.at[ch], idx_v.at[par], isem.at[par])

        def gathers(par, g):
            return [pltpu.make_async_copy(ys_hbm.at[idx_v.at[par, g * hk + j]], rows_v.at[g, j], gsem.at[g])
                    for j in range(hk)]

        def outs(ch, g):
            return [pltpu.make_async_copy(rows_v.at[g, j], yk_hbm.at[g * hk + j, pl.ds(ch * c, c)], osem.at[g])
                    for j in range(hk)]

        idx_load(base, 0).start()

        @pl.loop(0, per_worker, step=2)
        def _(i0):
            for par in range(2):
                i = i0 + par
                ch = base + i
                idx_load(ch, par).wait()

                @pl.when(i + 1 < per_worker)
                def _():
                    idx_load(ch + 1, 1 - par).start()

                for g in range(2):
                    @pl.when(i > 0)
                    def _():
                        for cp in outs(ch - 1, g):
                            cp.wait()
                    for cp in gathers(par, g):
                        cp.start()
                for g in range(2):
                    for cp in gathers(par, g):
                        cp.wait()
                    for cp in outs(ch, g):
                        cp.start()

        for g in range(2):
            for cp in outs(base + per_worker - 1, g):
                cp.wait()

    return pl.kernel(
        body, mesh=_sc_mesh(),
        out_type=jax.ShapeDtypeStruct((TOP_K, n_chunks * c, w), U32),
        scratch_types=[pltpu.VMEM((2, SLOTS, c), I32), pltpu.VMEM((2, hk, c, w), U32),
                       pltpu.SemaphoreType.DMA((2,)), pltpu.SemaphoreType.DMA((2,)), pltpu.SemaphoreType.DMA((2,))],
        name="gather_back",
    )(ys, dest3)


def _combine_body(yk_ref, gate_ref, hp_ref, x1_ref, mod_ref, wsg_ref, wsu_ref, wsd_ref, gfin_ref, o_ref):
    ha, hb = _unpack_halves(hp_ref[...])
    ha = ha.astype(BF16)
    hb = hb.astype(BF16)
    half = D_MODEL // 2
    g = _dot(ha, wsg_ref[:half, :]) + _dot(hb, wsg_ref[half:, :])
    u = _dot(ha, wsu_ref[:half, :]) + _dot(hb, wsu_ref[half:, :])
    shared = _dot((g * jax.nn.sigmoid(g) * u).astype(BF16), wsd_ref[...])

    gate = gate_ref[...]
    ra = jnp.zeros(ha.shape, F32)
    rb = jnp.zeros(ha.shape, F32)
    for k in range(TOP_K):
        ya, yb = _unpack_halves(yk_ref[k])
        gk = gate[:, k:k + 1]
        ra = ra + gk * ya
        rb = rb + gk * yb
    moe = shared + jnp.concatenate([ra, rb], axis=1)
    mod = mod_ref[0]
    x2 = x1_ref[...] + mod[5:6, :] * moe
    o_ref[...] = _rms(x2) * gfin_ref[...]


def _combine(yk, gate_t, hpk, x1, mod8, w, g_final, seq, tq):
    t, wd = hpk.shape
    tps = seq // tq
    return pl.pallas_call(
        _combine_body,
        out_shape=jax.ShapeDtypeStruct((t, D_MODEL), F32),
        grid=(t // tq,),
        in_specs=[pl.BlockSpec((TOP_K, tq, wd), lambda i: (0, i, 0)),
                  pl.BlockSpec((tq, SLOTS), lambda i: (i, 0)),
                  pl.BlockSpec((tq, wd), lambda i: (i, 0)),
                  pl.BlockSpec((tq, D_MODEL), lambda i: (i, 0)),
                  pl.BlockSpec((1, 8, D_MODEL), lambda i: (i // tps, 0, 0)),
                  _const_spec(w["wsg"].shape), _const_spec(w["wsu"].shape), _const_spec(w["wsd"].shape),
                  _const_spec((1, D_MODEL))],
        out_specs=pl.BlockSpec((tq, D_MODEL), lambda i: (i, 0)),
        compiler_params=_cparams(("parallel",)),
        name="combine",
    )(yk, gate_t, hpk, x1, mod8, w["wsg"], w["wsu"], w["wsd"], g_final)


def _prep_weights(w_in, b_forget, g_q_lat, w_q_up, g_kv_lat, w_kv_up, w_o_mla, w_o_fox, w_out,
                  w_router, w_sh_gate, w_sh_up, w_sh_down):
    o_q, o_kv, o_kr = 0, MLA_Q_LORA, MLA_Q_LORA + MLA_KV_LORA
    o_fq = o_kr + MLA_ROPE
    o_fk, o_fv = o_fq + FOX_WIDTH, o_fq + 2 * FOX_WIDTH
    o_fl = o_fq + 3 * FOX_WIDTH
    o_ga = o_fl + FOX_HEADS
    o_gb = o_ga + D_MODEL
    w = {}
    w["wlat"] = w_in[:, o_q:o_kr].astype(BF16)
    half = MLA_ROPE // 2

    def rope_pair(cols):
        x1, x2 = cols[..., :half], cols[..., half:]
        z = jnp.zeros(cols.shape[:-1] + (MLA_NOPE,), cols.dtype)
        zt = jnp.zeros(cols.shape[:-1] + (HEAD_PAD - MLA_QK,), cols.dtype)
        plain = jnp.concatenate([z, x1, x2, zt], axis=-1)
        rot = jnp.concatenate([z, -x2, x1, zt], axis=-1)
        return plain, rot

    kr_plain, kr_rot = rope_pair(w_in[:, o_kr:o_fq])
    w["wkr"] = jnp.concatenate([kr_plain, kr_rot], axis=1).astype(BF16)
    w["gq"] = g_q_lat.reshape(1, -1)
    w["gkv"] = g_kv_lat.reshape(1, -1)

    scale_a = LOG2E / math.sqrt(MLA_QK)
    wq = (w_q_up * scale_a).reshape(MLA_Q_LORA, MLA_HEADS, MLA_QK)
    q_plain, q_rot = rope_pair(wq[..., MLA_NOPE:])
    nope = jnp.concatenate([wq[..., :MLA_NOPE], jnp.zeros((MLA_Q_LORA, MLA_HEADS, HEAD_PAD - MLA_NOPE), F32)], -1)
    w["wqa"] = (nope + q_plain).reshape(MLA_Q_LORA, -1).astype(BF16)
    w["wqb"] = q_rot.reshape(MLA_Q_LORA, -1).astype(BF16)

    wkv = w_kv_up.reshape(MLA_KV_LORA, MLA_HEADS, MLA_NOPE + MLA_V)
    zpad = jnp.zeros((MLA_KV_LORA, MLA_HEADS, HEAD_PAD - MLA_NOPE), F32)
    w["wka"] = jnp.concatenate([wkv[..., :MLA_NOPE], zpad], -1).reshape(MLA_KV_LORA, -1).astype(BF16)

    w["wva"] = wkv[..., MLA_NOPE:].reshape(MLA_KV_LORA, -1).astype(BF16)

    w["wfq"] = (w_in[:, o_fq:o_fk] * (LOG2E / math.sqrt(FOX_HEAD_DIM))).astype(BF16)
    w["wfk"] = w_in[:, o_fk:o_fv].astype(BF16)
    w["wfv"] = w_in[:, o_fv:o_fl].astype(BF16)

    wfl = jnp.concatenate([w_in[:, o_fl:o_ga]] * 3 + [jnp.zeros((D_MODEL, LANES - 3 * FOX_HEADS), F32)], 1)
    wfl_hi = wfl.astype(BF16)
    w["wfl2"] = jnp.concatenate([wfl_hi, (wfl - wfl_hi.astype(F32)).astype(BF16)], axis=1)
    w["bfl"] = jnp.concatenate([b_forget] * 3 + [jnp.zeros((LANES - 3 * FOX_HEADS,), F32)]).reshape(1, LANES)

    pall = np.zeros((LANES, 2 * FOX_WIDTH), np.float32)
    ones_row = 3 * FOX_HEADS
    for hd in range(FOX_HEADS):
        gq = (hd ^ 1) * FOX_HEAD_DIM
        gk = FOX_WIDTH + (hd ^ 1) * FOX_HEAD_DIM
        for term in range(3):
            pall[term * FOX_HEADS + hd, gq + FQ_COL + term] = 1.0
            pall[ones_row, gq + FK_COL + term] = 1.0
            pall[ones_row, gk + FQ_COL + term] = 1.0
            pall[term * FOX_HEADS + hd, gk + FK_COL + term] = -1.0
    w["pall"] = jnp.asarray(pall, BF16)

    w["wg"] = w_in[:, o_ga:o_gb + D_MODEL].astype(BF16)
    w["womla"] = w_o_mla.astype(BF16)
    w["wofox"] = w_o_fox.astype(BF16)
    w["wout"] = w_out.astype(BF16)
    wr = jnp.concatenate([w_router, jnp.zeros((D_MODEL, LANES - N_EXPERTS), F32)], 1)
    wr_hi = wr.astype(BF16)
    w["wr2"] = jnp.concatenate([wr_hi, (wr - wr_hi.astype(F32)).astype(BF16)], axis=1)
    w["wsg"] = w_sh_gate.astype(BF16)
    w["wsu"] = w_sh_up.astype(BF16)
    w["wsd"] = w_sh_down.astype(BF16)
    return w


def _rope_freq_row():
    half = MLA_ROPE // 2
    inv = np.power(ROPE_THETA, -np.arange(half, dtype=np.float32) / half).astype(np.float32)
    row = np.zeros((1, LANES), np.float32)
    row[0, MLA_NOPE:MLA_NOPE + half] = inv
    row[0, MLA_NOPE + half:MLA_NOPE + 2 * half] = inv
    return jnp.asarray(row)


def kernel(x, c, positions, w_mod, b_mod, g_mix_norm, w_in, b_forget, g_q_lat, w_q_up, g_kv_lat, w_kv_up,
           w_o_mla, w_o_fox, w_out, g_ffn_norm, w_router, b_router, w_exp_gate, w_exp_up, w_exp_down,
           w_sh_gate, w_sh_up, w_sh_down, g_final):
    batch, seq, d = x.shape
    assert d == D_MODEL and w_mod.shape[0] == 1
    t = batch * seq
    tm = min(512, seq)
    bq = min(2048, seq)
    tr = min(512, t)
    to = min(256, seq)
    tc = min(512, seq)
    assert seq % tm == 0 and seq % bq == 0 and t % tr == 0 and seq % tc == 0 and seq % to == 0 and batch <= 8
    assert t % (2 * SC_CHUNK * SC_WORKERS) == 0

    def layer0(a):
        return a.reshape(a.shape[1:])

    w = _prep_weights(*(layer0(a) for a in (w_in, b_forget, g_q_lat, w_q_up, g_kv_lat, w_kv_up, w_o_mla, w_o_fox,
                                            w_out, w_router, w_sh_gate, w_sh_up, w_sh_down)))

    c8 = jnp.zeros((8, D_MODEL), F32).at[:batch].set(c)
    mod = _mod(c8, layer0(w_mod), b_mod)
    mod8 = jnp.zeros((batch, 8, D_MODEL), F32).at[:, :N_MOD].set(mod[:batch].reshape(batch, N_MOD, D_MODEL))

    x2 = x.reshape(t, D_MODEL)
    fdec = _fox_decay(x2, mod8, g_mix_norm, w["wfl2"], w["bfl"], seq, tm)
    q_all, k_all, v_all, sg = _in_proj(x2, mod8, g_mix_norm, fdec, positions.reshape(t, 1),
                                       _rope_freq_row(), w, batch, seq, tm)
    o = _attention(q_all, k_all, v_all, bq)
    x1, hpk, logits_t = _out_proj(o.reshape(t, D_MODEL), sg, x2, mod8, g_ffn_norm, w, seq, to)

    eidx, rank, gate, counts = _route(logits_t, b_router.reshape(N_EXPERTS, 1), tr)

    cnt = counts[:, 0]
    padded = (cnt + ROW_BLOCK - 1) // ROW_BLOCK * ROW_BLOCK
    pend = jnp.cumsum(padded)
    pstart = pend - padded
    n_blocks = t * TOP_K // ROW_BLOCK + N_EXPERTS
    n_rows = n_blocks * ROW_BLOCK
    block_row = jnp.arange(n_blocks, dtype=I32) * ROW_BLOCK
    block_e = jnp.minimum(jnp.sum(pend[None, :] <= block_row[:, None], axis=1), N_EXPERTS - 1).astype(I32)
    n_used = (pend[-1:] // ROW_BLOCK).astype(I32)
    block_valid = jnp.clip((pstart + cnt)[block_e] - block_row, 0, ROW_BLOCK).astype(I32)
    block_first = jnp.concatenate([jnp.ones((1,), I32), (block_e[1:] != block_e[:-1]).astype(I32)])

    dest = _dest(pstart.astype(I32), eidx, rank, min(2048, t))
    dest3 = dest.reshape(SLOTS, t // SC_CHUNK, SC_CHUNK).transpose(1, 0, 2)
    xs = _dispatch(hpk, dest3, n_rows)
    ys = _experts(block_e, block_valid, block_first, n_used, xs,
                  layer0(w_exp_gate), layer0(w_exp_up), layer0(w_exp_down))
    yk = _gather_back(ys, dest3)
    out = _combine(yk, gate.T, hpk, x1, mod8, w, g_final.reshape(1, D_MODEL), seq, tc)
    return out.reshape(batch, seq, D_MODEL)
```

```python
import functools
import math

import numpy as np
import jax
import jax.numpy as jnp
from jax import lax
from jax.experimental import pallas as pl
from jax.experimental.pallas import tpu as pltpu
from jax.experimental.pallas import tpu_sc as plsc

F32 = jnp.float32
BF16 = jnp.bfloat16
I32 = jnp.int32
U32 = jnp.uint32

D_MODEL = 1024
MLA_HEADS = 8
MLA_Q_LORA = 256
MLA_KV_LORA = 128
MLA_NOPE = 64
MLA_ROPE = 32
MLA_V = 64
MLA_QK = MLA_NOPE + MLA_ROPE
ROPE_THETA = 10000.0
FOX_HEADS = 8
FOX_HEAD_DIM = 64
FOX_WIDTH = FOX_HEADS * FOX_HEAD_DIM
N_HEADS = MLA_HEADS + FOX_HEADS
N_EXPERTS = 64
N_GROUPS = 8
GROUP_SIZE = N_EXPERTS // N_GROUPS
TOPK_GROUPS = 4
TOP_K = 6
D_EXPERT = 256
ROUTED_SCALE = 2.5
N_MOD = 6
NORM_EPS = 1e-6
NEG_INF = -1e30
LOG2E = math.log2(math.e)

LANES = 128
HEAD_PAD = 128
V_ROWS = MLA_V + 16
ROW_BLOCK = 512
MXU_ROWS = 256
SLOTS = 8
SC_CORES = 2
SC_SUBCORES = 16
SC_WORKERS = SC_CORES * SC_SUBCORES
SC_CHUNK = 32
VMEM_LIMIT = 56 * 1024 * 1024

FQ_COL = 0
FK_COL = 3


def _cparams(sem, vmem=VMEM_LIMIT):
    return pltpu.CompilerParams(dimension_semantics=sem, vmem_limit_bytes=vmem)


def _const_spec(shape):
    nd = len(shape)
    return pl.BlockSpec(shape, lambda *_: (0,) * nd, pipeline_mode=pl.Buffered(1))


def _rms(x):
    return x * lax.rsqrt(jnp.mean(x * x, axis=-1, keepdims=True) + NORM_EPS)


def _split3(x):
    hi = x.astype(BF16)
    r = x - hi.astype(F32)
    mid = r.astype(BF16)
    lo = (r - mid.astype(F32)).astype(BF16)
    return hi, mid, lo


def _dot(a, b):
    return jnp.dot(a, b, preferred_element_type=F32)


def _modulated_norm(x, gain, mod, shift_row, scale_row):
    shift = mod[shift_row:shift_row + 1, :]
    scale = mod[scale_row:scale_row + 1, :]
    return _rms(x) * gain * (1.0 + scale) + shift


def _mod_body(c_ref, w_ref, b_ref, o_ref):
    c = c_ref[...]
    cond = c * jax.nn.sigmoid(c)
    o_ref[...] = _dot(cond.astype(BF16), w_ref[...].astype(BF16)) + b_ref[...]


def _mod(c8, w_mod, b_mod):
    n = w_mod.shape[1]
    tn = D_MODEL
    return pl.pallas_call(
        _mod_body,
        out_shape=jax.ShapeDtypeStruct((8, n), F32),
        grid=(n // tn,),
        in_specs=[_const_spec((8, D_MODEL)),
                  pl.BlockSpec((D_MODEL, tn), lambda j: (0, j)),
                  pl.BlockSpec((1, tn), lambda j: (0, j))],
        out_specs=pl.BlockSpec((8, tn), lambda j: (0, j)),
        compiler_params=_cparams(("parallel",)),
        name="mod",
    )(c8, w_mod, b_mod)


def _decay_body(tiles_per_seq, x_ref, mod_ref, g_ref, w2_ref, b_ref, tri_ref, o_ref, carry_ref):
    i = pl.program_id(0)

    @pl.when(i % tiles_per_seq == 0)
    def _():
        carry_ref[...] = jnp.zeros_like(carry_ref)

    h = _modulated_norm(x_ref[...], g_ref[...], mod_ref[0], 0, 1)
    hhi = h.astype(BF16)
    hlo = (h - hhi.astype(F32)).astype(BF16)
    z2 = _dot(hhi, w2_ref[...])
    z = z2[:, :LANES] + z2[:, LANES:] + _dot(hlo, w2_ref[:, :LANES]) + b_ref[...]
    logf = jnp.minimum(z, 0.0) - jnp.log1p(jnp.exp(-jnp.abs(z)))
    tm = logf.shape[0]
    c3 = _dot(tri_ref[...], jnp.concatenate(_split3(logf), axis=1))
    cum = c3[:, :LANES] + c3[:, LANES:2 * LANES] + c3[:, 2 * LANES:] + carry_ref[...]
    o_ref[...] = cum
    carry_ref[...] = cum[tm - 1:tm, :]


def _fox_decay(x2, mod8, g_mix, wfl2, bfl, seq, tm):
    t = x2.shape[0]
    tps = seq // tm
    tri = jnp.asarray(np.tril(np.ones((tm, tm), np.float32)), BF16)
    return pl.pallas_call(
        functools.partial(_decay_body, tps),
        out_shape=jax.ShapeDtypeStruct((t, LANES), F32),
        grid=(t // tm,),
        in_specs=[pl.BlockSpec((tm, D_MODEL), lambda i: (i, 0)),
                  pl.BlockSpec((1, 8, D_MODEL), lambda i: (i // tps, 0, 0)),
                  _const_spec((1, D_MODEL)),
                  _const_spec((D_MODEL, 2 * LANES)),
                  _const_spec((1, LANES)),
                  _const_spec((tm, tm))],
        out_specs=pl.BlockSpec((tm, LANES), lambda i: (i, 0)),
        scratch_shapes=[pltpu.VMEM((1, LANES), F32)],
        compiler_params=_cparams(("arbitrary",)),
        name="fox_decay",
    )(x2, mod8, g_mix, wfl2, bfl, tri)


def _in_proj_body(x_ref, mod_ref, g_ref, f_ref, pos_ref, freq_ref,
                  wlat_ref, wkr_ref, gq_ref, gkv_ref, wqa_ref, wqb_ref, wka_ref, wva_ref,
                  wfq_ref, wfk_ref, wfv_ref, pall_ref, wg_ref,
                  q_ref, k_ref, v_ref, sg_ref):
    h = _modulated_norm(x_ref[...], g_ref[...], mod_ref[0], 0, 1)
    hb = h.astype(BF16)

    lat = _dot(hb, wlat_ref[...])
    qn = (_rms(lat[:, :MLA_Q_LORA]) * gq_ref[...]).astype(BF16)
    kvn = (_rms(lat[:, MLA_Q_LORA:]) * gkv_ref[...]).astype(BF16)
    ang = pos_ref[...].astype(F32) * freq_ref[...]
    cs = jnp.cos(ang)
    sn = jnp.sin(ang)
    kr = _dot(hb, wkr_ref[...])
    kpe = kr[:, :HEAD_PAD] * cs + kr[:, HEAD_PAD:] * sn
    qa = _dot(qn, wqa_ref[...])
    qb = _dot(qn, wqb_ref[...])
    ka = _dot(kvn, wka_ref[...])
    va = _dot(kvn, wva_ref[...])
    for hd in range(MLA_HEADS):
        sl = slice(hd * HEAD_PAD, (hd + 1) * HEAD_PAD)
        q_ref[0, hd] = (qa[:, sl] * cs + qb[:, sl] * sn).astype(BF16)
        k_ref[0, hd] = (ka[:, sl] + kpe).astype(BF16)
    ones_rows = jnp.ones((V_ROWS - MLA_V, va.shape[0]), BF16)

    def put_values(first_head, vals):
        for hp in range(vals.shape[1] // (2 * MLA_V)):
            vt = vals[:, hp * 2 * MLA_V:(hp + 1) * 2 * MLA_V].T.astype(BF16)
            for j in range(2):
                v_ref[0, first_head + 2 * hp + j, :MLA_V, :] = vt[j * MLA_V:(j + 1) * MLA_V, :]
                v_ref[0, first_head + 2 * hp + j, MLA_V:, :] = ones_rows

    put_values(0, va)

    hi, mid, lo = _split3(f_ref[...] * LOG2E)
    lane = lax.broadcasted_iota(I32, hi.shape, 1)
    f3 = jnp.where(lane < FOX_HEADS, hi.astype(F32), jnp.where(lane < 2 * FOX_HEADS, mid.astype(F32),
                   jnp.where(lane < 3 * FOX_HEADS, lo.astype(F32), jnp.where(lane == 3 * FOX_HEADS, 1.0, 0.0))))
    aug = _dot(f3.astype(BF16), pall_ref[...]).astype(BF16)
    fq = _dot(hb, wfq_ref[...]).astype(BF16)
    fk = _dot(hb, wfk_ref[...]).astype(BF16)
    hdim = FOX_HEAD_DIM
    for hd in range(FOX_HEADS):
        own = slice((hd % 2) * hdim, (hd % 2 + 1) * hdim)
        spare = slice((1 - hd % 2) * hdim, (2 - hd % 2) * hdim)
        src = slice(hd * hdim, (hd + 1) * hdim)
        asrc = (hd ^ 1) * hdim
        q_ref[0, MLA_HEADS + hd, :, own] = fq[:, src]
        k_ref[0, MLA_HEADS + hd, :, own] = fk[:, src]
        q_ref[0, MLA_HEADS + hd, :, spare] = aug[:, asrc:asrc + hdim]
        k_ref[0, MLA_HEADS + hd, :, spare] = aug[:, FOX_WIDTH + asrc:FOX_WIDTH + asrc + hdim]
    put_values(MLA_HEADS, _dot(hb, wfv_ref[...]))

    sg_ref[...] = jax.nn.sigmoid(_dot(hb, wg_ref[...])).astype(BF16)


def _in_proj(x2, mod8, g_mix, fdec, pos, freq, w, batch, seq, tm):
    t = x2.shape[0]
    tps = seq // tm
    consts = [w["wlat"], w["wkr"], w["gq"], w["gkv"], w["wqa"], w["wqb"], w["wka"], w["wva"],
              w["wfq"], w["wfk"], w["wfv"], w["pall"], w["wg"]]
    head_shape = jax.ShapeDtypeStruct((batch, N_HEADS, seq, HEAD_PAD), BF16)
    head_spec = pl.BlockSpec((1, N_HEADS, tm, HEAD_PAD), lambda i: (i // tps, 0, i % tps, 0))
    pair_shape = jax.ShapeDtypeStruct((batch, N_HEADS, V_ROWS, seq), BF16)
    pair_spec = pl.BlockSpec((1, N_HEADS, V_ROWS, tm), lambda i: (i // tps, 0, 0, i % tps))
    return pl.pallas_call(
        _in_proj_body,
        out_shape=(head_shape, head_shape, pair_shape,
                   jax.ShapeDtypeStruct((t, 2 * D_MODEL), BF16)),
        grid=(t // tm,),
        in_specs=[pl.BlockSpec((tm, D_MODEL), lambda i: (i, 0)),
                  pl.BlockSpec((1, 8, D_MODEL), lambda i: (i // tps, 0, 0)),
                  _const_spec((1, D_MODEL)),
                  pl.BlockSpec((tm, LANES), lambda i: (i, 0)),
                  pl.BlockSpec((tm, 1), lambda i: (i, 0)),
                  _const_spec((1, LANES))] + [_const_spec(a.shape) for a in consts],
        out_specs=(head_spec, head_spec, pair_spec,
                   pl.BlockSpec((tm, 2 * D_MODEL), lambda i: (i, 0))),
        compiler_params=_cparams(("parallel",)),
        name="in_proj",
    )(x2, mod8, g_mix, fdec, pos, freq, *consts)


def _attn_body(bq, sq, q_ref, k_ref, v_ref, o_ref, s_scr, p_scr, acc_scr):
    qi = pl.program_id(2)
    bk = sq
    n_heads = q_ref.shape[1]
    n_sub = bq // sq
    assert n_sub % 2 == 0
    chains = [(hh, u) for hh in range(n_heads) for u in range(n_sub)]
    n_chains = len(chains)
    n_main = qi * n_sub

    def chunk_start(j):
        return pl.multiple_of(jnp.maximum(j, 0) * bk, bk)

    def scores(c, j, par):
        hh, u = chains[c]
        k = k_ref[0, hh, pl.ds(chunk_start(j), bk), :]
        q = q_ref[0, hh, u * sq:(u + 1) * sq, :]
        s = lax.dot_general(k, q, (((1,), (1,)), ((), ())), preferred_element_type=F32)
        s_scr[par, c] = s
        return jnp.max(s, axis=0, keepdims=True)

    def values(c, j, par, alpha):
        vt = v_ref[0, chains[c][0], :, pl.ds(chunk_start(j), bk)]
        acc_scr[c] = alpha * acc_scr[c] + _dot(vt, p_scr[par, c])

    def softmax(c, par, m, smax, masked):
        s = s_scr[par, c]
        if masked:
            key = lax.broadcasted_iota(I32, (bk, sq), 0)
            qry = lax.broadcasted_iota(I32, (bk, sq), 1)
            s = jnp.where(key <= qry, s, NEG_INF)
            smax = jnp.max(s, axis=0, keepdims=True)
        m_new = jnp.maximum(m, smax)
        p_scr[par, c] = jnp.exp2((s - m_new).astype(BF16))
        return m_new, jnp.exp2(m - m_new)

    def stage(j, par, state, active, has_next, masked_of):
        nxt = {c: scores(c, j + 1, 1 - par) if has_next(c) else state[c][1] for c in active}
        for c in active:
            values(c, j - 1, 1 - par, state[c][2])
        out = list(state)
        for c in active:
            m, smax, _ = state[c]
            m_new, alpha = softmax(c, par, m, smax, masked_of(c))
            out[c] = (m_new, nxt[c], alpha)
        return out

    p_scr[1] = jnp.zeros(p_scr.shape[1:], BF16)
    acc_scr[...] = jnp.zeros_like(acc_scr)
    state = [(jnp.full((1, sq), NEG_INF, F32), scores(c, 0, 0), jnp.ones((1, sq), F32)) for c in range(n_chains)]

    def step(jj, st):
        for par in range(2):
            st = stage(2 * jj + par, par, list(st), range(n_chains), lambda c: True, lambda c: False)
        return tuple(st)

    state = list(lax.fori_loop(0, n_main // 2, step, tuple(state)))

    for t in range(n_sub):
        active = [c for c, (_, u) in enumerate(chains) if u >= t]
        state = stage(n_main + t, t % 2, state, active, lambda c, t=t: chains[c][1] > t,
                      lambda c, t=t: chains[c][1] == t)

    heads = []
    for hh in range(n_heads):
        parts = []
        for u in range(n_sub):
            c = hh * n_sub + u
            values(c, n_main + u, u % 2, state[c][2])
            parts.append(acc_scr[c, :MLA_V, :] / acc_scr[c, MLA_V:MLA_V + 1, :])
        heads.append(jnp.concatenate(parts, axis=1))
    o_ref[0] = jnp.concatenate(heads, axis=0).T.astype(BF16)


def _attention(q_all, k_all, v_all, bq):
    batch, _, seq, _ = q_all.shape
    sq = min(256, bq)
    n_chains = 2 * (bq // sq)
    return pl.pallas_call(
        functools.partial(_attn_body, bq, sq),
        out_shape=jax.ShapeDtypeStruct((batch, seq, N_HEADS * MLA_V), BF16),
        grid=(batch, N_HEADS // 2, seq // bq),
        in_specs=[pl.BlockSpec((1, 2, bq, HEAD_PAD), lambda b, hp, qi: (b, hp, qi, 0)),
                  pl.BlockSpec((1, 2, seq, HEAD_PAD), lambda b, hp, qi: (b, hp, 0, 0)),
                  pl.BlockSpec((1, 2, V_ROWS, seq), lambda b, hp, qi: (b, hp, 0, 0))],
        out_specs=pl.BlockSpec((1, bq, 2 * MLA_V), lambda b, hp, qi: (b, qi, hp)),
        scratch_shapes=[pltpu.VMEM((2, n_chains, sq, sq), F32),
                        pltpu.VMEM((2, n_chains, sq, sq), BF16),
                        pltpu.VMEM((n_chains, V_ROWS, sq), F32)],
        compiler_params=_cparams(("parallel", "parallel", "arbitrary")),
        name="attention",
    )(q_all, k_all, v_all)


def _pack_halves(x):
    w = x.shape[1] // 2
    a = lax.bitcast_convert_type(x[:, :w].astype(BF16).astype(F32), U32)
    b = lax.bitcast_convert_type(x[:, w:].astype(BF16).astype(F32), U32)
    return a | (b >> 16)


def _unpack_halves(p):
    a = lax.bitcast_convert_type(p & jnp.uint32(0xFFFF0000), F32)
    b = lax.bitcast_convert_type(p << 16, F32)
    return a, b


def _out_proj_body(o_ref, sg_ref, x_ref, mod_ref, gffn_ref, womla_ref, wofox_ref, wout_ref,
                   wr2_ref, x1_ref, hp_ref, lg_ref):
    o = o_ref[...]
    half = o.shape[1] // 2
    mo = _dot(o[:, :half], womla_ref[...])
    fo = _dot(o[:, half:], wofox_ref[...])
    sg = sg_ref[...]
    merged = sg[:, :D_MODEL].astype(F32) * mo + sg[:, D_MODEL:].astype(F32) * fo
    mix = _dot(merged.astype(BF16), wout_ref[...])
    mod = mod_ref[0]
    x1 = x_ref[...] + mod[2:3, :] * mix
    x1_ref[...] = x1
    h2 = _modulated_norm(x1, gffn_ref[...], mod, 3, 4)
    hhi = h2.astype(BF16)
    hlo = (h2 - hhi.astype(F32)).astype(BF16)
    l2 = _dot(hhi, wr2_ref[...])
    lg_ref[...] = (l2[:, :LANES] + l2[:, LANES:] + _dot(hlo, wr2_ref[:, :LANES])).T
    hp_ref[...] = _pack_halves(h2)


def _out_proj(o2, sg, x2, mod8, g_ffn, w, seq, tm):
    t = x2.shape[0]
    tps = seq // tm
    consts = [w["womla"], w["wofox"], w["wout"], w["wr2"]]
    return pl.pallas_call(
        _out_proj_body,
        out_shape=(jax.ShapeDtypeStruct((t, D_MODEL), F32),
                   jax.ShapeDtypeStruct((t, D_MODEL // 2), U32),
                   jax.ShapeDtypeStruct((LANES, t), F32)),
        grid=(t // tm,),
        in_specs=[pl.BlockSpec((tm, D_MODEL), lambda i: (i, 0)),
                  pl.BlockSpec((tm, 2 * D_MODEL), lambda i: (i, 0)),
                  pl.BlockSpec((tm, D_MODEL), lambda i: (i, 0)),
                  pl.BlockSpec((1, 8, D_MODEL), lambda i: (i // tps, 0, 0)),
                  _const_spec((1, D_MODEL))] + [_const_spec(a.shape) for a in consts],
        out_specs=(pl.BlockSpec((tm, D_MODEL), lambda i: (i, 0)),
                   pl.BlockSpec((tm, D_MODEL // 2), lambda i: (i, 0)),
                   pl.BlockSpec((LANES, tm), lambda i: (0, i))),
        compiler_params=_cparams(("parallel",)),
        name="out_proj",
    )(o2, sg, x2, mod8, g_ffn, *consts)


def _route_body(lt_ref, b_ref, eidx_ref, rank_ref, gate_ref, cnt_ref, carry_ref):
    i = pl.program_id(0)

    @pl.when(i == 0)
    def _():
        carry_ref[...] = jnp.zeros_like(carry_ref)

    s = jax.nn.sigmoid(lt_ref[...])
    c = s + b_ref[...]
    tr = s.shape[1]
    sub = lax.broadcasted_iota(I32, (GROUP_SIZE, tr), 0).astype(F32)

    gs = []
    for g in range(N_GROUPS):
        cg = c[g * GROUP_SIZE:(g + 1) * GROUP_SIZE, :]
        m1 = jnp.max(cg, axis=0, keepdims=True)
        i1 = jnp.min(jnp.where(cg == m1, sub, float(GROUP_SIZE)), axis=0, keepdims=True)
        m2 = jnp.max(jnp.where(sub == i1, NEG_INF, cg), axis=0, keepdims=True)
        gs.append(m1 + m2)

    masked = []
    for g in range(N_GROUPS):
        beats = jnp.zeros_like(gs[g])
        for o in range(N_GROUPS):
            if o == g:
                continue
            better = (gs[o] >= gs[g]) if o < g else (gs[o] > gs[g])
            beats = beats + jnp.where(better, 1.0, 0.0)
        keep = beats < float(TOPK_GROUPS)
        cg = c[g * GROUP_SIZE:(g + 1) * GROUP_SIZE, :]
        masked.append(jnp.where(keep, cg, NEG_INF))
    mc = jnp.concatenate(masked, axis=0)

    eio = lax.broadcasted_iota(I32, (N_EXPERTS, tr), 0).astype(F32)
    picks = []
    selected = jnp.zeros((N_EXPERTS, tr), F32)
    for _ in range(TOP_K):
        m = jnp.max(mc, axis=0, keepdims=True)
        idx = jnp.min(jnp.where(mc == m, eio, float(N_EXPERTS)), axis=0, keepdims=True)
        hit = eio == idx
        picks.append(idx)
        selected = jnp.where(hit, 1.0, selected)
        mc = jnp.where(hit, -3.0e38, mc)

    ssum = jnp.sum(selected * s, axis=0, keepdims=True)
    gate_full = selected * s / ssum * ROUTED_SCALE

    r_io = lax.broadcasted_iota(I32, (tr, tr), 0)
    c_io = lax.broadcasted_iota(I32, (tr, tr), 1)
    upper = jnp.where(r_io < c_io, 1.0, 0.0).astype(BF16)
    before = _dot(selected.astype(BF16), upper) + carry_ref[...]
    carry_new = carry_ref[...] + jnp.sum(selected, axis=1, keepdims=True)
    carry_ref[...] = carry_new
    cnt_ref[...] = jnp.broadcast_to(carry_new, cnt_ref.shape).astype(I32)

    for r in range(SLOTS):
        if r < TOP_K:
            hit = eio == picks[r]
            eidx_ref[r:r + 1, :] = picks[r].astype(I32)
            rank_ref[r:r + 1, :] = jnp.sum(jnp.where(hit, before, 0.0), axis=0, keepdims=True).astype(I32)
            gate_ref[r:r + 1, :] = jnp.sum(jnp.where(hit, gate_full, 0.0), axis=0, keepdims=True)
        else:
            eidx_ref[r:r + 1, :] = jnp.zeros((1, tr), I32)
            rank_ref[r:r + 1, :] = jnp.zeros((1, tr), I32)
            gate_ref[r:r + 1, :] = jnp.zeros((1, tr), F32)


def _route(logits_t, bias_col, tr):
    t = logits_t.shape[1]
    slot_i = jax.ShapeDtypeStruct((SLOTS, t), I32)
    slot_spec = pl.BlockSpec((SLOTS, tr), lambda i: (0, i))
    return pl.pallas_call(
        _route_body,
        out_shape=(slot_i, slot_i, jax.ShapeDtypeStruct((SLOTS, t), F32),
                   jax.ShapeDtypeStruct((N_EXPERTS, LANES), I32)),
        grid=(t // tr,),
        in_specs=[pl.BlockSpec((N_EXPERTS, tr), lambda i: (0, i)),
                  _const_spec((N_EXPERTS, 1))],
        out_specs=(slot_spec, slot_spec, slot_spec, _const_spec((N_EXPERTS, LANES))),
        scratch_shapes=[pltpu.VMEM((N_EXPERTS, 1), F32)],
        compiler_params=_cparams(("arbitrary",)),
        name="route",
    )(logits_t, bias_col)


def _dest_body(pstart_ref, eidx_ref, rank_ref, o_ref):
    e = eidx_ref[...]
    d = rank_ref[...]
    for j in range(N_EXPERTS):
        d = d + jnp.where(e == j, pstart_ref[j], 0)
    o_ref[...] = d


def _dest(pstart, eidx, rank, tr):
    t = eidx.shape[1]
    spec = pl.BlockSpec((SLOTS, tr), lambda i, ps: (0, i))
    return pl.pallas_call(
        _dest_body,
        out_shape=jax.ShapeDtypeStruct((SLOTS, t), I32),
        grid_spec=pltpu.PrefetchScalarGridSpec(
            num_scalar_prefetch=1, grid=(t // tr,), in_specs=[spec, spec], out_specs=spec),
        compiler_params=_cparams(("parallel",)),
        name="dest",
    )(pstart, eidx, rank)


def _sc_mesh():
    return plsc.VectorSubcoreMesh(core_axis_name="c", subcore_axis_name="s")


def _sc_worker_id():
    return lax.axis_index("s") * SC_CORES + lax.axis_index("c")


def _dispatch(hpk, dest3, n_rows):
    _, w = hpk.shape
    n_chunks, _, c = dest3.shape
    per_worker = n_chunks // SC_WORKERS
    assert per_worker % 2 == 0

    def body(h_hbm, d_hbm, xs_hbm, idx_v, rows_v, lsem, ssem):
        base = _sc_worker_id() * per_worker

        def loads(ch, slot):
            return (pltpu.make_async_copy(d_hbm.at[ch], idx_v.at[slot], lsem.at[slot]),
                    pltpu.make_async_copy(h_hbm.at[pl.ds(ch * c, c)], rows_v.at[slot], lsem.at[slot]))

        def scatters(slot):
            return [pltpu.make_async_copy(rows_v.at[slot], xs_hbm.at[idx_v.at[slot, k]], ssem.at[slot])
                    for k in range(TOP_K)]

        for cp in loads(base, 0):
            cp.start()

        @pl.loop(0, per_worker, step=2)
        def _(i0):
            for slot in range(2):
                i = i0 + slot
                for cp in loads(base + i, slot):
                    cp.wait()
                for cp in scatters(slot):
                    cp.start()

                @pl.when(i > 0)
                def _():
                    for cp in scatters(1 - slot):
                        cp.wait()

                @pl.when(i + 1 < per_worker)
                def _():
                    for cp in loads(base + i + 1, 1 - slot):
                        cp.start()

        for cp in scatters(1):
            cp.wait()

    return pl.kernel(
        body, mesh=_sc_mesh(),
        out_type=jax.ShapeDtypeStruct((n_rows, w), U32),
        scratch_types=[pltpu.VMEM((2, SLOTS, c), I32), pltpu.VMEM((2, c, w), U32),
                       pltpu.SemaphoreType.DMA((2,)), pltpu.SemaphoreType.DMA((2,))],
        name="dispatch",
    )(hpk, dest3)


def _experts_body(n_sub, be_ref, nv_ref, first_ref, nu_ref, xs_ref, wg_ref, wu_ref, wd_ref, ys_ref,
                  wgb_ref, wub_ref, wdb_ref):
    del be_ref
    i = pl.program_id(0)

    @pl.when(first_ref[i] == 1)
    def _():
        wgb_ref[...] = wg_ref[0].astype(BF16)
        wub_ref[...] = wu_ref[0].astype(BF16)
        wdb_ref[...] = wd_ref[0].astype(BF16)

    half = D_MODEL // 2
    sub = ROW_BLOCK // n_sub

    def run(n_active):
        row = lax.broadcasted_iota(I32, (sub, xs_ref.shape[1]), 0)
        gu = []
        for s in range(n_active):
            x = jnp.where(row + s * sub < nv_ref[i], xs_ref[s * sub:(s + 1) * sub, :], jnp.uint32(0))
            xa, xb = _unpack_halves(x)
            xa = xa.astype(BF16)
            xb = xb.astype(BF16)
            g = _dot(xa, wgb_ref[:half, :]) + _dot(xb, wgb_ref[half:, :])
            u = _dot(xa, wub_ref[:half, :]) + _dot(xb, wub_ref[half:, :])
            gu.append((g, u))
        for s, (g, u) in enumerate(gu):
            hb = (g * jax.nn.sigmoid(g) * u).astype(BF16)
            ys_ref[s * sub:(s + 1) * sub, :] = _pack_halves(_dot(hb, wdb_ref[...]))
        if n_active < n_sub:
            ys_ref[n_active * sub:, :] = jnp.zeros((ROW_BLOCK - n_active * sub, ys_ref.shape[1]), U32)

    used = i < nu_ref[0]
    for n_active in range(1, n_sub + 1):
        lo = (n_active - 1) * sub
        in_range = (nv_ref[i] > lo) if n_active == n_sub else ((nv_ref[i] > lo) & (nv_ref[i] <= lo + sub))
        pl.when(used & in_range)(functools.partial(run, n_active))

    @pl.when(jnp.logical_not(used))
    def _():
        ys_ref[...] = jnp.zeros_like(ys_ref)


def _experts(block_e, block_valid, block_first, n_used, xs, wg, wu, wd):
    n_rows, w = xs.shape
    n_blocks = n_rows // ROW_BLOCK

    def row_map(i, be, nv, bf, nu):
        return (jnp.minimum(i, nu[0] - 1), 0)

    def w_map(i, be, nv, bf, nu):
        return (be[i], 0, 0)

    return pl.pallas_call(
        functools.partial(_experts_body, ROW_BLOCK // MXU_ROWS),
        out_shape=jax.ShapeDtypeStruct((n_rows, w), U32),
        grid_spec=pltpu.PrefetchScalarGridSpec(
            num_scalar_prefetch=4, grid=(n_blocks,),
            in_specs=[pl.BlockSpec((ROW_BLOCK, w), row_map),
                      pl.BlockSpec((1, D_MODEL, D_EXPERT), w_map),
                      pl.BlockSpec((1, D_MODEL, D_EXPERT), w_map),
                      pl.BlockSpec((1, D_EXPERT, D_MODEL), w_map)],
            out_specs=pl.BlockSpec((ROW_BLOCK, w), lambda i, be, nv, bf, nu: (i, 0)),
            scratch_shapes=[pltpu.VMEM((D_MODEL, D_EXPERT), BF16), pltpu.VMEM((D_MODEL, D_EXPERT), BF16),
                            pltpu.VMEM((D_EXPERT, D_MODEL), BF16)]),
        compiler_params=_cparams(("arbitrary",)),
        name="experts",
    )(block_e, block_valid, block_first, n_used, xs, wg, wu, wd)


def _gather_back(ys, dest3):
    _, w = ys.shape
    n_chunks, _, c = dest3.shape
    per_worker = n_chunks // SC_WORKERS
    assert per_worker % 2 == 0 and TOP_K % 2 == 0
    hk = TOP_K // 2

    def body(ys_hbm, d_hbm, yk_hbm, idx_v, rows_v, isem, gsem, osem):
        base = _sc_worker_id() * per_worker

        def idx_load(ch, par):
            return pltpu.make_async_copy(d_hbm.at[ch], idx_v.at[par], isem.at[par])

        def gathers(par, g):
            return [pltpu.make_async_copy(ys_hbm.at[idx_v.at[par, g * hk + j]], rows_v.at[g, j], gsem.at[g])
                    for j in range(hk)]

        def outs(ch, g):
            return [pltpu.make_async_copy(rows_v.at[g, j], yk_hbm.at[g * hk + j, pl.ds(ch * c, c)], osem.at[g])
                    for j in range(hk)]

        idx_load(base, 0).start()

        @pl.loop(0, per_worker, step=2)
        def _(i0):
            for par in range(2):
                i = i0 + par
                ch = base + i
                idx_load(ch, par).wait()

                @pl.when(i + 1 < per_worker)
                def _():
                    idx_load(ch + 1, 1 - par).start()

                for g in range(2):
                    @pl.when(i > 0)
                    def _():
                        for cp in outs(ch - 1, g):
                            cp.wait()
                    for cp in gathers(par, g):
                        cp.start()
                for g in range(2):
                    for cp in gathers(par, g):
                        cp.wait()
                    for cp in outs(ch, g):
                        cp.start()

        for g in range(2):
            for cp in outs(base + per_worker - 1, g):
                cp.wait()

    return pl.kernel(
        body, mesh=_sc_mesh(),
        out_type=jax.ShapeDtypeStruct((TOP_K, n_chunks * c, w), U32),
        scratch_types=[pltpu.VMEM((2, SLOTS, c), I32), pltpu.VMEM((2, hk, c, w), U32),
                       pltpu.SemaphoreType.DMA((2,)), pltpu.SemaphoreType.DMA((2,)), pltpu.SemaphoreType.DMA((2,))],
        name="gather_back",
    )(ys, dest3)


def _combine_body(yk_ref, gate_ref, hp_ref, x1_ref, mod_ref, wsg_ref, wsu_ref, wsd_ref, gfin_ref, o_ref):
    ha, hb = _unpack_halves(hp_ref[...])
    ha = ha.astype(BF16)
    hb = hb.astype(BF16)
    half = D_MODEL // 2
    g = _dot(ha, wsg_ref[:half, :]) + _dot(hb, wsg_ref[half:, :])
    u = _dot(ha, wsu_ref[:half, :]) + _dot(hb, wsu_ref[half:, :])
    shared = _dot((g * jax.nn.sigmoid(g) * u).astype(BF16), wsd_ref[...])

    gate = gate_ref[...]
    ra = jnp.zeros(ha.shape, F32)
    rb = jnp.zeros(ha.shape, F32)
    for k in range(TOP_K):
        ya, yb = _unpack_halves(yk_ref[k])
        gk = gate[:, k:k + 1]
        ra = ra + gk * ya
        rb = rb + gk * yb
    moe = shared + jnp.concatenate([ra, rb], axis=1)
    mod = mod_ref[0]
    x2 = x1_ref[...] + mod[5:6, :] * moe
    o_ref[...] = _rms(x2) * gfin_ref[...]


def _combine(yk, gate_t, hpk, x1, mod8, w, g_final, seq, tq):
    t, wd = hpk.shape
    tps = seq // tq
    return pl.pallas_call(
        _combine_body,
        out_shape=jax.ShapeDtypeStruct((t, D_MODEL), F32),
        grid=(t // tq,),
        in_specs=[pl.BlockSpec((TOP_K, tq, wd), lambda i: (0, i, 0)),
                  pl.BlockSpec((tq, SLOTS), lambda i: (i, 0)),
                  pl.BlockSpec((tq, wd), lambda i: (i, 0)),
                  pl.BlockSpec((tq, D_MODEL), lambda i: (i, 0)),
                  pl.BlockSpec((1, 8, D_MODEL), lambda i: (i // tps, 0, 0)),
                  _const_spec(w["wsg"].shape), _const_spec(w["wsu"].shape), _const_spec(w["wsd"].shape),
                  _const_spec((1, D_MODEL))],
        out_specs=pl.BlockSpec((tq, D_MODEL), lambda i: (i, 0)),
        compiler_params=_cparams(("parallel",)),
        name="combine",
    )(yk, gate_t, hpk, x1, mod8, w["wsg"], w["wsu"], w["wsd"], g_final)


def _prep_weights(w_in, b_forget, g_q_lat, w_q_up, g_kv_lat, w_kv_up, w_o_mla, w_o_fox, w_out,
                  w_router, w_sh_gate, w_sh_up, w_sh_down):
    o_q, o_kv, o_kr = 0, MLA_Q_LORA, MLA_Q_LORA + MLA_KV_LORA
    o_fq = o_kr + MLA_ROPE
    o_fk, o_fv = o_fq + FOX_WIDTH, o_fq + 2 * FOX_WIDTH
    o_fl = o_fq + 3 * FOX_WIDTH
    o_ga = o_fl + FOX_HEADS
    o_gb = o_ga + D_MODEL
    w = {}
    w["wlat"] = w_in[:, o_q:o_kr].astype(BF16)
    half = MLA_ROPE // 2

    def rope_pair(cols):
        x1, x2 = cols[..., :half], cols[..., half:]
        z = jnp.zeros(cols.shape[:-1] + (MLA_NOPE,), cols.dtype)
        zt = jnp.zeros(cols.shape[:-1] + (HEAD_PAD - MLA_QK,), cols.dtype)
        plain = jnp.concatenate([z, x1, x2, zt], axis=-1)
        rot = jnp.concatenate([z, -x2, x1, zt], axis=-1)
        return plain, rot

    kr_plain, kr_rot = rope_pair(w_in[:, o_kr:o_fq])
    w["wkr"] = jnp.concatenate([kr_plain, kr_rot], axis=1).astype(BF16)
    w["gq"] = g_q_lat.reshape(1, -1)
    w["gkv"] = g_kv_lat.reshape(1, -1)

    scale_a = LOG2E / math.sqrt(MLA_QK)
    wq = (w_q_up * scale_a).reshape(MLA_Q_LORA, MLA_HEADS, MLA_QK)
    q_plain, q_rot = rope_pair(wq[..., MLA_NOPE:])
    nope = jnp.concatenate([wq[..., :MLA_NOPE], jnp.zeros((MLA_Q_LORA, MLA_HEADS, HEAD_PAD - MLA_NOPE), F32)], -1)
    w["wqa"] = (nope + q_plain).reshape(MLA_Q_LORA, -1).astype(BF16)
    w["wqb"] = q_rot.reshape(MLA_Q_LORA, -1).astype(BF16)

    wkv = w_kv_up.reshape(MLA_KV_LORA, MLA_HEADS, MLA_NOPE + MLA_V)
    zpad = jnp.zeros((MLA_KV_LORA, MLA_HEADS, HEAD_PAD - MLA_NOPE), F32)
    w["wka"] = jnp.concatenate([wkv[..., :MLA_NOPE], zpad], -1).reshape(MLA_KV_LORA, -1).astype(BF16)

    w["wva"] = wkv[..., MLA_NOPE:].reshape(MLA_KV_LORA, -1).astype(BF16)

    w["wfq"] = (w_in[:, o_fq:o_fk] * (LOG2E / math.sqrt(FOX_HEAD_DIM))).astype(BF16)
    w["wfk"] = w_in[:, o_fk:o_fv].astype(BF16)
    w["wfv"] = w_in[:, o_fv:o_fl].astype(BF16)

    wfl = jnp.concatenate([w_in[:, o_fl:o_ga]] * 3 + [jnp.zeros((D_MODEL, LANES - 3 * FOX_HEADS), F32)], 1)
    wfl_hi = wfl.astype(BF16)
    w["wfl2"] = jnp.concatenate([wfl_hi, (wfl - wfl_hi.astype(F32)).astype(BF16)], axis=1)
    w["bfl"] = jnp.concatenate([b_forget] * 3 + [jnp.zeros((LANES - 3 * FOX_HEADS,), F32)]).reshape(1, LANES)

    pall = np.zeros((LANES, 2 * FOX_WIDTH), np.float32)
    ones_row = 3 * FOX_HEADS
    for hd in range(FOX_HEADS):
        gq = (hd ^ 1) * FOX_HEAD_DIM
        gk = FOX_WIDTH + (hd ^ 1) * FOX_HEAD_DIM
        for term in range(3):
            pall[term * FOX_HEADS + hd, gq + FQ_COL + term] = 1.0
            pall[ones_row, gq + FK_COL + term] = 1.0
            pall[ones_row, gk + FQ_COL + term] = 1.0
            pall[term * FOX_HEADS + hd, gk + FK_COL + term] = -1.0
    w["pall"] = jnp.asarray(pall, BF16)

    w["wg"] = w_in[:, o_ga:o_gb + D_MODEL].astype(BF16)
    w["womla"] = w_o_mla.astype(BF16)
    w["wofox"] = w_o_fox.astype(BF16)
    w["wout"] = w_out.astype(BF16)
    wr = jnp.concatenate([w_router, jnp.zeros((D_MODEL, LANES - N_EXPERTS), F32)], 1)
    wr_hi = wr.astype(BF16)
    w["wr2"] = jnp.concatenate([wr_hi, (wr - wr_hi.astype(F32)).astype(BF16)], axis=1)
    w["wsg"] = w_sh_gate.astype(BF16)
    w["wsu"] = w_sh_up.astype(BF16)
    w["wsd"] = w_sh_down.astype(BF16)
    return w


def _rope_freq_row():
    half = MLA_ROPE // 2
    inv = np.power(ROPE_THETA, -np.arange(half, dtype=np.float32) / half).astype(np.float32)
    row = np.zeros((1, LANES), np.float32)
    row[0, MLA_NOPE:MLA_NOPE + half] = inv
    row[0, MLA_NOPE + half:MLA_NOPE + 2 * half] = inv
    return jnp.asarray(row)


def kernel(x, c, positions, w_mod, b_mod, g_mix_norm, w_in, b_forget, g_q_lat, w_q_up, g_kv_lat, w_kv_up,
           w_o_mla, w_o_fox, w_out, g_ffn_norm, w_router, b_router, w_exp_gate, w_exp_up, w_exp_down,
           w_sh_gate, w_sh_up, w_sh_down, g_final):
    batch, seq, d = x.shape
    assert d == D_MODEL and w_mod.shape[0] == 1
    t = batch * seq
    tm = min(512, seq)
    bq = min(2048, seq)
    tr = min(512, t)
    to = min(256, seq)
    tc = min(512, seq)
    assert seq % tm == 0 and seq % bq == 0 and t % tr == 0 and seq % tc == 0 and seq % to == 0 and batch <= 8
    assert t % (2 * SC_CHUNK * SC_WORKERS) == 0

    def layer0(a):
        return a.reshape(a.shape[1:])

    w = _prep_weights(*(layer0(a) for a in (w_in, b_forget, g_q_lat, w_q_up, g_kv_lat, w_kv_up, w_o_mla, w_o_fox,
                                            w_out, w_router, w_sh_gate, w_sh_up, w_sh_down)))

    c8 = jnp.zeros((8, D_MODEL), F32).at[:batch].set(c)
    mod = _mod(c8, layer0(w_mod), b_mod)
    mod8 = jnp.zeros((batch, 8, D_MODEL), F32).at[:, :N_MOD].set(mod[:batch].reshape(batch, N_MOD, D_MODEL))

    x2 = x.reshape(t, D_MODEL)
    fdec = _fox_decay(x2, mod8, g_mix_norm, w["wfl2"], w["bfl"], seq, tm)
    q_all, k_all, v_all, sg = _in_proj(x2, mod8, g_mix_norm, fdec, positions.reshape(t, 1),
                                       _rope_freq_row(), w, batch, seq, tm)
    o = _attention(q_all, k_all, v_all, bq)
    x1, hpk, logits_t = _out_proj(o.reshape(t, D_MODEL), sg, x2, mod8, g_ffn_norm, w, seq, to)

    eidx, rank, gate, counts = _route(logits_t, b_router.reshape(N_EXPERTS, 1), tr)

    cnt = counts[:, 0]
    padded = (cnt + ROW_BLOCK - 1) // ROW_BLOCK * ROW_BLOCK
    pend = jnp.cumsum(padded)
    pstart = pend - padded
    n_blocks = t * TOP_K // ROW_BLOCK + N_EXPERTS
    n_rows = n_blocks * ROW_BLOCK
    block_row = jnp.arange(n_blocks, dtype=I32) * ROW_BLOCK
    block_e = jnp.minimum(jnp.sum(pend[None, :] <= block_row[:, None], axis=1), N_EXPERTS - 1).astype(I32)
    n_used = (pend[-1:] // ROW_BLOCK).astype(I32)
    block_valid = jnp.clip((pstart + cnt)[block_e] - block_row, 0, ROW_BLOCK).astype(I32)
    block_first = jnp.concatenate([jnp.ones((1,), I32), (block_e[1:] != block_e[:-1]).astype(I32)])

    dest = _dest(pstart.astype(I32), eidx, rank, min(2048, t))
    dest3 = dest.reshape(SLOTS, t // SC_CHUNK, SC_CHUNK).transpose(1, 0, 2)
    xs = _dispatch(hpk, dest3, n_rows)
    ys = _experts(block_e, block_valid, block_first, n_used, xs,
                  layer0(w_exp_gate), layer0(w_exp_up), layer0(w_exp_down))
    yk = _gather_back(ys, dest3)
    out = _combine(yk, gate.T, hpk, x1, mod8, w, g_final.reshape(1, D_MODEL), seq, tc)
    return out.reshape(batch, seq, D_MODEL)
```
